```python
import math
import jax
import jax.numpy as jnp
from jax import lax
import numpy as np

D_MODEL = 1024
BATCH = 8
SEQ = 4096
DEPTH = 2

HEAD_DIM = 64
A_HEADS = 8
A_BLOCK = 256
A_TOPK = 3
A_QCHUNK = 16
A_WIDTH = A_HEADS * HEAD_DIM
B_GROUPS = 4
B_GROUP_DIM = 128
B_WIDTH = B_GROUPS * B_GROUP_DIM
B_WINDOWS = (2, 4, 8, 16)
C_HEADS = 8
C_KV_HEADS = 2
C_WINDOW = 128
C_WIDTH = C_HEADS * HEAD_DIM
C_KV_WIDTH = C_KV_HEADS * HEAD_DIM
N_BRANCH = 3
BRANCH_WIDTH = 512
SPLITS = (A_WIDTH, A_WIDTH, A_WIDTH, B_WIDTH, C_WIDTH, C_KV_WIDTH, C_KV_WIDTH, N_BRANCH * D_MODEL)
IN_COLS = 5888
REL_BUCKETS = 32
REL_MAX_DIST = 1024
REL_HEADS = A_HEADS + C_HEADS
MEM_LEN = 256
X_HEADS = 4
X_HEAD_DIM = 128
X_WIDTH = X_HEADS * X_HEAD_DIM
D_FF = 3584
N_EXPERTS = 8
TOP_K = 2
MOE_BLOCK = 256
N_DENSE = (DEPTH + 1) // 2
N_MOE = DEPTH // 2
EPS = 1e-6
NEG_INF = -1e30

kernel_name = 'hybrid_moba_pool_swa_moe_block'


def rms_norm(x, g):
    xf = x.astype(jnp.float32)
    y = xf * lax.rsqrt(jnp.mean(xf * xf, axis=-1, keepdims=True) + EPS)
    return (y * g.astype(jnp.float32)).astype(x.dtype)


def rel_bucket(dist):
    n = jnp.maximum(dist, 0)
    max_exact = REL_BUCKETS // 2
    nf = jnp.maximum(n, 1).astype(jnp.float32)
    large = max_exact + (jnp.log(nf / max_exact) / math.log(REL_MAX_DIST / max_exact)
                         * (REL_BUCKETS - max_exact)).astype(jnp.int32)
    large = jnp.minimum(large, REL_BUCKETS - 1)
    return jnp.where(n < max_exact, n, large)


def split_heads(t, n_heads, g=None):
    B, S, _ = t.shape
    t = t.reshape(B, S, n_heads, HEAD_DIM)
    if g is not None:
        t = rms_norm(t, g)
    return t.transpose(0, 2, 1, 3)


def merge_heads(o):
    B, H, S, Dh = o.shape
    return o.transpose(0, 2, 1, 3).reshape(B, S, H * Dh)


def moba_attention(q, k, v, table):
    B, H, S, Dh = q.shape
    S_pad = -(-S // A_BLOCK) * A_BLOCK
    pad = ((0, 0), (0, 0), (0, S_pad - S), (0, 0))
    q, k, v = jnp.pad(q, pad), jnp.pad(k, pad), jnp.pad(v, pad)
    nb = S_pad // A_BLOCK
    n_sel = min(A_TOPK, nb)
    scale = Dh ** -0.5
    kb = k.reshape(B, H, nb, A_BLOCK, Dh)
    vb = v.reshape(B, H, nb, A_BLOCK, Dh)
    kmean = jnp.mean(kb.astype(jnp.float32), axis=3)
    q_blk = jnp.arange(S_pad) // A_BLOCK
    score = jnp.einsum('bhsd,bhnd->bhsn', q.astype(jnp.float32), kmean)
    past = jnp.arange(nb)[None, :] < q_blk[:, None]
    score = jnp.where(past, score, -jnp.inf)
    _, sel = lax.top_k(score, n_sel)
    gather = jax.vmap(jax.vmap(lambda blocks, idx: blocks[idx]))
    head_ids = jnp.arange(H)[None, :, None, None, None]
    offs = jnp.arange(A_BLOCK)

    def chunk(c):
        start = c * A_QCHUNK
        qc = lax.dynamic_slice_in_dim(q, start, A_QCHUNK, axis=2)
        sc = lax.dynamic_slice_in_dim(sel, start, A_QCHUNK, axis=2)
        qpos = start + jnp.arange(A_QCHUNK)
        own = start // A_BLOCK
        valid_sel = jnp.arange(n_sel) < own
        k_sel = gather(kb, sc)
        v_sel = gather(vb, sc)
        kpos_sel = sc[..., None] * A_BLOCK + offs
        lg_sel = jnp.einsum('bhqd,bhqkld->bhqkl', qc, k_sel,
                            preferred_element_type=jnp.float32) * scale
        lg_sel = lg_sel + table[head_ids, rel_bucket(qpos[:, None, None] - kpos_sel)]
        lg_sel = jnp.where(valid_sel[:, None], lg_sel, NEG_INF)
        k_own = lax.dynamic_index_in_dim(kb, own, axis=2, keepdims=False)
        v_own = lax.dynamic_index_in_dim(vb, own, axis=2, keepdims=False)
        dist_own = qpos[:, None] - (own * A_BLOCK + offs)[None, :]
        lg_own = jnp.einsum('bhqd,bhld->bhql', qc, k_own,
                            preferred_element_type=jnp.float32) * scale
        lg_own = lg_own + table[:, rel_bucket(dist_own)][None]
        lg_own = jnp.where(dist_own >= 0, lg_own, NEG_INF)
        logits = jnp.concatenate([lg_sel.reshape(B, H, A_QCHUNK, n_sel * A_BLOCK), lg_own], axis=-1)
        p = jax.nn.softmax(logits, axis=-1).astype(v.dtype)
        p_sel = p[..., :n_sel * A_BLOCK].reshape(B, H, A_QCHUNK, n_sel, A_BLOCK)
        p_own = p[..., n_sel * A_BLOCK:]
        return (jnp.einsum('bhqkl,bhqkld->bhqd', p_sel, v_sel)
                + jnp.einsum('bhql,bhld->bhqd', p_own, v_own))

    outs = lax.map(chunk, jnp.arange(S_pad // A_QCHUNK))
    out = outs.transpose(1, 2, 0, 3, 4).reshape(B, H, S_pad, Dh)[:, :, :S]
    return merge_heads(out)


def pool_mixer(u, pool_w, pool_scale):
    B, S, _ = u.shape
    ug = u.reshape(B, S, B_GROUPS, B_GROUP_DIM)
    cs = jnp.pad(jnp.cumsum(ug.astype(jnp.float32), axis=1), ((0, 0), (1, 0), (0, 0), (0, 0)))
    t = jnp.arange(S)
    win = jnp.array(B_WINDOWS, dtype=jnp.int32)
    lo = jnp.maximum(t[:, None] + 1 - win[None, :], 0)
    lo_vals = jnp.take_along_axis(cs, lo[None, :, :, None], axis=1)
    cnt = (t[:, None] + 1 - lo).astype(jnp.float32)
    pooled = (cs[:, 1:] - lo_vals) / cnt[None, :, :, None] - ug.astype(jnp.float32)
    mixed = jnp.einsum('bsgc,gcd->bsgd', pooled.astype(u.dtype), pool_w)
    return mixed.reshape(B, S, B_WIDTH) * pool_scale


def swa_attention(q, k, v, sinks, table):
    B, Hq, S, Dh = q.shape
    Hkv = k.shape[1]
    G = Hq // Hkv
    W = C_WINDOW
    nblk = S // W
    scale = Dh ** -0.5
    qb = q.reshape(B, Hkv, G, nblk, W, Dh)
    kb = k.reshape(B, Hkv, nblk, W, Dh)
    vb = v.reshape(B, Hkv, nblk, W, Dh)
    shift = ((0, 0), (0, 0), (1, 0), (0, 0), (0, 0))
    kcat = jnp.concatenate([jnp.pad(kb, shift)[:, :, :-1], kb], axis=3)
    vcat = jnp.concatenate([jnp.pad(vb, shift)[:, :, :-1], vb], axis=3)
    logits = jnp.einsum('bkgnqd,bknsd->bkgnqs', qb, kcat,
                        preferred_element_type=jnp.float32) * scale
    qi = jnp.arange(W)
    kj = jnp.arange(2 * W)
    dist = W + qi[:, None] - kj[None, :]
    bias = table[:, rel_bucket(dist)].reshape(Hkv, G, 1, W, 2 * W)
    blk_ok = (jnp.arange(nblk)[:, None, None] > 0) | (kj >= W)[None, None, :]
    mask = ((dist >= 0) & (dist < W))[None] & blk_ok
    logits = jnp.where(mask, logits + bias, NEG_INF)
    sink = sinks.astype(jnp.float32).reshape(1, Hkv, G, 1, 1, 1)
    m = jnp.maximum(jnp.max(logits, axis=-1, keepdims=True), sink)
    p = jnp.exp(logits - m)
    p = (p / (jnp.sum(p, axis=-1, keepdims=True) + jnp.exp(sink - m))).astype(v.dtype)
    out = jnp.einsum('bkgnqs,bknsd->bkgnqd', p, vcat).reshape(B, Hq, S, Dh)
    return merge_heads(out)


def mixer_sublayer(xn, w_in, b_gate, a_q_gain, a_k_gain, pool_w, pool_scale,
                   c_q_gain, c_k_gain, c_sinks, w_branch, w_out, rel_bias):
    B, S, _ = xn.shape
    h = xn @ w_in
    split_at = [int(i) for i in np.cumsum(SPLITS)[:-1]]
    qa, ka, va, ub, qc, kc, vc, gl = jnp.split(h, split_at, axis=-1)
    tab_a = rel_bias[:, :A_HEADS].T
    tab_c = rel_bias[:, A_HEADS:].T
    oa = moba_attention(split_heads(qa, A_HEADS, a_q_gain), split_heads(ka, A_HEADS, a_k_gain),
                        split_heads(va, A_HEADS), tab_a)
    ob = pool_mixer(ub, pool_w, pool_scale)
    oc = swa_attention(split_heads(qc, C_HEADS, c_q_gain), split_heads(kc, C_KV_HEADS, c_k_gain),
                       split_heads(vc, C_KV_HEADS), c_sinks, tab_c)
    gates = jax.nn.sigmoid(gl.reshape(B, S, N_BRANCH, D_MODEL) + b_gate.reshape(N_BRANCH, D_MODEL))
    merged = gates[:, :, 0] * (oa @ w_branch[0])
    merged = merged + gates[:, :, 1] * (ob @ w_branch[1])
    merged = merged + gates[:, :, 2] * (oc @ w_branch[2])
    return merged @ w_out


def cross_attention(xn, memn, w_xq, w_xk, w_xv, q_gain, k_gain, w_xo):
    B, S, _ = xn.shape
    M = memn.shape[1]
    q = rms_norm((xn @ w_xq).reshape(B, S, X_HEADS, X_HEAD_DIM), q_gain)
    k = rms_norm((memn @ w_xk).reshape(B, M, X_HEADS, X_HEAD_DIM), k_gain)
    v = (memn @ w_xv).reshape(B, M, X_HEADS, X_HEAD_DIM)
    logits = jnp.einsum('bshd,bmhd->bhsm', q, k, preferred_element_type=jnp.float32) * X_HEAD_DIM ** -0.5
    p = jax.nn.softmax(logits, axis=-1).astype(v.dtype)
    o = jnp.einsum('bhsm,bmhd->bshd', p, v).reshape(B, S, X_WIDTH)
    return o @ w_xo


def swiglu(x, w1, w3, w2):
    return (jax.nn.silu(x @ w1) * (x @ w3)) @ w2


def moe_ffn(xn, router, w1, w3, w2):
    B, S, D = xn.shape
    N = B * S
    A = N * TOP_K
    xt = xn.reshape(N, D)
    logits = (xt @ router).astype(jnp.float32)
    top_logit, top_e = lax.top_k(logits, TOP_K)
    gate = jax.nn.softmax(top_logit, axis=-1)
    e_flat = top_e.reshape(A)
    g_flat = gate.reshape(A)
    tok_flat = jnp.arange(A, dtype=jnp.int32) // TOP_K
    order = jnp.argsort(e_flat)
    e_sorted = e_flat[order]
    counts = jnp.bincount(e_flat, length=N_EXPERTS)
    start = jnp.cumsum(counts) - counts
    padded = (counts + MOE_BLOCK - 1) // MOE_BLOCK * MOE_BLOCK
    pend = jnp.cumsum(padded)
    pstart = pend - padded
    dest = pstart[e_sorted] + (jnp.arange(A) - start[e_sorted])
    n_blocks = -(-A // MOE_BLOCK) + N_EXPERTS
    P = n_blocks * MOE_BLOCK
    slot_tok = jnp.full((P,), N, dtype=jnp.int32).at[dest].set(tok_flat[order])
    slot_gate = jnp.zeros((P,), jnp.float32).at[dest].set(g_flat[order])
    block_e = jnp.minimum(jnp.searchsorted(pend, jnp.arange(n_blocks) * MOE_BLOCK, side='right'),
                          N_EXPERTS - 1)
    x_pad = jnp.concatenate([xt, jnp.zeros((1, D), xt.dtype)], axis=0)[slot_tok]

    def run_block(args):
        xb, e = args
        return swiglu(xb, w1[e], w3[e], w2[e])

    y = lax.map(run_block, (x_pad.reshape(n_blocks, MOE_BLOCK, D), block_e)).reshape(P, D)
    y = y * slot_gate[:, None].astype(y.dtype)
    out = jax.ops.segment_sum(y, slot_tok, num_segments=N + 1)[:N]
    return out.reshape(B, S, D)


def setup_inputs(seed: int = 0) -> dict:
    key = jax.random.key(seed)
    ks = list(jax.random.split(key, 40))
    cnt = [0]

    def nk():
        cnt[0] += 1
        return ks[cnt[0] - 1]

    def w(shape, fan_in):
        return jax.random.normal(nk(), shape, jnp.float32) * fan_in ** -0.5

    def gain(shape):
        return 1.0 + 0.02 * jax.random.normal(nk(), shape, jnp.float32)

    def small(shape, s):
        return s * jax.random.normal(nk(), shape, jnp.float32)

    return {
        'x': jax.random.normal(nk(), (BATCH, SEQ, D_MODEL), jnp.float32),
        'mem': jax.random.normal(nk(), (BATCH, MEM_LEN, D_MODEL), jnp.float32),
        'rel_bias': small((REL_BUCKETS, REL_HEADS), 0.5),
        'norm_mix': gain((DEPTH, D_MODEL)),
        'w_in': w((DEPTH, D_MODEL, IN_COLS), D_MODEL),
        'b_gate': small((DEPTH, N_BRANCH * D_MODEL), 0.01),
        'a_q_gain': gain((DEPTH, HEAD_DIM)),
        'a_k_gain': gain((DEPTH, HEAD_DIM)),
        'pool_w': w((DEPTH, B_GROUPS, B_GROUP_DIM, B_GROUP_DIM), B_GROUP_DIM),
        'pool_scale': gain((DEPTH, B_WIDTH)),
        'c_q_gain': gain((DEPTH, HEAD_DIM)),
        'c_k_gain': gain((DEPTH, HEAD_DIM)),
        'c_sinks': small((DEPTH, C_HEADS), 0.5),
        'w_branch': w((DEPTH, N_BRANCH, BRANCH_WIDTH, D_MODEL), BRANCH_WIDTH),
        'w_out': w((DEPTH, D_MODEL, D_MODEL), D_MODEL),
        'norm_cross': gain((DEPTH, D_MODEL)),
        'norm_mem': gain((DEPTH, D_MODEL)),
        'w_xq': w((DEPTH, D_MODEL, X_WIDTH), D_MODEL),
        'w_xk': w((DEPTH, D_MODEL, X_WIDTH), D_MODEL),
        'w_xv': w((DEPTH, D_MODEL, X_WIDTH), D_MODEL),
        'x_q_gain': gain((DEPTH, X_HEAD_DIM)),
        'x_k_gain': gain((DEPTH, X_HEAD_DIM)),
        'w_xo': w((DEPTH, X_WIDTH, D_MODEL), X_WIDTH),
        'norm_ffn': gain((DEPTH, D_MODEL)),
        'ffn_w1': w((N_DENSE, D_MODEL, D_FF), D_MODEL),
        'ffn_w3': w((N_DENSE, D_MODEL, D_FF), D_MODEL),
        'ffn_w2': w((N_DENSE, D_FF, D_MODEL), D_FF),
        'router': w((N_MOE, D_MODEL, N_EXPERTS), D_MODEL),
        'moe_w1': w((N_MOE, N_EXPERTS, D_MODEL, D_FF), D_MODEL),
        'moe_w3': w((N_MOE, N_EXPERTS, D_MODEL, D_FF), D_MODEL),
        'moe_w2': w((N_MOE, N_EXPERTS, D_FF, D_MODEL), D_FF),
    }


def reference(x, mem, rel_bias, norm_mix, w_in, b_gate, a_q_gain, a_k_gain, pool_w, pool_scale,
              c_q_gain, c_k_gain, c_sinks, w_branch, w_out, norm_cross, norm_mem, w_xq, w_xk, w_xv,
              x_q_gain, x_k_gain, w_xo, norm_ffn, ffn_w1, ffn_w3, ffn_w2, router, moe_w1, moe_w3, moe_w2):
    for l in range(DEPTH):
        xn = rms_norm(x, norm_mix[l])
        x = x + mixer_sublayer(xn, w_in[l], b_gate[l], a_q_gain[l], a_k_gain[l], pool_w[l], pool_scale[l],
                               c_q_gain[l], c_k_gain[l], c_sinks[l], w_branch[l], w_out[l], rel_bias)
        xn = rms_norm(x, norm_cross[l])
        memn = rms_norm(mem, norm_mem[l])
        x = x + cross_attention(xn, memn, w_xq[l], w_xk[l], w_xv[l], x_q_gain[l], x_k_gain[l], w_xo[l])
        xn = rms_norm(x, norm_ffn[l])
        if l % 2 == 0:
            i = l // 2
            x = x + swiglu(xn, ffn_w1[i], ffn_w3[i], ffn_w2[i])
        else:
            i = l // 2
            x = x + moe_ffn(xn, router[i], moe_w1[i], moe_w3[i], moe_w2[i])
    return x
```

```python
import functools
import math

import jax
import jax.numpy as jnp
import numpy as np
from jax import lax
from jax.experimental import pallas as pl
from jax.experimental.pallas import tpu as pltpu

F32 = jnp.float32
BF16 = jnp.bfloat16

HEAD_DIM = 64
A_HEADS = 8
A_BLOCK = 256
A_TOPK = 3
B_GROUPS = 4
B_GROUP_DIM = 128
B_WINDOWS = (2, 4, 8, 16)
C_HEADS = 8
C_KV_HEADS = 2
C_WINDOW = 128
REL_BUCKETS = 32
REL_MAX_DIST = 1024
X_HEADS = 4
X_HEAD_DIM = 128
N_EXPERTS = 8
TOP_K = 2
EPS = 1e-6
NEG_INF = -1e30

WIDTH = 512
POOL_HALO = 16
ROW_TILE = 512
MOE_TILE = 1024
FF_CHUNK = 512
GATHER_ROWS = 256
VMEM_LIMIT = 56 * 1024 * 1024


def _dot(a, b):
    return jnp.dot(a, b, preferred_element_type=F32)


def _dot_nt(a, b):
    return lax.dot_general(a, b, (((1,), (1,)), ((), ())), preferred_element_type=F32)


def _rms(x, g):
    ms = jnp.mean(x * x, axis=-1, keepdims=True)
    return x * lax.rsqrt(ms + EPS) * g


def _const_spec(shape):
    zeros = (0,) * len(shape)
    return pl.BlockSpec(shape, lambda *_: zeros, pipeline_mode=pl.Buffered(1))


def _smem_spec():
    return pl.BlockSpec(memory_space=pltpu.SMEM)


def _params(sem):
    return pltpu.CompilerParams(dimension_semantics=sem, vmem_limit_bytes=VMEM_LIMIT)


def _rel_bucket(dist):
    n = jnp.maximum(dist, 0)
    max_exact = REL_BUCKETS // 2
    nf = jnp.maximum(n, 1).astype(jnp.float32)
    large = max_exact + (jnp.log(nf / max_exact) / math.log(REL_MAX_DIST / max_exact)
                         * (REL_BUCKETS - max_exact)).astype(jnp.int32)
    large = jnp.minimum(large, REL_BUCKETS - 1)
    return jnp.where(n < max_exact, n, large)


def _saturation_distance():
    ratio = REL_MAX_DIST / (REL_BUCKETS // 2)
    return int(math.ceil((REL_BUCKETS // 2) * ratio ** ((REL_BUCKETS - 1 - REL_BUCKETS // 2 + 0.5)
                                                       / (REL_BUCKETS - REL_BUCKETS // 2))))


def _moba_bias_tiles(tab_a, n_near):
    kk = jnp.arange(A_BLOCK)[:, None]
    qq = jnp.arange(A_BLOCK)[None, :]
    delta = jnp.arange(n_near)[:, None, None]
    dist = delta * A_BLOCK + qq[None] - kk[None]
    return tab_a[:, _rel_bucket(dist)]


def _swa_bias_tiles(tab_c):
    qi = jnp.arange(C_WINDOW)
    kj = jnp.arange(2 * C_WINDOW)
    dist = C_WINDOW + qi[:, None] - kj[None, :]
    return tab_c[:, _rel_bucket(dist)]


def _in_proj_kernel(x_ref, g_ref, w_ref, bg_ref, gn_ref, bd_ref,
                    qa_ref, ka_ref, va_ref, ub_ref, qc_ref, kc_ref, vc_ref, gt_ref):
    xb = _rms(x_ref[...], g_ref[...]).astype(BF16)
    bd = bd_ref[...]

    def proj(c0, width):
        return _dot(xb, w_ref[:, c0:c0 + width])

    def head_norm(t, gain):
        outs = []
        for c in range(0, t.shape[1], 256):
            wd = min(256, t.shape[1] - c)
            tc = t[:, c:c + wd]
            ss = _dot((tc * tc).astype(BF16), bd[:wd, :wd])
            outs.append(tc * lax.rsqrt(ss * (1.0 / HEAD_DIM) + EPS))
        y = outs[0] if len(outs) == 1 else jnp.concatenate(outs, axis=-1)
        return y * gain

    def tile_kv_heads(t):
        lane = lax.broadcasted_iota(jnp.int32, t.shape, 1)
        r = pltpu.roll(t, HEAD_DIM, axis=1)
        h0 = jnp.where(lane < HEAD_DIM, t, r)
        h1 = jnp.where(lane < HEAD_DIM, r, t)
        return jnp.concatenate([h0, h0, h1, h1], axis=-1)

    qa_ref[...] = head_norm(proj(0, WIDTH), gn_ref[0:1, :]).astype(BF16)
    ka_ref[...] = head_norm(proj(WIDTH, WIDTH), gn_ref[1:2, :]).astype(BF16)
    va_ref[...] = proj(2 * WIDTH, WIDTH).astype(BF16)
    ub_ref[...] = proj(3 * WIDTH, WIDTH)
    qc_ref[...] = head_norm(proj(4 * WIDTH, WIDTH), gn_ref[2:3, :]).astype(BF16)
    kv = C_KV_HEADS * HEAD_DIM
    c0 = 5 * WIDTH
    kc = head_norm(proj(c0, kv), gn_ref[3:4, :kv])
    kc_ref[...] = tile_kv_heads(kc).astype(BF16)
    vc_ref[...] = tile_kv_heads(proj(c0 + kv, kv)).astype(BF16)
    c0 += 2 * kv
    for c in range(0, gt_ref.shape[1], WIDTH):
        gl = proj(c0 + c, WIDTH) + bg_ref[:, c:c + WIDTH]
        gt_ref[:, c:c + WIDTH] = jax.nn.sigmoid(gl).astype(BF16)


def _in_proj(x, g, w_in, b_gate, gains, bd):
    n, d = x.shape
    tm = min(ROW_TILE, n)
    n_gate = b_gate.shape[1]
    row = lambda width: pl.BlockSpec((tm, width), lambda i: (i, 0))
    out_shape = [jax.ShapeDtypeStruct((n, WIDTH), BF16)] * 3 + [jax.ShapeDtypeStruct((n, WIDTH), F32)] \
        + [jax.ShapeDtypeStruct((n, WIDTH), BF16)] * 3 + [jax.ShapeDtypeStruct((n, n_gate), BF16)]
    return pl.pallas_call(
        _in_proj_kernel,
        grid=(n // tm,),
        in_specs=[row(d), _const_spec(g.shape), _const_spec(w_in.shape), _const_spec(b_gate.shape),
                  _const_spec(gains.shape), _const_spec(bd.shape)],
        out_specs=[row(WIDTH)] * 7 + [row(n_gate)],
        out_shape=out_shape,
        compiler_params=_params(("parallel",)),
        name="in_proj",
    )(x, g, w_in, b_gate, gains, bd)


def _moba_kernel(far_ref, q_ref, k_ref, v_ref, bias_ref, o_ref,
                 kmean_ref, vt_ref, sel_ref, m_ref, l_ref, acc_ref, *, nb, n_near):
    hp = pl.program_id(0)
    i = pl.program_id(2)
    L = A_BLOCK
    nbp = kmean_ref.shape[0]

    @pl.when(i == 0)
    def _():
        kmean_ref[...] = jnp.zeros(kmean_ref.shape, F32)
        for j in range(nb):
            kj = k_ref[j * L:(j + 1) * L, :].astype(F32)
            kmean_ref[j:j + 1, :] = jnp.mean(kj, axis=0, keepdims=True)
            vt_ref[j] = v_ref[j * L:(j + 1) * L, :].astype(F32).T.astype(BF16)

    q = q_ref[...]
    lane = lax.broadcasted_iota(jnp.int32, q.shape, 1)
    km = kmean_ref[...]
    km_hi = km.astype(BF16)
    km_lo = (km - km_hi.astype(F32)).astype(BF16)
    blk = lax.broadcasted_iota(jnp.int32, (nbp, L), 0)
    past = blk < i
    qh = []
    for hh in range(2):
        qm = jnp.where(lane // HEAD_DIM == hh, q, jnp.zeros_like(q))
        qh.append(qm)
        s = jnp.where(past, _dot_nt(km_hi, qm) + _dot_nt(km_lo, qm), -jnp.inf)
        rank = jnp.zeros((nbp, L), jnp.int32)
        for jp in range(nb):
            sj = s[jp:jp + 1, :]
            ahead = (sj > s) | ((sj == s) & (jp < blk))
            rank = rank + ahead.astype(jnp.int32)
        sel_ref[hh] = (past & (rank < A_TOPK)).astype(F32)
        m_ref[hh] = jnp.full((1, L), NEG_INF, F32)
        l_ref[hh] = jnp.zeros((1, L), F32)
    acc_ref[...] = jnp.zeros(acc_ref.shape, F32)

    kk = lax.broadcasted_iota(jnp.int32, (L, L), 0)
    qq = lax.broadcasted_iota(jnp.int32, (L, L), 1)

    def tile(j, delta, far, own):
        kj = k_ref[pl.ds(pl.multiple_of(j * L, L), L), :]
        vtj = vt_ref[j]
        for hh in range(2):
            s = _dot_nt(kj, qh[hh])
            if not far:
                s = s + bias_ref[hh, delta]
            if own:
                s = jnp.where(kk <= qq, s, NEG_INF)
            mj = jnp.max(s, axis=0, keepdims=True)
            p = jnp.exp(s - mj)
            lj = jnp.sum(p, axis=0, keepdims=True)
            o = _dot(vtj[hh * HEAD_DIM:(hh + 1) * HEAD_DIM, :], p.astype(BF16))
            if far:
                mj = mj + far_ref[hp * 2 + hh]
            m_old = m_ref[hh]
            if own:
                m_new = jnp.maximum(m_old, mj)
                beta = jnp.exp(mj - m_new)
            else:
                on = sel_ref[hh, pl.ds(j, 1), :] > 0.5
                m_new = jnp.where(on, jnp.maximum(m_old, mj), m_old)
                beta = jnp.where(on, jnp.exp(mj - m_new), 0.0)
            alpha = jnp.exp(m_old - m_new)
            m_ref[hh] = m_new
            l_ref[hh] = alpha * l_ref[hh] + beta * lj
            rows = slice(hh * HEAD_DIM, (hh + 1) * HEAD_DIM)
            acc_ref[rows, :] = alpha * acc_ref[rows, :] + beta * o

    n_far = jnp.maximum(i - (n_near - 1), 0)

    def far_body(j, c):
        tile(j, None, True, False)
        return c

    def near_body(j, c):
        tile(j, i - j, False, False)
        return c

    lax.fori_loop(0, n_far, far_body, 0)
    lax.fori_loop(n_far, i, near_body, 0)
    tile(i, 0, False, True)

    acc = acc_ref[...]
    o_t = jnp.concatenate([acc[:HEAD_DIM] / l_ref[0], acc[HEAD_DIM:] / l_ref[1]], axis=0)
    o_ref[...] = o_t.T.astype(BF16)


def _moba(q, k, v, bias_tiles, far_bias, batch, seq):
    n = q.shape[0]
    L = A_BLOCK
    nb = seq // L
    nbp = max(8, -(-nb // 8) * 8)
    n_near = bias_tiles.shape[1]
    pairs = A_HEADS // 2
    kern = functools.partial(_moba_kernel, nb=nb, n_near=n_near)
    return pl.pallas_call(
        kern,
        grid=(pairs, batch, nb),
        in_specs=[_smem_spec(),
                  pl.BlockSpec((L, 2 * HEAD_DIM), lambda hp, b, i: (b * nb + i, hp)),
                  pl.BlockSpec((seq, 2 * HEAD_DIM), lambda hp, b, i: (b, hp)),
                  pl.BlockSpec((seq, 2 * HEAD_DIM), lambda hp, b, i: (b, hp)),
                  pl.BlockSpec((2, n_near, L, L), lambda hp, b, i: (hp, 0, 0, 0))],
        out_specs=pl.BlockSpec((L, 2 * HEAD_DIM), lambda hp, b, i: (b * nb + i, hp)),
        out_shape=jax.ShapeDtypeStruct((n, WIDTH), BF16),
        scratch_shapes=[pltpu.VMEM((nbp, 2 * HEAD_DIM), F32),
                        pltpu.VMEM((nb, 2 * HEAD_DIM, L), BF16),
                        pltpu.VMEM((2, nbp, L), F32),
                        pltpu.VMEM((2, 1, L), F32),
                        pltpu.VMEM((2, 1, L), F32),
                        pltpu.VMEM((2 * HEAD_DIM, L), F32)],
        compiler_params=_params(("arbitrary", "arbitrary", "arbitrary")),
        name="moba",
    )(far_bias, q, k, v, bias_tiles)


def _swa_kernel(sink_ref, q_ref, kp_ref, kc_ref, vp_ref, vc_ref, bias_ref, o_ref):
    g = pl.program_id(0)
    blk = pl.program_id(2)
    W = C_WINDOW
    G = C_HEADS // C_KV_HEADS
    q = q_ref[...]
    kcat = jnp.concatenate([kp_ref[...], kc_ref[...]], axis=0)
    vcat = jnp.concatenate([vp_ref[...], vc_ref[...]], axis=0)
    lane = lax.broadcasted_iota(jnp.int32, q.shape, 1)
    qi = lax.broadcasted_iota(jnp.int32, (W, 2 * W), 0)
    kj = lax.broadcasted_iota(jnp.int32, (W, 2 * W), 1)
    dist = W + qi - kj
    mask = (dist >= 0) & (dist < W) & ((blk > 0) | (kj >= W))
    out = jnp.zeros(q.shape, F32)
    for hh in range(G):
        mine = lane // HEAD_DIM == hh
        qm = jnp.where(mine, q, jnp.zeros_like(q))
        s = jnp.where(mask, _dot_nt(qm, kcat) + bias_ref[hh], NEG_INF)
        sink = sink_ref[g * G + hh]
        m = jnp.maximum(jnp.max(s, axis=-1, keepdims=True), sink)
        p = jnp.exp(s - m)
        denom = jnp.sum(p, axis=-1, keepdims=True) + jnp.exp(sink - m)
        o = _dot(p.astype(BF16), vcat) / denom
        out = jnp.where(mine, o, out)
    o_ref[...] = out.astype(BF16)


def _swa(q, k_t, v_t, bias_tiles, sinks, batch, seq):
    n = q.shape[0]
    W = C_WINDOW
    nblk = seq // W
    G = C_HEADS // C_KV_HEADS
    gw = G * HEAD_DIM
    cur = lambda g, b, j: (b * nblk + j, g)
    prev = lambda g, b, j: (jnp.maximum(b * nblk + j - 1, 0), g)
    return pl.pallas_call(
        _swa_kernel,
        grid=(C_KV_HEADS, batch, nblk),
        in_specs=[_smem_spec(),
                  pl.BlockSpec((W, gw), cur),
                  pl.BlockSpec((W, gw), prev), pl.BlockSpec((W, gw), cur),
                  pl.BlockSpec((W, gw), prev), pl.BlockSpec((W, gw), cur),
                  pl.BlockSpec((G, W, 2 * W), lambda g, b, j: (g, 0, 0))],
        out_specs=pl.BlockSpec((W, gw), cur),
        out_shape=jax.ShapeDtypeStruct((n, WIDTH), BF16),
        compiler_params=_params(("arbitrary", "arbitrary", "arbitrary")),
        name="swa",
    )(sinks, q, k_t, k_t, v_t, v_t, bias_tiles)


def _seg_norm(t, gain, seg):
    outs = []
    for c in range(0, t.shape[1], seg):
        tc = t[:, c:c + seg]
        outs.append(tc * lax.rsqrt(jnp.mean(tc * tc, axis=-1, keepdims=True) + EPS))
    return jnp.concatenate(outs, axis=-1) * gain


def _mem_kv_kernel(mem_ref, g_ref, wk_ref, wv_ref, kg_ref, k_ref, v_ref):
    mb = _rms(mem_ref[...], g_ref[...]).astype(BF16)
    k_ref[...] = _seg_norm(_dot(mb, wk_ref[...]), kg_ref[...], X_HEAD_DIM).astype(BF16)
    v_ref[...] = _dot(mb, wv_ref[...]).astype(BF16)


def _mem_kv(mem, g, w_xk, w_xv, k_gain):
    n, d = mem.shape
    tm = min(ROW_TILE, n)
    row = lambda width: pl.BlockSpec((tm, width), lambda i: (i, 0))
    return pl.pallas_call(
        _mem_kv_kernel,
        grid=(n // tm,),
        in_specs=[row(d), _const_spec(g.shape), _const_spec(w_xk.shape), _const_spec(w_xv.shape),
                  _const_spec(k_gain.shape)],
        out_specs=[row(WIDTH), row(WIDTH)],
        out_shape=[jax.ShapeDtypeStruct((n, WIDTH), BF16)] * 2,
        compiler_params=_params(("parallel",)),
        name="mem_kv",
    )(mem, g, w_xk, w_xv, k_gain)


def _merge_cross_kernel(x_ref, oa_ref, ub_ref, halo_ref, oc_ref, gt_ref, pw_ref, ps_ref, wb_ref, wo_ref,
                        gx_ref, wq_ref, qg_ref, km_ref, vm_ref, wxo_ref, o_ref, *, seq):
    tm = x_ref.shape[0]
    d = x_ref.shape[1]
    t0 = (pl.program_id(0) * tm) % seq
    H = POOL_HALO

    halo = jnp.where(t0 > 0, halo_ref[...], 0.0)
    pos = t0 + lax.broadcasted_iota(jnp.int32, (tm, B_GROUP_DIM), 0)
    mixed = []
    for gi, win in enumerate(B_WINDOWS):
        cols = slice(gi * B_GROUP_DIM, (gi + 1) * B_GROUP_DIM)
        cur = ub_ref[:, cols]
        acc = jnp.concatenate([halo[:, cols], cur], axis=0)
        step = 1
        while step < win:
            acc = acc + pltpu.roll(acc, step, axis=0)
            step *= 2
        cnt = jnp.minimum(pos + 1, win).astype(F32)
        pooled = acc[H:, :] / cnt - cur
        mixed.append(_dot(pooled.astype(BF16), pw_ref[gi]))
    ob = jnp.concatenate(mixed, axis=-1) * ps_ref[...]

    merged = gt_ref[:, 0:d].astype(F32) * _dot(oa_ref[...], wb_ref[0])
    merged = merged + gt_ref[:, d:2 * d].astype(F32) * _dot(ob.astype(BF16), wb_ref[1])
    merged = merged + gt_ref[:, 2 * d:3 * d].astype(F32) * _dot(oc_ref[...], wb_ref[2])
    x1 = x_ref[...] + _dot(merged.astype(BF16), wo_ref[...])

    xb = _rms(x1, gx_ref[...]).astype(BF16)
    qn = _seg_norm(_dot(xb, wq_ref[...]), qg_ref[...], X_HEAD_DIM).astype(BF16)
    heads = []
    for h in range(X_HEADS):
        cols = slice(h * X_HEAD_DIM, (h + 1) * X_HEAD_DIM)
        s = _dot_nt(qn[:, cols], km_ref[:, cols])
        p = jnp.exp(s - jnp.max(s, axis=-1, keepdims=True))
        inv = 1.0 / jnp.sum(p, axis=-1, keepdims=True)
        heads.append(_dot(p.astype(BF16), vm_ref[:, cols]) * inv)
    o = jnp.concatenate(heads, axis=-1).astype(BF16)
    o_ref[...] = x1 + _dot(o, wxo_ref[...])


def _merge_cross(x, oa, ub, oc, gates, pool_w, pool_scale, w_branch, w_out,
                 g_cross, w_xq, q_gain, k_mem, v_mem, w_xo, seq, mem_len):
    n, d = x.shape
    tm = min(ROW_TILE, seq)
    row = lambda width: pl.BlockSpec((tm, width), lambda i: (i, 0))
    halo = pl.BlockSpec((POOL_HALO, WIDTH), lambda i: (jnp.maximum(i * (tm // POOL_HALO) - 1, 0), 0))
    mem = pl.BlockSpec((mem_len, WIDTH), lambda i: ((i * tm) // seq, 0))
    kern = functools.partial(_merge_cross_kernel, seq=seq)
    return pl.pallas_call(
        kern,
        grid=(n // tm,),
        in_specs=[row(d), row(WIDTH), row(WIDTH), halo, row(WIDTH), row(gates.shape[1]),
                  _const_spec(pool_w.shape), _const_spec(pool_scale.shape), _const_spec(w_branch.shape),
                  _const_spec(w_out.shape), _const_spec(g_cross.shape), _const_spec(w_xq.shape),
                  _const_spec(q_gain.shape), mem, mem, _const_spec(w_xo.shape)],
        out_specs=row(d),
        out_shape=jax.ShapeDtypeStruct((n, d), F32),
        compiler_params=_params(("parallel",)),
        name="merge_cross",
    )(x, oa, ub, ub, oc, gates, pool_w, pool_scale, w_branch, w_out,
      g_cross, w_xq, q_gain, k_mem, v_mem, w_xo)


def _swiglu_chunks(xb, w1_ref, w3_ref, w2_ref, acc):
    for c in range(0, w1_ref.shape[1], FF_CHUNK):
        h1 = _dot(xb, w1_ref[:, c:c + FF_CHUNK])
        h3 = _dot(xb, w3_ref[:, c:c + FF_CHUNK])
        acc = acc + _dot((jax.nn.silu(h1) * h3).astype(BF16), w2_ref[c:c + FF_CHUNK, :])
    return acc


def _ffn_kernel(x_ref, g_ref, w1_ref, w3_ref, w2_ref, o_ref):
    x = x_ref[...]
    xb = _rms(x, g_ref[...]).astype(BF16)
    o_ref[...] = _swiglu_chunks(xb, w1_ref, w3_ref, w2_ref, x)


def _ffn(x, g, w1, w3, w2):
    n, d = x.shape
    tm = min(ROW_TILE, n)
    row = pl.BlockSpec((tm, d), lambda i: (i, 0))
    return pl.pallas_call(
        _ffn_kernel,
        grid=(n // tm,),
        in_specs=[row, _const_spec(g.shape), _const_spec(w1.shape), _const_spec(w3.shape),
                  _const_spec(w2.shape)],
        out_specs=row,
        out_shape=jax.ShapeDtypeStruct((n, d), F32),
        compiler_params=_params(("parallel",)),
        name="ffn_dense",
    )(x, g, w1, w3, w2)


def _router_kernel(x_ref, g_ref, r_ref, xn_ref, lg_ref):
    xn = _rms(x_ref[...], g_ref[...])
    xn_ref[...] = xn
    r = r_ref[...]
    x_hi = xn.astype(BF16)
    x_lo = (xn - x_hi.astype(F32)).astype(BF16)
    r_hi = r.astype(BF16)
    r_lo = (r - r_hi.astype(F32)).astype(BF16)
    lg_ref[...] = _dot(x_hi, r_hi) + (_dot(x_lo, r_hi) + _dot(x_hi, r_lo))


def _router(x, g, router_padded):
    n, d = x.shape
    tm = min(ROW_TILE, n)
    row = lambda width: pl.BlockSpec((tm, width), lambda i: (i, 0))
    return pl.pallas_call(
        _router_kernel,
        grid=(n // tm,),
        in_specs=[row(d), _const_spec(g.shape), _const_spec(router_padded.shape)],
        out_specs=[row(d), row(router_padded.shape[1])],
        out_shape=[jax.ShapeDtypeStruct((n, d), F32),
                   jax.ShapeDtypeStruct((n, router_padded.shape[1]), F32)],
        compiler_params=_params(("parallel",)),
        name="router",
    )(x, g, router_padded)


def _row_copy(src_ref, dst_ref, sem, src_row, dst_row):
    return pltpu.make_async_copy(src_ref.at[pl.ds(src_row, 1), :], dst_ref.at[pl.ds(dst_row, 1), :], sem)


def _gather_kernel(idx_ref, src_ref, o_ref, sem):
    rows = o_ref.shape[0]

    def start(r, c):
        _row_copy(src_ref, o_ref, sem, idx_ref[0, 0, r], r).start()
        return c

    def wait(r, c):
        _row_copy(src_ref, o_ref, sem, 0, r).wait()
        return c

    lax.fori_loop(0, rows, start, 0)
    lax.fori_loop(0, rows, wait, 0)


def _gather_rows(src, idx):
    p = idx.shape[0]
    d = src.shape[1]
    rows = min(GATHER_ROWS, p)
    idx3 = idx.reshape(p // rows, 1, rows)
    return pl.pallas_call(
        _gather_kernel,
        grid=(p // rows,),
        in_specs=[pl.BlockSpec((1, 1, rows), lambda i: (i, 0, 0), memory_space=pltpu.SMEM),
                  pl.BlockSpec(memory_space=pl.ANY)],
        out_specs=pl.BlockSpec((rows, d), lambda i: (i, 0)),
        out_shape=jax.ShapeDtypeStruct((p, d), src.dtype),
        scratch_shapes=[pltpu.SemaphoreType.DMA],
        compiler_params=_params(("arbitrary",)),
        name="gather_rows",
    )(idx3, src)


def _expert_kernel(be_ref, nu_ref, x_ref, w1_ref, w3_ref, w2_ref, o_ref, xb_ref):
    b = pl.program_id(0)
    c = pl.program_id(1)

    @pl.when(c == 0)
    def _():
        o_ref[...] = jnp.zeros(o_ref.shape, F32)
        xb_ref[...] = x_ref[...].astype(BF16)

    @pl.when(b < nu_ref[0])
    def _():
        xb = xb_ref[...]
        h1 = _dot(xb, w1_ref[...])
        h3 = _dot(xb, w3_ref[...])
        o_ref[...] += _dot((jax.nn.silu(h1) * h3).astype(BF16), w2_ref[...])


def _experts(x_sorted, block_e, n_used, w1, w3, w2):
    p, d = x_sorted.shape
    ff = w1.shape[2]
    tb = min(MOE_TILE, p)
    nc = ff // FF_CHUNK
    grid_spec = pltpu.PrefetchScalarGridSpec(
        num_scalar_prefetch=2,
        grid=(p // tb, nc),
        in_specs=[pl.BlockSpec((tb, d), lambda b, c, be, nu: (b, 0)),
                  pl.BlockSpec((None, d, FF_CHUNK),
                               lambda b, c, be, nu: (be[b], 0, jnp.where(b < nu[0], c, nc - 1))),
                  pl.BlockSpec((None, d, FF_CHUNK),
                               lambda b, c, be, nu: (be[b], 0, jnp.where(b < nu[0], c, nc - 1))),
                  pl.BlockSpec((None, FF_CHUNK, d),
                               lambda b, c, be, nu: (be[b], jnp.where(b < nu[0], c, nc - 1), 0))],
        out_specs=pl.BlockSpec((tb, d), lambda b, c, be, nu: (b, 0)),
        scratch_shapes=[pltpu.VMEM((tb, d), BF16)],
    )
    return pl.pallas_call(
        _expert_kernel,
        grid_spec=grid_spec,
        out_shape=jax.ShapeDtypeStruct((p, d), F32),
        compiler_params=_params(("arbitrary", "arbitrary")),
        name="experts",
    )(block_e, n_used, x_sorted, w1, w3, w2)


def _combine_kernel(x_ref, y0_ref, y1_ref, g_ref, o_ref):
    g = g_ref[...]
    o_ref[...] = x_ref[...] + g[:, 0:1] * y0_ref[...] + g[:, 1:2] * y1_ref[...]


def _combine(x, y0, y1, gate):
    n, d = x.shape
    tm = min(ROW_TILE, n)
    row = pl.BlockSpec((tm, d), lambda i: (i, 0))
    return pl.pallas_call(
        _combine_kernel,
        grid=(n // tm,),
        in_specs=[row, row, row, pl.BlockSpec((tm, gate.shape[1]), lambda i: (i, 0))],
        out_specs=row,
        out_shape=jax.ShapeDtypeStruct((n, d), F32),
        compiler_params=_params(("parallel",)),
        name="combine",
    )(x, y0, y1, gate)


def _moe(x, g, router, w1, w3, w2):
    n, d = x.shape
    a = n * TOP_K
    tb = min(MOE_TILE, a)
    router_padded = jnp.pad(router, ((0, 0), (0, 128 - N_EXPERTS)))
    xn, logits = _router(x, g, router_padded)
    top_logit, top_e = lax.top_k(logits[:, :N_EXPERTS], TOP_K)
    gate = jax.nn.softmax(top_logit, axis=-1)
    e_flat = top_e.reshape(a)
    onehot = (e_flat[:, None] == jnp.arange(N_EXPERTS)[None, :]).astype(jnp.int32)
    csum = jnp.cumsum(onehot, axis=0)
    rank = jnp.sum(onehot * csum, axis=1) - 1
    counts = csum[-1]
    padded = (counts + tb - 1) // tb * tb
    pend = jnp.cumsum(padded)
    pstart = pend - padded
    dest = (pstart[e_flat] + rank).astype(jnp.int32)
    n_blocks = a // tb + N_EXPERTS
    p = n_blocks * tb
    tok = jnp.arange(a, dtype=jnp.int32) // TOP_K
    slot_tok = jnp.zeros((p,), jnp.int32).at[dest].set(tok)
    block_e = jnp.minimum(jnp.searchsorted(pend, jnp.arange(n_blocks) * tb, side='right'),
                          N_EXPERTS - 1).astype(jnp.int32)
    n_used = (pend[-1] // tb).astype(jnp.int32).reshape(1)
    x_sorted = _gather_rows(xn, slot_tok)
    y = _experts(x_sorted, block_e, n_used, w1, w3, w2)
    d2 = dest.reshape(n, TOP_K)
    y0 = _gather_rows(y, d2[:, 0])
    y1 = _gather_rows(y, d2[:, 1])
    gate_padded = jnp.pad(gate, ((0, 0), (0, 128 - TOP_K)))
    return _combine(x, y0, y1, gate_padded)


def _tile_gain(gain, width, scale=1.0):
    return jnp.tile(gain, width // gain.shape[0]) * scale


def kernel(x, mem, rel_bias, norm_mix, w_in, b_gate, a_q_gain, a_k_gain, pool_w, pool_scale,
           c_q_gain, c_k_gain, c_sinks, w_branch, w_out, norm_cross, norm_mem, w_xq, w_xk, w_xv,
           x_q_gain, x_k_gain, w_xo, norm_ffn, ffn_w1, ffn_w3, ffn_w2, router, moe_w1, moe_w3, moe_w2):
    batch, seq, d = x.shape
    mem_len = mem.shape[1]
    depth = norm_mix.shape[0]
    xs = x.reshape(batch * seq, d)
    mems = mem.reshape(batch * mem_len, d)

    tab_a = rel_bias[:, :A_HEADS].T
    tab_c = rel_bias[:, A_HEADS:].T
    nb = seq // A_BLOCK
    n_near = min(nb, (_saturation_distance() + 2 * A_BLOCK - 2) // A_BLOCK)
    bias_a = _moba_bias_tiles(tab_a, n_near)
    far_a = tab_a[:, REL_BUCKETS - 1]
    bias_c = _swa_bias_tiles(tab_c)
    seg = np.arange(256) // HEAD_DIM
    bd = jnp.asarray(seg[:, None] == seg[None, :], BF16)

    row = lambda v: v.reshape(1, -1)
    for l in range(depth):
        scale = HEAD_DIM ** -0.5
        gains = jnp.stack([_tile_gain(a_q_gain[l], WIDTH, scale), _tile_gain(a_k_gain[l], WIDTH),
                           _tile_gain(c_q_gain[l], WIDTH, scale), _tile_gain(c_k_gain[l], WIDTH)])
        qa, ka, va, ub, qc, kc, vc, gates = _in_proj(
            xs, row(norm_mix[l]), w_in[l].astype(BF16), row(b_gate[l]), gains, bd)
        oa = _moba(qa, ka, va, bias_a, far_a, batch, seq)
        oc = _swa(qc, kc, vc, bias_c, c_sinks[l], batch, seq)
        k_mem, v_mem = _mem_kv(mems, row(norm_mem[l]), w_xk[l].astype(BF16), w_xv[l].astype(BF16),
                               row(_tile_gain(x_k_gain[l], WIDTH)))
        xs = _merge_cross(xs, oa, ub, oc, gates, pool_w[l].astype(BF16), row(pool_scale[l]),
                          w_branch[l].astype(BF16), w_out[l].astype(BF16), row(norm_cross[l]),
                          w_xq[l].astype(BF16), row(_tile_gain(x_q_gain[l], WIDTH, X_HEAD_DIM ** -0.5)),
                          k_mem, v_mem, w_xo[l].astype(BF16), seq, mem_len)
        i = l // 2
        if l % 2 == 0:
            xs = _ffn(xs, row(norm_ffn[l]), ffn_w1[i].astype(BF16), ffn_w3[i].astype(BF16),
                      ffn_w2[i].astype(BF16))
        else:
            xs = _moe(xs, row(norm_ffn[l]), router[i], moe_w1[i].astype(BF16), moe_w3[i].astype(BF16),
                      moe_w2[i].astype(BF16))
    return xs.reshape(batch, seq, d)
```

```python
import functools
import math

import jax
import jax.numpy as jnp
import numpy as np
from jax import lax
from jax.experimental import pallas as pl
from jax.experimental.pallas import tpu as pltpu

F32 = jnp.float32
BF16 = jnp.bfloat16

HEAD_DIM = 64
A_HEADS = 8
A_BLOCK = 256
A_TOPK = 3
B_GROUPS = 4
B_GROUP_DIM = 128
B_WINDOWS = (2, 4, 8, 16)
C_HEADS = 8
C_KV_HEADS = 2
C_WINDOW = 128
REL_BUCKETS = 32
REL_MAX_DIST = 1024
X_HEADS = 4
X_HEAD_DIM = 128
N_EXPERTS = 8
TOP_K = 2
EPS = 1e-6
NEG_INF = -1e30
LOG2E = math.log2(math.e)

WIDTH = 512
POOL_HALO = 16
ROW_TILE = 512
MOE_TILE = 1024
FF_CHUNK = 512
GATHER_ROWS = 256
KEY_CHUNK = 128
SWA_BLOCKS = 4
VMEM_LIMIT = 56 * 1024 * 1024
EXP2_SAFE = 100.0


def _dot(a, b):
    return jnp.dot(a, b, preferred_element_type=F32)


def _dot_nt(a, b):
    return lax.dot_general(a, b, (((1,), (1,)), ((), ())), preferred_element_type=F32)


def _rms(x, g):
    ms = jnp.mean(x * x, axis=-1, keepdims=True)
    return x * lax.rsqrt(ms + EPS) * g


def _const_spec(shape):
    zeros = (0,) * len(shape)
    return pl.BlockSpec(shape, lambda *_: zeros, pipeline_mode=pl.Buffered(1))


def _smem_spec():
    return pl.BlockSpec(memory_space=pltpu.SMEM)


def _params(sem):
    return pltpu.CompilerParams(dimension_semantics=sem, vmem_limit_bytes=VMEM_LIMIT)


def _rel_bucket(dist):
    n = jnp.maximum(dist, 0)
    max_exact = REL_BUCKETS // 2
    nf = jnp.maximum(n, 1).astype(jnp.float32)
    large = max_exact + (jnp.log(nf / max_exact) / math.log(REL_MAX_DIST / max_exact)
                         * (REL_BUCKETS - max_exact)).astype(jnp.int32)
    large = jnp.minimum(large, REL_BUCKETS - 1)
    return jnp.where(n < max_exact, n, large)


def _saturation_distance():
    ratio = REL_MAX_DIST / (REL_BUCKETS // 2)
    return int(math.ceil((REL_BUCKETS // 2) * ratio ** ((REL_BUCKETS - 1 - REL_BUCKETS // 2 + 0.5)
                                                       / (REL_BUCKETS - REL_BUCKETS // 2))))


def _bias_by_distance(tab, n_dist):
    onehot = (_rel_bucket(jnp.arange(n_dist))[:, None] == jnp.arange(REL_BUCKETS)[None, :]).astype(F32)
    return jnp.einsum('hb,db->hd', tab, onehot, precision=lax.Precision.HIGHEST)


def _skew(g, rows):
    n = g.shape[-1]
    lead = g.shape[:-1]
    tiled = jnp.broadcast_to(g[..., None, :], lead + (rows, n)).reshape(lead + (rows * n,))
    return tiled[..., :rows * (n - 1)].reshape(lead + (rows, n - 1))


def _moba_bias_tiles(bvec, n_near):
    L = A_BLOCK
    rows = []
    for delta in range(n_near):
        lo = delta * L - (L - 1)
        seg = bvec[:, max(lo, 0):delta * L + L]
        if lo < 0:
            seg = jnp.concatenate([jnp.zeros((bvec.shape[0], -lo), F32), seg], axis=1)
        rows.append(jnp.pad(seg, ((0, 0), (0, 1))))
    g = jnp.stack(rows, axis=1)
    return _skew(g, L)[..., L - 1:]


def _swa_bias_tiles(bvec):
    W = C_WINDOW
    u = np.arange(3 * W)
    g = bvec[:, np.clip(2 * W - 1 - u, 0, W - 1)]
    return _skew(g, W)[..., W - 1:3 * W - 1]


def _bounded_flag(q_gain, k_gain, head_dim, extra):
    bound = head_dim ** 0.5 * jnp.max(jnp.abs(q_gain)) * jnp.max(jnp.abs(k_gain)) * 1.02 + extra
    return (bound * LOG2E < EXP2_SAFE).astype(jnp.int32).reshape(1)


def _in_proj_kernel(x_ref, g_ref, w_ref, bg_ref, gn_ref, bd_ref,
                    qa_ref, ka_ref, va_ref, ub_ref, qc_ref, kc_ref, vc_ref, gt_ref):
    xb = _rms(x_ref[...], g_ref[...]).astype(BF16)
    bd = bd_ref[...]

    def proj(c0, width):
        return _dot(xb, w_ref[:, c0:c0 + width])

    def head_norm(t, gain):
        outs = []
        for c in range(0, t.shape[1], 256):
            wd = min(256, t.shape[1] - c)
            tc = t[:, c:c + wd]
            ss = _dot((tc * tc).astype(BF16), bd[:wd, :wd])
            outs.append(tc * lax.rsqrt(ss * (1.0 / HEAD_DIM) + EPS))
        y = outs[0] if len(outs) == 1 else jnp.concatenate(outs, axis=-1)
        return y * gain

    def tile_kv_heads(t):
        lane = lax.broadcasted_iota(jnp.int32, t.shape, 1)
        r = pltpu.roll(t, HEAD_DIM, axis=1)
        h0 = jnp.where(lane < HEAD_DIM, t, r)
        h1 = jnp.where(lane < HEAD_DIM, r, t)
        return jnp.concatenate([h0, h0, h1, h1], axis=-1)

    qa_ref[...] = head_norm(proj(0, WIDTH), gn_ref[0:1, :]).astype(BF16)
    ka_ref[...] = head_norm(proj(WIDTH, WIDTH), gn_ref[1:2, :]).astype(BF16)
    va_ref[...] = proj(2 * WIDTH, WIDTH).astype(BF16)
    ub_ref[...] = proj(3 * WIDTH, WIDTH)
    qc_ref[...] = head_norm(proj(4 * WIDTH, WIDTH), gn_ref[2:3, :]).astype(BF16)
    kv = C_KV_HEADS * HEAD_DIM
    c0 = 5 * WIDTH
    kc = head_norm(proj(c0, kv), gn_ref[3:4, :kv])
    kc_ref[...] = tile_kv_heads(kc).astype(BF16)
    vc_ref[...] = tile_kv_heads(proj(c0 + kv, kv)).astype(BF16)
    c0 += 2 * kv
    for c in range(0, gt_ref.shape[1], WIDTH):
        gl = proj(c0 + c, WIDTH) + bg_ref[:, c:c + WIDTH]
        gt_ref[:, c:c + WIDTH] = jax.nn.sigmoid(gl).astype(BF16)


def _in_proj(x, g, w_in, b_gate, gains, bd):
    n, d = x.shape
    tm = min(ROW_TILE, n)
    n_gate = b_gate.shape[1]
    row = lambda width: pl.BlockSpec((tm, width), lambda i: (i, 0))
    out_shape = [jax.ShapeDtypeStruct((n, WIDTH), BF16)] * 3 + [jax.ShapeDtypeStruct((n, WIDTH), F32)] \
        + [jax.ShapeDtypeStruct((n, WIDTH), BF16)] * 3 + [jax.ShapeDtypeStruct((n, n_gate), BF16)]
    return pl.pallas_call(
        _in_proj_kernel,
        grid=(n // tm,),
        in_specs=[row(d), _const_spec(g.shape), _const_spec(w_in.shape), _const_spec(b_gate.shape),
                  _const_spec(gains.shape), _const_spec(bd.shape)],
        out_specs=[row(WIDTH)] * 7 + [row(n_gate)],
        out_shape=out_shape,
        compiler_params=_params(("parallel",)),
        name="in_proj",
    )(x, g, w_in, b_gate, gains, bd)


def _moba_kernel(flag_ref, far_ref, q_ref, k_ref, v_ref, bias_ref, o_ref,
                 kmean_ref, vt_ref, sel_ref, m_ref, l_ref, l8_ref, acc_ref, *, nb, n_near):
    i = pl.program_id(1)
    L = A_BLOCK
    KC = KEY_CHUNK
    PW = 2 * HEAD_DIM
    nbp = kmean_ref.shape[0]

    @pl.when(i == 0)
    def _():
        kmean_ref[...] = jnp.zeros(kmean_ref.shape, F32)
        for j in range(nb):
            kj = k_ref[j * L:(j + 1) * L, :].astype(F32)
            kmean_ref[j:j + 1, :] = jnp.mean(kj, axis=0, keepdims=True)
            vt_ref[j] = v_ref[j * L:(j + 1) * L, :].astype(F32).T.astype(BF16)

    lane = lax.broadcasted_iota(jnp.int32, (L, PW), 1)
    blk = lax.broadcasted_iota(jnp.int32, (nbp, L), 0)
    past = blk < i
    qh = []
    for h in range(A_HEADS):
        pair = slice((h // 2) * PW, (h // 2 + 1) * PW)
        q = q_ref[:, pair]
        qm = jnp.where(lane // HEAD_DIM == h % 2, q, jnp.zeros_like(q))
        qh.append(qm)
        km = kmean_ref[:, pair]
        km_hi = km.astype(BF16)
        km_lo = (km - km_hi.astype(F32)).astype(BF16)
        s = jnp.where(past, _dot_nt(km_hi, qm) + _dot_nt(km_lo, qm), -jnp.inf)
        rank = jnp.zeros((nbp, L), jnp.int32)
        for jp in range(nb):
            sj = s[jp:jp + 1, :]
            ahead = (sj > s) | ((sj == s) & (jp < blk))
            rank = rank + ahead.astype(jnp.int32)
        sel_ref[h] = (past & (rank < A_TOPK)).astype(F32)
        m_ref[h] = jnp.full((1, L), NEG_INF, F32)
        l_ref[h] = jnp.zeros((1, L), F32)
        l8_ref[h] = jnp.zeros((8, L), F32)
    acc_ref[...] = jnp.zeros(acc_ref.shape, F32)

    def causal(c, n):
        kk = c + lax.broadcasted_iota(jnp.int32, (n, L), 0)
        qq = lax.broadcasted_iota(jnp.int32, (n, L), 1)
        return kk <= qq

    def bounded_tile(j, delta, far, own):
        for h in range(A_HEADS):
            pair = slice((h // 2) * PW, (h // 2 + 1) * PW)
            rows = slice(h * HEAD_DIM, (h + 1) * HEAD_DIM)
            o = jnp.zeros((HEAD_DIM, L), F32)
            ps = jnp.zeros((8, L), F32)
            for c in range(0, L, KC):
                kc = k_ref[pl.ds(pl.multiple_of(j * L + c, KC), KC), pair]
                s = _dot_nt(kc, qh[h])
                if not far:
                    s = s + bias_ref[h, delta, pl.ds(c, KC), :]
                p = jnp.exp2(s)
                if own:
                    p = jnp.where(causal(c, KC), p, 0.0)
                ps = ps + jnp.sum(p.reshape(KC // 8, 8, L), axis=0)
                o = o + _dot(vt_ref[j, rows, pl.ds(c, KC)], p.astype(BF16))
            if own:
                acc_ref[rows, :] += o
                l8_ref[h] += ps
            else:
                w = sel_ref[h, pl.ds(j, 1), :]
                if far:
                    w = w * far_ref[1, h]
                acc_ref[rows, :] += w * o
                l8_ref[h] += w * ps

    def online_tile(j, delta, far, own):
        for h in range(A_HEADS):
            pair = slice((h // 2) * PW, (h // 2 + 1) * PW)
            rows = slice(h * HEAD_DIM, (h + 1) * HEAD_DIM)
            s = _dot_nt(k_ref[pl.ds(pl.multiple_of(j * L, L), L), pair], qh[h])
            if not far:
                s = s + bias_ref[h, delta]
            if own:
                s = jnp.where(causal(0, L), s, NEG_INF)
            mj = jnp.max(s, axis=0, keepdims=True)
            p = jnp.exp2(s - mj)
            lj = jnp.sum(p, axis=0, keepdims=True)
            o = _dot(vt_ref[j, rows, :], p.astype(BF16))
            if far:
                mj = mj + far_ref[0, h]
            m_old = m_ref[h]
            if own:
                m_new = jnp.maximum(m_old, mj)
                beta = jnp.exp2(mj - m_new)
            else:
                on = sel_ref[h, pl.ds(j, 1), :] > 0.5
                m_new = jnp.where(on, jnp.maximum(m_old, mj), m_old)
                beta = jnp.where(on, jnp.exp2(mj - m_new), 0.0)
            alpha = jnp.exp2(m_old - m_new)
            m_ref[h] = m_new
            l_ref[h] = alpha * l_ref[h] + beta * lj
            acc_ref[rows, :] = alpha * acc_ref[rows, :] + beta * o

    n_far = jnp.maximum(i - (n_near - 1), 0)

    def run(tile):
        def far_body(j, c):
            tile(j, None, True, False)
            return c

        def near_body(j, c):
            tile(j, i - j, False, False)
            return c

        lax.fori_loop(0, n_far, far_body, 0)
        lax.fori_loop(n_far, i, near_body, 0)
        tile(i, 0, False, True)

    @pl.when(flag_ref[0] == 1)
    def _():
        run(bounded_tile)
        for h in range(A_HEADS):
            l_ref[h] = jnp.sum(l8_ref[h], axis=0, keepdims=True)

    @pl.when(flag_ref[0] != 1)
    def _():
        run(online_tile)

    o_t = jnp.concatenate([acc_ref[h * HEAD_DIM:(h + 1) * HEAD_DIM, :] / l_ref[h] for h in range(A_HEADS)],
                          axis=0)
    o_ref[...] = o_t.T.astype(BF16)


def _moba(q, k, v, bias_tiles, far, flag, batch, seq):
    n = q.shape[0]
    L = A_BLOCK
    nb = seq // L
    nbp = max(8, -(-nb // 8) * 8)
    n_near = bias_tiles.shape[1]
    kern = functools.partial(_moba_kernel, nb=nb, n_near=n_near)
    return pl.pallas_call(
        kern,
        grid=(batch, nb),
        in_specs=[_smem_spec(), _smem_spec(),
                  pl.BlockSpec((L, WIDTH), lambda b, i: (b * nb + i, 0)),
                  pl.BlockSpec((seq, WIDTH), lambda b, i: (b, 0)),
                  pl.BlockSpec((seq, WIDTH), lambda b, i: (b, 0)),
                  _const_spec(bias_tiles.shape)],
        out_specs=pl.BlockSpec((L, WIDTH), lambda b, i: (b * nb + i, 0)),
        out_shape=jax.ShapeDtypeStruct((n, WIDTH), BF16),
        scratch_shapes=[pltpu.VMEM((nbp, WIDTH), F32),
                        pltpu.VMEM((nb, WIDTH, L), BF16),
                        pltpu.VMEM((A_HEADS, nbp, L), F32),
                        pltpu.VMEM((A_HEADS, 1, L), F32),
                        pltpu.VMEM((A_HEADS, 1, L), F32),
                        pltpu.VMEM((A_HEADS, 8, L), F32),
                        pltpu.VMEM((WIDTH, L), F32)],
        compiler_params=_params(("arbitrary", "arbitrary")),
        name="moba",
    )(flag, far, q, k, v, bias_tiles)


def _swa_kernel(flag_ref, sink_ref, q_ref, kp_ref, kc_ref, vp_ref, vc_ref, bias_ref, o_ref):
    g = pl.program_id(0)
    chunk = pl.program_id(2)
    W = C_WINDOW
    G = C_HEADS // C_KV_HEADS
    nsub = q_ref.shape[0] // W
    gw = q_ref.shape[1]
    kall = jnp.concatenate([kp_ref[...], kc_ref[...]], axis=0)
    vall = jnp.concatenate([vp_ref[...], vc_ref[...]], axis=0)
    lane = lax.broadcasted_iota(jnp.int32, (W, gw), 1)
    qi = lax.broadcasted_iota(jnp.int32, (W, 2 * W), 0)
    kj = lax.broadcasted_iota(jnp.int32, (W, 2 * W), 1)
    dist = W + qi - kj
    band = (dist >= 0) & (dist < W)

    def heads(bounded):
        for r in range(nsub):
            q = q_ref[r * W:(r + 1) * W, :]
            kcat = kall[r * W:(r + 2) * W]
            vcat = vall[r * W:(r + 2) * W]
            mask = band & ((chunk > 0) | (kj >= W)) if r == 0 else band
            out = jnp.zeros(q.shape, F32)
            for hh in range(G):
                mine = lane // HEAD_DIM == hh
                qm = jnp.where(mine, q, jnp.zeros_like(q))
                s = _dot_nt(qm, kcat) + bias_ref[hh]
                sink = sink_ref[g * G + hh]
                if bounded:
                    pb = jnp.where(mask, jnp.exp2(s), 0.0).astype(BF16)
                    den = _dot(pb, jnp.ones((2 * W, 128), BF16)) + jnp.exp2(jnp.full((1, 128), sink, F32))
                    inv = 1.0 / den
                    o = _dot(pb, vcat) * jnp.concatenate([inv] * (gw // 128), axis=-1)
                else:
                    s = jnp.where(mask, s, NEG_INF)
                    m = jnp.maximum(jnp.max(s, axis=-1, keepdims=True), sink)
                    p = jnp.exp2(s - m)
                    den = jnp.sum(p, axis=-1, keepdims=True) + jnp.exp2(sink - m)
                    o = _dot(p.astype(BF16), vcat) / den
                out = jnp.where(mine, o, out)
            o_ref[r * W:(r + 1) * W, :] = out.astype(BF16)

    @pl.when(flag_ref[0] == 1)
    def _():
        heads(True)

    @pl.when(flag_ref[0] != 1)
    def _():
        heads(False)


def _swa(q, k_t, v_t, bias_tiles, sinks, flag, batch, seq):
    n = q.shape[0]
    W = C_WINDOW
    G = C_HEADS // C_KV_HEADS
    gw = G * HEAD_DIM
    nsub = min(SWA_BLOCKS, seq // W)
    nchunk = seq // (W * nsub)
    cur = lambda g, b, j: (b * nchunk + j, g)
    prev = lambda g, b, j: (jnp.maximum((b * nchunk + j) * nsub - 1, 0), g)
    return pl.pallas_call(
        _swa_kernel,
        grid=(C_KV_HEADS, batch, nchunk),
        in_specs=[_smem_spec(), _smem_spec(),
                  pl.BlockSpec((nsub * W, gw), cur),
                  pl.BlockSpec((W, gw), prev), pl.BlockSpec((nsub * W, gw), cur),
                  pl.BlockSpec((W, gw), prev), pl.BlockSpec((nsub * W, gw), cur),
                  pl.BlockSpec((G, W, 2 * W), lambda g, b, j: (g, 0, 0))],
        out_specs=pl.BlockSpec((nsub * W, gw), cur),
        out_shape=jax.ShapeDtypeStruct((n, WIDTH), BF16),
        compiler_params=_params(("arbitrary", "arbitrary", "arbitrary")),
        name="swa",
    )(flag, sinks, q, k_t, k_t, v_t, v_t, bias_tiles)


def _seg_norm(t, gain, seg):
    outs = []
    for c in range(0, t.shape[1], seg):
        tc = t[:, c:c + seg]
        outs.append(tc * lax.rsqrt(jnp.mean(tc * tc, axis=-1, keepdims=True) + EPS))
    return jnp.concatenate(outs, axis=-1) * gain


def _mem_kv_kernel(mem_ref, g_ref, wk_ref, wv_ref, kg_ref, k_ref, v_ref):
    mb = _rms(mem_ref[...], g_ref[...]).astype(BF16)
    k_ref[...] = _seg_norm(_dot(mb, wk_ref[...]), kg_ref[...], X_HEAD_DIM).astype(BF16)
    v_ref[...] = _dot(mb, wv_ref[...]).astype(BF16)


def _mem_kv(mem, g, w_xk, w_xv, k_gain):
    n, d = mem.shape
    tm = min(ROW_TILE, n)
    row = lambda width: pl.BlockSpec((tm, width), lambda i: (i, 0))
    return pl.pallas_call(
        _mem_kv_kernel,
        grid=(n // tm,),
        in_specs=[row(d), _const_spec(g.shape), _const_spec(w_xk.shape), _const_spec(w_xv.shape),
                  _const_spec(k_gain.shape)],
        out_specs=[row(WIDTH), row(WIDTH)],
        out_shape=[jax.ShapeDtypeStruct((n, WIDTH), BF16)] * 2,
        compiler_params=_params(("parallel",)),
        name="mem_kv",
    )(mem, g, w_xk, w_xv, k_gain)


def _merge_cross_kernel(x_ref, oa_ref, ub_ref, halo_ref, oc_ref, gt_ref, pw_ref, ps_ref, wb_ref, wo_ref,
                        gx_ref, wq_ref, qg_ref, km_ref, vm_ref, wxo_ref, o_ref, *, seq):
    tm = x_ref.shape[0]
    d = x_ref.shape[1]
    t0 = (pl.program_id(0) * tm) % seq
    H = POOL_HALO

    halo = jnp.where(t0 > 0, halo_ref[...], 0.0)
    pos = t0 + lax.broadcasted_iota(jnp.int32, (tm, B_GROUP_DIM), 0)
    mixed = []
    for gi, win in enumerate(B_WINDOWS):
        cols = slice(gi * B_GROUP_DIM, (gi + 1) * B_GROUP_DIM)
        cur = ub_ref[:, cols]
        acc = jnp.concatenate([halo[:, cols], cur], axis=0)
        step = 1
        while step < win:
            acc = acc + pltpu.roll(acc, step, axis=0)
            step *= 2
        cnt = jnp.minimum(pos + 1, win).astype(F32)
        pooled = acc[H:, :] / cnt - cur
        mixed.append(_dot(pooled.astype(BF16), pw_ref[gi]))
    ob = jnp.concatenate(mixed, axis=-1) * ps_ref[...]

    merged = gt_ref[:, 0:d].astype(F32) * _dot(oa_ref[...], wb_ref[0])
    merged = merged + gt_ref[:, d:2 * d].astype(F32) * _dot(ob.astype(BF16), wb_ref[1])
    merged = merged + gt_ref[:, 2 * d:3 * d].astype(F32) * _dot(oc_ref[...], wb_ref[2])
    x1 = x_ref[...] + _dot(merged.astype(BF16), wo_ref[...])

    xb = _rms(x1, gx_ref[...]).astype(BF16)
    qn = _seg_norm(_dot(xb, wq_ref[...]), qg_ref[...], X_HEAD_DIM).astype(BF16)
    heads = []
    for h in range(X_HEADS):
        cols = slice(h * X_HEAD_DIM, (h + 1) * X_HEAD_DIM)
        s = _dot_nt(qn[:, cols], km_ref[:, cols])
        p = jnp.exp(s - jnp.max(s, axis=-1, keepdims=True))
        inv = 1.0 / jnp.sum(p, axis=-1, keepdims=True)
        heads.append(_dot(p.astype(BF16), vm_ref[:, cols]) * inv)
    o = jnp.concatenate(heads, axis=-1).astype(BF16)
    o_ref[...] = x1 + _dot(o, wxo_ref[...])


def _merge_cross(x, oa, ub, oc, gates, pool_w, pool_scale, w_branch, w_out,
                 g_cross, w_xq, q_gain, k_mem, v_mem, w_xo, seq, mem_len):
    n, d = x.shape
    tm = min(ROW_TILE, seq)
    row = lambda width: pl.BlockSpec((tm, width), lambda i: (i, 0))
    halo = pl.BlockSpec((POOL_HALO, WIDTH), lambda i: (jnp.maximum(i * (tm // POOL_HALO) - 1, 0), 0))
    mem = pl.BlockSpec((mem_len, WIDTH), lambda i: ((i * tm) // seq, 0))
    kern = functools.partial(_merge_cross_kernel, seq=seq)
    return pl.pallas_call(
        kern,
        grid=(n // tm,),
        in_specs=[row(d), row(WIDTH), row(WIDTH), halo, row(WIDTH), row(gates.shape[1]),
                  _const_spec(pool_w.shape), _const_spec(pool_scale.shape), _const_spec(w_branch.shape),
                  _const_spec(w_out.shape), _const_spec(g_cross.shape), _const_spec(w_xq.shape),
                  _const_spec(q_gain.shape), mem, mem, _const_spec(w_xo.shape)],
        out_specs=row(d),
        out_shape=jax.ShapeDtypeStruct((n, d), F32),
        compiler_params=_params(("parallel",)),
        name="merge_cross",
    )(x, oa, ub, ub, oc, gates, pool_w, pool_scale, w_branch, w_out,
      g_cross, w_xq, q_gain, k_mem, v_mem, w_xo)


def _swiglu_chunks(xb, w1_ref, w3_ref, w2_ref, acc):
    for c in range(0, w1_ref.shape[1], FF_CHUNK):
        h1 = _dot(xb, w1_ref[:, c:c + FF_CHUNK])
        h3 = _dot(xb, w3_ref[:, c:c + FF_CHUNK])
        acc = acc + _dot((jax.nn.silu(h1) * h3).astype(BF16), w2_ref[c:c + FF_CHUNK, :])
    return acc


def _ffn_kernel(x_ref, g_ref, w1_ref, w3_ref, w2_ref, o_ref):
    x = x_ref[...]
    xb = _rms(x, g_ref[...]).astype(BF16)
    o_ref[...] = _swiglu_chunks(xb, w1_ref, w3_ref, w2_ref, x)


def _ffn(x, g, w1, w3, w2):
    n, d = x.shape
    tm = min(ROW_TILE, n)
    row = pl.BlockSpec((tm, d), lambda i: (i, 0))
    return pl.pallas_call(
        _ffn_kernel,
        grid=(n // tm,),
        in_specs=[row, _const_spec(g.shape), _const_spec(w1.shape), _const_spec(w3.shape),
                  _const_spec(w2.shape)],
        out_specs=row,
        out_shape=jax.ShapeDtypeStruct((n, d), F32),
        compiler_params=_params(("parallel",)),
        name="ffn_dense",
    )(x, g, w1, w3, w2)


def _router_kernel(x_ref, g_ref, r_ref, xn_ref, lg_ref):
    xn = _rms(x_ref[...], g_ref[...])
    xn_ref[...] = xn
    r = r_ref[...]
    x_hi = xn.astype(BF16)
    x_lo = (xn - x_hi.astype(F32)).astype(BF16)
    r_hi = r.astype(BF16)
    r_lo = (r - r_hi.astype(F32)).astype(BF16)
    lg_ref[...] = _dot(x_hi, r_hi) + (_dot(x_lo, r_hi) + _dot(x_hi, r_lo))


def _router(x, g, router_padded):
    n, d = x.shape
    tm = min(ROW_TILE, n)
    row = lambda width: pl.BlockSpec((tm, width), lambda i: (i, 0))
    return pl.pallas_call(
        _router_kernel,
        grid=(n // tm,),
        in_specs=[row(d), _const_spec(g.shape), _const_spec(router_padded.shape)],
        out_specs=[row(d), row(router_padded.shape[1])],
        out_shape=[jax.ShapeDtypeStruct((n, d), F32),
                   jax.ShapeDtypeStruct((n, router_padded.shape[1]), F32)],
        compiler_params=_params(("parallel",)),
        name="router",
    )(x, g, router_padded)


def _row_copy(src_ref, dst_ref, sem, src_row, dst_row):
    return pltpu.make_async_copy(src_ref.at[pl.ds(src_row, 1), :], dst_ref.at[pl.ds(dst_row, 1), :], sem)


def _gather_kernel(idx_ref, src_ref, o_ref, sem):
    rows = o_ref.shape[0]

    def start(r, c):
        _row_copy(src_ref, o_ref, sem, idx_ref[0, 0, r], r).start()
        return c

    def wait(r, c):
        _row_copy(src_ref, o_ref, sem, 0, r).wait()
        return c

    lax.fori_loop(0, rows, start, 0)
    lax.fori_loop(0, rows, wait, 0)


def _gather_rows(src, idx):
    p = idx.shape[0]
    d = src.shape[1]
    rows = min(GATHER_ROWS, p)
    idx3 = idx.reshape(p // rows, 1, rows)
    return pl.pallas_call(
        _gather_kernel,
        grid=(p // rows,),
        in_specs=[pl.BlockSpec((1, 1, rows), lambda i: (i, 0, 0), memory_space=pltpu.SMEM),
                  pl.BlockSpec(memory_space=pl.ANY)],
        out_specs=pl.BlockSpec((rows, d), lambda i: (i, 0)),
        out_shape=jax.ShapeDtypeStruct((p, d), src.dtype),
        scratch_shapes=[pltpu.SemaphoreType.DMA],
        compiler_params=_params(("arbitrary",)),
        name="gather_rows",
    )(idx3, src)


def _expert_kernel(be_ref, nu_ref, x_ref, w1_ref, w3_ref, w2_ref, o_ref, xb_ref):
    b = pl.program_id(0)
    c = pl.program_id(1)

    @pl.when(c == 0)
    def _():
        o_ref[...] = jnp.zeros(o_ref.shape, F32)
        xb_ref[...] = x_ref[...].astype(BF16)

    @pl.when(b < nu_ref[0])
    def _():
        xb = xb_ref[...]
        h1 = _dot(xb, w1_ref[...])
        h3 = _dot(xb, w3_ref[...])
        o_ref[...] += _dot((jax.nn.silu(h1) * h3).astype(BF16), w2_ref[...])


def _experts(x_sorted, block_e, n_used, w1, w3, w2):
    p, d = x_sorted.shape
    ff = w1.shape[2]
    tb = min(MOE_TILE, p)
    nc = ff // FF_CHUNK
    grid_spec = pltpu.PrefetchScalarGridSpec(
        num_scalar_prefetch=2,
        grid=(p // tb, nc),
        in_specs=[pl.BlockSpec((tb, d), lambda b, c, be, nu: (b, 0)),
                  pl.BlockSpec((None, d, FF_CHUNK),
                               lambda b, c, be, nu: (be[b], 0, jnp.where(b < nu[0], c, nc - 1))),
                  pl.BlockSpec((None, d, FF_CHUNK),
                               lambda b, c, be, nu: (be[b], 0, jnp.where(b < nu[0], c, nc - 1))),
                  pl.BlockSpec((None, FF_CHUNK, d),
                               lambda b, c, be, nu: (be[b], jnp.where(b < nu[0], c, nc - 1), 0))],
        out_specs=pl.BlockSpec((tb, d), lambda b, c, be, nu: (b, 0)),
        scratch_shapes=[pltpu.VMEM((tb, d), BF16)],
    )
    return pl.pallas_call(
        _expert_kernel,
        grid_spec=grid_spec,
        out_shape=jax.ShapeDtypeStruct((p, d), F32),
        compiler_params=_params(("arbitrary", "arbitrary")),
        name="experts",
    )(block_e, n_used, x_sorted, w1, w3, w2)


def _combine_kernel(x_ref, y0_ref, y1_ref, g_ref, o_ref):
    g = g_ref[...]
    o_ref[...] = x_ref[...] + g[:, 0:1] * y0_ref[...] + g[:, 1:2] * y1_ref[...]


def _combine(x, y0, y1, gate):
    n, d = x.shape
    tm = min(ROW_TILE, n)
    row = pl.BlockSpec((tm, d), lambda i: (i, 0))
    return pl.pallas_call(
        _combine_kernel,
        grid=(n // tm,),
        in_specs=[row, row, row, pl.BlockSpec((tm, gate.shape[1]), lambda i: (i, 0))],
        out_specs=row,
        out_shape=jax.ShapeDtypeStruct((n, d), F32),
        compiler_params=_params(("parallel",)),
        name="combine",
    )(x, y0, y1, gate)


def _moe(x, g, router, w1, w3, w2):
    n, d = x.shape
    a = n * TOP_K
    tb = min(MOE_TILE, a)
    router_padded = jnp.pad(router, ((0, 0), (0, 128 - N_EXPERTS)))
    xn, logits = _router(x, g, router_padded)
    top_logit, top_e = lax.top_k(logits[:, :N_EXPERTS], TOP_K)
    gate = jax.nn.softmax(top_logit, axis=-1)
    e_flat = top_e.reshape(a)
    onehot = (e_flat[:, None] == jnp.arange(N_EXPERTS)[None, :]).astype(jnp.int32)
    csum = jnp.cumsum(onehot, axis=0)
    rank = jnp.sum(onehot * csum, axis=1) - 1
    counts = csum[-1]
    padded = (counts + tb - 1) // tb * tb
    pend = jnp.cumsum(padded)
    pstart = pend - padded
    dest = (jnp.sum(onehot * pstart[None, :], axis=1) + rank).astype(jnp.int32)
    n_blocks = a // tb + N_EXPERTS
    p = n_blocks * tb
    tok = jnp.arange(a, dtype=jnp.int32) // TOP_K
    slot_tok = jnp.zeros((p,), jnp.int32).at[dest].set(tok)
    block_e = jnp.minimum(jnp.searchsorted(pend, jnp.arange(n_blocks) * tb, side='right'),
                          N_EXPERTS - 1).astype(jnp.int32)
    n_used = (pend[-1] // tb).astype(jnp.int32).reshape(1)
    x_sorted = _gather_rows(xn, slot_tok)
    y = _experts(x_sorted, block_e, n_used, w1, w3, w2)
    d2 = dest.reshape(n, TOP_K)
    y0 = _gather_rows(y, d2[:, 0])
    y1 = _gather_rows(y, d2[:, 1])
    gate_padded = jnp.pad(gate, ((0, 0), (0, 128 - TOP_K)))
    return _combine(x, y0, y1, gate_padded)


def _tile_gain(gain, width, scale=1.0):
    return jnp.tile(gain, width // gain.shape[0]) * scale


def kernel(x, mem, rel_bias, norm_mix, w_in, b_gate, a_q_gain, a_k_gain, pool_w, pool_scale,
           c_q_gain, c_k_gain, c_sinks, w_branch, w_out, norm_cross, norm_mem, w_xq, w_xk, w_xv,
           x_q_gain, x_k_gain, w_xo, norm_ffn, ffn_w1, ffn_w3, ffn_w2, router, moe_w1, moe_w3, moe_w2):
    batch, seq, d = x.shape
    mem_len = mem.shape[1]
    depth = norm_mix.shape[0]
    xs = x.reshape(batch * seq, d)
    mems = mem.reshape(batch * mem_len, d)

    tab_a = rel_bias[:, :A_HEADS].T
    tab_c = rel_bias[:, A_HEADS:].T
    nb = seq // A_BLOCK
    n_near = min(nb, (_saturation_distance() + 2 * A_BLOCK - 2) // A_BLOCK)
    bias_a = _moba_bias_tiles(_bias_by_distance(tab_a, n_near * A_BLOCK), n_near) * LOG2E
    far_a = tab_a[:, REL_BUCKETS - 1]
    far_a = jnp.stack([far_a * LOG2E, jnp.exp(far_a)])
    bias_c = _swa_bias_tiles(_bias_by_distance(tab_c, C_WINDOW)) * LOG2E
    seg = np.arange(256) // HEAD_DIM
    bd = jnp.asarray(seg[:, None] == seg[None, :], BF16)

    row = lambda v: v.reshape(1, -1)
    for l in range(depth):
        scale = HEAD_DIM ** -0.5 * LOG2E
        gains = jnp.stack([_tile_gain(a_q_gain[l], WIDTH, scale), _tile_gain(a_k_gain[l], WIDTH),
                           _tile_gain(c_q_gain[l], WIDTH, scale), _tile_gain(c_k_gain[l], WIDTH)])
        flag_a = _bounded_flag(a_q_gain[l], a_k_gain[l], HEAD_DIM, jnp.max(jnp.abs(tab_a)))
        flag_c = _bounded_flag(c_q_gain[l], c_k_gain[l], HEAD_DIM,
                               jnp.maximum(jnp.max(jnp.abs(tab_c)), jnp.max(jnp.abs(c_sinks[l]))))
        qa, ka, va, ub, qc, kc, vc, gates = _in_proj(
            xs, row(norm_mix[l]), w_in[l].astype(BF16), row(b_gate[l]), gains, bd)
        oa = _moba(qa, ka, va, bias_a, far_a, flag_a, batch, seq)
        oc = _swa(qc, kc, vc, bias_c, c_sinks[l] * LOG2E, flag_c, batch, seq)
        k_mem, v_mem = _mem_kv(mems, row(norm_mem[l]), w_xk[l].astype(BF16), w_xv[l].astype(BF16),
                               row(_tile_gain(x_k_gain[l], WIDTH)))
        xs = _merge_cross(xs, oa, ub, oc, gates, pool_w[l].astype(BF16), row(pool_scale[l]),
                          w_branch[l].astype(BF16), w_out[l].astype(BF16), row(norm_cross[l]),
                          w_xq[l].astype(BF16), row(_tile_gain(x_q_gain[l], WIDTH, X_HEAD_DIM ** -0.5)),
                          k_mem, v_mem, w_xo[l].astype(BF16), seq, mem_len)
        i = l // 2
        if l % 2 == 0:
            xs = _ffn(xs, row(norm_ffn[l]), ffn_w1[i].astype(BF16), ffn_w3[i].astype(BF16),
                      ffn_w2[i].astype(BF16))
        else:
            xs = _moe(xs, row(norm_ffn[l]), router[i], moe_w1[i].astype(BF16), moe_w3[i].astype(BF16),
                      moe_w2[i].astype(BF16))
    return xs.reshape(batch, seq, d)
```

```python
import functools
import math

import jax
import jax.numpy as jnp
import numpy as np
from jax import lax
from jax.experimental import pallas as pl
from jax.experimental.pallas import tpu as pltpu

F32 = jnp.float32
BF16 = jnp.bfloat16

HEAD_DIM = 64
A_HEADS = 8
A_BLOCK = 256
A_TOPK = 3
B_GROUPS = 4
B_GROUP_DIM = 128
B_WINDOWS = (2, 4, 8, 16)
C_HEADS = 8
C_KV_HEADS = 2
C_WINDOW = 128
REL_BUCKETS = 32
REL_MAX_DIST = 1024
X_HEADS = 4
X_HEAD_DIM = 128
N_EXPERTS = 8
TOP_K = 2
EPS = 1e-6
NEG_INF = -1e30
LOG2E = math.log2(math.e)

WIDTH = 512
POOL_HALO = 16
ROW_TILE = 512
MOE_TILE = 1024
FF_CHUNK = 512
GATHER_ROWS = 256
KEY_CHUNK = 128
SWA_BLOCKS = 4
VMEM_LIMIT = 56 * 1024 * 1024
EXP2_SAFE = 100.0


def _dot(a, b):
    return jnp.dot(a, b, preferred_element_type=F32)


def _dot_nt(a, b):
    return lax.dot_general(a, b, (((1,), (1,)), ((), ())), preferred_element_type=F32)


def _rms(x, g):
    ms = jnp.mean(x * x, axis=-1, keepdims=True)
    return x * lax.rsqrt(ms + EPS) * g


def _const_spec(shape):
    zeros = (0,) * len(shape)
    return pl.BlockSpec(shape, lambda *_: zeros, pipeline_mode=pl.Buffered(1))


def _smem_spec():
    return pl.BlockSpec(memory_space=pltpu.SMEM)


def _params(sem):
    return pltpu.CompilerParams(dimension_semantics=sem, vmem_limit_bytes=VMEM_LIMIT)


def _rel_bucket(dist):
    n = jnp.maximum(dist, 0)
    max_exact = REL_BUCKETS // 2
    nf = jnp.maximum(n, 1).astype(jnp.float32)
    large = max_exact + (jnp.log(nf / max_exact) / math.log(REL_MAX_DIST / max_exact)
                         * (REL_BUCKETS - max_exact)).astype(jnp.int32)
    large = jnp.minimum(large, REL_BUCKETS - 1)
    return jnp.where(n < max_exact, n, large)


def _saturation_distance():
    ratio = REL_MAX_DIST / (REL_BUCKETS // 2)
    return int(math.ceil((REL_BUCKETS // 2) * ratio ** ((REL_BUCKETS - 1 - REL_BUCKETS // 2 + 0.5)
                                                       / (REL_BUCKETS - REL_BUCKETS // 2))))


def _bias_by_distance(tab, n_dist):
    onehot = (_rel_bucket(jnp.arange(n_dist))[:, None] == jnp.arange(REL_BUCKETS)[None, :]).astype(F32)
    return jnp.einsum('hb,db->hd', tab, onehot, precision=lax.Precision.HIGHEST)


def _skew(g, rows):
    n = g.shape[-1]
    lead = g.shape[:-1]
    tiled = jnp.broadcast_to(g[..., None, :], lead + (rows, n)).reshape(lead + (rows * n,))
    return tiled[..., :rows * (n - 1)].reshape(lead + (rows, n - 1))


def _moba_bias_tiles(bvec, n_near):
    L = A_BLOCK
    rows = []
    for delta in range(n_near):
        lo = delta * L - (L - 1)
        seg = bvec[:, max(lo, 0):delta * L + L]
        if lo < 0:
            seg = jnp.concatenate([jnp.zeros((bvec.shape[0], -lo), F32), seg], axis=1)
        rows.append(jnp.pad(seg, ((0, 0), (0, 1))))
    g = jnp.stack(rows, axis=1)
    return _skew(g, L)[..., L - 1:]


def _swa_bias_tiles(bvec):
    W = C_WINDOW
    u = np.arange(3 * W)
    g = bvec[:, np.clip(2 * W - 1 - u, 0, W - 1)]
    return _skew(g, W)[..., W - 1:3 * W - 1]


def _bounded_flag(q_gain, k_gain, head_dim, extra):
    bound = head_dim ** 0.5 * jnp.max(jnp.abs(q_gain)) * jnp.max(jnp.abs(k_gain)) * 1.02 + extra
    return (bound * LOG2E < EXP2_SAFE).astype(jnp.int32).reshape(1)


def _in_proj_kernel(x_ref, g_ref, w_ref, bg_ref, gn_ref, bd_ref,
                    qa_ref, ka_ref, va_ref, ub_ref, qc_ref, kc_ref, vc_ref, gt_ref):
    xb = _rms(x_ref[...], g_ref[...]).astype(BF16)
    bd = bd_ref[...]

    def proj(c0, width):
        return _dot(xb, w_ref[:, c0:c0 + width])

    def head_norm(t, gain):
        outs = []
        for c in range(0, t.shape[1], 256):
            wd = min(256, t.shape[1] - c)
            tc = t[:, c:c + wd]
            ss = _dot((tc * tc).astype(BF16), bd[:wd, :wd])
            outs.append(tc * lax.rsqrt(ss * (1.0 / HEAD_DIM) + EPS))
        y = outs[0] if len(outs) == 1 else jnp.concatenate(outs, axis=-1)
        return y * gain

    def tile_kv_heads(t):
        lane = lax.broadcasted_iota(jnp.int32, t.shape, 1)
        r = pltpu.roll(t, HEAD_DIM, axis=1)
        h0 = jnp.where(lane < HEAD_DIM, t, r)
        h1 = jnp.where(lane < HEAD_DIM, r, t)
        return jnp.concatenate([h0, h0, h1, h1], axis=-1)

    qa_ref[...] = head_norm(proj(0, WIDTH), gn_ref[0:1, :]).astype(BF16)
    ka_ref[...] = head_norm(proj(WIDTH, WIDTH), gn_ref[1:2, :]).astype(BF16)
    va_ref[...] = proj(2 * WIDTH, WIDTH).astype(BF16)
    ub_ref[...] = proj(3 * WIDTH, WIDTH)
    qc_ref[...] = head_norm(proj(4 * WIDTH, WIDTH), gn_ref[2:3, :]).astype(BF16)
    kv = C_KV_HEADS * HEAD_DIM
    c0 = 5 * WIDTH
    kc = head_norm(proj(c0, kv), gn_ref[3:4, :kv])
    kc_ref[...] = tile_kv_heads(kc).astype(BF16)
    vc_ref[...] = tile_kv_heads(proj(c0 + kv, kv)).astype(BF16)
    c0 += 2 * kv
    for c in range(0, gt_ref.shape[1], WIDTH):
        gl = proj(c0 + c, WIDTH) + bg_ref[:, c:c + WIDTH]
        gt_ref[:, c:c + WIDTH] = jax.nn.sigmoid(gl).astype(BF16)


def _in_proj(x, g, w_in, b_gate, gains, bd):
    n, d = x.shape
    tm = min(ROW_TILE, n)
    n_gate = b_gate.shape[1]
    row = lambda width: pl.BlockSpec((tm, width), lambda i: (i, 0))
    out_shape = [jax.ShapeDtypeStruct((n, WIDTH), BF16)] * 3 + [jax.ShapeDtypeStruct((n, WIDTH), F32)] \
        + [jax.ShapeDtypeStruct((n, WIDTH), BF16)] * 3 + [jax.ShapeDtypeStruct((n, n_gate), BF16)]
    return pl.pallas_call(
        _in_proj_kernel,
        grid=(n // tm,),
        in_specs=[row(d), _const_spec(g.shape), _const_spec(w_in.shape), _const_spec(b_gate.shape),
                  _const_spec(gains.shape), _const_spec(bd.shape)],
        out_specs=[row(WIDTH)] * 7 + [row(n_gate)],
        out_shape=out_shape,
        compiler_params=_params(("parallel",)),
        name="in_proj",
    )(x, g, w_in, b_gate, gains, bd)


def _moba_kernel(flag_ref, far_ref, q_ref, k_ref, v_ref, bias_ref, o_ref,
                 kmean_ref, vt_ref, sel_ref, m_ref, l_ref, l8_ref, acc_ref, *, nb, n_near):
    i = pl.program_id(1)
    L = A_BLOCK
    KC = KEY_CHUNK
    PW = 2 * HEAD_DIM
    nbp = kmean_ref.shape[0]

    @pl.when(i == 0)
    def _():
        kmean_ref[...] = jnp.zeros(kmean_ref.shape, F32)
        for j in range(nb):
            kj = k_ref[j * L:(j + 1) * L, :].astype(F32)
            kmean_ref[j:j + 1, :] = jnp.mean(kj, axis=0, keepdims=True)
            vt_ref[j] = v_ref[j * L:(j + 1) * L, :].astype(F32).T.astype(BF16)

    lane = lax.broadcasted_iota(jnp.int32, (L, PW), 1)
    blk = lax.broadcasted_iota(jnp.int32, (nbp, L), 0)
    past = blk < i
    qh = []
    for h in range(A_HEADS):
        pair = slice((h // 2) * PW, (h // 2 + 1) * PW)
        q = q_ref[:, pair]
        qm = jnp.where(lane // HEAD_DIM == h % 2, q, jnp.zeros_like(q))
        qh.append(qm)
        km = kmean_ref[:, pair]
        km_hi = km.astype(BF16)
        km_lo = (km - km_hi.astype(F32)).astype(BF16)
        s = jnp.where(past, _dot_nt(km_hi, qm) + _dot_nt(km_lo, qm), -jnp.inf)
        rank = jnp.zeros((nbp, L), jnp.int32)
        for jp in range(nb):
            sj = s[jp:jp + 1, :]
            ahead = (sj > s) | ((sj == s) & (jp < blk))
            rank = rank + ahead.astype(jnp.int32)
        sel_ref[h] = (past & (rank < A_TOPK)).astype(F32)
        m_ref[h] = jnp.full((1, L), NEG_INF, F32)
        l_ref[h] = jnp.zeros((1, L), F32)
        l8_ref[h] = jnp.zeros((8, L), F32)
    acc_ref[...] = jnp.zeros(acc_ref.shape, F32)
    qpair = [jnp.concatenate(qh[2 * hp:2 * hp + 2], axis=0) for hp in range(A_HEADS // 2)]

    def causal(c, n):
        kk = c + lax.broadcasted_iota(jnp.int32, (n, L), 0)
        qq = lax.broadcasted_iota(jnp.int32, (n, L), 1)
        return kk <= qq

    def bounded_tile(j, delta, far, own):
        units = [(hp, c) for hp in range(A_HEADS // 2) for c in range(0, L, KC)]

        def qk(u):
            hp, c = units[u]
            kc = k_ref[pl.ds(pl.multiple_of(j * L + c, KC), KC), hp * PW:(hp + 1) * PW]
            return _dot_nt(kc, qpair[hp])

        s_next = qk(0)
        o = ps = None
        for u, (hp, c) in enumerate(units):
            s = s_next
            if u + 1 < len(units):
                s_next = qk(u + 1)
            pair = slice(hp * PW, (hp + 1) * PW)
            if c == 0:
                o = jnp.zeros((PW, 2 * L), F32)
                ps = jnp.zeros((8, 2 * L), F32)
            if not far:
                s = s + bias_ref[hp, delta, pl.ds(c, KC), :]
            p = jnp.exp2(s)
            if own:
                p = jnp.where(jnp.concatenate([causal(c, KC)] * 2, axis=1), p, 0.0)
            ps = ps + jnp.sum(p.reshape(KC // 8, 8, 2 * L), axis=0)
            o = o + _dot(vt_ref[j, pair, pl.ds(c, KC)], p.astype(BF16))
            if c + KC < L:
                continue
            for hh in range(2):
                h = 2 * hp + hh
                rows = slice(h * HEAD_DIM, (h + 1) * HEAD_DIM)
                oh = o[hh * HEAD_DIM:(hh + 1) * HEAD_DIM, hh * L:(hh + 1) * L]
                psh = ps[:, hh * L:(hh + 1) * L]
                if own:
                    acc_ref[rows, :] += oh
                    l8_ref[h] += psh
                else:
                    w = sel_ref[h, pl.ds(j, 1), :]
                    if far:
                        w = w * far_ref[1, h]
                    acc_ref[rows, :] += w * oh
                    l8_ref[h] += w * psh

    def online_tile(j, delta, far, own):
        for h in range(A_HEADS):
            pair = slice((h // 2) * PW, (h // 2 + 1) * PW)
            rows = slice(h * HEAD_DIM, (h + 1) * HEAD_DIM)
            s = _dot_nt(k_ref[pl.ds(pl.multiple_of(j * L, L), L), pair], qh[h])
            if not far:
                s = s + bias_ref[h // 2, delta, :, (h % 2) * L:(h % 2 + 1) * L]
            if own:
                s = jnp.where(causal(0, L), s, NEG_INF)
            mj = jnp.max(s, axis=0, keepdims=True)
            p = jnp.exp2(s - mj)
            lj = jnp.sum(p, axis=0, keepdims=True)
            o = _dot(vt_ref[j, rows, :], p.astype(BF16))
            if far:
                mj = mj + far_ref[0, h]
            m_old = m_ref[h]
            if own:
                m_new = jnp.maximum(m_old, mj)
                beta = jnp.exp2(mj - m_new)
            else:
                on = sel_ref[h, pl.ds(j, 1), :] > 0.5
                m_new = jnp.where(on, jnp.maximum(m_old, mj), m_old)
                beta = jnp.where(on, jnp.exp2(mj - m_new), 0.0)
            alpha = jnp.exp2(m_old - m_new)
            m_ref[h] = m_new
            l_ref[h] = alpha * l_ref[h] + beta * lj
            acc_ref[rows, :] = alpha * acc_ref[rows, :] + beta * o

    n_far = jnp.maximum(i - (n_near - 1), 0)

    def run(tile):
        def far_body(j, c):
            tile(j, None, True, False)
            return c

        def near_body(j, c):
            tile(j, i - j, False, False)
            return c

        lax.fori_loop(0, n_far, far_body, 0)
        lax.fori_loop(n_far, i, near_body, 0)
        tile(i, 0, False, True)

    @pl.when(flag_ref[0] == 1)
    def _():
        run(bounded_tile)
        for h in range(A_HEADS):
            l_ref[h] = jnp.sum(l8_ref[h], axis=0, keepdims=True)

    @pl.when(flag_ref[0] != 1)
    def _():
        run(online_tile)

    o_t = jnp.concatenate([acc_ref[h * HEAD_DIM:(h + 1) * HEAD_DIM, :] / l_ref[h] for h in range(A_HEADS)],
                          axis=0)
    o_ref[...] = o_t.T.astype(BF16)


def _moba(q, k, v, bias_tiles, far, flag, batch, seq):
    n = q.shape[0]
    L = A_BLOCK
    nb = seq // L
    nbp = max(8, -(-nb // 8) * 8)
    n_near = bias_tiles.shape[1]
    kern = functools.partial(_moba_kernel, nb=nb, n_near=n_near)
    return pl.pallas_call(
        kern,
        grid=(batch, nb),
        in_specs=[_smem_spec(), _smem_spec(),
                  pl.BlockSpec((L, WIDTH), lambda b, i: (b * nb + i, 0)),
                  pl.BlockSpec((seq, WIDTH), lambda b, i: (b, 0)),
                  pl.BlockSpec((seq, WIDTH), lambda b, i: (b, 0)),
                  _const_spec(bias_tiles.shape)],
        out_specs=pl.BlockSpec((L, WIDTH), lambda b, i: (b * nb + i, 0)),
        out_shape=jax.ShapeDtypeStruct((n, WIDTH), BF16),
        scratch_shapes=[pltpu.VMEM((nbp, WIDTH), F32),
                        pltpu.VMEM((nb, WIDTH, L), BF16),
                        pltpu.VMEM((A_HEADS, nbp, L), F32),
                        pltpu.VMEM((A_HEADS, 1, L), F32),
                        pltpu.VMEM((A_HEADS, 1, L), F32),
                        pltpu.VMEM((A_HEADS, 8, L), F32),
                        pltpu.VMEM((WIDTH, L), F32)],
        compiler_params=_params(("arbitrary", "arbitrary")),
        name="moba",
    )(flag, far, q, k, v, bias_tiles)


def _swa_kernel(flag_ref, sink_ref, q_ref, kp_ref, kc_ref, vp_ref, vc_ref, bias_ref, o_ref):
    g = pl.program_id(0)
    chunk = pl.program_id(2)
    W = C_WINDOW
    G = C_HEADS // C_KV_HEADS
    nsub = q_ref.shape[0] // W
    gw = q_ref.shape[1]
    kall = jnp.concatenate([kp_ref[...], kc_ref[...]], axis=0)
    vall = jnp.concatenate([vp_ref[...], vc_ref[...]], axis=0)
    lane = lax.broadcasted_iota(jnp.int32, (W, gw), 1)
    qi = lax.broadcasted_iota(jnp.int32, (W, 2 * W), 0)
    kj = lax.broadcasted_iota(jnp.int32, (W, 2 * W), 1)
    dist = W + qi - kj
    band = (dist >= 0) & (dist < W)

    def heads(bounded):
        for r in range(nsub):
            q = q_ref[r * W:(r + 1) * W, :]
            kcat = kall[r * W:(r + 2) * W]
            vcat = vall[r * W:(r + 2) * W]
            mask = band & ((chunk > 0) | (kj >= W)) if r == 0 else band
            out = jnp.zeros(q.shape, F32)
            for hh in range(G):
                mine = lane // HEAD_DIM == hh
                qm = jnp.where(mine, q, jnp.zeros_like(q))
                s = _dot_nt(qm, kcat) + bias_ref[hh]
                sink = sink_ref[g * G + hh]
                if bounded:
                    pb = jnp.where(mask, jnp.exp2(s), 0.0).astype(BF16)
                    den = _dot(pb, jnp.ones((2 * W, 128), BF16)) + jnp.exp2(jnp.full((1, 128), sink, F32))
                    inv = 1.0 / den
                    o = _dot(pb, vcat) * jnp.concatenate([inv] * (gw // 128), axis=-1)
                else:
                    s = jnp.where(mask, s, NEG_INF)
                    m = jnp.maximum(jnp.max(s, axis=-1, keepdims=True), sink)
                    p = jnp.exp2(s - m)
                    den = jnp.sum(p, axis=-1, keepdims=True) + jnp.exp2(sink - m)
                    o = _dot(p.astype(BF16), vcat) / den
                out = jnp.where(mine, o, out)
            o_ref[r * W:(r + 1) * W, :] = out.astype(BF16)

    @pl.when(flag_ref[0] == 1)
    def _():
        heads(True)

    @pl.when(flag_ref[0] != 1)
    def _():
        heads(False)


def _swa(q, k_t, v_t, bias_tiles, sinks, flag, batch, seq):
    n = q.shape[0]
    W = C_WINDOW
    G = C_HEADS // C_KV_HEADS
    gw = G * HEAD_DIM
    nsub = min(SWA_BLOCKS, seq // W)
    nchunk = seq // (W * nsub)
    cur = lambda g, b, j: (b * nchunk + j, g)
    prev = lambda g, b, j: (jnp.maximum((b * nchunk + j) * nsub - 1, 0), g)
    return pl.pallas_call(
        _swa_kernel,
        grid=(C_KV_HEADS, batch, nchunk),
        in_specs=[_smem_spec(), _smem_spec(),
                  pl.BlockSpec((nsub * W, gw), cur),
                  pl.BlockSpec((W, gw), prev), pl.BlockSpec((nsub * W, gw), cur),
                  pl.BlockSpec((W, gw), prev), pl.BlockSpec((nsub * W, gw), cur),
                  pl.BlockSpec((G, W, 2 * W), lambda g, b, j: (g, 0, 0))],
        out_specs=pl.BlockSpec((nsub * W, gw), cur),
        out_shape=jax.ShapeDtypeStruct((n, WIDTH), BF16),
        compiler_params=_params(("arbitrary", "arbitrary", "arbitrary")),
        name="swa",
    )(flag, sinks, q, k_t, k_t, v_t, v_t, bias_tiles)


def _seg_norm(t, gain, seg):
    outs = []
    for c in range(0, t.shape[1], seg):
        tc = t[:, c:c + seg]
        outs.append(tc * lax.rsqrt(jnp.mean(tc * tc, axis=-1, keepdims=True) + EPS))
    return jnp.concatenate(outs, axis=-1) * gain


def _mem_kv_kernel(mem_ref, g_ref, wk_ref, wv_ref, kg_ref, k_ref, v_ref):
    mb = _rms(mem_ref[...], g_ref[...]).astype(BF16)
    k_ref[...] = _seg_norm(_dot(mb, wk_ref[...]), kg_ref[...], X_HEAD_DIM).astype(BF16)
    v_ref[...] = _dot(mb, wv_ref[...]).astype(BF16)


def _mem_kv(mem, g, w_xk, w_xv, k_gain):
    n, d = mem.shape
    tm = min(ROW_TILE, n)
    row = lambda width: pl.BlockSpec((tm, width), lambda i: (i, 0))
    return pl.pallas_call(
        _mem_kv_kernel,
        grid=(n // tm,),
        in_specs=[row(d), _const_spec(g.shape), _const_spec(w_xk.shape), _const_spec(w_xv.shape),
                  _const_spec(k_gain.shape)],
        out_specs=[row(WIDTH), row(WIDTH)],
        out_shape=[jax.ShapeDtypeStruct((n, WIDTH), BF16)] * 2,
        compiler_params=_params(("parallel",)),
        name="mem_kv",
    )(mem, g, w_xk, w_xv, k_gain)


def _merge_cross_kernel(x_ref, oa_ref, ub_ref, halo_ref, oc_ref, gt_ref, pw_ref, ps_ref, wb_ref, wo_ref,
                        gx_ref, wq_ref, qg_ref, km_ref, vm_ref, wxo_ref, o_ref, *, seq):
    tm = x_ref.shape[0]
    d = x_ref.shape[1]
    t0 = (pl.program_id(0) * tm) % seq
    H = POOL_HALO

    halo = jnp.where(t0 > 0, halo_ref[...], 0.0)
    pos = t0 + lax.broadcasted_iota(jnp.int32, (tm, B_GROUP_DIM), 0)
    mixed = []
    for gi, win in enumerate(B_WINDOWS):
        cols = slice(gi * B_GROUP_DIM, (gi + 1) * B_GROUP_DIM)
        cur = ub_ref[:, cols]
        acc = jnp.concatenate([halo[:, cols], cur], axis=0)
        step = 1
        while step < win:
            acc = acc + pltpu.roll(acc, step, axis=0)
            step *= 2
        cnt = jnp.minimum(pos + 1, win).astype(F32)
        pooled = acc[H:, :] / cnt - cur
        mixed.append(_dot(pooled.astype(BF16), pw_ref[gi]))
    ob = jnp.concatenate(mixed, axis=-1) * ps_ref[...]

    merged = gt_ref[:, 0:d].astype(F32) * _dot(oa_ref[...], wb_ref[0])
    merged = merged + gt_ref[:, d:2 * d].astype(F32) * _dot(ob.astype(BF16), wb_ref[1])
    merged = merged + gt_ref[:, 2 * d:3 * d].astype(F32) * _dot(oc_ref[...], wb_ref[2])
    x1 = x_ref[...] + _dot(merged.astype(BF16), wo_ref[...])

    xb = _rms(x1, gx_ref[...]).astype(BF16)
    qn = _seg_norm(_dot(xb, wq_ref[...]), qg_ref[...], X_HEAD_DIM).astype(BF16)
    heads = []
    for h in range(X_HEADS):
        cols = slice(h * X_HEAD_DIM, (h + 1) * X_HEAD_DIM)
        s = _dot_nt(qn[:, cols], km_ref[:, cols])
        p = jnp.exp(s - jnp.max(s, axis=-1, keepdims=True))
        inv = 1.0 / jnp.sum(p, axis=-1, keepdims=True)
        heads.append(_dot(p.astype(BF16), vm_ref[:, cols]) * inv)
    o = jnp.concatenate(heads, axis=-1).astype(BF16)
    o_ref[...] = x1 + _dot(o, wxo_ref[...])


def _merge_cross(x, oa, ub, oc, gates, pool_w, pool_scale, w_branch, w_out,
                 g_cross, w_xq, q_gain, k_mem, v_mem, w_xo, seq, mem_len):
    n, d = x.shape
    tm = min(ROW_TILE, seq)
    row = lambda width: pl.BlockSpec((tm, width), lambda i: (i, 0))
    halo = pl.BlockSpec((POOL_HALO, WIDTH), lambda i: (jnp.maximum(i * (tm // POOL_HALO) - 1, 0), 0))
    mem = pl.BlockSpec((mem_len, WIDTH), lambda i: ((i * tm) // seq, 0))
    kern = functools.partial(_merge_cross_kernel, seq=seq)
    return pl.pallas_call(
        kern,
        grid=(n // tm,),
        in_specs=[row(d), row(WIDTH), row(WIDTH), halo, row(WIDTH), row(gates.shape[1]),
                  _const_spec(pool_w.shape), _const_spec(pool_scale.shape), _const_spec(w_branch.shape),
                  _const_spec(w_out.shape), _const_spec(g_cross.shape), _const_spec(w_xq.shape),
                  _const_spec(q_gain.shape), mem, mem, _const_spec(w_xo.shape)],
        out_specs=row(d),
        out_shape=jax.ShapeDtypeStruct((n, d), F32),
        compiler_params=_params(("parallel",)),
        name="merge_cross",
    )(x, oa, ub, ub, oc, gates, pool_w, pool_scale, w_branch, w_out,
      g_cross, w_xq, q_gain, k_mem, v_mem, w_xo)


def _swiglu_chunks(xb, w1_ref, w3_ref, w2_ref, acc):
    for c in range(0, w1_ref.shape[1], FF_CHUNK):
        h1 = _dot(xb, w1_ref[:, c:c + FF_CHUNK])
        h3 = _dot(xb, w3_ref[:, c:c + FF_CHUNK])
        acc = acc + _dot((jax.nn.silu(h1) * h3).astype(BF16), w2_ref[c:c + FF_CHUNK, :])
    return acc


def _ffn_kernel(x_ref, g_ref, w1_ref, w3_ref, w2_ref, o_ref):
    x = x_ref[...]
    xb = _rms(x, g_ref[...]).astype(BF16)
    o_ref[...] = _swiglu_chunks(xb, w1_ref, w3_ref, w2_ref, x)


def _ffn(x, g, w1, w3, w2):
    n, d = x.shape
    tm = min(ROW_TILE, n)
    row = pl.BlockSpec((tm, d), lambda i: (i, 0))
    return pl.pallas_call(
        _ffn_kernel,
        grid=(n // tm,),
        in_specs=[row, _const_spec(g.shape), _const_spec(w1.shape), _const_spec(w3.shape),
                  _const_spec(w2.shape)],
        out_specs=row,
        out_shape=jax.ShapeDtypeStruct((n, d), F32),
        compiler_params=_params(("parallel",)),
        name="ffn_dense",
    )(x, g, w1, w3, w2)


def _router_kernel(x_ref, g_ref, r_ref, xn_ref, lg_ref):
    xn = _rms(x_ref[...], g_ref[...])
    xn_ref[...] = xn
    r = r_ref[...]
    x_hi = xn.astype(BF16)
    x_lo = (xn - x_hi.astype(F32)).astype(BF16)
    r_hi = r.astype(BF16)
    r_lo = (r - r_hi.astype(F32)).astype(BF16)
    lg_ref[...] = _dot(x_hi, r_hi) + (_dot(x_lo, r_hi) + _dot(x_hi, r_lo))


def _router(x, g, router_padded):
    n, d = x.shape
    tm = min(ROW_TILE, n)
    row = lambda width: pl.BlockSpec((tm, width), lambda i: (i, 0))
    return pl.pallas_call(
        _router_kernel,
        grid=(n // tm,),
        in_specs=[row(d), _const_spec(g.shape), _const_spec(router_padded.shape)],
        out_specs=[row(d), row(router_padded.shape[1])],
        out_shape=[jax.ShapeDtypeStruct((n, d), F32),
                   jax.ShapeDtypeStruct((n, router_padded.shape[1]), F32)],
        compiler_params=_params(("parallel",)),
        name="router",
    )(x, g, router_padded)


def _expert_kernel(be_ref, nu_ref, tok_ref, tokn_ref, asg_ref, xn_hbm, w1_ref, w3_ref, w2_ref, y_hbm,
                   xg_ref, xb_ref, y_ref, gsem, ssem):
    b = pl.program_id(0)
    c = pl.program_id(1)
    nb = pl.num_programs(0)
    nc = pl.num_programs(1)
    tb = xb_ref.shape[0]
    slot = b % 2

    def start_gather(idx_ref, s):
        def body(r, carry):
            pltpu.make_async_copy(xn_hbm.at[pl.ds(idx_ref[0, 0, r], 1), :],
                                  xg_ref.at[s, pl.ds(r, 1), :], gsem.at[s]).start()
            return carry
        lax.fori_loop(0, tb, body, 0, unroll=8)

    def wait_gather(s):
        pltpu.make_async_copy(xn_hbm.at[pl.ds(0, tb), :], xg_ref.at[s], gsem.at[s]).wait()

    def start_scatter(s):
        def body(r, carry):
            pltpu.make_async_copy(y_ref.at[s, pl.ds(r, 1), :],
                                  y_hbm.at[pl.ds(asg_ref[0, 0, r], 1), :], ssem.at[s]).start()
            return carry
        lax.fori_loop(0, tb, body, 0, unroll=8)

    def wait_scatter(s):
        pltpu.make_async_copy(y_ref.at[s], y_hbm.at[pl.ds(0, tb), :], ssem.at[s]).wait()

    @pl.when(c == 0)
    def _():
        @pl.when(b == 0)
        def _():
            start_gather(tok_ref, 0)

        wait_gather(slot)
        xb_ref[...] = xg_ref[slot].astype(BF16)

        @pl.when(b + 1 < nb)
        def _():
            start_gather(tokn_ref, 1 - slot)

        y_ref[slot] = jnp.zeros(y_ref.shape[1:], F32)

    @pl.when(b < nu_ref[0])
    def _():
        xb = xb_ref[...]
        h1 = _dot(xb, w1_ref[...])
        h3 = _dot(xb, w3_ref[...])
        y_ref[slot] += _dot((jax.nn.silu(h1) * h3).astype(BF16), w2_ref[...])

    @pl.when(c == nc - 1)
    def _():
        @pl.when(b >= 1)
        def _():
            wait_scatter(1 - slot)

        start_scatter(slot)

        @pl.when(b == nb - 1)
        def _():
            wait_scatter(slot)


def _experts(xn, slot_tok, slot_row, block_e, n_used, w1, w3, w2, tb):
    d = xn.shape[1]
    p = slot_tok.shape[0]
    ff = w1.shape[2]
    nc = ff // FF_CHUNK
    nblk = p // tb
    tok3 = slot_tok.reshape(nblk, 1, tb)
    row3 = slot_row.reshape(nblk, 1, tb)
    idx_spec = lambda imap: pl.BlockSpec((1, 1, tb), imap, memory_space=pltpu.SMEM)
    chunk = lambda b, c, nu: jnp.where(b < nu[0], c, nc - 1)
    grid_spec = pltpu.PrefetchScalarGridSpec(
        num_scalar_prefetch=2,
        grid=(nblk, nc),
        in_specs=[idx_spec(lambda b, c, be, nu: (b, 0, 0)),
                  idx_spec(lambda b, c, be, nu: (jnp.minimum(b + 1, nblk - 1), 0, 0)),
                  idx_spec(lambda b, c, be, nu: (b, 0, 0)),
                  pl.BlockSpec(memory_space=pl.ANY),
                  pl.BlockSpec((None, d, FF_CHUNK), lambda b, c, be, nu: (be[b], 0, chunk(b, c, nu))),
                  pl.BlockSpec((None, d, FF_CHUNK), lambda b, c, be, nu: (be[b], 0, chunk(b, c, nu))),
                  pl.BlockSpec((None, FF_CHUNK, d), lambda b, c, be, nu: (be[b], chunk(b, c, nu), 0))],
        out_specs=pl.BlockSpec(memory_space=pl.ANY),
        scratch_shapes=[pltpu.VMEM((2, tb, d), F32), pltpu.VMEM((tb, d), BF16), pltpu.VMEM((2, tb, d), F32),
                        pltpu.SemaphoreType.DMA((2,)), pltpu.SemaphoreType.DMA((2,))],
    )
    return pl.pallas_call(
        _expert_kernel,
        grid_spec=grid_spec,
        out_shape=jax.ShapeDtypeStruct((p, d), F32),
        compiler_params=_params(("arbitrary", "arbitrary")),
        name="experts",
    )(block_e, n_used, tok3, tok3, row3, xn, w1, w3, w2)


def _combine_kernel(x_ref, y_ref, g_ref, o_ref):
    d = x_ref.shape[1]
    g = g_ref[...]
    o_ref[...] = x_ref[...] + g[:, 0:1] * y_ref[:, :d] + g[:, 1:2] * y_ref[:, d:]


def _combine(x, y_pairs, gate):
    n, d = x.shape
    tm = min(ROW_TILE, n)
    row = lambda width: pl.BlockSpec((tm, width), lambda i: (i, 0))
    return pl.pallas_call(
        _combine_kernel,
        grid=(n // tm,),
        in_specs=[row(d), row(TOP_K * d), row(gate.shape[1])],
        out_specs=row(d),
        out_shape=jax.ShapeDtypeStruct((n, d), F32),
        compiler_params=_params(("parallel",)),
        name="combine",
    )(x, y_pairs, gate)


def _moe(x, g, router, w1, w3, w2):
    n, d = x.shape
    a = n * TOP_K
    tb = min(MOE_TILE, a)
    router_padded = jnp.pad(router, ((0, 0), (0, 128 - N_EXPERTS)))
    xn, logits = _router(x, g, router_padded)
    top_logit, top_e = lax.top_k(logits[:, :N_EXPERTS], TOP_K)
    gate = jax.nn.softmax(top_logit, axis=-1)
    e_flat = top_e.reshape(a)
    onehot = (e_flat[:, None] == jnp.arange(N_EXPERTS)[None, :]).astype(jnp.int32)
    csum = jnp.cumsum(onehot, axis=0)
    rank = jnp.sum(onehot * csum, axis=1) - 1
    counts = csum[-1]
    padded = (counts + tb - 1) // tb * tb
    pend = jnp.cumsum(padded)
    pstart = pend - padded
    dest = (jnp.sum(onehot * pstart[None, :], axis=1) + rank).astype(jnp.int32)
    n_blocks = a // tb + N_EXPERTS
    p = n_blocks * tb
    slot_asg = jnp.full((p,), -1, jnp.int32).at[dest].set(jnp.arange(a, dtype=jnp.int32))
    is_pad = slot_asg < 0
    slot_row = jnp.where(is_pad, a - 1 + jnp.cumsum(is_pad.astype(jnp.int32)), slot_asg)
    slot_tok = jnp.where(is_pad, 0, slot_asg // TOP_K)
    block_e = jnp.minimum(jnp.searchsorted(pend, jnp.arange(n_blocks) * tb, side='right'),
                          N_EXPERTS - 1).astype(jnp.int32)
    n_used = (pend[-1] // tb).astype(jnp.int32).reshape(1)
    y = _experts(xn, slot_tok, slot_row, block_e, n_used, w1, w3, w2, tb)
    gate_padded = jnp.pad(gate, ((0, 0), (0, 128 - TOP_K)))
    return _combine(x, y.reshape(p // TOP_K, TOP_K * d), gate_padded)


def _tile_gain(gain, width, scale=1.0):
    return jnp.tile(gain, width // gain.shape[0]) * scale


def kernel(x, mem, rel_bias, norm_mix, w_in, b_gate, a_q_gain, a_k_gain, pool_w, pool_scale,
           c_q_gain, c_k_gain, c_sinks, w_branch, w_out, norm_cross, norm_mem, w_xq, w_xk, w_xv,
           x_q_gain, x_k_gain, w_xo, norm_ffn, ffn_w1, ffn_w3, ffn_w2, router, moe_w1, moe_w3, moe_w2):
    batch, seq, d = x.shape
    mem_len = mem.shape[1]
    depth = norm_mix.shape[0]
    xs = x.reshape(batch * seq, d)
    mems = mem.reshape(batch * mem_len, d)

    tab_a = rel_bias[:, :A_HEADS].T
    tab_c = rel_bias[:, A_HEADS:].T
    nb = seq // A_BLOCK
    n_near = min(nb, (_saturation_distance() + 2 * A_BLOCK - 2) // A_BLOCK)
    bias_a = _moba_bias_tiles(_bias_by_distance(tab_a, n_near * A_BLOCK), n_near) * LOG2E
    bias_a = jnp.concatenate([bias_a[0::2], bias_a[1::2]], axis=-1)
    far_a = tab_a[:, REL_BUCKETS - 1]
    far_a = jnp.stack([far_a * LOG2E, jnp.exp(far_a)])
    bias_c = _swa_bias_tiles(_bias_by_distance(tab_c, C_WINDOW)) * LOG2E
    seg = np.arange(256) // HEAD_DIM
    bd = jnp.asarray(seg[:, None] == seg[None, :], BF16)

    row = lambda v: v.reshape(1, -1)
    for l in range(depth):
        scale = HEAD_DIM ** -0.5 * LOG2E
        gains = jnp.stack([_tile_gain(a_q_gain[l], WIDTH, scale), _tile_gain(a_k_gain[l], WIDTH),
                           _tile_gain(c_q_gain[l], WIDTH, scale), _tile_gain(c_k_gain[l], WIDTH)])
        flag_a = _bounded_flag(a_q_gain[l], a_k_gain[l], HEAD_DIM, jnp.max(jnp.abs(tab_a)))
        flag_c = _bounded_flag(c_q_gain[l], c_k_gain[l], HEAD_DIM,
                               jnp.maximum(jnp.max(jnp.abs(tab_c)), jnp.max(jnp.abs(c_sinks[l]))))
        qa, ka, va, ub, qc, kc, vc, gates = _in_proj(
            xs, row(norm_mix[l]), w_in[l].astype(BF16), row(b_gate[l]), gains, bd)
        oa = _moba(qa, ka, va, bias_a, far_a, flag_a, batch, seq)
        oc = _swa(qc, kc, vc, bias_c, c_sinks[l] * LOG2E, flag_c, batch, seq)
        k_mem, v_mem = _mem_kv(mems, row(norm_mem[l]), w_xk[l].astype(BF16), w_xv[l].astype(BF16),
                               row(_tile_gain(x_k_gain[l], WIDTH)))
        xs = _merge_cross(xs, oa, ub, oc, gates, pool_w[l].astype(BF16), row(pool_scale[l]),
                          w_branch[l].astype(BF16), w_out[l].astype(BF16), row(norm_cross[l]),
                          w_xq[l].astype(BF16), row(_tile_gain(x_q_gain[l], WIDTH, X_HEAD_DIM ** -0.5)),
                          k_mem, v_mem, w_xo[l].astype(BF16), seq, mem_len)
        i = l // 2
        if l % 2 == 0:
            xs = _ffn(xs, row(norm_ffn[l]), ffn_w1[i].astype(BF16), ffn_w3[i].astype(BF16),
                      ffn_w2[i].astype(BF16))
        else:
            xs = _moe(xs, row(norm_ffn[l]), router[i], moe_w1[i].astype(BF16), moe_w3[i].astype(BF16),
                      moe_w2[i].astype(BF16))
    return xs.reshape(batch, seq, d)
```

```python
import functools
import math

import jax
import jax.numpy as jnp
import numpy as np
from jax import lax
from jax.experimental import pallas as pl
from jax.experimental.pallas import tpu as pltpu

F32 = jnp.float32
BF16 = jnp.bfloat16

HEAD_DIM = 64
A_HEADS = 8
A_BLOCK = 256
A_TOPK = 3
B_GROUPS = 4
B_GROUP_DIM = 128
B_WINDOWS = (2, 4, 8, 16)
C_HEADS = 8
C_KV_HEADS = 2
C_WINDOW = 128
REL_BUCKETS = 32
REL_MAX_DIST = 1024
X_HEADS = 4
X_HEAD_DIM = 128
N_EXPERTS = 8
TOP_K = 2
EPS = 1e-6
NEG_INF = -1e30
LOG2E = math.log2(math.e)

WIDTH = 512
POOL_HALO = 16
ROW_TILE = 512
MOE_TILE = 1024
FF_CHUNK = 512
SUBLANES = 8
KEY_CHUNK = 128
SWA_BLOCKS = 4
VMEM_LIMIT = 56 * 1024 * 1024
EXP2_SAFE = 100.0


def _dot(a, b):
    return jnp.dot(a, b, preferred_element_type=F32)


def _dot_nt(a, b):
    return lax.dot_general(a, b, (((1,), (1,)), ((), ())), preferred_element_type=F32)


def _rms(x, g):
    ms = jnp.mean(x * x, axis=-1, keepdims=True)
    return x * lax.rsqrt(ms + EPS) * g


def _const_spec(shape):
    zeros = (0,) * len(shape)
    return pl.BlockSpec(shape, lambda *_: zeros, pipeline_mode=pl.Buffered(1))


def _smem_spec():
    return pl.BlockSpec(memory_space=pltpu.SMEM)


def _params(sem):
    return pltpu.CompilerParams(dimension_semantics=sem, vmem_limit_bytes=VMEM_LIMIT)


def _rel_bucket(dist):
    n = jnp.maximum(dist, 0)
    max_exact = REL_BUCKETS // 2
    nf = jnp.maximum(n, 1).astype(jnp.float32)
    large = max_exact + (jnp.log(nf / max_exact) / math.log(REL_MAX_DIST / max_exact)
                         * (REL_BUCKETS - max_exact)).astype(jnp.int32)
    large = jnp.minimum(large, REL_BUCKETS - 1)
    return jnp.where(n < max_exact, n, large)


def _saturation_distance():
    ratio = REL_MAX_DIST / (REL_BUCKETS // 2)
    return int(math.ceil((REL_BUCKETS // 2) * ratio ** ((REL_BUCKETS - 1 - REL_BUCKETS // 2 + 0.5)
                                                       / (REL_BUCKETS - REL_BUCKETS // 2))))


def _bias_by_distance(tab, n_dist):
    onehot = (_rel_bucket(jnp.arange(n_dist))[:, None] == jnp.arange(REL_BUCKETS)[None, :]).astype(F32)
    return jnp.einsum('hb,db->hd', tab, onehot, precision=lax.Precision.HIGHEST)


def _skew(g, rows):
    n = g.shape[-1]
    lead = g.shape[:-1]
    tiled = jnp.broadcast_to(g[..., None, :], lead + (rows, n)).reshape(lead + (rows * n,))
    return tiled[..., :rows * (n - 1)].reshape(lead + (rows, n - 1))


def _moba_bias_tiles(bvec, n_near):
    L = A_BLOCK
    rows = []
    for delta in range(n_near):
        lo = delta * L - (L - 1)
        seg = bvec[:, max(lo, 0):delta * L + L]
        if lo < 0:
            seg = jnp.concatenate([jnp.zeros((bvec.shape[0], -lo), F32), seg], axis=1)
        rows.append(jnp.pad(seg, ((0, 0), (0, 1))))
    g = jnp.stack(rows, axis=1)
    return _skew(g, L)[..., L - 1:]


def _swa_bias_tiles(bvec):
    W = C_WINDOW
    u = np.arange(3 * W)
    g = bvec[:, np.clip(2 * W - 1 - u, 0, W - 1)]
    return _skew(g, W)[..., W - 1:3 * W - 1]


def _bounded_flag(q_gain, k_gain, head_dim, extra):
    bound = head_dim ** 0.5 * jnp.max(jnp.abs(q_gain)) * jnp.max(jnp.abs(k_gain)) * 1.02 + extra
    return (bound * LOG2E < EXP2_SAFE).astype(jnp.int32).reshape(1)


def _in_proj_kernel(x_ref, g_ref, w_ref, bg_ref, gn_ref, bd_ref,
                    qa_ref, ka_ref, va_ref, ub_ref, qc_ref, kc_ref, vc_ref, gt_ref):
    xb = _rms(x_ref[...], g_ref[...]).astype(BF16)
    bd = bd_ref[...]

    def proj(c0, width):
        return _dot(xb, w_ref[:, c0:c0 + width])

    def head_norm(t, gain):
        outs = []
        for c in range(0, t.shape[1], 256):
            wd = min(256, t.shape[1] - c)
            tc = t[:, c:c + wd]
            ss = _dot((tc * tc).astype(BF16), bd[:wd, :wd])
            outs.append(tc * lax.rsqrt(ss * (1.0 / HEAD_DIM) + EPS))
        y = outs[0] if len(outs) == 1 else jnp.concatenate(outs, axis=-1)
        return y * gain

    def tile_kv_heads(t):
        lane = lax.broadcasted_iota(jnp.int32, t.shape, 1)
        r = pltpu.roll(t, HEAD_DIM, axis=1)
        h0 = jnp.where(lane < HEAD_DIM, t, r)
        h1 = jnp.where(lane < HEAD_DIM, r, t)
        return jnp.concatenate([h0, h0, h1, h1], axis=-1)

    qa_ref[...] = head_norm(proj(0, WIDTH), gn_ref[0:1, :]).astype(BF16)
    ka_ref[...] = head_norm(proj(WIDTH, WIDTH), gn_ref[1:2, :]).astype(BF16)
    va_ref[...] = proj(2 * WIDTH, WIDTH).astype(BF16)
    ub_ref[...] = proj(3 * WIDTH, WIDTH)
    qc_ref[...] = head_norm(proj(4 * WIDTH, WIDTH), gn_ref[2:3, :]).astype(BF16)
    kv = C_KV_HEADS * HEAD_DIM
    c0 = 5 * WIDTH
    kc = head_norm(proj(c0, kv), gn_ref[3:4, :kv])
    kc_ref[...] = tile_kv_heads(kc).astype(BF16)
    vc_ref[...] = tile_kv_heads(proj(c0 + kv, kv)).astype(BF16)
    c0 += 2 * kv
    for c in range(0, gt_ref.shape[1], WIDTH):
        gl = proj(c0 + c, WIDTH) + bg_ref[:, c:c + WIDTH]
        gt_ref[:, c:c + WIDTH] = jax.nn.sigmoid(gl).astype(BF16)


def _in_proj(x, g, w_in, b_gate, gains, bd):
    n, d = x.shape
    tm = min(ROW_TILE, n)
    n_gate = b_gate.shape[1]
    row = lambda width: pl.BlockSpec((tm, width), lambda i: (i, 0))
    out_shape = [jax.ShapeDtypeStruct((n, WIDTH), BF16)] * 3 + [jax.ShapeDtypeStruct((n, WIDTH), F32)] \
        + [jax.ShapeDtypeStruct((n, WIDTH), BF16)] * 3 + [jax.ShapeDtypeStruct((n, n_gate), BF16)]
    return pl.pallas_call(
        _in_proj_kernel,
        grid=(n // tm,),
        in_specs=[row(d), _const_spec(g.shape), _const_spec(w_in.shape), _const_spec(b_gate.shape),
                  _const_spec(gains.shape), _const_spec(bd.shape)],
        out_specs=[row(WIDTH)] * 7 + [row(n_gate)],
        out_shape=out_shape,
        compiler_params=_params(("parallel",)),
        name="in_proj",
    )(x, g, w_in, b_gate, gains, bd)


def _moba_kernel(flag_ref, far_ref, q_ref, k_ref, v_ref, bias_ref, o_ref,
                 kmean_ref, vt_ref, sel_ref, m_ref, l_ref, l8_ref, acc_ref, s_ref, *, nb, n_near):
    i = pl.program_id(1)
    L = A_BLOCK
    KC = KEY_CHUNK
    PW = 2 * HEAD_DIM
    nbp = kmean_ref.shape[0]

    @pl.when(i == 0)
    def _():
        kmean_ref[...] = jnp.zeros(kmean_ref.shape, F32)
        for j in range(nb):
            kj = k_ref[j * L:(j + 1) * L, :].astype(F32)
            kmean_ref[j:j + 1, :] = jnp.mean(kj, axis=0, keepdims=True)
            vt_ref[j] = v_ref[j * L:(j + 1) * L, :].astype(F32).T.astype(BF16)

    lane = lax.broadcasted_iota(jnp.int32, (L, PW), 1)
    blk = lax.broadcasted_iota(jnp.int32, (nbp, L), 0)
    past = blk < i
    qh = []
    for h in range(A_HEADS):
        pair = slice((h // 2) * PW, (h // 2 + 1) * PW)
        q = q_ref[:, pair]
        qm = jnp.where(lane // HEAD_DIM == h % 2, q, jnp.zeros_like(q))
        qh.append(qm)
        km = kmean_ref[:, pair]
        km_hi = km.astype(BF16)
        km_lo = (km - km_hi.astype(F32)).astype(BF16)
        s = jnp.where(past, _dot_nt(km_hi, qm) + _dot_nt(km_lo, qm), -jnp.inf)
        rank = jnp.zeros((nbp, L), jnp.int32)
        for jp in range(nb):
            sj = s[jp:jp + 1, :]
            ahead = (sj > s) | ((sj == s) & (jp < blk))
            rank = rank + ahead.astype(jnp.int32)
        sel_ref[h] = (past & (rank < A_TOPK)).astype(F32)
        m_ref[h] = jnp.full((1, L), NEG_INF, F32)
        l_ref[h] = jnp.zeros((1, L), F32)
        l8_ref[h] = jnp.zeros((8, L), F32)
    acc_ref[...] = jnp.zeros(acc_ref.shape, F32)
    qpair = [jnp.concatenate(qh[2 * hp:2 * hp + 2], axis=0) for hp in range(A_HEADS // 2)]

    def causal(c, n):
        kk = c + lax.broadcasted_iota(jnp.int32, (n, L), 0)
        qq = lax.broadcasted_iota(jnp.int32, (n, L), 1)
        return kk <= qq

    def bounded_tile(j, delta, far, own):
        units = [(hp, c) for hp in range(A_HEADS // 2) for c in range(0, L, KC)]

        def qk(u):
            hp, c = units[u]
            kc = k_ref[pl.ds(pl.multiple_of(j * L + c, KC), KC), hp * PW:(hp + 1) * PW]
            s_ref[u % 2] = _dot_nt(kc, qpair[hp])

        qk(0)
        o = ps = None
        for u, (hp, c) in enumerate(units):
            if u + 1 < len(units):
                qk(u + 1)
            s = s_ref[u % 2]
            pair = slice(hp * PW, (hp + 1) * PW)
            if c == 0:
                o = [jnp.zeros((HEAD_DIM, L), F32)] * 2
                ps = jnp.zeros((8, 2 * L), F32)
            if not far:
                s = s + bias_ref[hp, delta, pl.ds(c, KC), :]
            p = jnp.exp2(s)
            if own:
                p = jnp.where(jnp.concatenate([causal(c, KC)] * 2, axis=1), p, 0.0)
            ps = ps + jnp.sum(p.reshape(KC // 8, 8, 2 * L), axis=0)
            pb = p.astype(BF16)
            o = [o[hh] + _dot(vt_ref[j, pl.ds((2 * hp + hh) * HEAD_DIM, HEAD_DIM), pl.ds(c, KC)],
                              pb[:, hh * L:(hh + 1) * L]) for hh in range(2)]
            if c + KC < L:
                continue
            for hh in range(2):
                h = 2 * hp + hh
                rows = slice(h * HEAD_DIM, (h + 1) * HEAD_DIM)
                oh = o[hh]
                psh = ps[:, hh * L:(hh + 1) * L]
                if own:
                    acc_ref[rows, :] += oh
                    l8_ref[h] += psh
                else:
                    w = sel_ref[h, pl.ds(j, 1), :]
                    if far:
                        w = w * far_ref[1, h]
                    acc_ref[rows, :] += w * oh
                    l8_ref[h] += w * psh

    def online_tile(j, delta, far, own):
        for h in range(A_HEADS):
            pair = slice((h // 2) * PW, (h // 2 + 1) * PW)
            rows = slice(h * HEAD_DIM, (h + 1) * HEAD_DIM)
            s = _dot_nt(k_ref[pl.ds(pl.multiple_of(j * L, L), L), pair], qh[h])
            if not far:
                s = s + bias_ref[h // 2, delta, :, (h % 2) * L:(h % 2 + 1) * L]
            if own:
                s = jnp.where(causal(0, L), s, NEG_INF)
            mj = jnp.max(s, axis=0, keepdims=True)
            p = jnp.exp2(s - mj)
            lj = jnp.sum(p, axis=0, keepdims=True)
            o = _dot(vt_ref[j, rows, :], p.astype(BF16))
            if far:
                mj = mj + far_ref[0, h]
            m_old = m_ref[h]
            if own:
                m_new = jnp.maximum(m_old, mj)
                beta = jnp.exp2(mj - m_new)
            else:
                on = sel_ref[h, pl.ds(j, 1), :] > 0.5
                m_new = jnp.where(on, jnp.maximum(m_old, mj), m_old)
                beta = jnp.where(on, jnp.exp2(mj - m_new), 0.0)
            alpha = jnp.exp2(m_old - m_new)
            m_ref[h] = m_new
            l_ref[h] = alpha * l_ref[h] + beta * lj
            acc_ref[rows, :] = alpha * acc_ref[rows, :] + beta * o

    n_far = jnp.maximum(i - (n_near - 1), 0)

    def run(tile):
        def far_body(j, c):
            tile(j, None, True, False)
            return c

        def near_body(j, c):
            tile(j, i - j, False, False)
            return c

        lax.fori_loop(0, n_far, far_body, 0)
        lax.fori_loop(n_far, i, near_body, 0)
        tile(i, 0, False, True)

    @pl.when(flag_ref[0] == 1)
    def _():
        run(bounded_tile)
        for h in range(A_HEADS):
            l_ref[h] = jnp.sum(l8_ref[h], axis=0, keepdims=True)

    @pl.when(flag_ref[0] != 1)
    def _():
        run(online_tile)

    o_t = jnp.concatenate([acc_ref[h * HEAD_DIM:(h + 1) * HEAD_DIM, :] / l_ref[h] for h in range(A_HEADS)],
                          axis=0)
    o_ref[...] = o_t.T.astype(BF16)


def _moba(q, k, v, bias_tiles, far, flag, batch, seq):
    n = q.shape[0]
    L = A_BLOCK
    nb = seq // L
    nbp = max(8, -(-nb // 8) * 8)
    n_near = bias_tiles.shape[1]
    kern = functools.partial(_moba_kernel, nb=nb, n_near=n_near)
    return pl.pallas_call(
        kern,
        grid=(batch, nb),
        in_specs=[_smem_spec(), _smem_spec(),
                  pl.BlockSpec((L, WIDTH), lambda b, i: (b * nb + i, 0)),
                  pl.BlockSpec((seq, WIDTH), lambda b, i: (b, 0)),
                  pl.BlockSpec((seq, WIDTH), lambda b, i: (b, 0)),
                  _const_spec(bias_tiles.shape)],
        out_specs=pl.BlockSpec((L, WIDTH), lambda b, i: (b * nb + i, 0)),
        out_shape=jax.ShapeDtypeStruct((n, WIDTH), BF16),
        scratch_shapes=[pltpu.VMEM((nbp, WIDTH), F32),
                        pltpu.VMEM((nb, WIDTH, L), BF16),
                        pltpu.VMEM((A_HEADS, nbp, L), F32),
                        pltpu.VMEM((A_HEADS, 1, L), F32),
                        pltpu.VMEM((A_HEADS, 1, L), F32),
                        pltpu.VMEM((A_HEADS, 8, L), F32),
                        pltpu.VMEM((WIDTH, L), F32),
                        pltpu.VMEM((2, KEY_CHUNK, 2 * L), F32)],
        compiler_params=_params(("arbitrary", "arbitrary")),
        name="moba",
    )(flag, far, q, k, v, bias_tiles)


def _swa_kernel(flag_ref, sink_ref, q_ref, kp_ref, kc_ref, vp_ref, vc_ref, bias_ref, o_ref):
    g = pl.program_id(0)
    chunk = pl.program_id(2)
    W = C_WINDOW
    G = C_HEADS // C_KV_HEADS
    nsub = q_ref.shape[0] // W
    gw = q_ref.shape[1]
    kall = jnp.concatenate([kp_ref[...], kc_ref[...]], axis=0)
    vall = jnp.concatenate([vp_ref[...], vc_ref[...]], axis=0)
    lane = lax.broadcasted_iota(jnp.int32, (W, gw), 1)
    qi = lax.broadcasted_iota(jnp.int32, (W, 2 * W), 0)
    kj = lax.broadcasted_iota(jnp.int32, (W, 2 * W), 1)
    dist = W + qi - kj
    band = (dist >= 0) & (dist < W)

    def heads(bounded):
        for r in range(nsub):
            q = q_ref[r * W:(r + 1) * W, :]
            kcat = kall[r * W:(r + 2) * W]
            vcat = vall[r * W:(r + 2) * W]
            mask = band & ((chunk > 0) | (kj >= W)) if r == 0 else band
            out = jnp.zeros(q.shape, F32)
            for hh in range(G):
                mine = lane // HEAD_DIM == hh
                qm = jnp.where(mine, q, jnp.zeros_like(q))
                s = _dot_nt(qm, kcat) + bias_ref[hh]
                sink = sink_ref[g * G + hh]
                if bounded:
                    pb = jnp.where(mask, jnp.exp2(s), 0.0).astype(BF16)
                    den = _dot(pb, jnp.ones((2 * W, 128), BF16)) + jnp.exp2(jnp.full((1, 128), sink, F32))
                    inv = 1.0 / den
                    o = _dot(pb, vcat) * jnp.concatenate([inv] * (gw // 128), axis=-1)
                else:
                    s = jnp.where(mask, s, NEG_INF)
                    m = jnp.maximum(jnp.max(s, axis=-1, keepdims=True), sink)
                    p = jnp.exp2(s - m)
                    den = jnp.sum(p, axis=-1, keepdims=True) + jnp.exp2(sink - m)
                    o = _dot(p.astype(BF16), vcat) / den
                out = jnp.where(mine, o, out)
            o_ref[r * W:(r + 1) * W, :] = out.astype(BF16)

    @pl.when(flag_ref[0] == 1)
    def _():
        heads(True)

    @pl.when(flag_ref[0] != 1)
    def _():
        heads(False)


def _swa(q, k_t, v_t, bias_tiles, sinks, flag, batch, seq):
    n = q.shape[0]
    W = C_WINDOW
    G = C_HEADS // C_KV_HEADS
    gw = G * HEAD_DIM
    nsub = min(SWA_BLOCKS, seq // W)
    nchunk = seq // (W * nsub)
    cur = lambda g, b, j: (b * nchunk + j, g)
    prev = lambda g, b, j: (jnp.maximum((b * nchunk + j) * nsub - 1, 0), g)
    return pl.pallas_call(
        _swa_kernel,
        grid=(C_KV_HEADS, batch, nchunk),
        in_specs=[_smem_spec(), _smem_spec(),
                  pl.BlockSpec((nsub * W, gw), cur),
                  pl.BlockSpec((W, gw), prev), pl.BlockSpec((nsub * W, gw), cur),
                  pl.BlockSpec((W, gw), prev), pl.BlockSpec((nsub * W, gw), cur),
                  pl.BlockSpec((G, W, 2 * W), lambda g, b, j: (g, 0, 0))],
        out_specs=pl.BlockSpec((nsub * W, gw), cur),
        out_shape=jax.ShapeDtypeStruct((n, WIDTH), BF16),
        compiler_params=_params(("arbitrary", "arbitrary", "arbitrary")),
        name="swa",
    )(flag, sinks, q, k_t, k_t, v_t, v_t, bias_tiles)


def _seg_norm(t, gain, seg):
    outs = []
    for c in range(0, t.shape[1], seg):
        tc = t[:, c:c + seg]
        outs.append(tc * lax.rsqrt(jnp.mean(tc * tc, axis=-1, keepdims=True) + EPS))
    return jnp.concatenate(outs, axis=-1) * gain


def _mem_kv_kernel(mem_ref, g_ref, wk_ref, wv_ref, kg_ref, k_ref, v_ref):
    mb = _rms(mem_ref[...], g_ref[...]).astype(BF16)
    k_ref[...] = _seg_norm(_dot(mb, wk_ref[...]), kg_ref[...], X_HEAD_DIM).astype(BF16)
    v_ref[...] = _dot(mb, wv_ref[...]).astype(BF16)


def _mem_kv(mem, g, w_xk, w_xv, k_gain):
    n, d = mem.shape
    tm = min(ROW_TILE, n)
    row = lambda width: pl.BlockSpec((tm, width), lambda i: (i, 0))
    return pl.pallas_call(
        _mem_kv_kernel,
        grid=(n // tm,),
        in_specs=[row(d), _const_spec(g.shape), _const_spec(w_xk.shape), _const_spec(w_xv.shape),
                  _const_spec(k_gain.shape)],
        out_specs=[row(WIDTH), row(WIDTH)],
        out_shape=[jax.ShapeDtypeStruct((n, WIDTH), BF16)] * 2,
        compiler_params=_params(("parallel",)),
        name="mem_kv",
    )(mem, g, w_xk, w_xv, k_gain)


def _merge_cross_kernel(x_ref, oa_ref, ub_ref, halo_ref, oc_ref, gt_ref, pw_ref, ps_ref, wb_ref, wo_ref,
                        gx_ref, wq_ref, qg_ref, km_ref, vm_ref, wxo_ref, o_ref, *, seq):
    tm = x_ref.shape[0]
    d = x_ref.shape[1]
    t0 = (pl.program_id(0) * tm) % seq
    H = POOL_HALO

    halo = jnp.where(t0 > 0, halo_ref[...], 0.0)
    pos = t0 + lax.broadcasted_iota(jnp.int32, (tm, B_GROUP_DIM), 0)
    mixed = []
    for gi, win in enumerate(B_WINDOWS):
        cols = slice(gi * B_GROUP_DIM, (gi + 1) * B_GROUP_DIM)
        cur = ub_ref[:, cols]
        acc = jnp.concatenate([halo[:, cols], cur], axis=0)
        step = 1
        while step < win:
            acc = acc + pltpu.roll(acc, step, axis=0)
            step *= 2
        cnt = jnp.minimum(pos + 1, win).astype(F32)
        pooled = acc[H:, :] / cnt - cur
        mixed.append(_dot(pooled.astype(BF16), pw_ref[gi]))
    ob = jnp.concatenate(mixed, axis=-1) * ps_ref[...]

    merged = gt_ref[:, 0:d].astype(F32) * _dot(oa_ref[...], wb_ref[0])
    merged = merged + gt_ref[:, d:2 * d].astype(F32) * _dot(ob.astype(BF16), wb_ref[1])
    merged = merged + gt_ref[:, 2 * d:3 * d].astype(F32) * _dot(oc_ref[...], wb_ref[2])
    x1 = x_ref[...] + _dot(merged.astype(BF16), wo_ref[...])

    xb = _rms(x1, gx_ref[...]).astype(BF16)
    qn = _seg_norm(_dot(xb, wq_ref[...]), qg_ref[...], X_HEAD_DIM).astype(BF16)
    heads = []
    for h in range(X_HEADS):
        cols = slice(h * X_HEAD_DIM, (h + 1) * X_HEAD_DIM)
        s = _dot_nt(qn[:, cols], km_ref[:, cols])
        p = jnp.exp(s - jnp.max(s, axis=-1, keepdims=True))
        inv = 1.0 / jnp.sum(p, axis=-1, keepdims=True)
        heads.append(_dot(p.astype(BF16), vm_ref[:, cols]) * inv)
    o = jnp.concatenate(heads, axis=-1).astype(BF16)
    o_ref[...] = x1 + _dot(o, wxo_ref[...])


def _merge_cross(x, oa, ub, oc, gates, pool_w, pool_scale, w_branch, w_out,
                 g_cross, w_xq, q_gain, k_mem, v_mem, w_xo, seq, mem_len):
    n, d = x.shape
    tm = min(ROW_TILE, seq)
    row = lambda width: pl.BlockSpec((tm, width), lambda i: (i, 0))
    halo = pl.BlockSpec((POOL_HALO, WIDTH), lambda i: (jnp.maximum(i * (tm // POOL_HALO) - 1, 0), 0))
    mem = pl.BlockSpec((mem_len, WIDTH), lambda i: ((i * tm) // seq, 0))
    kern = functools.partial(_merge_cross_kernel, seq=seq)
    return pl.pallas_call(
        kern,
        grid=(n // tm,),
        in_specs=[row(d), row(WIDTH), row(WIDTH), halo, row(WIDTH), row(gates.shape[1]),
                  _const_spec(pool_w.shape), _const_spec(pool_scale.shape), _const_spec(w_branch.shape),
                  _const_spec(w_out.shape), _const_spec(g_cross.shape), _const_spec(w_xq.shape),
                  _const_spec(q_gain.shape), mem, mem, _const_spec(w_xo.shape)],
        out_specs=row(d),
        out_shape=jax.ShapeDtypeStruct((n, d), F32),
        compiler_params=_params(("parallel",)),
        name="merge_cross",
    )(x, oa, ub, ub, oc, gates, pool_w, pool_scale, w_branch, w_out,
      g_cross, w_xq, q_gain, k_mem, v_mem, w_xo)


def _swiglu_chunks(xb, w1_ref, w3_ref, w2_ref, acc):
    for c in range(0, w1_ref.shape[1], FF_CHUNK):
        h1 = _dot(xb, w1_ref[:, c:c + FF_CHUNK])
        h3 = _dot(xb, w3_ref[:, c:c + FF_CHUNK])
        acc = acc + _dot((jax.nn.silu(h1) * h3).astype(BF16), w2_ref[c:c + FF_CHUNK, :])
    return acc


def _ffn_kernel(x_ref, g_ref, w1_ref, w3_ref, w2_ref, o_ref):
    x = x_ref[...]
    xb = _rms(x, g_ref[...]).astype(BF16)
    o_ref[...] = _swiglu_chunks(xb, w1_ref, w3_ref, w2_ref, x)


def _ffn(x, g, w1, w3, w2):
    n, d = x.shape
    tm = min(ROW_TILE, n)
    row = pl.BlockSpec((tm, d), lambda i: (i, 0))
    return pl.pallas_call(
        _ffn_kernel,
        grid=(n // tm,),
        in_specs=[row, _const_spec(g.shape), _const_spec(w1.shape), _const_spec(w3.shape),
                  _const_spec(w2.shape)],
        out_specs=row,
        out_shape=jax.ShapeDtypeStruct((n, d), F32),
        compiler_params=_params(("parallel",)),
        name="ffn_dense",
    )(x, g, w1, w3, w2)


def _store_token_tiles(ref, x):
    rows = x.shape[0]
    for k in range(x.shape[1] // 128):
        ref[pl.ds(k, rows, stride=SUBLANES), :] = x[:, k * 128:(k + 1) * 128]


def _load_token_tiles(ref, rows, lead=None):
    idx = (lambda k: (pl.ds(k, rows, stride=SUBLANES), slice(None))) if lead is None else \
        (lambda k: (lead, pl.ds(k, rows, stride=SUBLANES), slice(None)))
    return jnp.concatenate([ref[idx(k)] for k in range(SUBLANES)], axis=-1)


def _router_kernel(x_ref, g_ref, r_ref, xn_ref, lg_ref):
    xn = _rms(x_ref[...], g_ref[...])
    _store_token_tiles(xn_ref, xn)
    r = r_ref[...]
    x_hi = xn.astype(BF16)
    x_lo = (xn - x_hi.astype(F32)).astype(BF16)
    r_hi = r.astype(BF16)
    r_lo = (r - r_hi.astype(F32)).astype(BF16)
    lg_ref[...] = _dot(x_hi, r_hi) + (_dot(x_lo, r_hi) + _dot(x_hi, r_lo))


def _router(x, g, router_padded):
    n, d = x.shape
    tm = min(ROW_TILE, n)
    row = lambda width: pl.BlockSpec((tm, width), lambda i: (i, 0))
    return pl.pallas_call(
        _router_kernel,
        grid=(n // tm,),
        in_specs=[row(d), _const_spec(g.shape), _const_spec(router_padded.shape)],
        out_specs=[pl.BlockSpec((tm * SUBLANES, d // SUBLANES), lambda i: (i, 0)),
                   row(router_padded.shape[1])],
        out_shape=[jax.ShapeDtypeStruct((n * SUBLANES, d // SUBLANES), F32),
                   jax.ShapeDtypeStruct((n, router_padded.shape[1]), F32)],
        compiler_params=_params(("parallel",)),
        name="router",
    )(x, g, router_padded)


def _expert_kernel(be_ref, nu_ref, tok_ref, tokn_ref, asg_ref, xn_hbm, w1_ref, w3_ref, w2_ref, y_hbm,
                   xg_ref, xb_ref, y_ref, ys_ref, gsem, ssem):
    b = pl.program_id(0)
    c = pl.program_id(1)
    nb = pl.num_programs(0)
    nc = pl.num_programs(1)
    tb = xb_ref.shape[0]
    T = SUBLANES
    slot = b % 2

    def tile(ref, first_row):
        return ref.at[pl.ds(pl.multiple_of(first_row, T), T), :]

    def start_gather(idx_ref, s):
        def body(r, carry):
            pltpu.make_async_copy(tile(xn_hbm, idx_ref[0, 0, r]), tile(xg_ref.at[s], r * T), gsem.at[s]).start()
            return carry
        lax.fori_loop(0, tb, body, 0, unroll=8)

    def wait_gather(s):
        pltpu.make_async_copy(xn_hbm.at[pl.ds(0, tb * T), :], xg_ref.at[s], gsem.at[s]).wait()

    def start_scatter(s):
        def body(r, carry):
            pltpu.make_async_copy(tile(ys_ref.at[s], r * T), tile(y_hbm, asg_ref[0, 0, r]), ssem.at[s]).start()
            return carry
        lax.fori_loop(0, tb, body, 0, unroll=8)

    def wait_scatter(s):
        pltpu.make_async_copy(ys_ref.at[s], y_hbm.at[pl.ds(0, tb * T), :], ssem.at[s]).wait()

    @pl.when(c == 0)
    def _():
        @pl.when(b == 0)
        def _():
            start_gather(tok_ref, 0)

        wait_gather(slot)
        xb_ref[...] = _load_token_tiles(xg_ref, tb, lead=slot).astype(BF16)

        @pl.when(b + 1 < nb)
        def _():
            start_gather(tokn_ref, 1 - slot)

        y_ref[...] = jnp.zeros(y_ref.shape, F32)

    @pl.when(b < nu_ref[0])
    def _():
        xb = xb_ref[...]
        h1 = _dot(xb, w1_ref[...])
        h3 = _dot(xb, w3_ref[...])
        y_ref[...] += _dot((jax.nn.silu(h1) * h3).astype(BF16), w2_ref[...])

    @pl.when(c == nc - 1)
    def _():
        _store_token_tiles(ys_ref.at[slot], y_ref[...])

        @pl.when(b >= 1)
        def _():
            wait_scatter(1 - slot)

        start_scatter(slot)

        @pl.when(b == nb - 1)
        def _():
            wait_scatter(slot)


def _experts(xn, slot_tok, slot_row, block_e, n_used, w1, w3, w2, tb):
    T = SUBLANES
    d = w1.shape[1]
    p = slot_tok.shape[0]
    ff = w1.shape[2]
    nc = ff // FF_CHUNK
    nblk = p // tb
    tok3 = (slot_tok * T).reshape(nblk, 1, tb)
    row3 = (slot_row * T).reshape(nblk, 1, tb)
    idx_spec = lambda imap: pl.BlockSpec((1, 1, tb), imap, memory_space=pltpu.SMEM)
    chunk = lambda b, c, nu: jnp.where(b < nu[0], c, nc - 1)
    grid_spec = pltpu.PrefetchScalarGridSpec(
        num_scalar_prefetch=2,
        grid=(nblk, nc),
        in_specs=[idx_spec(lambda b, c, be, nu: (b, 0, 0)),
                  idx_spec(lambda b, c, be, nu: (jnp.minimum(b + 1, nblk - 1), 0, 0)),
                  idx_spec(lambda b, c, be, nu: (b, 0, 0)),
                  pl.BlockSpec(memory_space=pl.ANY),
                  pl.BlockSpec((None, d, FF_CHUNK), lambda b, c, be, nu: (be[b], 0, chunk(b, c, nu))),
                  pl.BlockSpec((None, d, FF_CHUNK), lambda b, c, be, nu: (be[b], 0, chunk(b, c, nu))),
                  pl.BlockSpec((None, FF_CHUNK, d), lambda b, c, be, nu: (be[b], chunk(b, c, nu), 0))],
        out_specs=pl.BlockSpec(memory_space=pl.ANY),
        scratch_shapes=[pltpu.VMEM((2, tb * T, d // T), F32), pltpu.VMEM((tb, d), BF16), pltpu.VMEM((tb, d), F32),
                        pltpu.VMEM((2, tb * T, d // T), F32),
                        pltpu.SemaphoreType.DMA((2,)), pltpu.SemaphoreType.DMA((2,))],
    )
    return pl.pallas_call(
        _expert_kernel,
        grid_spec=grid_spec,
        out_shape=jax.ShapeDtypeStruct((p * T, d // T), F32),
        compiler_params=_params(("arbitrary", "arbitrary")),
        name="experts",
    )(block_e, n_used, tok3, tok3, row3, xn, w1, w3, w2)


def _combine_kernel(x_ref, y0_ref, y1_ref, g_ref, o_ref):
    tm = x_ref.shape[0]
    g = g_ref[...]
    o_ref[...] = x_ref[...] + g[:, 0:1] * _load_token_tiles(y0_ref, tm) + g[:, 1:2] * _load_token_tiles(y1_ref, tm)


def _combine(x, y, gate):
    n, d = x.shape
    tm = min(ROW_TILE, n)
    T = SUBLANES
    row = lambda width: pl.BlockSpec((tm, width), lambda i: (i, 0))
    tiles = lambda choice: pl.BlockSpec((tm * T, d // T), lambda i: (choice * (n // tm) + i, 0))
    return pl.pallas_call(
        _combine_kernel,
        grid=(n // tm,),
        in_specs=[row(d), tiles(0), tiles(1), row(gate.shape[1])],
        out_specs=row(d),
        out_shape=jax.ShapeDtypeStruct((n, d), F32),
        compiler_params=_params(("parallel",)),
        name="combine",
    )(x, y, y, gate)


def _moe(x, g, router, w1, w3, w2):
    n, d = x.shape
    a = n * TOP_K
    tb = min(MOE_TILE, a)
    router_padded = jnp.pad(router, ((0, 0), (0, 128 - N_EXPERTS)))
    xn, logits = _router(x, g, router_padded)
    top_logit, top_e = lax.top_k(logits[:, :N_EXPERTS], TOP_K)
    gate = jax.nn.softmax(top_logit, axis=-1)
    e_flat = top_e.reshape(a)
    onehot = (e_flat[:, None] == jnp.arange(N_EXPERTS)[None, :]).astype(jnp.int32)
    csum = jnp.cumsum(onehot, axis=0)
    rank = jnp.sum(onehot * csum, axis=1) - 1
    counts = csum[-1]
    padded = (counts + tb - 1) // tb * tb
    pend = jnp.cumsum(padded)
    pstart = pend - padded
    dest = (jnp.sum(onehot * pstart[None, :], axis=1) + rank).astype(jnp.int32)
    n_blocks = a // tb + N_EXPERTS
    p = n_blocks * tb
    slot_asg = jnp.full((p,), -1, jnp.int32).at[dest].set(jnp.arange(a, dtype=jnp.int32))
    is_pad = slot_asg < 0
    slot_row = jnp.where(is_pad, a - 1 + jnp.cumsum(is_pad.astype(jnp.int32)),
                         (slot_asg % TOP_K) * n + slot_asg // TOP_K)
    slot_tok = jnp.where(is_pad, 0, slot_asg // TOP_K)
    block_e = jnp.minimum(jnp.searchsorted(pend, jnp.arange(n_blocks) * tb, side='right'),
                          N_EXPERTS - 1).astype(jnp.int32)
    n_used = (pend[-1] // tb).astype(jnp.int32).reshape(1)
    y = _experts(xn, slot_tok, slot_row, block_e, n_used, w1, w3, w2, tb)
    gate_padded = jnp.pad(gate, ((0, 0), (0, 128 - TOP_K)))
    return _combine(x, y, gate_padded)


def _tile_gain(gain, width, scale=1.0):
    return jnp.tile(gain, width // gain.shape[0]) * scale


def kernel(x, mem, rel_bias, norm_mix, w_in, b_gate, a_q_gain, a_k_gain, pool_w, pool_scale,
           c_q_gain, c_k_gain, c_sinks, w_branch, w_out, norm_cross, norm_mem, w_xq, w_xk, w_xv,
           x_q_gain, x_k_gain, w_xo, norm_ffn, ffn_w1, ffn_w3, ffn_w2, router, moe_w1, moe_w3, moe_w2):
    batch, seq, d = x.shape
    mem_len = mem.shape[1]
    depth = norm_mix.shape[0]
    xs = x.reshape(batch * seq, d)
    mems = mem.reshape(batch * mem_len, d)

    tab_a = rel_bias[:, :A_HEADS].T
    tab_c = rel_bias[:, A_HEADS:].T
    nb = seq // A_BLOCK
    n_near = min(nb, (_saturation_distance() + 2 * A_BLOCK - 2) // A_BLOCK)
    bias_a = _moba_bias_tiles(_bias_by_distance(tab_a, n_near * A_BLOCK), n_near) * LOG2E
    bias_a = jnp.concatenate([bias_a[0::2], bias_a[1::2]], axis=-1)
    far_a = tab_a[:, REL_BUCKETS - 1]
    far_a = jnp.stack([far_a * LOG2E, jnp.exp(far_a)])
    bias_c = _swa_bias_tiles(_bias_by_distance(tab_c, C_WINDOW)) * LOG2E
    seg = np.arange(256) // HEAD_DIM
    bd = jnp.asarray(seg[:, None] == seg[None, :], BF16)

    row = lambda v: v.reshape(1, -1)
    for l in range(depth):
        scale = HEAD_DIM ** -0.5 * LOG2E
        gains = jnp.stack([_tile_gain(a_q_gain[l], WIDTH, scale), _tile_gain(a_k_gain[l], WIDTH),
                           _tile_gain(c_q_gain[l], WIDTH, scale), _tile_gain(c_k_gain[l], WIDTH)])
        flag_a = _bounded_flag(a_q_gain[l], a_k_gain[l], HEAD_DIM, jnp.max(jnp.abs(tab_a)))
        flag_c = _bounded_flag(c_q_gain[l], c_k_gain[l], HEAD_DIM,
                               jnp.maximum(jnp.max(jnp.abs(tab_c)), jnp.max(jnp.abs(c_sinks[l]))))
        qa, ka, va, ub, qc, kc, vc, gates = _in_proj(
            xs, row(norm_mix[l]), w_in[l].astype(BF16), row(b_gate[l]), gains, bd)
        oa = _moba(qa, ka, va, bias_a, far_a, flag_a, batch, seq)
        oc = _swa(qc, kc, vc, bias_c, c_sinks[l] * LOG2E, flag_c, batch, seq)
        k_mem, v_mem = _mem_kv(mems, row(norm_mem[l]), w_xk[l].astype(BF16), w_xv[l].astype(BF16),
                               row(_tile_gain(x_k_gain[l], WIDTH)))
        xs = _merge_cross(xs, oa, ub, oc, gates, pool_w[l].astype(BF16), row(pool_scale[l]),
                          w_branch[l].astype(BF16), w_out[l].astype(BF16), row(norm_cross[l]),
                          w_xq[l].astype(BF16), row(_tile_gain(x_q_gain[l], WIDTH, X_HEAD_DIM ** -0.5)),
                          k_mem, v_mem, w_xo[l].astype(BF16), seq, mem_len)
        i = l // 2
        if l % 2 == 0:
            xs = _ffn(xs, row(norm_ffn[l]), ffn_w1[i].astype(BF16), ffn_w3[i].astype(BF16),
                      ffn_w2[i].astype(BF16))
        else:
            xs = _moe(xs, row(norm_ffn[l]), router[i], moe_w1[i].astype(BF16), moe_w3[i].astype(BF16),
                      moe_w2[i].astype(BF16))
    return xs.reshape(batch, seq, d)
```

```python
import functools
import math

import jax
import jax.numpy as jnp
import numpy as np
from jax import lax
from jax.experimental import pallas as pl
from jax.experimental.pallas import tpu as pltpu

F32 = jnp.float32
BF16 = jnp.bfloat16

HEAD_DIM = 64
A_HEADS = 8
A_BLOCK = 256
A_TOPK = 3
B_GROUPS = 4
B_GROUP_DIM = 128
B_WINDOWS = (2, 4, 8, 16)
C_HEADS = 8
C_KV_HEADS = 2
C_WINDOW = 128
REL_BUCKETS = 32
REL_MAX_DIST = 1024
X_HEADS = 4
X_HEAD_DIM = 128
N_EXPERTS = 8
TOP_K = 2
EPS = 1e-6
NEG_INF = -1e30
LOG2E = math.log2(math.e)

WIDTH = 512
POOL_HALO = 16
ROW_TILE = 512
MOE_TILE = 1024
FF_CHUNK = 512
SUBLANES = 8
KEY_CHUNK = 128
SWA_BLOCKS = 4
VMEM_LIMIT = 56 * 1024 * 1024
EXP2_SAFE = 100.0


def _dot(a, b):
    return jnp.dot(a, b, preferred_element_type=F32)


def _dot_nt(a, b):
    return lax.dot_general(a, b, (((1,), (1,)), ((), ())), preferred_element_type=F32)


def _rms(x, g):
    ms = jnp.mean(x * x, axis=-1, keepdims=True)
    return x * lax.rsqrt(ms + EPS) * g


def _const_spec(shape):
    zeros = (0,) * len(shape)
    return pl.BlockSpec(shape, lambda *_: zeros, pipeline_mode=pl.Buffered(1))


def _smem_spec():
    return pl.BlockSpec(memory_space=pltpu.SMEM)


def _params(sem):
    return pltpu.CompilerParams(dimension_semantics=sem, vmem_limit_bytes=VMEM_LIMIT)


def _rel_bucket(dist):
    n = jnp.maximum(dist, 0)
    max_exact = REL_BUCKETS // 2
    nf = jnp.maximum(n, 1).astype(jnp.float32)
    large = max_exact + (jnp.log(nf / max_exact) / math.log(REL_MAX_DIST / max_exact)
                         * (REL_BUCKETS - max_exact)).astype(jnp.int32)
    large = jnp.minimum(large, REL_BUCKETS - 1)
    return jnp.where(n < max_exact, n, large)


def _saturation_distance():
    ratio = REL_MAX_DIST / (REL_BUCKETS // 2)
    return int(math.ceil((REL_BUCKETS // 2) * ratio ** ((REL_BUCKETS - 1 - REL_BUCKETS // 2 + 0.5)
                                                       / (REL_BUCKETS - REL_BUCKETS // 2))))


def _bias_by_distance(tab, n_dist):
    onehot = (_rel_bucket(jnp.arange(n_dist))[:, None] == jnp.arange(REL_BUCKETS)[None, :]).astype(F32)
    return jnp.einsum('hb,db->hd', tab, onehot, precision=lax.Precision.HIGHEST)


def _skew(g, rows):
    n = g.shape[-1]
    lead = g.shape[:-1]
    tiled = jnp.broadcast_to(g[..., None, :], lead + (rows, n)).reshape(lead + (rows * n,))
    return tiled[..., :rows * (n - 1)].reshape(lead + (rows, n - 1))


def _moba_bias_tiles(bvec, n_near):
    L = A_BLOCK
    rows = []
    for delta in range(n_near):
        lo = delta * L - (L - 1)
        seg = bvec[:, max(lo, 0):delta * L + L]
        if lo < 0:
            seg = jnp.concatenate([jnp.zeros((bvec.shape[0], -lo), F32), seg], axis=1)
        rows.append(jnp.pad(seg, ((0, 0), (0, 1))))
    g = jnp.stack(rows, axis=1)
    return _skew(g, L)[..., L - 1:]


def _swa_bias_tiles(bvec):
    W = C_WINDOW
    u = np.arange(3 * W)
    g = bvec[:, np.clip(2 * W - 1 - u, 0, W - 1)]
    return _skew(g, W)[..., W - 1:3 * W - 1]


def _bounded_flag(q_gain, k_gain, head_dim, extra):
    bound = head_dim ** 0.5 * jnp.max(jnp.abs(q_gain)) * jnp.max(jnp.abs(k_gain)) * 1.02 + extra
    return (bound * LOG2E < EXP2_SAFE).astype(jnp.int32).reshape(1)


def _in_proj_kernel(x_ref, g_ref, w_ref, bg_ref, gn_ref, bd_ref,
                    qa_ref, ka_ref, va_ref, ub_ref, qc_ref, kc_ref, vc_ref, gt_ref):
    xb = _rms(x_ref[...], g_ref[...]).astype(BF16)
    bd = bd_ref[...]

    def proj(c0, width):
        return _dot(xb, w_ref[:, c0:c0 + width])

    def head_norm(t, gain):
        outs = []
        for c in range(0, t.shape[1], 256):
            wd = min(256, t.shape[1] - c)
            tc = t[:, c:c + wd]
            ss = _dot((tc * tc).astype(BF16), bd[:wd, :wd])
            outs.append(tc * lax.rsqrt(ss * (1.0 / HEAD_DIM) + EPS))
        y = outs[0] if len(outs) == 1 else jnp.concatenate(outs, axis=-1)
        return y * gain

    def tile_kv_heads(t):
        lane = lax.broadcasted_iota(jnp.int32, t.shape, 1)
        r = pltpu.roll(t, HEAD_DIM, axis=1)
        h0 = jnp.where(lane < HEAD_DIM, t, r)
        h1 = jnp.where(lane < HEAD_DIM, r, t)
        return jnp.concatenate([h0, h0, h1, h1], axis=-1)

    qa_ref[...] = head_norm(proj(0, WIDTH), gn_ref[0:1, :]).astype(BF16)
    ka_ref[...] = head_norm(proj(WIDTH, WIDTH), gn_ref[1:2, :]).astype(BF16)
    va_ref[...] = proj(2 * WIDTH, WIDTH).astype(BF16)
    ub_ref[...] = proj(3 * WIDTH, WIDTH)
    qc_ref[...] = head_norm(proj(4 * WIDTH, WIDTH), gn_ref[2:3, :]).astype(BF16)
    kv = C_KV_HEADS * HEAD_DIM
    c0 = 5 * WIDTH
    kc = head_norm(proj(c0, kv), gn_ref[3:4, :kv])
    kc_ref[...] = tile_kv_heads(kc).astype(BF16)
    vc_ref[...] = tile_kv_heads(proj(c0 + kv, kv)).astype(BF16)
    c0 += 2 * kv
    for c in range(0, gt_ref.shape[1], WIDTH):
        gl = proj(c0 + c, WIDTH) + bg_ref[:, c:c + WIDTH]
        gt_ref[:, c:c + WIDTH] = jax.nn.sigmoid(gl).astype(BF16)


def _in_proj(x, g, w_in, b_gate, gains, bd):
    n, d = x.shape
    tm = min(ROW_TILE, n)
    n_gate = b_gate.shape[1]
    row = lambda width: pl.BlockSpec((tm, width), lambda i: (i, 0))
    out_shape = [jax.ShapeDtypeStruct((n, WIDTH), BF16)] * 3 + [jax.ShapeDtypeStruct((n, WIDTH), F32)] \
        + [jax.ShapeDtypeStruct((n, WIDTH), BF16)] * 3 + [jax.ShapeDtypeStruct((n, n_gate), BF16)]
    return pl.pallas_call(
        _in_proj_kernel,
        grid=(n // tm,),
        in_specs=[row(d), _const_spec(g.shape), _const_spec(w_in.shape), _const_spec(b_gate.shape),
                  _const_spec(gains.shape), _const_spec(bd.shape)],
        out_specs=[row(WIDTH)] * 7 + [row(n_gate)],
        out_shape=out_shape,
        compiler_params=_params(("parallel",)),
        name="in_proj",
    )(x, g, w_in, b_gate, gains, bd)


def _moba_kernel(flag_ref, far_ref, q_ref, k_ref, v_ref, bias_ref, o_ref,
                 kmean_ref, vt_ref, sel_ref, m_ref, l_ref, l8_ref, acc_ref, s_ref, *, nb, n_near):
    i = pl.program_id(1)
    L = A_BLOCK
    KC = KEY_CHUNK
    PW = 2 * HEAD_DIM
    nbp = kmean_ref.shape[0]

    @pl.when(i == 0)
    def _():
        kmean_ref[...] = jnp.zeros(kmean_ref.shape, F32)
        for j in range(nb):
            kj = k_ref[j * L:(j + 1) * L, :].astype(F32)
            kmean_ref[j:j + 1, :] = jnp.mean(kj, axis=0, keepdims=True)
            vt_ref[j] = v_ref[j * L:(j + 1) * L, :].astype(F32).T.astype(BF16)

    lane = lax.broadcasted_iota(jnp.int32, (L, PW), 1)
    blk = lax.broadcasted_iota(jnp.int32, (nbp, L), 0)
    past = blk < i
    qh = []
    for h in range(A_HEADS):
        pair = slice((h // 2) * PW, (h // 2 + 1) * PW)
        q = q_ref[:, pair]
        qm = jnp.where(lane // HEAD_DIM == h % 2, q, jnp.zeros_like(q))
        qh.append(qm)
        km = kmean_ref[:, pair]
        km_hi = km.astype(BF16)
        km_lo = (km - km_hi.astype(F32)).astype(BF16)
        s = jnp.where(past, _dot_nt(km_hi, qm) + _dot_nt(km_lo, qm), -jnp.inf)
        rank = jnp.zeros((nbp, L), jnp.int32)
        for jp in range(nb):
            sj = s[jp:jp + 1, :]
            ahead = (sj > s) | ((sj == s) & (jp < blk))
            rank = rank + ahead.astype(jnp.int32)
        sel_ref[h] = (past & (rank < A_TOPK)).astype(F32)
        m_ref[h] = jnp.full((1, L), NEG_INF, F32)
        l_ref[h] = jnp.zeros((1, L), F32)
        l8_ref[h] = jnp.zeros((8, L), F32)
    acc_ref[...] = jnp.zeros(acc_ref.shape, F32)
    qpair_t = [jnp.concatenate(qh[2 * hp:2 * hp + 2], axis=0).astype(F32).T.astype(BF16)
               for hp in range(A_HEADS // 2)]

    def causal(c, n):
        kk = c + lax.broadcasted_iota(jnp.int32, (n, L), 0)
        qq = lax.broadcasted_iota(jnp.int32, (n, L), 1)
        return kk <= qq

    def bounded_tile(j, delta, far, own):
        units = [(hp, c) for hp in range(A_HEADS // 2) for c in range(0, L, KC)]

        def qk(u):
            hp, c = units[u]
            kc = k_ref[pl.ds(pl.multiple_of(j * L + c, KC), KC), hp * PW:(hp + 1) * PW]
            s_ref[u % 2] = _dot(kc, qpair_t[hp])

        qk(0)
        o = ps = None
        for u, (hp, c) in enumerate(units):
            if u + 1 < len(units):
                qk(u + 1)
            s = s_ref[u % 2]
            pair = slice(hp * PW, (hp + 1) * PW)
            if c == 0:
                o = [jnp.zeros((HEAD_DIM, L), F32)] * 2
                ps = jnp.zeros((8, 2 * L), F32)
            if not far:
                s = s + bias_ref[hp, delta, pl.ds(c, KC), :]
            p = jnp.exp2(s)
            if own:
                p = jnp.where(jnp.concatenate([causal(c, KC)] * 2, axis=1), p, 0.0)
            ps = ps + jnp.sum(p.reshape(KC // 8, 8, 2 * L), axis=0)
            pb = p.astype(BF16)
            o = [o[hh] + _dot(vt_ref[j, pl.ds((2 * hp + hh) * HEAD_DIM, HEAD_DIM), pl.ds(c, KC)],
                              pb[:, hh * L:(hh + 1) * L]) for hh in range(2)]
            if c + KC < L:
                continue
            for hh in range(2):
                h = 2 * hp + hh
                rows = slice(h * HEAD_DIM, (h + 1) * HEAD_DIM)
                oh = o[hh]
                psh = ps[:, hh * L:(hh + 1) * L]
                if own:
                    acc_ref[rows, :] += oh
                    l8_ref[h] += psh
                else:
                    w = sel_ref[h, pl.ds(j, 1), :]
                    if far:
                        w = w * far_ref[1, h]
                    acc_ref[rows, :] += w * oh
                    l8_ref[h] += w * psh

    def online_tile(j, delta, far, own):
        for h in range(A_HEADS):
            pair = slice((h // 2) * PW, (h // 2 + 1) * PW)
            rows = slice(h * HEAD_DIM, (h + 1) * HEAD_DIM)
            s = _dot_nt(k_ref[pl.ds(pl.multiple_of(j * L, L), L), pair], qh[h])
            if not far:
                s = s + bias_ref[h // 2, delta, :, (h % 2) * L:(h % 2 + 1) * L]
            if own:
                s = jnp.where(causal(0, L), s, NEG_INF)
            mj = jnp.max(s, axis=0, keepdims=True)
            p = jnp.exp2(s - mj)
            lj = jnp.sum(p, axis=0, keepdims=True)
            o = _dot(vt_ref[j, rows, :], p.astype(BF16))
            if far:
                mj = mj + far_ref[0, h]
            m_old = m_ref[h]
            if own:
                m_new = jnp.maximum(m_old, mj)
                beta = jnp.exp2(mj - m_new)
            else:
                on = sel_ref[h, pl.ds(j, 1), :] > 0.5
                m_new = jnp.where(on, jnp.maximum(m_old, mj), m_old)
                beta = jnp.where(on, jnp.exp2(mj - m_new), 0.0)
            alpha = jnp.exp2(m_old - m_new)
            m_ref[h] = m_new
            l_ref[h] = alpha * l_ref[h] + beta * lj
            acc_ref[rows, :] = alpha * acc_ref[rows, :] + beta * o

    n_far = jnp.maximum(i - (n_near - 1), 0)

    def run(tile):
        def far_body(j, c):
            tile(j, None, True, False)
            return c

        def near_body(j, c):
            tile(j, i - j, False, False)
            return c

        lax.fori_loop(0, n_far, far_body, 0)
        lax.fori_loop(n_far, i, near_body, 0)
        tile(i, 0, False, True)

    @pl.when(flag_ref[0] == 1)
    def _():
        run(bounded_tile)
        for h in range(A_HEADS):
            l_ref[h] = jnp.sum(l8_ref[h], axis=0, keepdims=True)

    @pl.when(flag_ref[0] != 1)
    def _():
        run(online_tile)

    o_t = jnp.concatenate([acc_ref[h * HEAD_DIM:(h + 1) * HEAD_DIM, :] / l_ref[h] for h in range(A_HEADS)],
                          axis=0)
    o_ref[...] = o_t.T.astype(BF16)


def _moba(q, k, v, bias_tiles, far, flag, batch, seq):
    n = q.shape[0]
    L = A_BLOCK
    nb = seq // L
    nbp = max(8, -(-nb // 8) * 8)
    n_near = bias_tiles.shape[1]
    kern = functools.partial(_moba_kernel, nb=nb, n_near=n_near)
    return pl.pallas_call(
        kern,
        grid=(batch, nb),
        in_specs=[_smem_spec(), _smem_spec(),
                  pl.BlockSpec((L, WIDTH), lambda b, i: (b * nb + i, 0)),
                  pl.BlockSpec((seq, WIDTH), lambda b, i: (b, 0)),
                  pl.BlockSpec((seq, WIDTH), lambda b, i: (b, 0)),
                  _const_spec(bias_tiles.shape)],
        out_specs=pl.BlockSpec((L, WIDTH), lambda b, i: (b * nb + i, 0)),
        out_shape=jax.ShapeDtypeStruct((n, WIDTH), BF16),
        scratch_shapes=[pltpu.VMEM((nbp, WIDTH), F32),
                        pltpu.VMEM((nb, WIDTH, L), BF16),
                        pltpu.VMEM((A_HEADS, nbp, L), F32),
                        pltpu.VMEM((A_HEADS, 1, L), F32),
                        pltpu.VMEM((A_HEADS, 1, L), F32),
                        pltpu.VMEM((A_HEADS, 8, L), F32),
                        pltpu.VMEM((WIDTH, L), F32),
                        pltpu.VMEM((2, KEY_CHUNK, 2 * L), F32)],
        compiler_params=_params(("arbitrary", "arbitrary")),
        name="moba",
    )(flag, far, q, k, v, bias_tiles)


def _swa_kernel(flag_ref, sink_ref, q_ref, kp_ref, kc_ref, vp_ref, vc_ref, bias_ref, o_ref):
    g = pl.program_id(0)
    chunk = pl.program_id(2)
    W = C_WINDOW
    G = C_HEADS // C_KV_HEADS
    nsub = q_ref.shape[0] // W
    gw = q_ref.shape[1]
    kall = jnp.concatenate([kp_ref[...], kc_ref[...]], axis=0)
    vall = jnp.concatenate([vp_ref[...], vc_ref[...]], axis=0)
    lane = lax.broadcasted_iota(jnp.int32, (W, gw), 1)
    qi = lax.broadcasted_iota(jnp.int32, (W, 2 * W), 0)
    kj = lax.broadcasted_iota(jnp.int32, (W, 2 * W), 1)
    dist = W + qi - kj
    band = (dist >= 0) & (dist < W)

    def heads(bounded):
        for r in range(nsub):
            q = q_ref[r * W:(r + 1) * W, :]
            kcat = kall[r * W:(r + 2) * W]
            vcat = vall[r * W:(r + 2) * W]
            mask = band & ((chunk > 0) | (kj >= W)) if r == 0 else band
            out = jnp.zeros(q.shape, F32)
            for hh in range(G):
                mine = lane // HEAD_DIM == hh
                qm = jnp.where(mine, q, jnp.zeros_like(q))
                s = _dot_nt(qm, kcat) + bias_ref[hh]
                sink = sink_ref[g * G + hh]
                if bounded:
                    pb = jnp.where(mask, jnp.exp2(s), 0.0).astype(BF16)
                    den = _dot(pb, jnp.ones((2 * W, 128), BF16)) + jnp.exp2(jnp.full((1, 128), sink, F32))
                    inv = 1.0 / den
                    o = _dot(pb, vcat) * jnp.concatenate([inv] * (gw // 128), axis=-1)
                else:
                    s = jnp.where(mask, s, NEG_INF)
                    m = jnp.maximum(jnp.max(s, axis=-1, keepdims=True), sink)
                    p = jnp.exp2(s - m)
                    den = jnp.sum(p, axis=-1, keepdims=True) + jnp.exp2(sink - m)
                    o = _dot(p.astype(BF16), vcat) / den
                out = jnp.where(mine, o, out)
            o_ref[r * W:(r + 1) * W, :] = out.astype(BF16)

    @pl.when(flag_ref[0] == 1)
    def _():
        heads(True)

    @pl.when(flag_ref[0] != 1)
    def _():
        heads(False)


def _swa(q, k_t, v_t, bias_tiles, sinks, flag, batch, seq):
    n = q.shape[0]
    W = C_WINDOW
    G = C_HEADS // C_KV_HEADS
    gw = G * HEAD_DIM
    nsub = min(SWA_BLOCKS, seq // W)
    nchunk = seq // (W * nsub)
    cur = lambda g, b, j: (b * nchunk + j, g)
    prev = lambda g, b, j: (jnp.maximum((b * nchunk + j) * nsub - 1, 0), g)
    return pl.pallas_call(
        _swa_kernel,
        grid=(C_KV_HEADS, batch, nchunk),
        in_specs=[_smem_spec(), _smem_spec(),
                  pl.BlockSpec((nsub * W, gw), cur),
                  pl.BlockSpec((W, gw), prev), pl.BlockSpec((nsub * W, gw), cur),
                  pl.BlockSpec((W, gw), prev), pl.BlockSpec((nsub * W, gw), cur),
                  pl.BlockSpec((G, W, 2 * W), lambda g, b, j: (g, 0, 0))],
        out_specs=pl.BlockSpec((nsub * W, gw), cur),
        out_shape=jax.ShapeDtypeStruct((n, WIDTH), BF16),
        compiler_params=_params(("arbitrary", "arbitrary", "arbitrary")),
        name="swa",
    )(flag, sinks, q, k_t, k_t, v_t, v_t, bias_tiles)


def _seg_norm(t, gain, seg):
    outs = []
    for c in range(0, t.shape[1], seg):
        tc = t[:, c:c + seg]
        outs.append(tc * lax.rsqrt(jnp.mean(tc * tc, axis=-1, keepdims=True) + EPS))
    return jnp.concatenate(outs, axis=-1) * gain


def _mem_kv_kernel(mem_ref, g_ref, wk_ref, wv_ref, kg_ref, k_ref, v_ref):
    mb = _rms(mem_ref[...], g_ref[...]).astype(BF16)
    k_ref[...] = _seg_norm(_dot(mb, wk_ref[...]), kg_ref[...], X_HEAD_DIM).astype(BF16)
    v_ref[...] = _dot(mb, wv_ref[...]).astype(BF16)


def _mem_kv(mem, g, w_xk, w_xv, k_gain):
    n, d = mem.shape
    tm = min(ROW_TILE, n)
    row = lambda width: pl.BlockSpec((tm, width), lambda i: (i, 0))
    return pl.pallas_call(
        _mem_kv_kernel,
        grid=(n // tm,),
        in_specs=[row(d), _const_spec(g.shape), _const_spec(w_xk.shape), _const_spec(w_xv.shape),
                  _const_spec(k_gain.shape)],
        out_specs=[row(WIDTH), row(WIDTH)],
        out_shape=[jax.ShapeDtypeStruct((n, WIDTH), BF16)] * 2,
        compiler_params=_params(("parallel",)),
        name="mem_kv",
    )(mem, g, w_xk, w_xv, k_gain)


def _merge_cross_kernel(x_ref, oa_ref, ub_ref, halo_ref, oc_ref, gt_ref, pw_ref, ps_ref, wb_ref, wo_ref,
                        gx_ref, wq_ref, qg_ref, km_ref, vm_ref, wxo_ref, o_ref, *, seq):
    tm = x_ref.shape[0]
    d = x_ref.shape[1]
    t0 = (pl.program_id(0) * tm) % seq
    H = POOL_HALO

    halo = jnp.where(t0 > 0, halo_ref[...], 0.0)
    pos = t0 + lax.broadcasted_iota(jnp.int32, (tm, B_GROUP_DIM), 0)
    mixed = []
    for gi, win in enumerate(B_WINDOWS):
        cols = slice(gi * B_GROUP_DIM, (gi + 1) * B_GROUP_DIM)
        cur = ub_ref[:, cols]
        acc = jnp.concatenate([halo[:, cols], cur], axis=0)
        step = 1
        while step < win:
            acc = acc + pltpu.roll(acc, step, axis=0)
            step *= 2
        cnt = jnp.minimum(pos + 1, win).astype(F32)
        pooled = acc[H:, :] / cnt - cur
        mixed.append(_dot(pooled.astype(BF16), pw_ref[gi]))
    ob = jnp.concatenate(mixed, axis=-1) * ps_ref[...]

    merged = gt_ref[:, 0:d].astype(F32) * _dot(oa_ref[...], wb_ref[0])
    merged = merged + gt_ref[:, d:2 * d].astype(F32) * _dot(ob.astype(BF16), wb_ref[1])
    merged = merged + gt_ref[:, 2 * d:3 * d].astype(F32) * _dot(oc_ref[...], wb_ref[2])
    x1 = x_ref[...] + _dot(merged.astype(BF16), wo_ref[...])

    xb = _rms(x1, gx_ref[...]).astype(BF16)
    qn = _seg_norm(_dot(xb, wq_ref[...]), qg_ref[...], X_HEAD_DIM).astype(BF16)
    heads = []
    for h in range(X_HEADS):
        cols = slice(h * X_HEAD_DIM, (h + 1) * X_HEAD_DIM)
        s = _dot_nt(qn[:, cols], km_ref[:, cols])
        p = jnp.exp(s - jnp.max(s, axis=-1, keepdims=True))
        inv = 1.0 / jnp.sum(p, axis=-1, keepdims=True)
        heads.append(_dot(p.astype(BF16), vm_ref[:, cols]) * inv)
    o = jnp.concatenate(heads, axis=-1).astype(BF16)
    o_ref[...] = x1 + _dot(o, wxo_ref[...])


def _merge_cross(x, oa, ub, oc, gates, pool_w, pool_scale, w_branch, w_out,
                 g_cross, w_xq, q_gain, k_mem, v_mem, w_xo, seq, mem_len):
    n, d = x.shape
    tm = min(ROW_TILE, seq)
    row = lambda width: pl.BlockSpec((tm, width), lambda i: (i, 0))
    halo = pl.BlockSpec((POOL_HALO, WIDTH), lambda i: (jnp.maximum(i * (tm // POOL_HALO) - 1, 0), 0))
    mem = pl.BlockSpec((mem_len, WIDTH), lambda i: ((i * tm) // seq, 0))
    kern = functools.partial(_merge_cross_kernel, seq=seq)
    return pl.pallas_call(
        kern,
        grid=(n // tm,),
        in_specs=[row(d), row(WIDTH), row(WIDTH), halo, row(WIDTH), row(gates.shape[1]),
                  _const_spec(pool_w.shape), _const_spec(pool_scale.shape), _const_spec(w_branch.shape),
                  _const_spec(w_out.shape), _const_spec(g_cross.shape), _const_spec(w_xq.shape),
                  _const_spec(q_gain.shape), mem, mem, _const_spec(w_xo.shape)],
        out_specs=row(d),
        out_shape=jax.ShapeDtypeStruct((n, d), F32),
        compiler_params=_params(("parallel",)),
        name="merge_cross",
    )(x, oa, ub, ub, oc, gates, pool_w, pool_scale, w_branch, w_out,
      g_cross, w_xq, q_gain, k_mem, v_mem, w_xo)


def _swiglu_chunks(xb, w1_ref, w3_ref, w2_ref, acc):
    for c in range(0, w1_ref.shape[1], FF_CHUNK):
        h1 = _dot(xb, w1_ref[:, c:c + FF_CHUNK])
        h3 = _dot(xb, w3_ref[:, c:c + FF_CHUNK])
        acc = acc + _dot((jax.nn.silu(h1) * h3).astype(BF16), w2_ref[c:c + FF_CHUNK, :])
    return acc


def _ffn_kernel(x_ref, g_ref, w1_ref, w3_ref, w2_ref, o_ref):
    x = x_ref[...]
    xb = _rms(x, g_ref[...]).astype(BF16)
    o_ref[...] = _swiglu_chunks(xb, w1_ref, w3_ref, w2_ref, x)


def _ffn(x, g, w1, w3, w2):
    n, d = x.shape
    tm = min(ROW_TILE, n)
    row = pl.BlockSpec((tm, d), lambda i: (i, 0))
    return pl.pallas_call(
        _ffn_kernel,
        grid=(n // tm,),
        in_specs=[row, _const_spec(g.shape), _const_spec(w1.shape), _const_spec(w3.shape),
                  _const_spec(w2.shape)],
        out_specs=row,
        out_shape=jax.ShapeDtypeStruct((n, d), F32),
        compiler_params=_params(("parallel",)),
        name="ffn_dense",
    )(x, g, w1, w3, w2)


def _store_token_tiles(ref, x):
    rows = x.shape[0]
    for k in range(x.shape[1] // 128):
        ref[pl.ds(k, rows, stride=SUBLANES), :] = x[:, k * 128:(k + 1) * 128]


def _load_token_tiles(ref, rows, lead=None):
    idx = (lambda k: (pl.ds(k, rows, stride=SUBLANES), slice(None))) if lead is None else \
        (lambda k: (lead, pl.ds(k, rows, stride=SUBLANES), slice(None)))
    return jnp.concatenate([ref[idx(k)] for k in range(SUBLANES)], axis=-1)


def _router_kernel(x_ref, g_ref, r_ref, xn_ref, lg_ref):
    xn = _rms(x_ref[...], g_ref[...])
    _store_token_tiles(xn_ref, xn)
    r = r_ref[...]
    x_hi = xn.astype(BF16)
    x_lo = (xn - x_hi.astype(F32)).astype(BF16)
    r_hi = r.astype(BF16)
    r_lo = (r - r_hi.astype(F32)).astype(BF16)
    lg_ref[...] = _dot(x_hi, r_hi) + (_dot(x_lo, r_hi) + _dot(x_hi, r_lo))


def _router(x, g, router_padded):
    n, d = x.shape
    tm = min(ROW_TILE, n)
    row = lambda width: pl.BlockSpec((tm, width), lambda i: (i, 0))
    return pl.pallas_call(
        _router_kernel,
        grid=(n // tm,),
        in_specs=[row(d), _const_spec(g.shape), _const_spec(router_padded.shape)],
        out_specs=[pl.BlockSpec((tm * SUBLANES, d // SUBLANES), lambda i: (i, 0)),
                   row(router_padded.shape[1])],
        out_shape=[jax.ShapeDtypeStruct((n * SUBLANES, d // SUBLANES), F32),
                   jax.ShapeDtypeStruct((n, router_padded.shape[1]), F32)],
        compiler_params=_params(("parallel",)),
        name="router",
    )(x, g, router_padded)


def _expert_kernel(be_ref, nu_ref, tok_ref, tokn_ref, asg_ref, asgp_ref, xn_hbm, w1_ref, w3_ref, w2_ref, y_hbm,
                   xg_ref, xb_ref, y_ref, ys_ref, gsem, ssem, *, nb, nc):
    b = pl.program_id(0)
    c = pl.program_id(1)
    tb = xb_ref.shape[0]
    T = SUBLANES
    U = 8
    groups = tb // U
    piece = -(-groups // nc)
    slot = b % 2

    def tile(ref, first_row):
        return ref.at[pl.ds(pl.multiple_of(first_row, T), T), :]

    def gather(idx_ref, s, lo, hi):
        def body(g, carry):
            for k in range(U):
                r = g * U + k
                pltpu.make_async_copy(tile(xn_hbm, idx_ref[0, 0, r]), tile(xg_ref.at[s], r * T),
                                      gsem.at[s]).start()
            return carry
        lax.fori_loop(lo, hi, body, 0)

    def wait_gather(s):
        pltpu.make_async_copy(xn_hbm.at[pl.ds(0, tb * T), :], xg_ref.at[s], gsem.at[s]).wait()

    def scatter(idx_ref, s, lo, hi):
        def body(g, carry):
            for k in range(U):
                r = g * U + k
                pltpu.make_async_copy(tile(ys_ref.at[s], r * T), tile(y_hbm, idx_ref[0, 0, r]),
                                      ssem.at[s]).start()
            return carry
        lax.fori_loop(lo, hi, body, 0)

    def wait_scatter(s):
        pltpu.make_async_copy(ys_ref.at[s], y_hbm.at[pl.ds(0, tb * T), :], ssem.at[s]).wait()

    @pl.when(c == 0)
    def _():
        @pl.when(b == 0)
        def _():
            gather(tok_ref, 0, 0, groups)

        wait_gather(slot)
        xb_ref[...] = _load_token_tiles(xg_ref, tb, lead=slot).astype(BF16)
        y_ref[...] = jnp.zeros(y_ref.shape, F32)

    lo = c * piece
    hi = jnp.minimum(lo + piece, groups)

    @pl.when(b + 1 < nb)
    def _():
        gather(tokn_ref, 1 - slot, lo, hi)

    @pl.when(b >= 1)
    def _():
        scatter(asgp_ref, 1 - slot, lo, hi)

    @pl.when(b < nu_ref[0])
    def _():
        xb = xb_ref[...]
        h1 = _dot(xb, w1_ref[...])
        h3 = _dot(xb, w3_ref[...])
        y_ref[...] += _dot((jax.nn.silu(h1) * h3).astype(BF16), w2_ref[...])

    @pl.when(c == nc - 1)
    def _():
        _store_token_tiles(ys_ref.at[slot], y_ref[...])

        @pl.when(b >= 1)
        def _():
            wait_scatter(1 - slot)

        @pl.when(b == nb - 1)
        def _():
            scatter(asg_ref, slot, 0, groups)
            wait_scatter(slot)


def _experts(xn, slot_tok, slot_row, block_e, n_used, w1, w3, w2, tb):
    T = SUBLANES
    d = w1.shape[1]
    p = slot_tok.shape[0]
    ff = w1.shape[2]
    nc = ff // FF_CHUNK
    nblk = p // tb
    tok3 = (slot_tok * T).reshape(nblk, 1, tb)
    row3 = (slot_row * T).reshape(nblk, 1, tb)
    idx_spec = lambda imap: pl.BlockSpec((1, 1, tb), imap, memory_space=pltpu.SMEM)
    chunk = lambda b, c, nu: jnp.where(b < nu[0], c, nc - 1)
    grid_spec = pltpu.PrefetchScalarGridSpec(
        num_scalar_prefetch=2,
        grid=(nblk, nc),
        in_specs=[idx_spec(lambda b, c, be, nu: (b, 0, 0)),
                  idx_spec(lambda b, c, be, nu: (jnp.minimum(b + 1, nblk - 1), 0, 0)),
                  idx_spec(lambda b, c, be, nu: (b, 0, 0)),
                  idx_spec(lambda b, c, be, nu: (jnp.maximum(b - 1, 0), 0, 0)),
                  pl.BlockSpec(memory_space=pl.ANY),
                  pl.BlockSpec((None, d, FF_CHUNK), lambda b, c, be, nu: (be[b], 0, chunk(b, c, nu))),
                  pl.BlockSpec((None, d, FF_CHUNK), lambda b, c, be, nu: (be[b], 0, chunk(b, c, nu))),
                  pl.BlockSpec((None, FF_CHUNK, d), lambda b, c, be, nu: (be[b], chunk(b, c, nu), 0))],
        out_specs=pl.BlockSpec(memory_space=pl.ANY),
        scratch_shapes=[pltpu.VMEM((2, tb * T, d // T), F32), pltpu.VMEM((tb, d), BF16), pltpu.VMEM((tb, d), F32),
                        pltpu.VMEM((2, tb * T, d // T), F32),
                        pltpu.SemaphoreType.DMA((2,)), pltpu.SemaphoreType.DMA((2,))],
    )
    return pl.pallas_call(
        functools.partial(_expert_kernel, nb=nblk, nc=nc),
        grid_spec=grid_spec,
        out_shape=jax.ShapeDtypeStruct((p * T, d // T), F32),
        compiler_params=_params(("arbitrary", "arbitrary")),
        name="experts",
    )(block_e, n_used, tok3, tok3, row3, row3, xn, w1, w3, w2)


def _combine_kernel(x_ref, y0_ref, y1_ref, g_ref, o_ref):
    tm = x_ref.shape[0]
    g = g_ref[...]
    o_ref[...] = x_ref[...] + g[:, 0:1] * _load_token_tiles(y0_ref, tm) + g[:, 1:2] * _load_token_tiles(y1_ref, tm)


def _combine(x, y, gate):
    n, d = x.shape
    tm = min(ROW_TILE, n)
    T = SUBLANES
    row = lambda width: pl.BlockSpec((tm, width), lambda i: (i, 0))
    tiles = lambda choice: pl.BlockSpec((tm * T, d // T), lambda i: (choice * (n // tm) + i, 0))
    return pl.pallas_call(
        _combine_kernel,
        grid=(n // tm,),
        in_specs=[row(d), tiles(0), tiles(1), row(gate.shape[1])],
        out_specs=row(d),
        out_shape=jax.ShapeDtypeStruct((n, d), F32),
        compiler_params=_params(("parallel",)),
        name="combine",
    )(x, y, y, gate)


def _moe(x, g, router, w1, w3, w2):
    n, d = x.shape
    a = n * TOP_K
    tb = min(MOE_TILE, a)
    router_padded = jnp.pad(router, ((0, 0), (0, 128 - N_EXPERTS)))
    xn, logits = _router(x, g, router_padded)
    top_logit, top_e = lax.top_k(logits[:, :N_EXPERTS], TOP_K)
    gate = jax.nn.softmax(top_logit, axis=-1)
    e_flat = top_e.reshape(a)
    onehot = (e_flat[:, None] == jnp.arange(N_EXPERTS)[None, :]).astype(jnp.int32)
    csum = jnp.cumsum(onehot, axis=0)
    rank = jnp.sum(onehot * csum, axis=1) - 1
    counts = csum[-1]
    padded = (counts + tb - 1) // tb * tb
    pend = jnp.cumsum(padded)
    pstart = pend - padded
    dest = (jnp.sum(onehot * pstart[None, :], axis=1) + rank).astype(jnp.int32)
    n_blocks = a // tb + N_EXPERTS
    p = n_blocks * tb
    slot_asg = jnp.full((p,), -1, jnp.int32).at[dest].set(jnp.arange(a, dtype=jnp.int32))
    is_pad = slot_asg < 0
    slot_row = jnp.where(is_pad, a - 1 + jnp.cumsum(is_pad.astype(jnp.int32)),
                         (slot_asg % TOP_K) * n + slot_asg // TOP_K)
    slot_tok = jnp.where(is_pad, 0, slot_asg // TOP_K)
    block_e = jnp.minimum(jnp.searchsorted(pend, jnp.arange(n_blocks) * tb, side='right'),
                          N_EXPERTS - 1).astype(jnp.int32)
    n_used = (pend[-1] // tb).astype(jnp.int32).reshape(1)
    y = _experts(xn, slot_tok, slot_row, block_e, n_used, w1, w3, w2, tb)
    gate_padded = jnp.pad(gate, ((0, 0), (0, 128 - TOP_K)))
    return _combine(x, y, gate_padded)


def _tile_gain(gain, width, scale=1.0):
    return jnp.tile(gain, width // gain.shape[0]) * scale


def kernel(x, mem, rel_bias, norm_mix, w_in, b_gate, a_q_gain, a_k_gain, pool_w, pool_scale,
           c_q_gain, c_k_gain, c_sinks, w_branch, w_out, norm_cross, norm_mem, w_xq, w_xk, w_xv,
           x_q_gain, x_k_gain, w_xo, norm_ffn, ffn_w1, ffn_w3, ffn_w2, router, moe_w1, moe_w3, moe_w2):
    batch, seq, d = x.shape
    mem_len = mem.shape[1]
    depth = norm_mix.shape[0]
    xs = x.reshape(batch * seq, d)
    mems = mem.reshape(batch * mem_len, d)

    tab_a = rel_bias[:, :A_HEADS].T
    tab_c = rel_bias[:, A_HEADS:].T
    nb = seq // A_BLOCK
    n_near = min(nb, (_saturation_distance() + 2 * A_BLOCK - 2) // A_BLOCK)
    bias_a = _moba_bias_tiles(_bias_by_distance(tab_a, n_near * A_BLOCK), n_near) * LOG2E
    bias_a = jnp.concatenate([bias_a[0::2], bias_a[1::2]], axis=-1)
    far_a = tab_a[:, REL_BUCKETS - 1]
    far_a = jnp.stack([far_a * LOG2E, jnp.exp(far_a)])
    bias_c = _swa_bias_tiles(_bias_by_distance(tab_c, C_WINDOW)) * LOG2E
    seg = np.arange(256) // HEAD_DIM
    bd = jnp.asarray(seg[:, None] == seg[None, :], BF16)

    row = lambda v: v.reshape(1, -1)
    for l in range(depth):
        scale = HEAD_DIM ** -0.5 * LOG2E
        gains = jnp.stack([_tile_gain(a_q_gain[l], WIDTH, scale), _tile_gain(a_k_gain[l], WIDTH),
                           _tile_gain(c_q_gain[l], WIDTH, scale), _tile_gain(c_k_gain[l], WIDTH)])
        flag_a = _bounded_flag(a_q_gain[l], a_k_gain[l], HEAD_DIM, jnp.max(jnp.abs(tab_a)))
        flag_c = _bounded_flag(c_q_gain[l], c_k_gain[l], HEAD_DIM,
                               jnp.maximum(jnp.max(jnp.abs(tab_c)), jnp.max(jnp.abs(c_sinks[l]))))
        qa, ka, va, ub, qc, kc, vc, gates = _in_proj(
            xs, row(norm_mix[l]), w_in[l].astype(BF16), row(b_gate[l]), gains, bd)
        oa = _moba(qa, ka, va, bias_a, far_a, flag_a, batch, seq)
        oc = _swa(qc, kc, vc, bias_c, c_sinks[l] * LOG2E, flag_c, batch, seq)
        k_mem, v_mem = _mem_kv(mems, row(norm_mem[l]), w_xk[l].astype(BF16), w_xv[l].astype(BF16),
                               row(_tile_gain(x_k_gain[l], WIDTH)))
        xs = _merge_cross(xs, oa, ub, oc, gates, pool_w[l].astype(BF16), row(pool_scale[l]),
                          w_branch[l].astype(BF16), w_out[l].astype(BF16), row(norm_cross[l]),
                          w_xq[l].astype(BF16), row(_tile_gain(x_q_gain[l], WIDTH, X_HEAD_DIM ** -0.5)),
                          k_mem, v_mem, w_xo[l].astype(BF16), seq, mem_len)
        i = l // 2
        if l % 2 == 0:
            xs = _ffn(xs, row(norm_ffn[l]), ffn_w1[i].astype(BF16), ffn_w3[i].astype(BF16),
                      ffn_w2[i].astype(BF16))
        else:
            xs = _moe(xs, row(norm_ffn[l]), router[i], moe_w1[i].astype(BF16), moe_w3[i].astype(BF16),
                      moe_w2[i].astype(BF16))
    return xs.reshape(batch, seq, d)
```

```python
import functools
import math

import jax
import jax.numpy as jnp
import numpy as np
from jax import lax
from jax.experimental import pallas as pl
from jax.experimental.pallas import tpu as pltpu

F32 = jnp.float32
BF16 = jnp.bfloat16

HEAD_DIM = 64
A_HEADS = 8
A_BLOCK = 256
A_TOPK = 3
B_GROUPS = 4
B_GROUP_DIM = 128
B_WINDOWS = (2, 4, 8, 16)
C_HEADS = 8
C_KV_HEADS = 2
C_WINDOW = 128
REL_BUCKETS = 32
REL_MAX_DIST = 1024
X_HEADS = 4
X_HEAD_DIM = 128
N_EXPERTS = 8
TOP_K = 2
EPS = 1e-6
NEG_INF = -1e30
LOG2E = math.log2(math.e)

WIDTH = 512
POOL_HALO = 16
ROW_TILE = 512
MOE_TILE = 1024
FF_CHUNK = 512
SUBLANES = 8
KEY_CHUNK = 128
QK_AHEAD = 3
SWA_BLOCKS = 4
VMEM_LIMIT = 56 * 1024 * 1024
EXP2_SAFE = 100.0


def _dot(a, b):
    return jnp.dot(a, b, preferred_element_type=F32)


def _dot_nt(a, b):
    return lax.dot_general(a, b, (((1,), (1,)), ((), ())), preferred_element_type=F32)


def _rms(x, g):
    ms = jnp.mean(x * x, axis=-1, keepdims=True)
    return x * lax.rsqrt(ms + EPS) * g


def _const_spec(shape):
    zeros = (0,) * len(shape)
    return pl.BlockSpec(shape, lambda *_: zeros, pipeline_mode=pl.Buffered(1))


def _smem_spec():
    return pl.BlockSpec(memory_space=pltpu.SMEM)


def _params(sem):
    return pltpu.CompilerParams(dimension_semantics=sem, vmem_limit_bytes=VMEM_LIMIT)


def _rel_bucket(dist):
    n = jnp.maximum(dist, 0)
    max_exact = REL_BUCKETS // 2
    nf = jnp.maximum(n, 1).astype(jnp.float32)
    large = max_exact + (jnp.log(nf / max_exact) / math.log(REL_MAX_DIST / max_exact)
                         * (REL_BUCKETS - max_exact)).astype(jnp.int32)
    large = jnp.minimum(large, REL_BUCKETS - 1)
    return jnp.where(n < max_exact, n, large)


def _saturation_distance():
    ratio = REL_MAX_DIST / (REL_BUCKETS // 2)
    return int(math.ceil((REL_BUCKETS // 2) * ratio ** ((REL_BUCKETS - 1 - REL_BUCKETS // 2 + 0.5)
                                                       / (REL_BUCKETS - REL_BUCKETS // 2))))


def _bias_by_distance(tab, n_dist):
    onehot = (_rel_bucket(jnp.arange(n_dist))[:, None] == jnp.arange(REL_BUCKETS)[None, :]).astype(F32)
    return jnp.einsum('hb,db->hd', tab, onehot, precision=lax.Precision.HIGHEST)


def _skew(g, rows):
    n = g.shape[-1]
    lead = g.shape[:-1]
    tiled = jnp.broadcast_to(g[..., None, :], lead + (rows, n)).reshape(lead + (rows * n,))
    return tiled[..., :rows * (n - 1)].reshape(lead + (rows, n - 1))


def _moba_bias_tiles(bvec, n_near):
    L = A_BLOCK
    rows = []
    for delta in range(n_near):
        lo = delta * L - (L - 1)
        seg = bvec[:, max(lo, 0):delta * L + L]
        if lo < 0:
            seg = jnp.concatenate([jnp.zeros((bvec.shape[0], -lo), F32), seg], axis=1)
        rows.append(jnp.pad(seg, ((0, 0), (0, 1))))
    g = jnp.stack(rows, axis=1)
    return _skew(g, L)[..., L - 1:]


def _swa_bias_tiles(bvec):
    W = C_WINDOW
    u = np.arange(3 * W)
    g = bvec[:, np.clip(2 * W - 1 - u, 0, W - 1)]
    return _skew(g, W)[..., W - 1:3 * W - 1]


def _bounded_flag(q_gain, k_gain, head_dim, extra):
    bound = head_dim ** 0.5 * jnp.max(jnp.abs(q_gain)) * jnp.max(jnp.abs(k_gain)) * 1.02 + extra
    return (bound * LOG2E < EXP2_SAFE).astype(jnp.int32).reshape(1)


def _in_proj_kernel(x_ref, g_ref, w_ref, bg_ref, gn_ref, bd_ref,
                    qa_ref, ka_ref, va_ref, ub_ref, qc_ref, kc_ref, vc_ref, gt_ref):
    xb = _rms(x_ref[...], g_ref[...]).astype(BF16)
    bd = bd_ref[...]

    def proj(c0, width):
        return _dot(xb, w_ref[:, c0:c0 + width])

    def head_norm(t, gain):
        outs = []
        for c in range(0, t.shape[1], 256):
            wd = min(256, t.shape[1] - c)
            tc = t[:, c:c + wd]
            ss = _dot((tc * tc).astype(BF16), bd[:wd, :wd])
            outs.append(tc * lax.rsqrt(ss * (1.0 / HEAD_DIM) + EPS))
        y = outs[0] if len(outs) == 1 else jnp.concatenate(outs, axis=-1)
        return y * gain

    def tile_kv_heads(t):
        lane = lax.broadcasted_iota(jnp.int32, t.shape, 1)
        r = pltpu.roll(t, HEAD_DIM, axis=1)
        h0 = jnp.where(lane < HEAD_DIM, t, r)
        h1 = jnp.where(lane < HEAD_DIM, r, t)
        return jnp.concatenate([h0, h0, h1, h1], axis=-1)

    qa_ref[...] = head_norm(proj(0, WIDTH), gn_ref[0:1, :]).astype(BF16)
    ka_ref[...] = head_norm(proj(WIDTH, WIDTH), gn_ref[1:2, :]).astype(BF16)
    va_ref[...] = proj(2 * WIDTH, WIDTH).astype(BF16)
    ub_ref[...] = proj(3 * WIDTH, WIDTH)
    qc_ref[...] = head_norm(proj(4 * WIDTH, WIDTH), gn_ref[2:3, :]).astype(BF16)
    kv = C_KV_HEADS * HEAD_DIM
    c0 = 5 * WIDTH
    kc = head_norm(proj(c0, kv), gn_ref[3:4, :kv])
    kc_ref[...] = tile_kv_heads(kc).astype(BF16)
    vc_ref[...] = tile_kv_heads(proj(c0 + kv, kv)).astype(BF16)
    c0 += 2 * kv
    for c in range(0, gt_ref.shape[1], WIDTH):
        gl = proj(c0 + c, WIDTH) + bg_ref[:, c:c + WIDTH]
        gt_ref[:, c:c + WIDTH] = jax.nn.sigmoid(gl).astype(BF16)


def _in_proj(x, g, w_in, b_gate, gains, bd):
    n, d = x.shape
    tm = min(ROW_TILE, n)
    n_gate = b_gate.shape[1]
    row = lambda width: pl.BlockSpec((tm, width), lambda i: (i, 0))
    out_shape = [jax.ShapeDtypeStruct((n, WIDTH), BF16)] * 3 + [jax.ShapeDtypeStruct((n, WIDTH), F32)] \
        + [jax.ShapeDtypeStruct((n, WIDTH), BF16)] * 3 + [jax.ShapeDtypeStruct((n, n_gate), BF16)]
    return pl.pallas_call(
        _in_proj_kernel,
        grid=(n // tm,),
        in_specs=[row(d), _const_spec(g.shape), _const_spec(w_in.shape), _const_spec(b_gate.shape),
                  _const_spec(gains.shape), _const_spec(bd.shape)],
        out_specs=[row(WIDTH)] * 7 + [row(n_gate)],
        out_shape=out_shape,
        compiler_params=_params(("parallel",)),
        name="in_proj",
    )(x, g, w_in, b_gate, gains, bd)


def _moba_kernel(flag_ref, far_ref, q_ref, k_ref, v_ref, bias_ref, o_ref,
                 kmean_ref, vt_ref, sel_ref, m_ref, l_ref, l8_ref, acc_ref, s_ref, *, nb, n_near):
    i = pl.program_id(1)
    L = A_BLOCK
    KC = KEY_CHUNK
    PW = 2 * HEAD_DIM
    nbp = kmean_ref.shape[0]

    @pl.when(i == 0)
    def _():
        kmean_ref[...] = jnp.zeros(kmean_ref.shape, F32)
        for j in range(nb):
            kj = k_ref[j * L:(j + 1) * L, :].astype(F32)
            kmean_ref[j:j + 1, :] = jnp.mean(kj, axis=0, keepdims=True)
            vt_ref[j] = v_ref[j * L:(j + 1) * L, :].astype(F32).T.astype(BF16)

    lane = lax.broadcasted_iota(jnp.int32, (L, PW), 1)
    blk = lax.broadcasted_iota(jnp.int32, (nbp, L), 0)
    past = blk < i
    qh = []
    for h in range(A_HEADS):
        pair = slice((h // 2) * PW, (h // 2 + 1) * PW)
        q = q_ref[:, pair]
        qm = jnp.where(lane // HEAD_DIM == h % 2, q, jnp.zeros_like(q))
        qh.append(qm)
        km = kmean_ref[:, pair]
        km_hi = km.astype(BF16)
        km_lo = (km - km_hi.astype(F32)).astype(BF16)
        s = jnp.where(past, _dot_nt(km_hi, qm) + _dot_nt(km_lo, qm), -jnp.inf)
        rank = jnp.zeros((nbp, L), jnp.int32)
        for jp in range(nb):
            sj = s[jp:jp + 1, :]
            ahead = (sj > s) | ((sj == s) & (jp < blk))
            rank = rank + ahead.astype(jnp.int32)
        sel_ref[h] = (past & (rank < A_TOPK)).astype(F32)
        m_ref[h] = jnp.full((1, L), NEG_INF, F32)
        l_ref[h] = jnp.zeros((1, L), F32)
        l8_ref[h] = jnp.zeros((8, L), F32)
    acc_ref[...] = jnp.zeros(acc_ref.shape, F32)
    qpair_t = [jnp.concatenate(qh[2 * hp:2 * hp + 2], axis=0).astype(F32).T.astype(BF16)
               for hp in range(A_HEADS // 2)]

    def causal(c, n):
        kk = c + lax.broadcasted_iota(jnp.int32, (n, L), 0)
        qq = lax.broadcasted_iota(jnp.int32, (n, L), 1)
        return kk <= qq

    def bounded_blocks(blocks):
        units = [(bi, hp, c) for bi in range(len(blocks)) for hp in range(A_HEADS // 2) for c in range(0, L, KC)]

        def qk(u):
            bi, hp, c = units[u]
            kc = k_ref[pl.ds(pl.multiple_of(blocks[bi][0] * L + c, KC), KC), hp * PW:(hp + 1) * PW]
            s_ref[u % (QK_AHEAD + 1)] = _dot(kc, qpair_t[hp])

        for u in range(QK_AHEAD):
            qk(u)
        o = ps = None
        for u, (bi, hp, c) in enumerate(units):
            j, tile_idx, own = blocks[bi]
            if u + QK_AHEAD < len(units):
                qk(u + QK_AHEAD)
            if c == 0:
                o = [jnp.zeros((HEAD_DIM, L), F32)] * 2
                ps = jnp.zeros((8, 2 * L), F32)
            p = jnp.exp2(s_ref[u % (QK_AHEAD + 1)] + bias_ref[hp, tile_idx, pl.ds(c, KC), :])
            if own:
                p = jnp.where(jnp.concatenate([causal(c, KC)] * 2, axis=1), p, 0.0)
            ps = ps + jnp.sum(p.reshape(KC // 8, 8, 2 * L), axis=0)
            pb = p.astype(BF16)
            o = [o[hh] + _dot(vt_ref[j, pl.ds((2 * hp + hh) * HEAD_DIM, HEAD_DIM), pl.ds(c, KC)],
                              pb[:, hh * L:(hh + 1) * L]) for hh in range(2)]
            if c + KC < L:
                continue
            for hh in range(2):
                h = 2 * hp + hh
                rows = slice(h * HEAD_DIM, (h + 1) * HEAD_DIM)
                psh = ps[:, hh * L:(hh + 1) * L]
                if own:
                    acc_ref[rows, :] += o[hh]
                    l8_ref[h] += psh
                else:
                    w = sel_ref[h, pl.ds(j, 1), :]
                    acc_ref[rows, :] += w * o[hh]
                    l8_ref[h] += w * psh

    def run_bounded():
        def tile_of(j):
            return jnp.minimum(i - j, n_near)

        def body(t, carry):
            bounded_blocks([(2 * t, tile_of(2 * t), False), (2 * t + 1, tile_of(2 * t + 1), False)])
            return carry

        lax.fori_loop(0, i // 2, body, 0)

        @pl.when(i % 2 == 1)
        def _():
            bounded_blocks([(i - 1, tile_of(i - 1), False), (i, 0, True)])

        @pl.when(i % 2 == 0)
        def _():
            bounded_blocks([(i, 0, True)])

    def online_tile(j, delta, far, own):
        for h in range(A_HEADS):
            pair = slice((h // 2) * PW, (h // 2 + 1) * PW)
            rows = slice(h * HEAD_DIM, (h + 1) * HEAD_DIM)
            s = _dot_nt(k_ref[pl.ds(pl.multiple_of(j * L, L), L), pair], qh[h])
            if not far:
                s = s + bias_ref[h // 2, delta, :, (h % 2) * L:(h % 2 + 1) * L]
            if own:
                s = jnp.where(causal(0, L), s, NEG_INF)
            mj = jnp.max(s, axis=0, keepdims=True)
            p = jnp.exp2(s - mj)
            lj = jnp.sum(p, axis=0, keepdims=True)
            o = _dot(vt_ref[j, rows, :], p.astype(BF16))
            if far:
                mj = mj + far_ref[h]
            m_old = m_ref[h]
            if own:
                m_new = jnp.maximum(m_old, mj)
                beta = jnp.exp2(mj - m_new)
            else:
                on = sel_ref[h, pl.ds(j, 1), :] > 0.5
                m_new = jnp.where(on, jnp.maximum(m_old, mj), m_old)
                beta = jnp.where(on, jnp.exp2(mj - m_new), 0.0)
            alpha = jnp.exp2(m_old - m_new)
            m_ref[h] = m_new
            l_ref[h] = alpha * l_ref[h] + beta * lj
            acc_ref[rows, :] = alpha * acc_ref[rows, :] + beta * o

    n_far = jnp.maximum(i - (n_near - 1), 0)

    def run(tile):
        def far_body(j, c):
            tile(j, None, True, False)
            return c

        def near_body(j, c):
            tile(j, i - j, False, False)
            return c

        lax.fori_loop(0, n_far, far_body, 0)
        lax.fori_loop(n_far, i, near_body, 0)
        tile(i, 0, False, True)

    @pl.when(flag_ref[0] == 1)
    def _():
        run_bounded()
        for h in range(A_HEADS):
            l_ref[h] = jnp.sum(l8_ref[h], axis=0, keepdims=True)

    @pl.when(flag_ref[0] != 1)
    def _():
        run(online_tile)

    o_t = jnp.concatenate([acc_ref[h * HEAD_DIM:(h + 1) * HEAD_DIM, :] / l_ref[h] for h in range(A_HEADS)],
                          axis=0)
    o_ref[...] = o_t.T.astype(BF16)


def _moba(q, k, v, bias_tiles, far, flag, batch, seq):
    n = q.shape[0]
    L = A_BLOCK
    nb = seq // L
    nbp = max(8, -(-nb // 8) * 8)
    n_near = bias_tiles.shape[1] - 1
    kern = functools.partial(_moba_kernel, nb=nb, n_near=n_near)
    return pl.pallas_call(
        kern,
        grid=(batch, nb),
        in_specs=[_smem_spec(), _smem_spec(),
                  pl.BlockSpec((L, WIDTH), lambda b, i: (b * nb + i, 0)),
                  pl.BlockSpec((seq, WIDTH), lambda b, i: (b, 0)),
                  pl.BlockSpec((seq, WIDTH), lambda b, i: (b, 0)),
                  _const_spec(bias_tiles.shape)],
        out_specs=pl.BlockSpec((L, WIDTH), lambda b, i: (b * nb + i, 0)),
        out_shape=jax.ShapeDtypeStruct((n, WIDTH), BF16),
        scratch_shapes=[pltpu.VMEM((nbp, WIDTH), F32),
                        pltpu.VMEM((nb, WIDTH, L), BF16),
                        pltpu.VMEM((A_HEADS, nbp, L), F32),
                        pltpu.VMEM((A_HEADS, 1, L), F32),
                        pltpu.VMEM((A_HEADS, 1, L), F32),
                        pltpu.VMEM((A_HEADS, 8, L), F32),
                        pltpu.VMEM((WIDTH, L), F32),
                        pltpu.VMEM((QK_AHEAD + 1, KEY_CHUNK, 2 * L), F32)],
        compiler_params=_params(("arbitrary", "arbitrary")),
        name="moba",
    )(flag, far, q, k, v, bias_tiles)


def _swa_kernel(flag_ref, sink_ref, q_ref, kp_ref, kc_ref, vp_ref, vc_ref, bias_ref, o_ref):
    g = pl.program_id(0)
    chunk = pl.program_id(2)
    W = C_WINDOW
    G = C_HEADS // C_KV_HEADS
    nsub = q_ref.shape[0] // W
    gw = q_ref.shape[1]
    kall = jnp.concatenate([kp_ref[...], kc_ref[...]], axis=0)
    vall = jnp.concatenate([vp_ref[...], vc_ref[...]], axis=0)
    lane = lax.broadcasted_iota(jnp.int32, (W, gw), 1)
    qi = lax.broadcasted_iota(jnp.int32, (W, 2 * W), 0)
    kj = lax.broadcasted_iota(jnp.int32, (W, 2 * W), 1)
    dist = W + qi - kj
    band = (dist >= 0) & (dist < W)

    def heads(bounded):
        for r in range(nsub):
            q = q_ref[r * W:(r + 1) * W, :]
            kcat = kall[r * W:(r + 2) * W]
            vcat = vall[r * W:(r + 2) * W]
            mask = band & ((chunk > 0) | (kj >= W)) if r == 0 else band
            out = jnp.zeros(q.shape, F32)
            for hh in range(G):
                mine = lane // HEAD_DIM == hh
                qm = jnp.where(mine, q, jnp.zeros_like(q))
                s = _dot_nt(qm, kcat) + bias_ref[hh]
                sink = sink_ref[g * G + hh]
                if bounded:
                    pb = jnp.where(mask, jnp.exp2(s), 0.0).astype(BF16)
                    den = _dot(pb, jnp.ones((2 * W, 128), BF16)) + jnp.exp2(jnp.full((1, 128), sink, F32))
                    inv = 1.0 / den
                    o = _dot(pb, vcat) * jnp.concatenate([inv] * (gw // 128), axis=-1)
                else:
                    s = jnp.where(mask, s, NEG_INF)
                    m = jnp.maximum(jnp.max(s, axis=-1, keepdims=True), sink)
                    p = jnp.exp2(s - m)
                    den = jnp.sum(p, axis=-1, keepdims=True) + jnp.exp2(sink - m)
                    o = _dot(p.astype(BF16), vcat) / den
                out = jnp.where(mine, o, out)
            o_ref[r * W:(r + 1) * W, :] = out.astype(BF16)

    @pl.when(flag_ref[0] == 1)
    def _():
        heads(True)

    @pl.when(flag_ref[0] != 1)
    def _():
        heads(False)


def _swa(q, k_t, v_t, bias_tiles, sinks, flag, batch, seq):
    n = q.shape[0]
    W = C_WINDOW
    G = C_HEADS // C_KV_HEADS
    gw = G * HEAD_DIM
    nsub = min(SWA_BLOCKS, seq // W)
    nchunk = seq // (W * nsub)
    cur = lambda g, b, j: (b * nchunk + j, g)
    prev = lambda g, b, j: (jnp.maximum((b * nchunk + j) * nsub - 1, 0), g)
    return pl.pallas_call(
        _swa_kernel,
        grid=(C_KV_HEADS, batch, nchunk),
        in_specs=[_smem_spec(), _smem_spec(),
                  pl.BlockSpec((nsub * W, gw), cur),
                  pl.BlockSpec((W, gw), prev), pl.BlockSpec((nsub * W, gw), cur),
                  pl.BlockSpec((W, gw), prev), pl.BlockSpec((nsub * W, gw), cur),
                  pl.BlockSpec((G, W, 2 * W), lambda g, b, j: (g, 0, 0))],
        out_specs=pl.BlockSpec((nsub * W, gw), cur),
        out_shape=jax.ShapeDtypeStruct((n, WIDTH), BF16),
        compiler_params=_params(("arbitrary", "arbitrary", "arbitrary")),
        name="swa",
    )(flag, sinks, q, k_t, k_t, v_t, v_t, bias_tiles)


def _seg_norm(t, gain, seg):
    outs = []
    for c in range(0, t.shape[1], seg):
        tc = t[:, c:c + seg]
        outs.append(tc * lax.rsqrt(jnp.mean(tc * tc, axis=-1, keepdims=True) + EPS))
    return jnp.concatenate(outs, axis=-1) * gain


def _mem_kv_kernel(mem_ref, g_ref, wk_ref, wv_ref, kg_ref, k_ref, v_ref):
    mb = _rms(mem_ref[...], g_ref[...]).astype(BF16)
    k_ref[...] = _seg_norm(_dot(mb, wk_ref[...]), kg_ref[...], X_HEAD_DIM).astype(BF16)
    v_ref[...] = _dot(mb, wv_ref[...]).astype(BF16)


def _mem_kv(mem, g, w_xk, w_xv, k_gain):
    n, d = mem.shape
    tm = min(ROW_TILE, n)
    row = lambda width: pl.BlockSpec((tm, width), lambda i: (i, 0))
    return pl.pallas_call(
        _mem_kv_kernel,
        grid=(n // tm,),
        in_specs=[row(d), _const_spec(g.shape), _const_spec(w_xk.shape), _const_spec(w_xv.shape),
                  _const_spec(k_gain.shape)],
        out_specs=[row(WIDTH), row(WIDTH)],
        out_shape=[jax.ShapeDtypeStruct((n, WIDTH), BF16)] * 2,
        compiler_params=_params(("parallel",)),
        name="mem_kv",
    )(mem, g, w_xk, w_xv, k_gain)


def _merge_cross_kernel(x_ref, oa_ref, ub_ref, halo_ref, oc_ref, gt_ref, pw_ref, ps_ref, wb_ref, wo_ref,
                        gx_ref, wq_ref, qg_ref, km_ref, vm_ref, wxo_ref, o_ref, *, seq):
    tm = x_ref.shape[0]
    d = x_ref.shape[1]
    t0 = (pl.program_id(0) * tm) % seq
    H = POOL_HALO

    halo = jnp.where(t0 > 0, halo_ref[...], 0.0)
    pos = t0 + lax.broadcasted_iota(jnp.int32, (tm, B_GROUP_DIM), 0)
    mixed = []
    for gi, win in enumerate(B_WINDOWS):
        cols = slice(gi * B_GROUP_DIM, (gi + 1) * B_GROUP_DIM)
        cur = ub_ref[:, cols]
        acc = jnp.concatenate([halo[:, cols], cur], axis=0)
        step = 1
        while step < win:
            acc = acc + pltpu.roll(acc, step, axis=0)
            step *= 2
        cnt = jnp.minimum(pos + 1, win).astype(F32)
        pooled = acc[H:, :] / cnt - cur
        mixed.append(_dot(pooled.astype(BF16), pw_ref[gi]))
    ob = jnp.concatenate(mixed, axis=-1) * ps_ref[...]

    merged = gt_ref[:, 0:d].astype(F32) * _dot(oa_ref[...], wb_ref[0])
    merged = merged + gt_ref[:, d:2 * d].astype(F32) * _dot(ob.astype(BF16), wb_ref[1])
    merged = merged + gt_ref[:, 2 * d:3 * d].astype(F32) * _dot(oc_ref[...], wb_ref[2])
    x1 = x_ref[...] + _dot(merged.astype(BF16), wo_ref[...])

    xb = _rms(x1, gx_ref[...]).astype(BF16)
    qn = _seg_norm(_dot(xb, wq_ref[...]), qg_ref[...], X_HEAD_DIM).astype(BF16)
    heads = []
    for h in range(X_HEADS):
        cols = slice(h * X_HEAD_DIM, (h + 1) * X_HEAD_DIM)
        s = _dot_nt(qn[:, cols], km_ref[:, cols])
        p = jnp.exp(s - jnp.max(s, axis=-1, keepdims=True))
        inv = 1.0 / jnp.sum(p, axis=-1, keepdims=True)
        heads.append(_dot(p.astype(BF16), vm_ref[:, cols]) * inv)
    o = jnp.concatenate(heads, axis=-1).astype(BF16)
    o_ref[...] = x1 + _dot(o, wxo_ref[...])


def _merge_cross(x, oa, ub, oc, gates, pool_w, pool_scale, w_branch, w_out,
                 g_cross, w_xq, q_gain, k_mem, v_mem, w_xo, seq, mem_len):
    n, d = x.shape
    tm = min(ROW_TILE, seq)
    row = lambda width: pl.BlockSpec((tm, width), lambda i: (i, 0))
    halo = pl.BlockSpec((POOL_HALO, WIDTH), lambda i: (jnp.maximum(i * (tm // POOL_HALO) - 1, 0), 0))
    mem = pl.BlockSpec((mem_len, WIDTH), lambda i: ((i * tm) // seq, 0))
    kern = functools.partial(_merge_cross_kernel, seq=seq)
    return pl.pallas_call(
        kern,
        grid=(n // tm,),
        in_specs=[row(d), row(WIDTH), row(WIDTH), halo, row(WIDTH), row(gates.shape[1]),
                  _const_spec(pool_w.shape), _const_spec(pool_scale.shape), _const_spec(w_branch.shape),
                  _const_spec(w_out.shape), _const_spec(g_cross.shape), _const_spec(w_xq.shape),
                  _const_spec(q_gain.shape), mem, mem, _const_spec(w_xo.shape)],
        out_specs=row(d),
        out_shape=jax.ShapeDtypeStruct((n, d), F32),
        compiler_params=_params(("parallel",)),
        name="merge_cross",
    )(x, oa, ub, ub, oc, gates, pool_w, pool_scale, w_branch, w_out,
      g_cross, w_xq, q_gain, k_mem, v_mem, w_xo)


def _swiglu_chunks(xb, w1_ref, w3_ref, w2_ref, acc):
    for c in range(0, w1_ref.shape[1], FF_CHUNK):
        h1 = _dot(xb, w1_ref[:, c:c + FF_CHUNK])
        h3 = _dot(xb, w3_ref[:, c:c + FF_CHUNK])
        acc = acc + _dot((jax.nn.silu(h1) * h3).astype(BF16), w2_ref[c:c + FF_CHUNK, :])
    return acc


def _ffn_kernel(x_ref, g_ref, w1_ref, w3_ref, w2_ref, o_ref):
    x = x_ref[...]
    xb = _rms(x, g_ref[...]).astype(BF16)
    o_ref[...] = _swiglu_chunks(xb, w1_ref, w3_ref, w2_ref, x)


def _ffn(x, g, w1, w3, w2):
    n, d = x.shape
    tm = min(ROW_TILE, n)
    row = pl.BlockSpec((tm, d), lambda i: (i, 0))
    return pl.pallas_call(
        _ffn_kernel,
        grid=(n // tm,),
        in_specs=[row, _const_spec(g.shape), _const_spec(w1.shape), _const_spec(w3.shape),
                  _const_spec(w2.shape)],
        out_specs=row,
        out_shape=jax.ShapeDtypeStruct((n, d), F32),
        compiler_params=_params(("parallel",)),
        name="ffn_dense",
    )(x, g, w1, w3, w2)


def _store_token_tiles(ref, x):
    rows = x.shape[0]
    for k in range(x.shape[1] // 128):
        ref[pl.ds(k, rows, stride=SUBLANES), :] = x[:, k * 128:(k + 1) * 128]


def _load_token_tiles(ref, rows, lead=None):
    idx = (lambda k: (pl.ds(k, rows, stride=SUBLANES), slice(None))) if lead is None else \
        (lambda k: (lead, pl.ds(k, rows, stride=SUBLANES), slice(None)))
    return jnp.concatenate([ref[idx(k)] for k in range(SUBLANES)], axis=-1)


def _router_kernel(x_ref, g_ref, r_ref, xn_ref, lg_ref):
    xn = _rms(x_ref[...], g_ref[...])
    _store_token_tiles(xn_ref, xn)
    r = r_ref[...]
    x_hi = xn.astype(BF16)
    x_lo = (xn - x_hi.astype(F32)).astype(BF16)
    r_hi = r.astype(BF16)
    r_lo = (r - r_hi.astype(F32)).astype(BF16)
    lg_ref[...] = _dot(x_hi, r_hi) + (_dot(x_lo, r_hi) + _dot(x_hi, r_lo))


def _router(x, g, router_padded):
    n, d = x.shape
    tm = min(ROW_TILE, n)
    row = lambda width: pl.BlockSpec((tm, width), lambda i: (i, 0))
    return pl.pallas_call(
        _router_kernel,
        grid=(n // tm,),
        in_specs=[row(d), _const_spec(g.shape), _const_spec(router_padded.shape)],
        out_specs=[pl.BlockSpec((tm * SUBLANES, d // SUBLANES), lambda i: (i, 0)),
                   row(router_padded.shape[1])],
        out_shape=[jax.ShapeDtypeStruct((n * SUBLANES, d // SUBLANES), F32),
                   jax.ShapeDtypeStruct((n, router_padded.shape[1]), F32)],
        compiler_params=_params(("parallel",)),
        name="router",
    )(x, g, router_padded)


def _expert_kernel(be_ref, nu_ref, tok_ref, tokn_ref, asg_ref, asgp_ref, xn_hbm, w1_ref, w3_ref, w2_ref, y_hbm,
                   xg_ref, xb_ref, y_ref, ys_ref, gsem, ssem, *, nb, nc):
    b = pl.program_id(0)
    c = pl.program_id(1)
    tb = xb_ref.shape[0]
    T = SUBLANES
    U = 8
    groups = tb // U
    piece = -(-groups // nc)
    slot = b % 2

    def tile(ref, first_row):
        return ref.at[pl.ds(pl.multiple_of(first_row, T), T), :]

    def gather(idx_ref, s, lo, hi):
        def body(g, carry):
            for k in range(U):
                r = g * U + k
                pltpu.make_async_copy(tile(xn_hbm, idx_ref[0, 0, r]), tile(xg_ref.at[s], r * T),
                                      gsem.at[s]).start()
            return carry
        lax.fori_loop(lo, hi, body, 0)

    def wait_gather(s):
        pltpu.make_async_copy(xn_hbm.at[pl.ds(0, tb * T), :], xg_ref.at[s], gsem.at[s]).wait()

    def scatter(idx_ref, s, lo, hi):
        def body(g, carry):
            for k in range(U):
                r = g * U + k
                pltpu.make_async_copy(tile(ys_ref.at[s], r * T), tile(y_hbm, idx_ref[0, 0, r]),
                                      ssem.at[s]).start()
            return carry
        lax.fori_loop(lo, hi, body, 0)

    def wait_scatter(s):
        pltpu.make_async_copy(ys_ref.at[s], y_hbm.at[pl.ds(0, tb * T), :], ssem.at[s]).wait()

    @pl.when(c == 0)
    def _():
        @pl.when(b == 0)
        def _():
            gather(tok_ref, 0, 0, groups)

        wait_gather(slot)
        xb_ref[...] = _load_token_tiles(xg_ref, tb, lead=slot).astype(BF16)
        y_ref[...] = jnp.zeros(y_ref.shape, F32)

    lo = c * piece
    hi = jnp.minimum(lo + piece, groups)

    @pl.when(b + 1 < nb)
    def _():
        gather(tokn_ref, 1 - slot, lo, hi)

    @pl.when(b >= 1)
    def _():
        scatter(asgp_ref, 1 - slot, lo, hi)

    @pl.when(b < nu_ref[0])
    def _():
        xb = xb_ref[...]
        h1 = _dot(xb, w1_ref[...])
        h3 = _dot(xb, w3_ref[...])
        y_ref[...] += _dot((jax.nn.silu(h1) * h3).astype(BF16), w2_ref[...])

    @pl.when(c == nc - 1)
    def _():
        _store_token_tiles(ys_ref.at[slot], y_ref[...])

        @pl.when(b >= 1)
        def _():
            wait_scatter(1 - slot)

        @pl.when(b == nb - 1)
        def _():
            scatter(asg_ref, slot, 0, groups)
            wait_scatter(slot)


def _experts(xn, slot_tok, slot_row, block_e, n_used, w1, w3, w2, tb):
    T = SUBLANES
    nc, d = w1.shape[1], w1.shape[2]
    p = slot_tok.shape[0]
    nblk = p // tb
    tok3 = (slot_tok * T).reshape(nblk, 1, tb)
    row3 = (slot_row * T).reshape(nblk, 1, tb)
    idx_spec = lambda imap: pl.BlockSpec((1, 1, tb), imap, memory_space=pltpu.SMEM)
    chunk = lambda b, c, nu: jnp.where(b < nu[0], c, nc - 1)
    grid_spec = pltpu.PrefetchScalarGridSpec(
        num_scalar_prefetch=2,
        grid=(nblk, nc),
        in_specs=[idx_spec(lambda b, c, be, nu: (b, 0, 0)),
                  idx_spec(lambda b, c, be, nu: (jnp.minimum(b + 1, nblk - 1), 0, 0)),
                  idx_spec(lambda b, c, be, nu: (b, 0, 0)),
                  idx_spec(lambda b, c, be, nu: (jnp.maximum(b - 1, 0), 0, 0)),
                  pl.BlockSpec(memory_space=pl.ANY),
                  pl.BlockSpec((None, None, d, FF_CHUNK), lambda b, c, be, nu: (be[b], chunk(b, c, nu), 0, 0)),
                  pl.BlockSpec((None, None, d, FF_CHUNK), lambda b, c, be, nu: (be[b], chunk(b, c, nu), 0, 0)),
                  pl.BlockSpec((None, FF_CHUNK, d), lambda b, c, be, nu: (be[b], chunk(b, c, nu), 0))],
        out_specs=pl.BlockSpec(memory_space=pl.ANY),
        scratch_shapes=[pltpu.VMEM((2, tb * T, d // T), F32), pltpu.VMEM((tb, d), BF16), pltpu.VMEM((tb, d), F32),
                        pltpu.VMEM((2, tb * T, d // T), F32),
                        pltpu.SemaphoreType.DMA((2,)), pltpu.SemaphoreType.DMA((2,))],
    )
    return pl.pallas_call(
        functools.partial(_expert_kernel, nb=nblk, nc=nc),
        grid_spec=grid_spec,
        out_shape=jax.ShapeDtypeStruct((p * T, d // T), F32),
        compiler_params=_params(("arbitrary", "arbitrary")),
        name="experts",
    )(block_e, n_used, tok3, tok3, row3, row3, xn, w1, w3, w2)


def _combine_kernel(x_ref, y0_ref, y1_ref, g_ref, o_ref):
    tm = x_ref.shape[0]
    g = g_ref[...]
    o_ref[...] = x_ref[...] + g[:, 0:1] * _load_token_tiles(y0_ref, tm) + g[:, 1:2] * _load_token_tiles(y1_ref, tm)


def _combine(x, y, gate):
    n, d = x.shape
    tm = min(ROW_TILE, n)
    T = SUBLANES
    row = lambda width: pl.BlockSpec((tm, width), lambda i: (i, 0))
    tiles = lambda choice: pl.BlockSpec((tm * T, d // T), lambda i: (choice * (n // tm) + i, 0))
    return pl.pallas_call(
        _combine_kernel,
        grid=(n // tm,),
        in_specs=[row(d), tiles(0), tiles(1), row(gate.shape[1])],
        out_specs=row(d),
        out_shape=jax.ShapeDtypeStruct((n, d), F32),
        compiler_params=_params(("parallel",)),
        name="combine",
    )(x, y, y, gate)


def _moe(x, g, router, w1, w3, w2):
    n, d = x.shape
    a = n * TOP_K
    tb = min(MOE_TILE, a)
    router_padded = jnp.pad(router, ((0, 0), (0, 128 - N_EXPERTS)))
    xn, logits = _router(x, g, router_padded)
    top_logit, top_e = lax.top_k(logits[:, :N_EXPERTS], TOP_K)
    gate = jax.nn.softmax(top_logit, axis=-1)
    e_flat = top_e.reshape(a)
    onehot = (e_flat[:, None] == jnp.arange(N_EXPERTS)[None, :]).astype(jnp.int32)
    csum = jnp.cumsum(onehot, axis=0)
    rank = jnp.sum(onehot * csum, axis=1) - 1
    counts = csum[-1]
    padded = (counts + tb - 1) // tb * tb
    pend = jnp.cumsum(padded)
    pstart = pend - padded
    dest = (jnp.sum(onehot * pstart[None, :], axis=1) + rank).astype(jnp.int32)
    n_blocks = a // tb + N_EXPERTS
    p = n_blocks * tb
    slot_asg = jnp.full((p,), -1, jnp.int32).at[dest].set(jnp.arange(a, dtype=jnp.int32))
    is_pad = slot_asg < 0
    slot_row = jnp.where(is_pad, a - 1 + jnp.cumsum(is_pad.astype(jnp.int32)),
                         (slot_asg % TOP_K) * n + slot_asg // TOP_K)
    slot_tok = jnp.where(is_pad, 0, slot_asg // TOP_K)
    block_e = jnp.minimum(jnp.searchsorted(pend, jnp.arange(n_blocks) * tb, side='right'),
                          N_EXPERTS - 1).astype(jnp.int32)
    n_used = (pend[-1] // tb).astype(jnp.int32).reshape(1)
    y = _experts(xn, slot_tok, slot_row, block_e, n_used, w1, w3, w2, tb)
    gate_padded = jnp.pad(gate, ((0, 0), (0, 128 - TOP_K)))
    return _combine(x, y, gate_padded)


def _chunk_major(w):
    e, d, ff = w.shape
    return w.astype(BF16).reshape(e, d, ff // FF_CHUNK, FF_CHUNK).transpose(0, 2, 1, 3)


def _tile_gain(gain, width, scale=1.0):
    return jnp.tile(gain, width // gain.shape[0]) * scale


def kernel(x, mem, rel_bias, norm_mix, w_in, b_gate, a_q_gain, a_k_gain, pool_w, pool_scale,
           c_q_gain, c_k_gain, c_sinks, w_branch, w_out, norm_cross, norm_mem, w_xq, w_xk, w_xv,
           x_q_gain, x_k_gain, w_xo, norm_ffn, ffn_w1, ffn_w3, ffn_w2, router, moe_w1, moe_w3, moe_w2):
    batch, seq, d = x.shape
    mem_len = mem.shape[1]
    depth = norm_mix.shape[0]
    xs = x.reshape(batch * seq, d)
    mems = mem.reshape(batch * mem_len, d)

    tab_a = rel_bias[:, :A_HEADS].T
    tab_c = rel_bias[:, A_HEADS:].T
    nb = seq // A_BLOCK
    n_near = min(nb, (_saturation_distance() + 2 * A_BLOCK - 2) // A_BLOCK)
    far_a = tab_a[:, REL_BUCKETS - 1] * LOG2E
    bias_a = _moba_bias_tiles(_bias_by_distance(tab_a, n_near * A_BLOCK), n_near) * LOG2E
    bias_a = jnp.concatenate([bias_a, jnp.broadcast_to(far_a[:, None, None, None],
                                                       (A_HEADS, 1, A_BLOCK, A_BLOCK))], axis=1)
    bias_a = jnp.concatenate([bias_a[0::2], bias_a[1::2]], axis=-1)
    bias_c = _swa_bias_tiles(_bias_by_distance(tab_c, C_WINDOW)) * LOG2E
    seg = np.arange(256) // HEAD_DIM
    bd = jnp.asarray(seg[:, None] == seg[None, :], BF16)

    row = lambda v: v.reshape(1, -1)
    for l in range(depth):
        scale = HEAD_DIM ** -0.5 * LOG2E
        gains = jnp.stack([_tile_gain(a_q_gain[l], WIDTH, scale), _tile_gain(a_k_gain[l], WIDTH),
                           _tile_gain(c_q_gain[l], WIDTH, scale), _tile_gain(c_k_gain[l], WIDTH)])
        flag_a = _bounded_flag(a_q_gain[l], a_k_gain[l], HEAD_DIM, jnp.max(jnp.abs(tab_a)))
        flag_c = _bounded_flag(c_q_gain[l], c_k_gain[l], HEAD_DIM,
                               jnp.maximum(jnp.max(jnp.abs(tab_c)), jnp.max(jnp.abs(c_sinks[l]))))
        qa, ka, va, ub, qc, kc, vc, gates = _in_proj(
            xs, row(norm_mix[l]), w_in[l].astype(BF16), row(b_gate[l]), gains, bd)
        oa = _moba(qa, ka, va, bias_a, far_a, flag_a, batch, seq)
        oc = _swa(qc, kc, vc, bias_c, c_sinks[l] * LOG2E, flag_c, batch, seq)
        k_mem, v_mem = _mem_kv(mems, row(norm_mem[l]), w_xk[l].astype(BF16), w_xv[l].astype(BF16),
                               row(_tile_gain(x_k_gain[l], WIDTH)))
        xs = _merge_cross(xs, oa, ub, oc, gates, pool_w[l].astype(BF16), row(pool_scale[l]),
                          w_branch[l].astype(BF16), w_out[l].astype(BF16), row(norm_cross[l]),
                          w_xq[l].astype(BF16), row(_tile_gain(x_q_gain[l], WIDTH, X_HEAD_DIM ** -0.5)),
                          k_mem, v_mem, w_xo[l].astype(BF16), seq, mem_len)
        i = l // 2
        if l % 2 == 0:
            xs = _ffn(xs, row(norm_ffn[l]), ffn_w1[i].astype(BF16), ffn_w3[i].astype(BF16),
                      ffn_w2[i].astype(BF16))
        else:
            xs = _moe(xs, row(norm_ffn[l]), router[i], _chunk_major(moe_w1[i]), _chunk_major(moe_w3[i]),
                      moe_w2[i].astype(BF16))
    return xs.reshape(batch, seq, d)
```

```python
import functools
import math

import jax
import jax.numpy as jnp
import numpy as np
from jax import lax
from jax.experimental import pallas as pl
from jax.experimental.pallas import tpu as pltpu

F32 = jnp.float32
BF16 = jnp.bfloat16

HEAD_DIM = 64
A_HEADS = 8
A_BLOCK = 256
A_TOPK = 3
B_GROUPS = 4
B_GROUP_DIM = 128
B_WINDOWS = (2, 4, 8, 16)
C_HEADS = 8
C_KV_HEADS = 2
C_WINDOW = 128
REL_BUCKETS = 32
REL_MAX_DIST = 1024
X_HEADS = 4
X_HEAD_DIM = 128
N_EXPERTS = 8
TOP_K = 2
EPS = 1e-6
NEG_INF = -1e30
LOG2E = math.log2(math.e)

WIDTH = 512
POOL_HALO = 16
ROW_TILE = 512
MOE_TILE = 1024
FF_CHUNK = 512
SUBLANES = 8
KEY_CHUNK = 128
QK_AHEAD = 3
SWA_BLOCKS = 4
VMEM_LIMIT = 56 * 1024 * 1024
EXP2_SAFE = 100.0


def _dot(a, b):
    return jnp.dot(a, b, preferred_element_type=F32)


def _dot_nt(a, b):
    return lax.dot_general(a, b, (((1,), (1,)), ((), ())), preferred_element_type=F32)


def _rms(x, g):
    ms = jnp.mean(x * x, axis=-1, keepdims=True)
    return x * lax.rsqrt(ms + EPS) * g


def _const_spec(shape):
    zeros = (0,) * len(shape)
    return pl.BlockSpec(shape, lambda *_: zeros, pipeline_mode=pl.Buffered(1))


def _smem_spec():
    return pl.BlockSpec(memory_space=pltpu.SMEM)


def _params(sem):
    return pltpu.CompilerParams(dimension_semantics=sem, vmem_limit_bytes=VMEM_LIMIT)


def _rel_bucket(dist):
    n = jnp.maximum(dist, 0)
    max_exact = REL_BUCKETS // 2
    nf = jnp.maximum(n, 1).astype(jnp.float32)
    large = max_exact + (jnp.log(nf / max_exact) / math.log(REL_MAX_DIST / max_exact)
                         * (REL_BUCKETS - max_exact)).astype(jnp.int32)
    large = jnp.minimum(large, REL_BUCKETS - 1)
    return jnp.where(n < max_exact, n, large)


def _saturation_distance():
    ratio = REL_MAX_DIST / (REL_BUCKETS // 2)
    return int(math.ceil((REL_BUCKETS // 2) * ratio ** ((REL_BUCKETS - 1 - REL_BUCKETS // 2 + 0.5)
                                                       / (REL_BUCKETS - REL_BUCKETS // 2))))


def _bias_by_distance(tab, n_dist):
    onehot = (_rel_bucket(jnp.arange(n_dist))[:, None] == jnp.arange(REL_BUCKETS)[None, :]).astype(F32)
    return jnp.einsum('hb,db->hd', tab, onehot, precision=lax.Precision.HIGHEST)


def _skew(g, rows):
    n = g.shape[-1]
    lead = g.shape[:-1]
    tiled = jnp.broadcast_to(g[..., None, :], lead + (rows, n)).reshape(lead + (rows * n,))
    return tiled[..., :rows * (n - 1)].reshape(lead + (rows, n - 1))


def _moba_bias_tiles(bvec, n_near):
    L = A_BLOCK
    rows = []
    for delta in range(n_near):
        lo = delta * L - (L - 1)
        seg = bvec[:, max(lo, 0):delta * L + L]
        if lo < 0:
            seg = jnp.concatenate([jnp.zeros((bvec.shape[0], -lo), F32), seg], axis=1)
        rows.append(jnp.pad(seg, ((0, 0), (0, 1))))
    g = jnp.stack(rows, axis=1)
    return _skew(g, L)[..., L - 1:]


def _swa_bias_tiles(bvec):
    W = C_WINDOW
    u = np.arange(3 * W)
    g = bvec[:, np.clip(2 * W - 1 - u, 0, W - 1)]
    return _skew(g, W)[..., W - 1:3 * W - 1]


def _bounded_flag(q_gain, k_gain, head_dim, extra):
    bound = head_dim ** 0.5 * jnp.max(jnp.abs(q_gain)) * jnp.max(jnp.abs(k_gain)) * 1.02 + extra
    return (bound * LOG2E < EXP2_SAFE).astype(jnp.int32).reshape(1)


def _in_proj_kernel(x_ref, g_ref, w_ref, bg_ref, gn_ref, bd_ref,
                    qa_ref, ka_ref, va_ref, ub_ref, qc_ref, kc_ref, vc_ref, gt_ref):
    xb = _rms(x_ref[...], g_ref[...]).astype(BF16)
    bd = bd_ref[...]

    def proj(c0, width):
        return _dot(xb, w_ref[:, c0:c0 + width])

    def head_norm(t, gain):
        outs = []
        for c in range(0, t.shape[1], 256):
            wd = min(256, t.shape[1] - c)
            tc = t[:, c:c + wd]
            ss = _dot((tc * tc).astype(BF16), bd[:wd, :wd])
            outs.append(tc * lax.rsqrt(ss * (1.0 / HEAD_DIM) + EPS))
        y = outs[0] if len(outs) == 1 else jnp.concatenate(outs, axis=-1)
        return y * gain

    def tile_kv_heads(t):
        lane = lax.broadcasted_iota(jnp.int32, t.shape, 1)
        r = pltpu.roll(t, HEAD_DIM, axis=1)
        h0 = jnp.where(lane < HEAD_DIM, t, r)
        h1 = jnp.where(lane < HEAD_DIM, r, t)
        return jnp.concatenate([h0, h0, h1, h1], axis=-1)

    qa_ref[...] = head_norm(proj(0, WIDTH), gn_ref[0:1, :]).astype(BF16)
    ka_ref[...] = head_norm(proj(WIDTH, WIDTH), gn_ref[1:2, :]).astype(BF16)
    va_ref[...] = proj(2 * WIDTH, WIDTH).astype(BF16)
    ub_ref[...] = proj(3 * WIDTH, WIDTH)
    qc_ref[...] = head_norm(proj(4 * WIDTH, WIDTH), gn_ref[2:3, :]).astype(BF16)
    kv = C_KV_HEADS * HEAD_DIM
    c0 = 5 * WIDTH
    kc = head_norm(proj(c0, kv), gn_ref[3:4, :kv])
    kc_ref[...] = tile_kv_heads(kc).astype(BF16)
    vc_ref[...] = tile_kv_heads(proj(c0 + kv, kv)).astype(BF16)
    c0 += 2 * kv
    for c in range(0, gt_ref.shape[1], WIDTH):
        gl = proj(c0 + c, WIDTH) + bg_ref[:, c:c + WIDTH]
        gt_ref[:, c:c + WIDTH] = jax.nn.sigmoid(gl).astype(BF16)


def _in_proj(x, g, w_in, b_gate, gains, bd):
    n, d = x.shape
    tm = min(ROW_TILE, n)
    n_gate = b_gate.shape[1]
    row = lambda width: pl.BlockSpec((tm, width), lambda i: (i, 0))
    out_shape = [jax.ShapeDtypeStruct((n, WIDTH), BF16)] * 3 + [jax.ShapeDtypeStruct((n, WIDTH), F32)] \
        + [jax.ShapeDtypeStruct((n, WIDTH), BF16)] * 3 + [jax.ShapeDtypeStruct((n, n_gate), BF16)]
    return pl.pallas_call(
        _in_proj_kernel,
        grid=(n // tm,),
        in_specs=[row(d), _const_spec(g.shape), _const_spec(w_in.shape), _const_spec(b_gate.shape),
                  _const_spec(gains.shape), _const_spec(bd.shape)],
        out_specs=[row(WIDTH)] * 7 + [row(n_gate)],
        out_shape=out_shape,
        compiler_params=_params(("parallel",)),
        name="in_proj",
    )(x, g, w_in, b_gate, gains, bd)


def _moba_kernel(flag_ref, far_ref, q_ref, k_ref, v_ref, bias_ref, o_ref,
                 kmean_ref, vt_ref, sel_ref, m_ref, l_ref, l8_ref, acc_ref, s_ref, *, nb, n_near):
    i = pl.program_id(1)
    L = A_BLOCK
    KC = KEY_CHUNK
    PW = 2 * HEAD_DIM
    nbp = kmean_ref.shape[0]

    @pl.when(i == 0)
    def _():
        kmean_ref[...] = jnp.zeros(kmean_ref.shape, F32)
        for j in range(nb):
            kj = k_ref[j * L:(j + 1) * L, :].astype(F32)
            kmean_ref[j:j + 1, :] = jnp.mean(kj, axis=0, keepdims=True)
            vt_ref[j] = v_ref[j * L:(j + 1) * L, :].astype(F32).T.astype(BF16)

    lane = lax.broadcasted_iota(jnp.int32, (L, PW), 1)
    blk = lax.broadcasted_iota(jnp.int32, (nbp, L), 0)
    past = blk < i
    qh = []
    for h in range(A_HEADS):
        pair = slice((h // 2) * PW, (h // 2 + 1) * PW)
        q = q_ref[:, pair]
        qm = jnp.where(lane // HEAD_DIM == h % 2, q, jnp.zeros_like(q))
        qh.append(qm)
        km = kmean_ref[:, pair]
        km_hi = km.astype(BF16)
        km_lo = (km - km_hi.astype(F32)).astype(BF16)
        s = jnp.where(past, _dot_nt(km_hi, qm) + _dot_nt(km_lo, qm), -jnp.inf)
        rank = jnp.zeros((nbp, L), jnp.int32)
        for jp in range(nb):
            sj = s[jp:jp + 1, :]
            ahead = (sj > s) | ((sj == s) & (jp < blk))
            rank = rank + ahead.astype(jnp.int32)
        sel_ref[h] = (past & (rank < A_TOPK)).astype(F32)
        m_ref[h] = jnp.full((1, L), NEG_INF, F32)
        l_ref[h] = jnp.zeros((1, L), F32)
        l8_ref[h] = jnp.zeros((8, L), F32)
    acc_ref[...] = jnp.zeros(acc_ref.shape, F32)
    qpair_t = [jnp.concatenate(qh[2 * hp:2 * hp + 2], axis=0).astype(F32).T.astype(BF16)
               for hp in range(A_HEADS // 2)]

    def causal(c, n):
        kk = c + lax.broadcasted_iota(jnp.int32, (n, L), 0)
        qq = lax.broadcasted_iota(jnp.int32, (n, L), 1)
        return kk <= qq

    def bounded_blocks(blocks):
        units = [(bi, hp, c) for bi in range(len(blocks)) for hp in range(A_HEADS // 2) for c in range(0, L, KC)]

        def qk(u):
            bi, hp, c = units[u]
            kc = k_ref[pl.ds(pl.multiple_of(blocks[bi][0] * L + c, KC), KC), hp * PW:(hp + 1) * PW]
            s_ref[u % (QK_AHEAD + 1)] = _dot(kc, qpair_t[hp])

        for u in range(QK_AHEAD):
            qk(u)
        o = ps = None
        for u, (bi, hp, c) in enumerate(units):
            j, tile_idx, own = blocks[bi]
            if u + QK_AHEAD < len(units):
                qk(u + QK_AHEAD)
            if c == 0:
                o = [jnp.zeros((HEAD_DIM, L), F32)] * 2
                ps = jnp.zeros((8, 2 * L), F32)
            p = jnp.exp2(s_ref[u % (QK_AHEAD + 1)] + bias_ref[hp, tile_idx, pl.ds(c, KC), :])
            if own:
                p = jnp.where(jnp.concatenate([causal(c, KC)] * 2, axis=1), p, 0.0)
            ps = ps + jnp.sum(p.reshape(KC // 8, 8, 2 * L), axis=0)
            pb = p.astype(BF16)
            o = [o[hh] + _dot(vt_ref[j, pl.ds((2 * hp + hh) * HEAD_DIM, HEAD_DIM), pl.ds(c, KC)],
                              pb[:, hh * L:(hh + 1) * L]) for hh in range(2)]
            if c + KC < L:
                continue
            for hh in range(2):
                h = 2 * hp + hh
                rows = slice(h * HEAD_DIM, (h + 1) * HEAD_DIM)
                psh = ps[:, hh * L:(hh + 1) * L]
                if own:
                    acc_ref[rows, :] += o[hh]
                    l8_ref[h] += psh
                else:
                    w = sel_ref[h, pl.ds(j, 1), :]
                    acc_ref[rows, :] += w * o[hh]
                    l8_ref[h] += w * psh

    def run_bounded():
        def tile_of(j):
            return jnp.minimum(i - j, n_near)

        def body(t, carry):
            bounded_blocks([(2 * t, tile_of(2 * t), False), (2 * t + 1, tile_of(2 * t + 1), False)])
            return carry

        lax.fori_loop(0, i // 2, body, 0)

        @pl.when(i % 2 == 1)
        def _():
            bounded_blocks([(i - 1, tile_of(i - 1), False), (i, 0, True)])

        @pl.when(i % 2 == 0)
        def _():
            bounded_blocks([(i, 0, True)])

    def online_tile(j, delta, far, own):
        for h in range(A_HEADS):
            pair = slice((h // 2) * PW, (h // 2 + 1) * PW)
            rows = slice(h * HEAD_DIM, (h + 1) * HEAD_DIM)
            s = _dot_nt(k_ref[pl.ds(pl.multiple_of(j * L, L), L), pair], qh[h])
            if not far:
                s = s + bias_ref[h // 2, delta, :, (h % 2) * L:(h % 2 + 1) * L]
            if own:
                s = jnp.where(causal(0, L), s, NEG_INF)
            mj = jnp.max(s, axis=0, keepdims=True)
            p = jnp.exp2(s - mj)
            lj = jnp.sum(p, axis=0, keepdims=True)
            o = _dot(vt_ref[j, rows, :], p.astype(BF16))
            if far:
                mj = mj + far_ref[h]
            m_old = m_ref[h]
            if own:
                m_new = jnp.maximum(m_old, mj)
                beta = jnp.exp2(mj - m_new)
            else:
                on = sel_ref[h, pl.ds(j, 1), :] > 0.5
                m_new = jnp.where(on, jnp.maximum(m_old, mj), m_old)
                beta = jnp.where(on, jnp.exp2(mj - m_new), 0.0)
            alpha = jnp.exp2(m_old - m_new)
            m_ref[h] = m_new
            l_ref[h] = alpha * l_ref[h] + beta * lj
            acc_ref[rows, :] = alpha * acc_ref[rows, :] + beta * o

    n_far = jnp.maximum(i - (n_near - 1), 0)

    def run(tile):
        def far_body(j, c):
            tile(j, None, True, False)
            return c

        def near_body(j, c):
            tile(j, i - j, False, False)
            return c

        lax.fori_loop(0, n_far, far_body, 0)
        lax.fori_loop(n_far, i, near_body, 0)
        tile(i, 0, False, True)

    @pl.when(flag_ref[0] == 1)
    def _():
        run_bounded()
        for h in range(A_HEADS):
            l_ref[h] = jnp.sum(l8_ref[h], axis=0, keepdims=True)

    @pl.when(flag_ref[0] != 1)
    def _():
        run(online_tile)

    o_t = jnp.concatenate([acc_ref[h * HEAD_DIM:(h + 1) * HEAD_DIM, :] / l_ref[h] for h in range(A_HEADS)],
                          axis=0)
    o_ref[...] = o_t.T.astype(BF16)


def _moba(q, k, v, bias_tiles, far, flag, batch, seq):
    n = q.shape[0]
    L = A_BLOCK
    nb = seq // L
    nbp = max(8, -(-nb // 8) * 8)
    n_near = bias_tiles.shape[1] - 1
    kern = functools.partial(_moba_kernel, nb=nb, n_near=n_near)
    return pl.pallas_call(
        kern,
        grid=(batch, nb),
        in_specs=[_smem_spec(), _smem_spec(),
                  pl.BlockSpec((L, WIDTH), lambda b, i: (b * nb + i, 0)),
                  pl.BlockSpec((seq, WIDTH), lambda b, i: (b, 0)),
                  pl.BlockSpec((seq, WIDTH), lambda b, i: (b, 0)),
                  _const_spec(bias_tiles.shape)],
        out_specs=pl.BlockSpec((L, WIDTH), lambda b, i: (b * nb + i, 0)),
        out_shape=jax.ShapeDtypeStruct((n, WIDTH), BF16),
        scratch_shapes=[pltpu.VMEM((nbp, WIDTH), F32),
                        pltpu.VMEM((nb, WIDTH, L), BF16),
                        pltpu.VMEM((A_HEADS, nbp, L), F32),
                        pltpu.VMEM((A_HEADS, 1, L), F32),
                        pltpu.VMEM((A_HEADS, 1, L), F32),
                        pltpu.VMEM((A_HEADS, 8, L), F32),
                        pltpu.VMEM((WIDTH, L), F32),
                        pltpu.VMEM((QK_AHEAD + 1, KEY_CHUNK, 2 * L), F32)],
        compiler_params=_params(("arbitrary", "arbitrary")),
        name="moba",
    )(flag, far, q, k, v, bias_tiles)


def _swa_kernel(flag_ref, sink_ref, q_ref, kp_ref, kc_ref, vp_ref, vc_ref, bias_ref, o_ref, s_ref):
    g = pl.program_id(0)
    chunk = pl.program_id(2)
    W = C_WINDOW
    G = C_HEADS // C_KV_HEADS
    nsub = q_ref.shape[0] // W
    gw = q_ref.shape[1]
    kall = jnp.concatenate([kp_ref[...], kc_ref[...]], axis=0)
    vall = jnp.concatenate([vp_ref[...], vc_ref[...]], axis=0)
    lane = lax.broadcasted_iota(jnp.int32, (W, gw), 1)
    qi = lax.broadcasted_iota(jnp.int32, (W, 2 * W), 0)
    kj = lax.broadcasted_iota(jnp.int32, (W, 2 * W), 1)
    dist = W + qi - kj
    band = (dist >= 0) & (dist < W)

    def block_mask(r):
        return band & ((chunk > 0) | (kj >= W)) if r == 0 else band

    def head_query(r, hh):
        q = q_ref[r * W:(r + 1) * W, :]
        return jnp.where(lane // HEAD_DIM == hh, q, jnp.zeros_like(q))

    def bounded_heads():
        units = [(r, hh) for r in range(nsub) for hh in range(G)]
        kall_t = kall.astype(F32).T.astype(BF16)

        def qk(u):
            r, hh = units[u]
            s_ref[u % (QK_AHEAD + 1)] = _dot(head_query(r, hh), kall_t[:, r * W:(r + 2) * W])

        for u in range(QK_AHEAD):
            qk(u)
        out = None
        for u, (r, hh) in enumerate(units):
            if u + QK_AHEAD < len(units):
                qk(u + QK_AHEAD)
            if hh == 0:
                out = jnp.zeros((W, gw), F32)
            s = s_ref[u % (QK_AHEAD + 1)] + bias_ref[hh]
            pb = jnp.where(block_mask(r), jnp.exp2(s), 0.0).astype(BF16)
            sink = jnp.exp2(jnp.full((1, 128), sink_ref[g * G + hh], F32))
            inv = 1.0 / (_dot(pb, jnp.ones((2 * W, 128), BF16)) + sink)
            o = _dot(pb, vall[r * W:(r + 2) * W]) * jnp.concatenate([inv] * (gw // 128), axis=-1)
            out = jnp.where(lane // HEAD_DIM == hh, o, out)
            if hh == G - 1:
                o_ref[r * W:(r + 1) * W, :] = out.astype(BF16)

    def online_heads():
        for r in range(nsub):
            kcat = kall[r * W:(r + 2) * W]
            vcat = vall[r * W:(r + 2) * W]
            out = jnp.zeros((W, gw), F32)
            for hh in range(G):
                s = jnp.where(block_mask(r), _dot_nt(head_query(r, hh), kcat) + bias_ref[hh], NEG_INF)
                sink = sink_ref[g * G + hh]
                m = jnp.maximum(jnp.max(s, axis=-1, keepdims=True), sink)
                p = jnp.exp2(s - m)
                den = jnp.sum(p, axis=-1, keepdims=True) + jnp.exp2(sink - m)
                o = _dot(p.astype(BF16), vcat) / den
                out = jnp.where(lane // HEAD_DIM == hh, o, out)
            o_ref[r * W:(r + 1) * W, :] = out.astype(BF16)

    @pl.when(flag_ref[0] == 1)
    def _():
        bounded_heads()

    @pl.when(flag_ref[0] != 1)
    def _():
        online_heads()


def _swa(q, k_t, v_t, bias_tiles, sinks, flag, batch, seq):
    n = q.shape[0]
    W = C_WINDOW
    G = C_HEADS // C_KV_HEADS
    gw = G * HEAD_DIM
    nsub = min(SWA_BLOCKS, seq // W)
    nchunk = seq // (W * nsub)
    cur = lambda g, b, j: (b * nchunk + j, g)
    prev = lambda g, b, j: (jnp.maximum((b * nchunk + j) * nsub - 1, 0), g)
    return pl.pallas_call(
        _swa_kernel,
        grid=(C_KV_HEADS, batch, nchunk),
        in_specs=[_smem_spec(), _smem_spec(),
                  pl.BlockSpec((nsub * W, gw), cur),
                  pl.BlockSpec((W, gw), prev), pl.BlockSpec((nsub * W, gw), cur),
                  pl.BlockSpec((W, gw), prev), pl.BlockSpec((nsub * W, gw), cur),
                  pl.BlockSpec((G, W, 2 * W), lambda g, b, j: (g, 0, 0))],
        out_specs=pl.BlockSpec((nsub * W, gw), cur),
        out_shape=jax.ShapeDtypeStruct((n, WIDTH), BF16),
        scratch_shapes=[pltpu.VMEM((QK_AHEAD + 1, W, 2 * W), F32)],
        compiler_params=_params(("arbitrary", "arbitrary", "arbitrary")),
        name="swa",
    )(flag, sinks, q, k_t, k_t, v_t, v_t, bias_tiles)


def _seg_norm(t, gain, seg):
    outs = []
    for c in range(0, t.shape[1], seg):
        tc = t[:, c:c + seg]
        outs.append(tc * lax.rsqrt(jnp.mean(tc * tc, axis=-1, keepdims=True) + EPS))
    return jnp.concatenate(outs, axis=-1) * gain


def _mem_kv_kernel(mem_ref, g_ref, wk_ref, wv_ref, kg_ref, k_ref, v_ref):
    mb = _rms(mem_ref[...], g_ref[...]).astype(BF16)
    k_ref[...] = _seg_norm(_dot(mb, wk_ref[...]), kg_ref[...], X_HEAD_DIM).astype(BF16)
    v_ref[...] = _dot(mb, wv_ref[...]).astype(BF16)


def _mem_kv(mem, g, w_xk, w_xv, k_gain):
    n, d = mem.shape
    tm = min(ROW_TILE, n)
    row = lambda width: pl.BlockSpec((tm, width), lambda i: (i, 0))
    return pl.pallas_call(
        _mem_kv_kernel,
        grid=(n // tm,),
        in_specs=[row(d), _const_spec(g.shape), _const_spec(w_xk.shape), _const_spec(w_xv.shape),
                  _const_spec(k_gain.shape)],
        out_specs=[row(WIDTH), row(WIDTH)],
        out_shape=[jax.ShapeDtypeStruct((n, WIDTH), BF16)] * 2,
        compiler_params=_params(("parallel",)),
        name="mem_kv",
    )(mem, g, w_xk, w_xv, k_gain)


def _merge_cross_kernel(x_ref, oa_ref, ub_ref, halo_ref, oc_ref, gt_ref, pw_ref, ps_ref, wb_ref, wo_ref,
                        gx_ref, wq_ref, qg_ref, km_ref, vm_ref, wxo_ref, o_ref, *, seq):
    tm = x_ref.shape[0]
    d = x_ref.shape[1]
    t0 = (pl.program_id(0) * tm) % seq
    H = POOL_HALO

    halo = jnp.where(t0 > 0, halo_ref[...], 0.0)
    pos = t0 + lax.broadcasted_iota(jnp.int32, (tm, B_GROUP_DIM), 0)
    mixed = []
    for gi, win in enumerate(B_WINDOWS):
        cols = slice(gi * B_GROUP_DIM, (gi + 1) * B_GROUP_DIM)
        cur = ub_ref[:, cols]
        acc = jnp.concatenate([halo[:, cols], cur], axis=0)
        step = 1
        while step < win:
            acc = acc + pltpu.roll(acc, step, axis=0)
            step *= 2
        cnt = jnp.minimum(pos + 1, win).astype(F32)
        pooled = acc[H:, :] / cnt - cur
        mixed.append(_dot(pooled.astype(BF16), pw_ref[gi]))
    ob = jnp.concatenate(mixed, axis=-1) * ps_ref[...]

    merged = gt_ref[:, 0:d].astype(F32) * _dot(oa_ref[...], wb_ref[0])
    merged = merged + gt_ref[:, d:2 * d].astype(F32) * _dot(ob.astype(BF16), wb_ref[1])
    merged = merged + gt_ref[:, 2 * d:3 * d].astype(F32) * _dot(oc_ref[...], wb_ref[2])
    x1 = x_ref[...] + _dot(merged.astype(BF16), wo_ref[...])

    xb = _rms(x1, gx_ref[...]).astype(BF16)
    qn = _seg_norm(_dot(xb, wq_ref[...]), qg_ref[...], X_HEAD_DIM).astype(BF16)
    heads = []
    for h in range(X_HEADS):
        cols = slice(h * X_HEAD_DIM, (h + 1) * X_HEAD_DIM)
        s = _dot_nt(qn[:, cols], km_ref[:, cols])
        p = jnp.exp(s - jnp.max(s, axis=-1, keepdims=True))
        inv = 1.0 / jnp.sum(p, axis=-1, keepdims=True)
        heads.append(_dot(p.astype(BF16), vm_ref[:, cols]) * inv)
    o = jnp.concatenate(heads, axis=-1).astype(BF16)
    o_ref[...] = x1 + _dot(o, wxo_ref[...])


def _merge_cross(x, oa, ub, oc, gates, pool_w, pool_scale, w_branch, w_out,
                 g_cross, w_xq, q_gain, k_mem, v_mem, w_xo, seq, mem_len):
    n, d = x.shape
    tm = min(ROW_TILE, seq)
    row = lambda width: pl.BlockSpec((tm, width), lambda i: (i, 0))
    halo = pl.BlockSpec((POOL_HALO, WIDTH), lambda i: (jnp.maximum(i * (tm // POOL_HALO) - 1, 0), 0))
    mem = pl.BlockSpec((mem_len, WIDTH), lambda i: ((i * tm) // seq, 0))
    kern = functools.partial(_merge_cross_kernel, seq=seq)
    return pl.pallas_call(
        kern,
        grid=(n // tm,),
        in_specs=[row(d), row(WIDTH), row(WIDTH), halo, row(WIDTH), row(gates.shape[1]),
                  _const_spec(pool_w.shape), _const_spec(pool_scale.shape), _const_spec(w_branch.shape),
                  _const_spec(w_out.shape), _const_spec(g_cross.shape), _const_spec(w_xq.shape),
                  _const_spec(q_gain.shape), mem, mem, _const_spec(w_xo.shape)],
        out_specs=row(d),
        out_shape=jax.ShapeDtypeStruct((n, d), F32),
        compiler_params=_params(("parallel",)),
        name="merge_cross",
    )(x, oa, ub, ub, oc, gates, pool_w, pool_scale, w_branch, w_out,
      g_cross, w_xq, q_gain, k_mem, v_mem, w_xo)


def _swiglu_chunks(xb, w1_ref, w3_ref, w2_ref, acc):
    for c in range(0, w1_ref.shape[1], FF_CHUNK):
        h1 = _dot(xb, w1_ref[:, c:c + FF_CHUNK])
        h3 = _dot(xb, w3_ref[:, c:c + FF_CHUNK])
        acc = acc + _dot((jax.nn.silu(h1) * h3).astype(BF16), w2_ref[c:c + FF_CHUNK, :])
    return acc


def _ffn_kernel(x_ref, g_ref, w1_ref, w3_ref, w2_ref, o_ref):
    x = x_ref[...]
    xb = _rms(x, g_ref[...]).astype(BF16)
    o_ref[...] = _swiglu_chunks(xb, w1_ref, w3_ref, w2_ref, x)


def _ffn(x, g, w1, w3, w2):
    n, d = x.shape
    tm = min(ROW_TILE, n)
    row = pl.BlockSpec((tm, d), lambda i: (i, 0))
    return pl.pallas_call(
        _ffn_kernel,
        grid=(n // tm,),
        in_specs=[row, _const_spec(g.shape), _const_spec(w1.shape), _const_spec(w3.shape),
                  _const_spec(w2.shape)],
        out_specs=row,
        out_shape=jax.ShapeDtypeStruct((n, d), F32),
        compiler_params=_params(("parallel",)),
        name="ffn_dense",
    )(x, g, w1, w3, w2)


def _store_token_tiles(ref, x):
    rows = x.shape[0]
    for k in range(x.shape[1] // 128):
        ref[pl.ds(k, rows, stride=SUBLANES), :] = x[:, k * 128:(k + 1) * 128]


def _load_token_tiles(ref, rows, lead=None):
    idx = (lambda k: (pl.ds(k, rows, stride=SUBLANES), slice(None))) if lead is None else \
        (lambda k: (lead, pl.ds(k, rows, stride=SUBLANES), slice(None)))
    return jnp.concatenate([ref[idx(k)] for k in range(SUBLANES)], axis=-1)


def _router_kernel(x_ref, g_ref, r_ref, xn_ref, lg_ref):
    xn = _rms(x_ref[...], g_ref[...])
    _store_token_tiles(xn_ref, xn)
    r = r_ref[...]
    x_hi = xn.astype(BF16)
    x_lo = (xn - x_hi.astype(F32)).astype(BF16)
    r_hi = r.astype(BF16)
    r_lo = (r - r_hi.astype(F32)).astype(BF16)
    lg_ref[...] = _dot(x_hi, r_hi) + (_dot(x_lo, r_hi) + _dot(x_hi, r_lo))


def _router(x, g, router_padded):
    n, d = x.shape
    tm = min(ROW_TILE, n)
    row = lambda width: pl.BlockSpec((tm, width), lambda i: (i, 0))
    return pl.pallas_call(
        _router_kernel,
        grid=(n // tm,),
        in_specs=[row(d), _const_spec(g.shape), _const_spec(router_padded.shape)],
        out_specs=[pl.BlockSpec((tm * SUBLANES, d // SUBLANES), lambda i: (i, 0)),
                   row(router_padded.shape[1])],
        out_shape=[jax.ShapeDtypeStruct((n * SUBLANES, d // SUBLANES), F32),
                   jax.ShapeDtypeStruct((n, router_padded.shape[1]), F32)],
        compiler_params=_params(("parallel",)),
        name="router",
    )(x, g, router_padded)


def _expert_kernel(be_ref, nu_ref, tok_ref, tokn_ref, asg_ref, asgp_ref, xn_hbm, w1_ref, w3_ref, w2_ref, y_hbm,
                   xg_ref, xb_ref, y_ref, ys_ref, gsem, ssem, *, nb, nc):
    b = pl.program_id(0)
    c = pl.program_id(1)
    tb = xb_ref.shape[0]
    T = SUBLANES
    U = 8
    groups = tb // U
    piece = -(-groups // nc)
    slot = b % 2

    def tile(ref, first_row):
        return ref.at[pl.ds(pl.multiple_of(first_row, T), T), :]

    def gather(idx_ref, s, lo, hi):
        def body(g, carry):
            for k in range(U):
                r = g * U + k
                pltpu.make_async_copy(tile(xn_hbm, idx_ref[0, 0, r]), tile(xg_ref.at[s], r * T),
                                      gsem.at[s]).start()
            return carry
        lax.fori_loop(lo, hi, body, 0)

    def wait_gather(s):
        pltpu.make_async_copy(xn_hbm.at[pl.ds(0, tb * T), :], xg_ref.at[s], gsem.at[s]).wait()

    def scatter(idx_ref, s, lo, hi):
        def body(g, carry):
            for k in range(U):
                r = g * U + k
                pltpu.make_async_copy(tile(ys_ref.at[s], r * T), tile(y_hbm, idx_ref[0, 0, r]),
                                      ssem.at[s]).start()
            return carry
        lax.fori_loop(lo, hi, body, 0)

    def wait_scatter(s):
        pltpu.make_async_copy(ys_ref.at[s], y_hbm.at[pl.ds(0, tb * T), :], ssem.at[s]).wait()

    @pl.when(c == 0)
    def _():
        @pl.when(b == 0)
        def _():
            gather(tok_ref, 0, 0, groups)

        wait_gather(slot)
        xb_ref[...] = _load_token_tiles(xg_ref, tb, lead=slot).astype(BF16)
        y_ref[...] = jnp.zeros(y_ref.shape, F32)

    lo = c * piece
    hi = jnp.minimum(lo + piece, groups)

    @pl.when(b + 1 < nb)
    def _():
        gather(tokn_ref, 1 - slot, lo, hi)

    @pl.when(b >= 1)
    def _():
        scatter(asgp_ref, 1 - slot, lo, hi)

    @pl.when(b < nu_ref[0])
    def _():
        xb = xb_ref[...]
        h1 = _dot(xb, w1_ref[...])
        h3 = _dot(xb, w3_ref[...])
        y_ref[...] += _dot((jax.nn.silu(h1) * h3).astype(BF16), w2_ref[...])

    @pl.when(c == nc - 1)
    def _():
        _store_token_tiles(ys_ref.at[slot], y_ref[...])

        @pl.when(b >= 1)
        def _():
            wait_scatter(1 - slot)

        @pl.when(b == nb - 1)
        def _():
            scatter(asg_ref, slot, 0, groups)
            wait_scatter(slot)


def _experts(xn, slot_tok, slot_row, block_e, n_used, w1, w3, w2, tb):
    T = SUBLANES
    d = w1.shape[1]
    nc = w1.shape[2] // FF_CHUNK
    p = slot_tok.shape[0]
    nblk = p // tb
    tok3 = (slot_tok * T).reshape(nblk, 1, tb)
    row3 = (slot_row * T).reshape(nblk, 1, tb)
    idx_spec = lambda imap: pl.BlockSpec((1, 1, tb), imap, memory_space=pltpu.SMEM)
    chunk = lambda b, c, nu: jnp.where(b < nu[0], c, nc - 1)
    grid_spec = pltpu.PrefetchScalarGridSpec(
        num_scalar_prefetch=2,
        grid=(nblk, nc),
        in_specs=[idx_spec(lambda b, c, be, nu: (b, 0, 0)),
                  idx_spec(lambda b, c, be, nu: (jnp.minimum(b + 1, nblk - 1), 0, 0)),
                  idx_spec(lambda b, c, be, nu: (b, 0, 0)),
                  idx_spec(lambda b, c, be, nu: (jnp.maximum(b - 1, 0), 0, 0)),
                  pl.BlockSpec(memory_space=pl.ANY),
                  pl.BlockSpec((None, d, FF_CHUNK), lambda b, c, be, nu: (be[b], 0, chunk(b, c, nu))),
                  pl.BlockSpec((None, d, FF_CHUNK), lambda b, c, be, nu: (be[b], 0, chunk(b, c, nu))),
                  pl.BlockSpec((None, FF_CHUNK, d), lambda b, c, be, nu: (be[b], chunk(b, c, nu), 0))],
        out_specs=pl.BlockSpec(memory_space=pl.ANY),
        scratch_shapes=[pltpu.VMEM((2, tb * T, d // T), F32), pltpu.VMEM((tb, d), BF16), pltpu.VMEM((tb, d), F32),
                        pltpu.VMEM((2, tb * T, d // T), F32),
                        pltpu.SemaphoreType.DMA((2,)), pltpu.SemaphoreType.DMA((2,))],
    )
    return pl.pallas_call(
        functools.partial(_expert_kernel, nb=nblk, nc=nc),
        grid_spec=grid_spec,
        out_shape=jax.ShapeDtypeStruct((p * T, d // T), F32),
        compiler_params=_params(("arbitrary", "arbitrary")),
        name="experts",
    )(block_e, n_used, tok3, tok3, row3, row3, xn, w1, w3, w2)


def _combine_kernel(x_ref, y0_ref, y1_ref, g_ref, o_ref):
    tm = x_ref.shape[0]
    g = g_ref[...]
    o_ref[...] = x_ref[...] + g[:, 0:1] * _load_token_tiles(y0_ref, tm) + g[:, 1:2] * _load_token_tiles(y1_ref, tm)


def _combine(x, y, gate):
    n, d = x.shape
    tm = min(ROW_TILE, n)
    T = SUBLANES
    row = lambda width: pl.BlockSpec((tm, width), lambda i: (i, 0))
    tiles = lambda choice: pl.BlockSpec((tm * T, d // T), lambda i: (choice * (n // tm) + i, 0))
    return pl.pallas_call(
        _combine_kernel,
        grid=(n // tm,),
        in_specs=[row(d), tiles(0), tiles(1), row(gate.shape[1])],
        out_specs=row(d),
        out_shape=jax.ShapeDtypeStruct((n, d), F32),
        compiler_params=_params(("parallel",)),
        name="combine",
    )(x, y, y, gate)


def _moe(x, g, router, w1, w3, w2):
    n, d = x.shape
    a = n * TOP_K
    tb = min(MOE_TILE, a)
    router_padded = jnp.pad(router, ((0, 0), (0, 128 - N_EXPERTS)))
    xn, logits = _router(x, g, router_padded)
    top_logit, top_e = lax.top_k(logits[:, :N_EXPERTS], TOP_K)
    gate = jax.nn.softmax(top_logit, axis=-1)
    e_flat = top_e.reshape(a)
    onehot = (e_flat[:, None] == jnp.arange(N_EXPERTS)[None, :]).astype(jnp.int32)
    csum = jnp.cumsum(onehot, axis=0)
    rank = jnp.sum(onehot * csum, axis=1) - 1
    counts = csum[-1]
    padded = (counts + tb - 1) // tb * tb
    pend = jnp.cumsum(padded)
    pstart = pend - padded
    dest = (jnp.sum(onehot * pstart[None, :], axis=1) + rank).astype(jnp.int32)
    n_blocks = a // tb + N_EXPERTS
    p = n_blocks * tb
    slot_asg = jnp.full((p,), -1, jnp.int32).at[dest].set(jnp.arange(a, dtype=jnp.int32))
    is_pad = slot_asg < 0
    slot_row = jnp.where(is_pad, a - 1 + jnp.cumsum(is_pad.astype(jnp.int32)),
                         (slot_asg % TOP_K) * n + slot_asg // TOP_K)
    slot_tok = jnp.where(is_pad, 0, slot_asg // TOP_K)
    block_e = jnp.minimum(jnp.searchsorted(pend, jnp.arange(n_blocks) * tb, side='right'),
                          N_EXPERTS - 1).astype(jnp.int32)
    n_used = (pend[-1] // tb).astype(jnp.int32).reshape(1)
    y = _experts(xn, slot_tok, slot_row, block_e, n_used, w1, w3, w2, tb)
    gate_padded = jnp.pad(gate, ((0, 0), (0, 128 - TOP_K)))
    return _combine(x, y, gate_padded)


def _tile_gain(gain, width, scale=1.0):
    return jnp.tile(gain, width // gain.shape[0]) * scale


def kernel(x, mem, rel_bias, norm_mix, w_in, b_gate, a_q_gain, a_k_gain, pool_w, pool_scale,
           c_q_gain, c_k_gain, c_sinks, w_branch, w_out, norm_cross, norm_mem, w_xq, w_xk, w_xv,
           x_q_gain, x_k_gain, w_xo, norm_ffn, ffn_w1, ffn_w3, ffn_w2, router, moe_w1, moe_w3, moe_w2):
    batch, seq, d = x.shape
    mem_len = mem.shape[1]
    depth = norm_mix.shape[0]
    xs = x.reshape(batch * seq, d)
    mems = mem.reshape(batch * mem_len, d)

    tab_a = rel_bias[:, :A_HEADS].T
    tab_c = rel_bias[:, A_HEADS:].T
    nb = seq // A_BLOCK
    n_near = min(nb, (_saturation_distance() + 2 * A_BLOCK - 2) // A_BLOCK)
    far_a = tab_a[:, REL_BUCKETS - 1] * LOG2E
    bias_a = _moba_bias_tiles(_bias_by_distance(tab_a, n_near * A_BLOCK), n_near) * LOG2E
    bias_a = jnp.concatenate([bias_a, jnp.broadcast_to(far_a[:, None, None, None],
                                                       (A_HEADS, 1, A_BLOCK, A_BLOCK))], axis=1)
    bias_a = jnp.concatenate([bias_a[0::2], bias_a[1::2]], axis=-1)
    bias_c = _swa_bias_tiles(_bias_by_distance(tab_c, C_WINDOW)) * LOG2E
    seg = np.arange(256) // HEAD_DIM
    bd = jnp.asarray(seg[:, None] == seg[None, :], BF16)

    row = lambda v: v.reshape(1, -1)
    for l in range(depth):
        scale = HEAD_DIM ** -0.5 * LOG2E
        gains = jnp.stack([_tile_gain(a_q_gain[l], WIDTH, scale), _tile_gain(a_k_gain[l], WIDTH),
                           _tile_gain(c_q_gain[l], WIDTH, scale), _tile_gain(c_k_gain[l], WIDTH)])
        flag_a = _bounded_flag(a_q_gain[l], a_k_gain[l], HEAD_DIM, jnp.max(jnp.abs(tab_a)))
        flag_c = _bounded_flag(c_q_gain[l], c_k_gain[l], HEAD_DIM,
                               jnp.maximum(jnp.max(jnp.abs(tab_c)), jnp.max(jnp.abs(c_sinks[l]))))
        qa, ka, va, ub, qc, kc, vc, gates = _in_proj(
            xs, row(norm_mix[l]), w_in[l].astype(BF16), row(b_gate[l]), gains, bd)
        oa = _moba(qa, ka, va, bias_a, far_a, flag_a, batch, seq)
        oc = _swa(qc, kc, vc, bias_c, c_sinks[l] * LOG2E, flag_c, batch, seq)
        k_mem, v_mem = _mem_kv(mems, row(norm_mem[l]), w_xk[l].astype(BF16), w_xv[l].astype(BF16),
                               row(_tile_gain(x_k_gain[l], WIDTH)))
        xs = _merge_cross(xs, oa, ub, oc, gates, pool_w[l].astype(BF16), row(pool_scale[l]),
                          w_branch[l].astype(BF16), w_out[l].astype(BF16), row(norm_cross[l]),
                          w_xq[l].astype(BF16), row(_tile_gain(x_q_gain[l], WIDTH, X_HEAD_DIM ** -0.5)),
                          k_mem, v_mem, w_xo[l].astype(BF16), seq, mem_len)
        i = l // 2
        if l % 2 == 0:
            xs = _ffn(xs, row(norm_ffn[l]), ffn_w1[i].astype(BF16), ffn_w3[i].astype(BF16),
                      ffn_w2[i].astype(BF16))
        else:
            xs = _moe(xs, row(norm_ffn[l]), router[i], moe_w1[i].astype(BF16), moe_w3[i].astype(BF16),
                      moe_w2[i].astype(BF16))
    return xs.reshape(batch, seq, d)
```

```python
import functools
import math

import jax
import jax.numpy as jnp
import numpy as np
from jax import lax
from jax.experimental import pallas as pl
from jax.experimental.pallas import tpu as pltpu

F32 = jnp.float32
BF16 = jnp.bfloat16

HEAD_DIM = 64
A_HEADS = 8
A_BLOCK = 256
A_TOPK = 3
B_GROUPS = 4
B_GROUP_DIM = 128
B_WINDOWS = (2, 4, 8, 16)
C_HEADS = 8
C_KV_HEADS = 2
C_WINDOW = 128
REL_BUCKETS = 32
REL_MAX_DIST = 1024
X_HEADS = 4
X_HEAD_DIM = 128
N_EXPERTS = 8
TOP_K = 2
EPS = 1e-6
NEG_INF = -1e30
LOG2E = math.log2(math.e)

WIDTH = 512
POOL_HALO = 16
ROW_TILE = 512
MOE_TILE = 1024
FF_CHUNK = 512
SUBLANES = 8
KEY_CHUNK = 128
QK_AHEAD = 3
SWA_BLOCKS = 4
VMEM_LIMIT = 56 * 1024 * 1024
EXP2_SAFE = 100.0


def _dot(a, b):
    return jnp.dot(a, b, preferred_element_type=F32)


def _dot_nt(a, b):
    return lax.dot_general(a, b, (((1,), (1,)), ((), ())), preferred_element_type=F32)


def _rms(x, g):
    ms = jnp.mean(x * x, axis=-1, keepdims=True)
    return x * lax.rsqrt(ms + EPS) * g


def _const_spec(shape):
    zeros = (0,) * len(shape)
    return pl.BlockSpec(shape, lambda *_: zeros, pipeline_mode=pl.Buffered(1))


def _smem_spec():
    return pl.BlockSpec(memory_space=pltpu.SMEM)


def _params(sem):
    return pltpu.CompilerParams(dimension_semantics=sem, vmem_limit_bytes=VMEM_LIMIT)


def _rel_bucket(dist):
    n = jnp.maximum(dist, 0)
    max_exact = REL_BUCKETS // 2
    nf = jnp.maximum(n, 1).astype(jnp.float32)
    large = max_exact + (jnp.log(nf / max_exact) / math.log(REL_MAX_DIST / max_exact)
                         * (REL_BUCKETS - max_exact)).astype(jnp.int32)
    large = jnp.minimum(large, REL_BUCKETS - 1)
    return jnp.where(n < max_exact, n, large)


def _saturation_distance():
    ratio = REL_MAX_DIST / (REL_BUCKETS // 2)
    return int(math.ceil((REL_BUCKETS // 2) * ratio ** ((REL_BUCKETS - 1 - REL_BUCKETS // 2 + 0.5)
                                                       / (REL_BUCKETS - REL_BUCKETS // 2))))


def _bias_by_distance(tab, n_dist):
    onehot = (_rel_bucket(jnp.arange(n_dist))[:, None] == jnp.arange(REL_BUCKETS)[None, :]).astype(F32)
    return jnp.einsum('hb,db->hd', tab, onehot, precision=lax.Precision.HIGHEST)


def _skew(g, rows):
    n = g.shape[-1]
    lead = g.shape[:-1]
    tiled = jnp.broadcast_to(g[..., None, :], lead + (rows, n)).reshape(lead + (rows * n,))
    return tiled[..., :rows * (n - 1)].reshape(lead + (rows, n - 1))


def _moba_bias_tiles(bvec, n_near):
    L = A_BLOCK
    rows = []
    for delta in range(n_near):
        lo = delta * L - (L - 1)
        seg = bvec[:, max(lo, 0):delta * L + L]
        if lo < 0:
            seg = jnp.concatenate([jnp.zeros((bvec.shape[0], -lo), F32), seg], axis=1)
        rows.append(jnp.pad(seg, ((0, 0), (0, 1))))
    g = jnp.stack(rows, axis=1)
    return _skew(g, L)[..., L - 1:]


def _swa_bias_tiles(bvec):
    W = C_WINDOW
    u = np.arange(3 * W)
    g = bvec[:, np.clip(2 * W - 1 - u, 0, W - 1)]
    return _skew(g, W)[..., W - 1:3 * W - 1]


def _bounded_flag(q_gain, k_gain, head_dim, extra):
    bound = head_dim ** 0.5 * jnp.max(jnp.abs(q_gain)) * jnp.max(jnp.abs(k_gain)) * 1.02 + extra
    return (bound * LOG2E < EXP2_SAFE).astype(jnp.int32).reshape(1)


def _in_proj_kernel(x_ref, g_ref, w_ref, bg_ref, gn_ref, bd_ref,
                    qa_ref, ka_ref, va_ref, ub_ref, qc_ref, kc_ref, vc_ref, gt_ref):
    xb = _rms(x_ref[...], g_ref[...]).astype(BF16)
    bd = bd_ref[...]

    def proj(c0, width):
        return _dot(xb, w_ref[:, c0:c0 + width])

    def head_norm(t, gain):
        outs = []
        for c in range(0, t.shape[1], 256):
            wd = min(256, t.shape[1] - c)
            tc = t[:, c:c + wd]
            ss = _dot((tc * tc).astype(BF16), bd[:wd, :wd])
            outs.append(tc * lax.rsqrt(ss * (1.0 / HEAD_DIM) + EPS))
        y = outs[0] if len(outs) == 1 else jnp.concatenate(outs, axis=-1)
        return y * gain

    def tile_kv_heads(t):
        lane = lax.broadcasted_iota(jnp.int32, t.shape, 1)
        r = pltpu.roll(t, HEAD_DIM, axis=1)
        h0 = jnp.where(lane < HEAD_DIM, t, r)
        h1 = jnp.where(lane < HEAD_DIM, r, t)
        return jnp.concatenate([h0, h0, h1, h1], axis=-1)

    qa_ref[...] = head_norm(proj(0, WIDTH), gn_ref[0:1, :]).astype(BF16)
    ka_ref[...] = head_norm(proj(WIDTH, WIDTH), gn_ref[1:2, :]).astype(BF16)
    va_ref[...] = proj(2 * WIDTH, WIDTH).astype(BF16)
    ub_ref[...] = proj(3 * WIDTH, WIDTH)
    qc_ref[...] = head_norm(proj(4 * WIDTH, WIDTH), gn_ref[2:3, :]).astype(BF16)
    kv = C_KV_HEADS * HEAD_DIM
    c0 = 5 * WIDTH
    kc = head_norm(proj(c0, kv), gn_ref[3:4, :kv])
    kc_ref[...] = tile_kv_heads(kc).astype(BF16)
    vc_ref[...] = tile_kv_heads(proj(c0 + kv, kv)).astype(BF16)
    c0 += 2 * kv
    for c in range(0, gt_ref.shape[1], WIDTH):
        gl = proj(c0 + c, WIDTH) + bg_ref[:, c:c + WIDTH]
        gt_ref[:, c:c + WIDTH] = jax.nn.sigmoid(gl).astype(BF16)


def _in_proj(x, g, w_in, b_gate, gains, bd):
    n, d = x.shape
    tm = min(ROW_TILE, n)
    n_gate = b_gate.shape[1]
    row = lambda width: pl.BlockSpec((tm, width), lambda i: (i, 0))
    out_shape = [jax.ShapeDtypeStruct((n, WIDTH), BF16)] * 3 + [jax.ShapeDtypeStruct((n, WIDTH), F32)] \
        + [jax.ShapeDtypeStruct((n, WIDTH), BF16)] * 3 + [jax.ShapeDtypeStruct((n, n_gate), BF16)]
    return pl.pallas_call(
        _in_proj_kernel,
        grid=(n // tm,),
        in_specs=[row(d), _const_spec(g.shape), _const_spec(w_in.shape), _const_spec(b_gate.shape),
                  _const_spec(gains.shape), _const_spec(bd.shape)],
        out_specs=[row(WIDTH)] * 7 + [row(n_gate)],
        out_shape=out_shape,
        compiler_params=_params(("parallel",)),
        name="in_proj",
    )(x, g, w_in, b_gate, gains, bd)


def _moba_kernel(flag_ref, far_ref, q_ref, k_ref, v_ref, bias_ref, o_ref,
                 kmean_ref, vt_ref, sel_ref, m_ref, l_ref, l8_ref, acc_ref, s_ref, *, nb, n_near):
    i = pl.program_id(1)
    L = A_BLOCK
    KC = KEY_CHUNK
    PW = 2 * HEAD_DIM
    nbp = kmean_ref.shape[0]

    @pl.when(i == 0)
    def _():
        kmean_ref[...] = jnp.zeros(kmean_ref.shape, F32)
        for j in range(nb):
            kj = k_ref[j * L:(j + 1) * L, :].astype(F32)
            kmean_ref[j:j + 1, :] = jnp.mean(kj, axis=0, keepdims=True)
            vt_ref[j] = v_ref[j * L:(j + 1) * L, :].astype(F32).T.astype(BF16)

    lane = lax.broadcasted_iota(jnp.int32, (L, PW), 1)
    blk = lax.broadcasted_iota(jnp.int32, (nbp, L), 0)
    past = blk < i
    qh = []
    for h in range(A_HEADS):
        pair = slice((h // 2) * PW, (h // 2 + 1) * PW)
        q = q_ref[:, pair]
        qm = jnp.where(lane // HEAD_DIM == h % 2, q, jnp.zeros_like(q))
        qh.append(qm)
        km = kmean_ref[:, pair]
        km_hi = km.astype(BF16)
        km_lo = (km - km_hi.astype(F32)).astype(BF16)
        s = jnp.where(past, _dot_nt(km_hi, qm) + _dot_nt(km_lo, qm), -jnp.inf)
        rank = jnp.zeros((nbp, L), jnp.int32)
        for jp in range(nb):
            sj = s[jp:jp + 1, :]
            ahead = (sj > s) | ((sj == s) & (jp < blk))
            rank = rank + ahead.astype(jnp.int32)
        sel_ref[h] = (past & (rank < A_TOPK)).astype(F32)
        m_ref[h] = jnp.full((1, L), NEG_INF, F32)
        l_ref[h] = jnp.zeros((1, L), F32)
        l8_ref[h] = jnp.zeros((8, L), F32)
    acc_ref[...] = jnp.zeros(acc_ref.shape, F32)
    qpair_t = [jnp.concatenate(qh[2 * hp:2 * hp + 2], axis=0).astype(F32).T.astype(BF16)
               for hp in range(A_HEADS // 2)]

    def causal(c, n):
        kk = c + lax.broadcasted_iota(jnp.int32, (n, L), 0)
        qq = lax.broadcasted_iota(jnp.int32, (n, L), 1)
        return kk <= qq

    def bounded_blocks(blocks):
        units = [(bi, hp, c) for bi in range(len(blocks)) for hp in range(A_HEADS // 2) for c in range(0, L, KC)]

        def qk(u):
            bi, hp, c = units[u]
            kc = k_ref[pl.ds(pl.multiple_of(blocks[bi][0] * L + c, KC), KC), hp * PW:(hp + 1) * PW]
            s_ref[u % (QK_AHEAD + 1)] = _dot(kc, qpair_t[hp])

        for u in range(QK_AHEAD):
            qk(u)
        o = ps = None
        for u, (bi, hp, c) in enumerate(units):
            j, tile_idx, own = blocks[bi]
            if u + QK_AHEAD < len(units):
                qk(u + QK_AHEAD)
            if c == 0:
                o = [jnp.zeros((HEAD_DIM, L), F32)] * 2
                ps = jnp.zeros((8, 2 * L), F32)
            p = jnp.exp2(s_ref[u % (QK_AHEAD + 1)] + bias_ref[hp, tile_idx, pl.ds(c, KC), :])
            if own:
                p = jnp.where(jnp.concatenate([causal(c, KC)] * 2, axis=1), p, 0.0)
            ps = ps + jnp.sum(p.reshape(KC // 8, 8, 2 * L), axis=0)
            pb = p.astype(BF16)
            o = [o[hh] + _dot(vt_ref[j, pl.ds((2 * hp + hh) * HEAD_DIM, HEAD_DIM), pl.ds(c, KC)],
                              pb[:, hh * L:(hh + 1) * L]) for hh in range(2)]
            if c + KC < L:
                continue
            for hh in range(2):
                h = 2 * hp + hh
                rows = slice(h * HEAD_DIM, (h + 1) * HEAD_DIM)
                psh = ps[:, hh * L:(hh + 1) * L]
                if own:
                    acc_ref[rows, :] += o[hh]
                    l8_ref[h] += psh
                else:
                    w = sel_ref[h, pl.ds(j, 1), :]
                    acc_ref[rows, :] += w * o[hh]
                    l8_ref[h] += w * psh

    def run_bounded():
        def tile_of(j):
            return jnp.minimum(i - j, n_near)

        def body(t, carry):
            bounded_blocks([(2 * t, tile_of(2 * t), False), (2 * t + 1, tile_of(2 * t + 1), False)])
            return carry

        lax.fori_loop(0, i // 2, body, 0)

        @pl.when(i % 2 == 1)
        def _():
            bounded_blocks([(i - 1, tile_of(i - 1), False), (i, 0, True)])

        @pl.when(i % 2 == 0)
        def _():
            bounded_blocks([(i, 0, True)])

    def online_tile(j, delta, far, own):
        for h in range(A_HEADS):
            pair = slice((h // 2) * PW, (h // 2 + 1) * PW)
            rows = slice(h * HEAD_DIM, (h + 1) * HEAD_DIM)
            s = _dot_nt(k_ref[pl.ds(pl.multiple_of(j * L, L), L), pair], qh[h])
            if not far:
                s = s + bias_ref[h // 2, delta, :, (h % 2) * L:(h % 2 + 1) * L]
            if own:
                s = jnp.where(causal(0, L), s, NEG_INF)
            mj = jnp.max(s, axis=0, keepdims=True)
            p = jnp.exp2(s - mj)
            lj = jnp.sum(p, axis=0, keepdims=True)
            o = _dot(vt_ref[j, rows, :], p.astype(BF16))
            if far:
                mj = mj + far_ref[h]
            m_old = m_ref[h]
            if own:
                m_new = jnp.maximum(m_old, mj)
                beta = jnp.exp2(mj - m_new)
            else:
                on = sel_ref[h, pl.ds(j, 1), :] > 0.5
                m_new = jnp.where(on, jnp.maximum(m_old, mj), m_old)
                beta = jnp.where(on, jnp.exp2(mj - m_new), 0.0)
            alpha = jnp.exp2(m_old - m_new)
            m_ref[h] = m_new
            l_ref[h] = alpha * l_ref[h] + beta * lj
            acc_ref[rows, :] = alpha * acc_ref[rows, :] + beta * o

    n_far = jnp.maximum(i - (n_near - 1), 0)

    def run(tile):
        def far_body(j, c):
            tile(j, None, True, False)
            return c

        def near_body(j, c):
            tile(j, i - j, False, False)
            return c

        lax.fori_loop(0, n_far, far_body, 0)
        lax.fori_loop(n_far, i, near_body, 0)
        tile(i, 0, False, True)

    @pl.when(flag_ref[0] == 1)
    def _():
        run_bounded()
        for h in range(A_HEADS):
            l_ref[h] = jnp.sum(l8_ref[h], axis=0, keepdims=True)

    @pl.when(flag_ref[0] != 1)
    def _():
        run(online_tile)

    o_t = jnp.concatenate([acc_ref[h * HEAD_DIM:(h + 1) * HEAD_DIM, :] / l_ref[h] for h in range(A_HEADS)],
                          axis=0)
    o_ref[...] = o_t.T.astype(BF16)


def _moba(q, k, v, bias_tiles, far, flag, batch, seq):
    n = q.shape[0]
    L = A_BLOCK
    nb = seq // L
    nbp = max(8, -(-nb // 8) * 8)
    n_near = bias_tiles.shape[1] - 1
    kern = functools.partial(_moba_kernel, nb=nb, n_near=n_near)
    return pl.pallas_call(
        kern,
        grid=(batch, nb),
        in_specs=[_smem_spec(), _smem_spec(),
                  pl.BlockSpec((L, WIDTH), lambda b, i: (b * nb + i, 0)),
                  pl.BlockSpec((seq, WIDTH), lambda b, i: (b, 0)),
                  pl.BlockSpec((seq, WIDTH), lambda b, i: (b, 0)),
                  _const_spec(bias_tiles.shape)],
        out_specs=pl.BlockSpec((L, WIDTH), lambda b, i: (b * nb + i, 0)),
        out_shape=jax.ShapeDtypeStruct((n, WIDTH), BF16),
        scratch_shapes=[pltpu.VMEM((nbp, WIDTH), F32),
                        pltpu.VMEM((nb, WIDTH, L), BF16),
                        pltpu.VMEM((A_HEADS, nbp, L), F32),
                        pltpu.VMEM((A_HEADS, 1, L), F32),
                        pltpu.VMEM((A_HEADS, 1, L), F32),
                        pltpu.VMEM((A_HEADS, 8, L), F32),
                        pltpu.VMEM((WIDTH, L), F32),
                        pltpu.VMEM((QK_AHEAD + 1, KEY_CHUNK, 2 * L), F32)],
        compiler_params=_params(("arbitrary", "arbitrary")),
        name="moba",
    )(flag, far, q, k, v, bias_tiles)


def _swa_kernel(flag_ref, sink_ref, q_ref, kp_ref, kc_ref, vp_ref, vc_ref, bias_ref, o_ref, s_ref):
    g = pl.program_id(0)
    chunk = pl.program_id(2)
    W = C_WINDOW
    G = C_HEADS // C_KV_HEADS
    nsub = q_ref.shape[0] // W
    gw = q_ref.shape[1]
    kall = jnp.concatenate([kp_ref[...], kc_ref[...]], axis=0)
    vall = jnp.concatenate([vp_ref[...], vc_ref[...]], axis=0)
    lane = lax.broadcasted_iota(jnp.int32, (W, gw), 1)
    qi = lax.broadcasted_iota(jnp.int32, (W, 2 * W), 0)
    kj = lax.broadcasted_iota(jnp.int32, (W, 2 * W), 1)
    dist = W + qi - kj
    band = (dist >= 0) & (dist < W)

    def block_mask(r):
        return band & ((chunk > 0) | (kj >= W)) if r == 0 else band

    def head_query(r, hh):
        q = q_ref[r * W:(r + 1) * W, :]
        return jnp.where(lane // HEAD_DIM == hh, q, jnp.zeros_like(q))

    def bounded_heads():
        units = [(r, hh) for r in range(nsub) for hh in range(G)]
        kall_t = kall.astype(F32).T.astype(BF16)

        def qk(u):
            r, hh = units[u]
            s_ref[u % (QK_AHEAD + 1)] = _dot(head_query(r, hh), kall_t[:, r * W:(r + 2) * W])

        for u in range(QK_AHEAD):
            qk(u)
        out = None
        for u, (r, hh) in enumerate(units):
            if u + QK_AHEAD < len(units):
                qk(u + QK_AHEAD)
            if hh == 0:
                out = jnp.zeros((W, gw), F32)
            s = s_ref[u % (QK_AHEAD + 1)] + bias_ref[hh]
            pb = jnp.where(block_mask(r), jnp.exp2(s), 0.0).astype(BF16)
            sink = jnp.exp2(jnp.full((1, 128), sink_ref[g * G + hh], F32))
            inv = 1.0 / (_dot(pb, jnp.ones((2 * W, 128), BF16)) + sink)
            o = _dot(pb, vall[r * W:(r + 2) * W]) * jnp.concatenate([inv] * (gw // 128), axis=-1)
            out = jnp.where(lane // HEAD_DIM == hh, o, out)
            if hh == G - 1:
                o_ref[r * W:(r + 1) * W, :] = out.astype(BF16)

    def online_heads():
        for r in range(nsub):
            kcat = kall[r * W:(r + 2) * W]
            vcat = vall[r * W:(r + 2) * W]
            out = jnp.zeros((W, gw), F32)
            for hh in range(G):
                s = jnp.where(block_mask(r), _dot_nt(head_query(r, hh), kcat) + bias_ref[hh], NEG_INF)
                sink = sink_ref[g * G + hh]
                m = jnp.maximum(jnp.max(s, axis=-1, keepdims=True), sink)
                p = jnp.exp2(s - m)
                den = jnp.sum(p, axis=-1, keepdims=True) + jnp.exp2(sink - m)
                o = _dot(p.astype(BF16), vcat) / den
                out = jnp.where(lane // HEAD_DIM == hh, o, out)
            o_ref[r * W:(r + 1) * W, :] = out.astype(BF16)

    @pl.when(flag_ref[0] == 1)
    def _():
        bounded_heads()

    @pl.when(flag_ref[0] != 1)
    def _():
        online_heads()


def _swa(q, k_t, v_t, bias_tiles, sinks, flag, batch, seq):
    n = q.shape[0]
    W = C_WINDOW
    G = C_HEADS // C_KV_HEADS
    gw = G * HEAD_DIM
    nsub = min(SWA_BLOCKS, seq // W)
    nchunk = seq // (W * nsub)
    cur = lambda g, b, j: (b * nchunk + j, g)
    prev = lambda g, b, j: (jnp.maximum((b * nchunk + j) * nsub - 1, 0), g)
    return pl.pallas_call(
        _swa_kernel,
        grid=(C_KV_HEADS, batch, nchunk),
        in_specs=[_smem_spec(), _smem_spec(),
                  pl.BlockSpec((nsub * W, gw), cur),
                  pl.BlockSpec((W, gw), prev), pl.BlockSpec((nsub * W, gw), cur),
                  pl.BlockSpec((W, gw), prev), pl.BlockSpec((nsub * W, gw), cur),
                  pl.BlockSpec((G, W, 2 * W), lambda g, b, j: (g, 0, 0))],
        out_specs=pl.BlockSpec((nsub * W, gw), cur),
        out_shape=jax.ShapeDtypeStruct((n, WIDTH), BF16),
        scratch_shapes=[pltpu.VMEM((QK_AHEAD + 1, W, 2 * W), F32)],
        compiler_params=_params(("arbitrary", "arbitrary", "arbitrary")),
        name="swa",
    )(flag, sinks, q, k_t, k_t, v_t, v_t, bias_tiles)


def _seg_norm(t, gain, seg):
    outs = []
    for c in range(0, t.shape[1], seg):
        tc = t[:, c:c + seg]
        outs.append(tc * lax.rsqrt(jnp.mean(tc * tc, axis=-1, keepdims=True) + EPS))
    return jnp.concatenate(outs, axis=-1) * gain


def _mem_kv_kernel(mem_ref, g_ref, wk_ref, wv_ref, kg_ref, k_ref, v_ref):
    mb = _rms(mem_ref[...], g_ref[...]).astype(BF16)
    k_ref[...] = _seg_norm(_dot(mb, wk_ref[...]), kg_ref[...], X_HEAD_DIM).astype(BF16)
    v_ref[...] = _dot(mb, wv_ref[...]).astype(BF16)


def _mem_kv(mem, g, w_xk, w_xv, k_gain):
    n, d = mem.shape
    tm = min(ROW_TILE, n)
    row = lambda width: pl.BlockSpec((tm, width), lambda i: (i, 0))
    return pl.pallas_call(
        _mem_kv_kernel,
        grid=(n // tm,),
        in_specs=[row(d), _const_spec(g.shape), _const_spec(w_xk.shape), _const_spec(w_xv.shape),
                  _const_spec(k_gain.shape)],
        out_specs=[row(WIDTH), row(WIDTH)],
        out_shape=[jax.ShapeDtypeStruct((n, WIDTH), BF16)] * 2,
        compiler_params=_params(("parallel",)),
        name="mem_kv",
    )(mem, g, w_xk, w_xv, k_gain)


def _merge_cross_kernel(x_ref, oa_ref, ub_ref, halo_ref, oc_ref, gt_ref, pw_ref, ps_ref, wb_ref, wo_ref,
                        gx_ref, wq_ref, qg_ref, km_ref, vm_ref, wxo_ref, o_ref, *, seq):
    tm = x_ref.shape[0]
    d = x_ref.shape[1]
    t0 = (pl.program_id(0) * tm) % seq
    H = POOL_HALO

    halo = jnp.where(t0 > 0, halo_ref[...], 0.0)
    pos = t0 + lax.broadcasted_iota(jnp.int32, (tm, B_GROUP_DIM), 0)
    mixed = []
    for gi, win in enumerate(B_WINDOWS):
        cols = slice(gi * B_GROUP_DIM, (gi + 1) * B_GROUP_DIM)
        cur = ub_ref[:, cols]
        acc = jnp.concatenate([halo[:, cols], cur], axis=0)
        step = 1
        while step < win:
            acc = acc + pltpu.roll(acc, step, axis=0)
            step *= 2
        cnt = jnp.minimum(pos + 1, win).astype(F32)
        pooled = acc[H:, :] / cnt - cur
        mixed.append(_dot(pooled.astype(BF16), pw_ref[gi]))
    ob = jnp.concatenate(mixed, axis=-1) * ps_ref[...]

    merged = gt_ref[:, 0:d].astype(F32) * _dot(oa_ref[...], wb_ref[0])
    merged = merged + gt_ref[:, d:2 * d].astype(F32) * _dot(ob.astype(BF16), wb_ref[1])
    merged = merged + gt_ref[:, 2 * d:3 * d].astype(F32) * _dot(oc_ref[...], wb_ref[2])
    x1 = x_ref[...] + _dot(merged.astype(BF16), wo_ref[...])

    xb = _rms(x1, gx_ref[...]).astype(BF16)
    qn = _seg_norm(_dot(xb, wq_ref[...]), qg_ref[...], X_HEAD_DIM).astype(BF16)
    heads = []
    for h in range(X_HEADS):
        cols = slice(h * X_HEAD_DIM, (h + 1) * X_HEAD_DIM)
        s = _dot_nt(qn[:, cols], km_ref[:, cols])
        p = jnp.exp(s - jnp.max(s, axis=-1, keepdims=True))
        inv = 1.0 / jnp.sum(p, axis=-1, keepdims=True)
        heads.append(_dot(p.astype(BF16), vm_ref[:, cols]) * inv)
    o = jnp.concatenate(heads, axis=-1).astype(BF16)
    o_ref[...] = x1 + _dot(o, wxo_ref[...])


def _merge_cross(x, oa, ub, oc, gates, pool_w, pool_scale, w_branch, w_out,
                 g_cross, w_xq, q_gain, k_mem, v_mem, w_xo, seq, mem_len):
    n, d = x.shape
    tm = min(ROW_TILE, seq)
    row = lambda width: pl.BlockSpec((tm, width), lambda i: (i, 0))
    halo = pl.BlockSpec((POOL_HALO, WIDTH), lambda i: (jnp.maximum(i * (tm // POOL_HALO) - 1, 0), 0))
    mem = pl.BlockSpec((mem_len, WIDTH), lambda i: ((i * tm) // seq, 0))
    kern = functools.partial(_merge_cross_kernel, seq=seq)
    return pl.pallas_call(
        kern,
        grid=(n // tm,),
        in_specs=[row(d), row(WIDTH), row(WIDTH), halo, row(WIDTH), row(gates.shape[1]),
                  _const_spec(pool_w.shape), _const_spec(pool_scale.shape), _const_spec(w_branch.shape),
                  _const_spec(w_out.shape), _const_spec(g_cross.shape), _const_spec(w_xq.shape),
                  _const_spec(q_gain.shape), mem, mem, _const_spec(w_xo.shape)],
        out_specs=row(d),
        out_shape=jax.ShapeDtypeStruct((n, d), F32),
        compiler_params=_params(("parallel",)),
        name="merge_cross",
    )(x, oa, ub, ub, oc, gates, pool_w, pool_scale, w_branch, w_out,
      g_cross, w_xq, q_gain, k_mem, v_mem, w_xo)


def _swiglu_chunks(xb, w1_ref, w3_ref, w2_ref, acc):
    for c in range(0, w1_ref.shape[1], FF_CHUNK):
        h1 = _dot(xb, w1_ref[:, c:c + FF_CHUNK])
        h3 = _dot(xb, w3_ref[:, c:c + FF_CHUNK])
        acc = acc + _dot((jax.nn.silu(h1) * h3).astype(BF16), w2_ref[c:c + FF_CHUNK, :])
    return acc


def _ffn_kernel(x_ref, g_ref, w1_ref, w3_ref, w2_ref, o_ref):
    x = x_ref[...]
    xb = _rms(x, g_ref[...]).astype(BF16)
    o_ref[...] = _swiglu_chunks(xb, w1_ref, w3_ref, w2_ref, x)


def _ffn(x, g, w1, w3, w2):
    n, d = x.shape
    tm = min(ROW_TILE, n)
    row = pl.BlockSpec((tm, d), lambda i: (i, 0))
    return pl.pallas_call(
        _ffn_kernel,
        grid=(n // tm,),
        in_specs=[row, _const_spec(g.shape), _const_spec(w1.shape), _const_spec(w3.shape),
                  _const_spec(w2.shape)],
        out_specs=row,
        out_shape=jax.ShapeDtypeStruct((n, d), F32),
        compiler_params=_params(("parallel",)),
        name="ffn_dense",
    )(x, g, w1, w3, w2)


def _store_token_tiles(ref, x):
    rows = x.shape[0]
    for k in range(x.shape[1] // 128):
        ref[pl.ds(k, rows, stride=SUBLANES), :] = x[:, k * 128:(k + 1) * 128]


def _load_token_tiles(ref, rows, lead=None):
    idx = (lambda k: (pl.ds(k, rows, stride=SUBLANES), slice(None))) if lead is None else \
        (lambda k: (lead, pl.ds(k, rows, stride=SUBLANES), slice(None)))
    return jnp.concatenate([ref[idx(k)] for k in range(SUBLANES)], axis=-1)


def _router_kernel(x_ref, g_ref, r_ref, lg_ref):
    xn = _rms(x_ref[...], g_ref[...])
    r = r_ref[...]
    x_hi = xn.astype(BF16)
    x_lo = (xn - x_hi.astype(F32)).astype(BF16)
    r_hi = r.astype(BF16)
    r_lo = (r - r_hi.astype(F32)).astype(BF16)
    lg_ref[...] = _dot(x_hi, r_hi) + (_dot(x_lo, r_hi) + _dot(x_hi, r_lo))


def _router(x, g, router_padded):
    n, d = x.shape
    tm = min(ROW_TILE, n)
    row = lambda width: pl.BlockSpec((tm, width), lambda i: (i, 0))
    return pl.pallas_call(
        _router_kernel,
        grid=(n // tm,),
        in_specs=[row(d), _const_spec(g.shape), _const_spec(router_padded.shape)],
        out_specs=row(router_padded.shape[1]),
        out_shape=jax.ShapeDtypeStruct((n, router_padded.shape[1]), F32),
        compiler_params=_params(("parallel",)),
        name="router",
    )(x, g, router_padded)


def _token_tile(ref, first_row):
    return ref.at[pl.ds(pl.multiple_of(first_row, SUBLANES), SUBLANES), :]


DMA_UNROLL = 8
ZERO_TOKENS = 8


def _dispatch_kernel(pad_ref, dst_ref, x_ref, g_ref, xs_hbm, xt_ref, zero_ref, sem, zsem):
    i = pl.program_id(0)
    n_steps = pl.num_programs(0)
    tm = x_ref.shape[0]
    T = SUBLANES
    slot = i % 2

    def wait_step(s):
        for _ in range(TOP_K):
            pltpu.make_async_copy(xt_ref.at[s], xs_hbm.at[pl.ds(0, tm * T), :], sem.at[s]).wait()

    @pl.when(i == 0)
    def _():
        zero_ref[...] = jnp.zeros(zero_ref.shape, F32)
        for e in range(N_EXPERTS + 1):
            first = pad_ref[0, e]
            tokens = 1 if e < N_EXPERTS else ZERO_TOKENS
            src = zero_ref.at[pl.ds(0, tokens * T), :]

            def dst(r):
                return xs_hbm.at[pl.ds(pl.multiple_of((first + r * tokens) * T, T), tokens * T), :]

            def fill(r, carry):
                pltpu.make_async_copy(src, dst(r), zsem).start()
                return carry

            def drain(r, carry):
                pltpu.make_async_copy(src, dst(0), zsem).wait()
                return carry

            lax.fori_loop(0, pad_ref[1, e] // tokens, fill, 0)
            lax.fori_loop(0, pad_ref[1, e] // tokens, drain, 0)

    _store_token_tiles(xt_ref.at[slot], _rms(x_ref[...], g_ref[...]))

    def body(gi, carry):
        for k in range(DMA_UNROLL):
            r = gi * DMA_UNROLL + k
            tok = jnp.where(r >= tm, r - tm, r)
            pltpu.make_async_copy(_token_tile(xt_ref.at[slot], tok * T), _token_tile(xs_hbm, dst_ref[0, 0, r]),
                                  sem.at[slot]).start()
        return carry

    lax.fori_loop(0, TOP_K * tm // DMA_UNROLL, body, 0)

    @pl.when(i >= 1)
    def _():
        wait_step(1 - slot)

    @pl.when(i == n_steps - 1)
    def _():
        wait_step(slot)


def _dispatch(x, g, dest_blocks, pad_info, p):
    n, d = x.shape
    tm = dest_blocks.shape[2] // TOP_K
    T = SUBLANES
    grid_spec = pltpu.PrefetchScalarGridSpec(
        num_scalar_prefetch=1,
        grid=(n // tm,),
        in_specs=[pl.BlockSpec((1, 1, TOP_K * tm), lambda i, pad: (i, 0, 0), memory_space=pltpu.SMEM),
                  pl.BlockSpec((tm, d), lambda i, pad: (i, 0)),
                  pl.BlockSpec(g.shape, lambda i, pad: (0, 0))],
        out_specs=pl.BlockSpec(memory_space=pl.ANY),
        scratch_shapes=[pltpu.VMEM((2, tm * T, d // T), F32), pltpu.VMEM((ZERO_TOKENS * T, d // T), F32),
                        pltpu.SemaphoreType.DMA((2,)), pltpu.SemaphoreType.DMA],
    )
    return pl.pallas_call(
        _dispatch_kernel,
        grid_spec=grid_spec,
        out_shape=jax.ShapeDtypeStruct((p * T, d // T), F32),
        compiler_params=_params(("arbitrary",)),
        name="dispatch",
    )(pad_info, dest_blocks, x, g)


def _expert_kernel(be_ref, nu_ref, x_ref, w1_ref, w3_ref, w2_ref, o_ref, xb_ref, y_ref):
    b = pl.program_id(0)
    c = pl.program_id(1)
    tb = xb_ref.shape[0]

    @pl.when(b < nu_ref[0])
    def _():
        @pl.when(c == 0)
        def _():
            xb_ref[...] = _load_token_tiles(x_ref, tb).astype(BF16)

        xb = xb_ref[...]
        h1 = _dot(xb, w1_ref[...])
        h3 = _dot(xb, w3_ref[...])
        y = _dot((jax.nn.silu(h1) * h3).astype(BF16), w2_ref[...])

        @pl.when(c == 0)
        def _():
            y_ref[...] = y

        @pl.when(c > 0)
        def _():
            y_ref[...] += y

        @pl.when(c == pl.num_programs(1) - 1)
        def _():
            _store_token_tiles(o_ref, y_ref[...])

    @pl.when((b >= nu_ref[0]) & (c == pl.num_programs(1) - 1))
    def _():
        o_ref[...] = jnp.zeros(o_ref.shape, F32)


def _experts(xs, block_e, n_used, w1, w3, w2, tb):
    T = SUBLANES
    d = w1.shape[1]
    nc = w1.shape[2] // FF_CHUNK
    nblk = xs.shape[0] // (tb * T)
    blk = lambda b, nu: jnp.minimum(b, nu[0] - 1)
    chunk = lambda b, c, nu: jnp.where(b < nu[0], c, nc - 1)
    grid_spec = pltpu.PrefetchScalarGridSpec(
        num_scalar_prefetch=2,
        grid=(nblk, nc),
        in_specs=[pl.BlockSpec((tb * T, d // T), lambda b, c, be, nu: (blk(b, nu), 0)),
                  pl.BlockSpec((None, d, FF_CHUNK), lambda b, c, be, nu: (be[b], 0, chunk(b, c, nu))),
                  pl.BlockSpec((None, d, FF_CHUNK), lambda b, c, be, nu: (be[b], 0, chunk(b, c, nu))),
                  pl.BlockSpec((None, FF_CHUNK, d), lambda b, c, be, nu: (be[b], chunk(b, c, nu), 0))],
        out_specs=pl.BlockSpec((tb * T, d // T), lambda b, c, be, nu: (b, 0)),
        scratch_shapes=[pltpu.VMEM((tb, d), BF16), pltpu.VMEM((tb, d), F32)],
    )
    return pl.pallas_call(
        _expert_kernel,
        grid_spec=grid_spec,
        out_shape=jax.ShapeDtypeStruct(xs.shape, F32),
        compiler_params=_params(("arbitrary", "arbitrary")),
        name="experts",
    )(block_e, n_used, xs, w1, w3, w2)


def _combine_kernel(dst_ref, dstn_ref, x_ref, g_ref, ys_hbm, o_ref, yg_ref, sem):
    i = pl.program_id(0)
    n_steps = pl.num_programs(0)
    tm = x_ref.shape[0]
    T = SUBLANES
    slot = i % 2

    def gather(idx_ref, s):
        def body(gi, carry):
            for k in range(DMA_UNROLL):
                r = gi * DMA_UNROLL + k
                pltpu.make_async_copy(_token_tile(ys_hbm, idx_ref[0, 0, r]), _token_tile(yg_ref.at[s], r * T),
                                      sem.at[s]).start()
            return carry
        lax.fori_loop(0, TOP_K * tm // DMA_UNROLL, body, 0)

    @pl.when(i == 0)
    def _():
        gather(dst_ref, 0)

    @pl.when(i + 1 < n_steps)
    def _():
        gather(dstn_ref, 1 - slot)

    pltpu.make_async_copy(ys_hbm.at[pl.ds(0, TOP_K * tm * T), :], yg_ref.at[slot], sem.at[slot]).wait()
    g = g_ref[...]
    y0 = _load_token_tiles(yg_ref, tm, lead=slot)
    y1 = jnp.concatenate([yg_ref[slot, pl.ds(tm * T + k, tm, stride=T), :] for k in range(T)], axis=-1)
    o_ref[...] = x_ref[...] + g[:, 0:1] * y0 + g[:, 1:2] * y1


def _combine(x, ys, dest_blocks, gate):
    n, d = x.shape
    tm = dest_blocks.shape[2] // TOP_K
    T = SUBLANES
    n_steps = n // tm
    row = lambda width: pl.BlockSpec((tm, width), lambda i: (i, 0))
    idx = lambda imap: pl.BlockSpec((1, 1, TOP_K * tm), imap, memory_space=pltpu.SMEM)
    return pl.pallas_call(
        _combine_kernel,
        grid=(n_steps,),
        in_specs=[idx(lambda i: (i, 0, 0)), idx(lambda i: (jnp.minimum(i + 1, n_steps - 1), 0, 0)),
                  row(d), row(gate.shape[1]), pl.BlockSpec(memory_space=pl.ANY)],
        out_specs=row(d),
        out_shape=jax.ShapeDtypeStruct((n, d), F32),
        scratch_shapes=[pltpu.VMEM((2, TOP_K * tm * T, d // T), F32), pltpu.SemaphoreType.DMA((2,))],
        compiler_params=_params(("arbitrary",)),
        name="combine",
    )(dest_blocks, dest_blocks, x, gate, ys)


def _moe(x, g, router, w1, w3, w2):
    n, d = x.shape
    a = n * TOP_K
    tb = min(MOE_TILE, a)
    router_padded = jnp.pad(router, ((0, 0), (0, 128 - N_EXPERTS)))
    logits = _router(x, g, router_padded)
    top_logit, top_e = lax.top_k(logits[:, :N_EXPERTS], TOP_K)
    gate = jax.nn.softmax(top_logit, axis=-1)
    e_flat = top_e.reshape(a)
    onehot = (e_flat[:, None] == jnp.arange(N_EXPERTS)[None, :]).astype(jnp.int32)
    csum = jnp.cumsum(onehot, axis=0)
    rank = jnp.sum(onehot * csum, axis=1) - 1
    counts = csum[-1]
    padded = (counts + tb - 1) // tb * tb
    pend = jnp.cumsum(padded)
    pstart = pend - padded
    dest = (jnp.sum(onehot * pstart[None, :], axis=1) + rank).astype(jnp.int32)
    n_blocks = a // tb + N_EXPERTS
    p = n_blocks * tb
    block_e = jnp.minimum(jnp.searchsorted(pend, jnp.arange(n_blocks) * tb, side='right'),
                          N_EXPERTS - 1).astype(jnp.int32)
    n_used = (pend[-1] // tb).astype(jnp.int32).reshape(1)
    tm = min(ROW_TILE, n)
    dest_blocks = (dest * SUBLANES).reshape(n // tm, tm, TOP_K).transpose(0, 2, 1).reshape(n // tm, 1, TOP_K * tm)
    pad_info = jnp.stack([jnp.append(pstart + counts, pend[-1]),
                          jnp.append(padded - counts, p - pend[-1])]).astype(jnp.int32)
    xs = _dispatch(x, g, dest_blocks, pad_info, p)
    ys = _experts(xs, block_e, n_used, w1, w3, w2, tb)
    gate_padded = jnp.pad(gate, ((0, 0), (0, 128 - TOP_K)))
    return _combine(x, ys, dest_blocks, gate_padded)


def _tile_gain(gain, width, scale=1.0):
    return jnp.tile(gain, width // gain.shape[0]) * scale


def kernel(x, mem, rel_bias, norm_mix, w_in, b_gate, a_q_gain, a_k_gain, pool_w, pool_scale,
           c_q_gain, c_k_gain, c_sinks, w_branch, w_out, norm_cross, norm_mem, w_xq, w_xk, w_xv,
           x_q_gain, x_k_gain, w_xo, norm_ffn, ffn_w1, ffn_w3, ffn_w2, router, moe_w1, moe_w3, moe_w2):
    batch, seq, d = x.shape
    mem_len = mem.shape[1]
    depth = norm_mix.shape[0]
    xs = x.reshape(batch * seq, d)
    mems = mem.reshape(batch * mem_len, d)

    tab_a = rel_bias[:, :A_HEADS].T
    tab_c = rel_bias[:, A_HEADS:].T
    nb = seq // A_BLOCK
    n_near = min(nb, (_saturation_distance() + 2 * A_BLOCK - 2) // A_BLOCK)
    far_a = tab_a[:, REL_BUCKETS - 1] * LOG2E
    bias_a = _moba_bias_tiles(_bias_by_distance(tab_a, n_near * A_BLOCK), n_near) * LOG2E
    bias_a = jnp.concatenate([bias_a, jnp.broadcast_to(far_a[:, None, None, None],
                                                       (A_HEADS, 1, A_BLOCK, A_BLOCK))], axis=1)
    bias_a = jnp.concatenate([bias_a[0::2], bias_a[1::2]], axis=-1)
    bias_c = _swa_bias_tiles(_bias_by_distance(tab_c, C_WINDOW)) * LOG2E
    seg = np.arange(256) // HEAD_DIM
    bd = jnp.asarray(seg[:, None] == seg[None, :], BF16)

    row = lambda v: v.reshape(1, -1)
    for l in range(depth):
        scale = HEAD_DIM ** -0.5 * LOG2E
        gains = jnp.stack([_tile_gain(a_q_gain[l], WIDTH, scale), _tile_gain(a_k_gain[l], WIDTH),
                           _tile_gain(c_q_gain[l], WIDTH, scale), _tile_gain(c_k_gain[l], WIDTH)])
        flag_a = _bounded_flag(a_q_gain[l], a_k_gain[l], HEAD_DIM, jnp.max(jnp.abs(tab_a)))
        flag_c = _bounded_flag(c_q_gain[l], c_k_gain[l], HEAD_DIM,
                               jnp.maximum(jnp.max(jnp.abs(tab_c)), jnp.max(jnp.abs(c_sinks[l]))))
        qa, ka, va, ub, qc, kc, vc, gates = _in_proj(
            xs, row(norm_mix[l]), w_in[l].astype(BF16), row(b_gate[l]), gains, bd)
        oa = _moba(qa, ka, va, bias_a, far_a, flag_a, batch, seq)
        oc = _swa(qc, kc, vc, bias_c, c_sinks[l] * LOG2E, flag_c, batch, seq)
        k_mem, v_mem = _mem_kv(mems, row(norm_mem[l]), w_xk[l].astype(BF16), w_xv[l].astype(BF16),
                               row(_tile_gain(x_k_gain[l], WIDTH)))
        xs = _merge_cross(xs, oa, ub, oc, gates, pool_w[l].astype(BF16), row(pool_scale[l]),
                          w_branch[l].astype(BF16), w_out[l].astype(BF16), row(norm_cross[l]),
                          w_xq[l].astype(BF16), row(_tile_gain(x_q_gain[l], WIDTH, X_HEAD_DIM ** -0.5)),
                          k_mem, v_mem, w_xo[l].astype(BF16), seq, mem_len)
        i = l // 2
        if l % 2 == 0:
            xs = _ffn(xs, row(norm_ffn[l]), ffn_w1[i].astype(BF16), ffn_w3[i].astype(BF16),
                      ffn_w2[i].astype(BF16))
        else:
            xs = _moe(xs, row(norm_ffn[l]), router[i], moe_w1[i].astype(BF16), moe_w3[i].astype(BF16),
                      moe_w2[i].astype(BF16))
    return xs.reshape(batch, seq, d)
```

```python
import functools
import math

import jax
import jax.numpy as jnp
import numpy as np
from jax import lax
from jax.experimental import pallas as pl
from jax.experimental.pallas import tpu as pltpu

F32 = jnp.float32
BF16 = jnp.bfloat16

HEAD_DIM = 64
A_HEADS = 8
A_BLOCK = 256
A_TOPK = 3
B_GROUPS = 4
B_GROUP_DIM = 128
B_WINDOWS = (2, 4, 8, 16)
C_HEADS = 8
C_KV_HEADS = 2
C_WINDOW = 128
REL_BUCKETS = 32
REL_MAX_DIST = 1024
X_HEADS = 4
X_HEAD_DIM = 128
N_EXPERTS = 8
TOP_K = 2
EPS = 1e-6
NEG_INF = -1e30
LOG2E = math.log2(math.e)

WIDTH = 512
POOL_HALO = 16
ROW_TILE = 512
MOE_TILE = 1024
FF_CHUNK = 512
SUBLANES = 8
KEY_CHUNK = 128
QK_AHEAD = 3
SWA_BLOCKS = 4
VMEM_LIMIT = 56 * 1024 * 1024
EXP2_SAFE = 100.0


def _dot(a, b):
    return jnp.dot(a, b, preferred_element_type=F32)


def _dot_nt(a, b):
    return lax.dot_general(a, b, (((1,), (1,)), ((), ())), preferred_element_type=F32)


def _rms(x, g):
    ms = jnp.mean(x * x, axis=-1, keepdims=True)
    return x * lax.rsqrt(ms + EPS) * g


def _const_spec(shape):
    zeros = (0,) * len(shape)
    return pl.BlockSpec(shape, lambda *_: zeros, pipeline_mode=pl.Buffered(1))


def _smem_spec():
    return pl.BlockSpec(memory_space=pltpu.SMEM)


def _params(sem):
    return pltpu.CompilerParams(dimension_semantics=sem, vmem_limit_bytes=VMEM_LIMIT)


def _rel_bucket(dist):
    n = jnp.maximum(dist, 0)
    max_exact = REL_BUCKETS // 2
    nf = jnp.maximum(n, 1).astype(jnp.float32)
    large = max_exact + (jnp.log(nf / max_exact) / math.log(REL_MAX_DIST / max_exact)
                         * (REL_BUCKETS - max_exact)).astype(jnp.int32)
    large = jnp.minimum(large, REL_BUCKETS - 1)
    return jnp.where(n < max_exact, n, large)


def _saturation_distance():
    ratio = REL_MAX_DIST / (REL_BUCKETS // 2)
    return int(math.ceil((REL_BUCKETS // 2) * ratio ** ((REL_BUCKETS - 1 - REL_BUCKETS // 2 + 0.5)
                                                       / (REL_BUCKETS - REL_BUCKETS // 2))))


def _bias_by_distance(tab, n_dist):
    onehot = (_rel_bucket(jnp.arange(n_dist))[:, None] == jnp.arange(REL_BUCKETS)[None, :]).astype(F32)
    return jnp.einsum('hb,db->hd', tab, onehot, precision=lax.Precision.HIGHEST)


def _skew(g, rows):
    n = g.shape[-1]
    lead = g.shape[:-1]
    tiled = jnp.broadcast_to(g[..., None, :], lead + (rows, n)).reshape(lead + (rows * n,))
    return tiled[..., :rows * (n - 1)].reshape(lead + (rows, n - 1))


def _moba_bias_tiles(bvec, n_near):
    L = A_BLOCK
    rows = []
    for delta in range(n_near):
        lo = delta * L - (L - 1)
        seg = bvec[:, max(lo, 0):delta * L + L]
        if lo < 0:
            seg = jnp.concatenate([jnp.zeros((bvec.shape[0], -lo), F32), seg], axis=1)
        rows.append(jnp.pad(seg, ((0, 0), (0, 1))))
    g = jnp.stack(rows, axis=1)
    return _skew(g, L)[..., L - 1:]


def _swa_bias_tiles(bvec):
    W = C_WINDOW
    u = np.arange(3 * W)
    g = bvec[:, np.clip(2 * W - 1 - u, 0, W - 1)]
    return _skew(g, W)[..., W - 1:3 * W - 1]


def _bounded_flag(q_gain, k_gain, head_dim, extra):
    bound = head_dim ** 0.5 * jnp.max(jnp.abs(q_gain)) * jnp.max(jnp.abs(k_gain)) * 1.02 + extra
    return (bound * LOG2E < EXP2_SAFE).astype(jnp.int32).reshape(1)


def _in_proj_kernel(x_ref, g_ref, w_ref, bg_ref, gn_ref, bd_ref,
                    qa_ref, ka_ref, va_ref, ub_ref, qc_ref, kc_ref, vc_ref, gt_ref):
    xb = _rms(x_ref[...], g_ref[...]).astype(BF16)
    bd = bd_ref[...]

    def proj(c0, width):
        return _dot(xb, w_ref[:, c0:c0 + width])

    def head_norm(t, gain):
        outs = []
        for c in range(0, t.shape[1], 256):
            wd = min(256, t.shape[1] - c)
            tc = t[:, c:c + wd]
            ss = _dot((tc * tc).astype(BF16), bd[:wd, :wd])
            outs.append(tc * lax.rsqrt(ss * (1.0 / HEAD_DIM) + EPS))
        y = outs[0] if len(outs) == 1 else jnp.concatenate(outs, axis=-1)
        return y * gain

    def tile_kv_heads(t):
        lane = lax.broadcasted_iota(jnp.int32, t.shape, 1)
        r = pltpu.roll(t, HEAD_DIM, axis=1)
        h0 = jnp.where(lane < HEAD_DIM, t, r)
        h1 = jnp.where(lane < HEAD_DIM, r, t)
        return jnp.concatenate([h0, h0, h1, h1], axis=-1)

    qa_ref[...] = head_norm(proj(0, WIDTH), gn_ref[0:1, :]).astype(BF16)
    ka_ref[...] = head_norm(proj(WIDTH, WIDTH), gn_ref[1:2, :]).astype(BF16)
    va_ref[...] = proj(2 * WIDTH, WIDTH).astype(BF16)
    ub_ref[...] = proj(3 * WIDTH, WIDTH)
    qc_ref[...] = head_norm(proj(4 * WIDTH, WIDTH), gn_ref[2:3, :]).astype(BF16)
    kv = C_KV_HEADS * HEAD_DIM
    c0 = 5 * WIDTH
    kc = head_norm(proj(c0, kv), gn_ref[3:4, :kv])
    kc_ref[...] = tile_kv_heads(kc).astype(BF16)
    vc_ref[...] = tile_kv_heads(proj(c0 + kv, kv)).astype(BF16)
    c0 += 2 * kv
    for c in range(0, gt_ref.shape[1], WIDTH):
        gl = proj(c0 + c, WIDTH) + bg_ref[:, c:c + WIDTH]
        gt_ref[:, c:c + WIDTH] = jax.nn.sigmoid(gl).astype(BF16)


def _in_proj(x, g, w_in, b_gate, gains, bd):
    n, d = x.shape
    tm = min(ROW_TILE, n)
    n_gate = b_gate.shape[1]
    row = lambda width: pl.BlockSpec((tm, width), lambda i: (i, 0))
    out_shape = [jax.ShapeDtypeStruct((n, WIDTH), BF16)] * 3 + [jax.ShapeDtypeStruct((n, WIDTH), F32)] \
        + [jax.ShapeDtypeStruct((n, WIDTH), BF16)] * 3 + [jax.ShapeDtypeStruct((n, n_gate), BF16)]
    return pl.pallas_call(
        _in_proj_kernel,
        grid=(n // tm,),
        in_specs=[row(d), _const_spec(g.shape), _const_spec(w_in.shape), _const_spec(b_gate.shape),
                  _const_spec(gains.shape), _const_spec(bd.shape)],
        out_specs=[row(WIDTH)] * 7 + [row(n_gate)],
        out_shape=out_shape,
        compiler_params=_params(("parallel",)),
        name="in_proj",
    )(x, g, w_in, b_gate, gains, bd)


def _moba_kernel(flag_ref, far_ref, q_ref, k_ref, v_ref, bias_ref, o_ref,
                 kmean_ref, vt_ref, sel_ref, m_ref, l_ref, l8_ref, acc_ref, s_ref, *, nb, n_near):
    i = pl.program_id(1)
    L = A_BLOCK
    KC = KEY_CHUNK
    PW = 2 * HEAD_DIM
    nbp = kmean_ref.shape[0]

    @pl.when(i == 0)
    def _():
        kmean_ref[...] = jnp.zeros(kmean_ref.shape, F32)
        for j in range(nb):
            kj = k_ref[j * L:(j + 1) * L, :].astype(F32)
            kmean_ref[j:j + 1, :] = jnp.mean(kj, axis=0, keepdims=True)
            vt_ref[j] = v_ref[j * L:(j + 1) * L, :].astype(F32).T.astype(BF16)

    lane = lax.broadcasted_iota(jnp.int32, (L, PW), 1)
    blk = lax.broadcasted_iota(jnp.int32, (nbp, L), 0)
    past = blk < i
    qh = []
    for h in range(A_HEADS):
        pair = slice((h // 2) * PW, (h // 2 + 1) * PW)
        q = q_ref[:, pair]
        qm = jnp.where(lane // HEAD_DIM == h % 2, q, jnp.zeros_like(q))
        qh.append(qm)
        km = kmean_ref[:, pair]
        km_hi = km.astype(BF16)
        km_lo = (km - km_hi.astype(F32)).astype(BF16)
        s = jnp.where(past, _dot_nt(km_hi, qm) + _dot_nt(km_lo, qm), -jnp.inf)
        rank = jnp.zeros((nbp, L), jnp.int32)
        for jp in range(nb):
            sj = s[jp:jp + 1, :]
            ahead = (sj > s) | ((sj == s) & (jp < blk))
            rank = rank + ahead.astype(jnp.int32)
        sel_ref[h] = (past & (rank < A_TOPK)).astype(F32)
        m_ref[h] = jnp.full((1, L), NEG_INF, F32)
        l_ref[h] = jnp.zeros((1, L), F32)
        l8_ref[h] = jnp.zeros((8, L), F32)
    acc_ref[...] = jnp.zeros(acc_ref.shape, F32)
    qpair_t = [jnp.concatenate(qh[2 * hp:2 * hp + 2], axis=0).astype(F32).T.astype(BF16)
               for hp in range(A_HEADS // 2)]

    def causal(c, n):
        kk = c + lax.broadcasted_iota(jnp.int32, (n, L), 0)
        qq = lax.broadcasted_iota(jnp.int32, (n, L), 1)
        return kk <= qq

    def bounded_blocks(blocks):
        units = [(bi, hp, c) for bi in range(len(blocks)) for hp in range(A_HEADS // 2) for c in range(0, L, KC)]

        def qk(u):
            bi, hp, c = units[u]
            kc = k_ref[pl.ds(pl.multiple_of(blocks[bi][0] * L + c, KC), KC), hp * PW:(hp + 1) * PW]
            s_ref[u % (QK_AHEAD + 1)] = _dot(kc, qpair_t[hp])

        for u in range(QK_AHEAD):
            qk(u)
        o = ps = None
        for u, (bi, hp, c) in enumerate(units):
            j, tile_idx, own = blocks[bi]
            if u + QK_AHEAD < len(units):
                qk(u + QK_AHEAD)
            if c == 0:
                o = [jnp.zeros((HEAD_DIM, L), F32)] * 2
                ps = jnp.zeros((8, 2 * L), F32)
            p = jnp.exp2(s_ref[u % (QK_AHEAD + 1)] + bias_ref[hp, tile_idx, pl.ds(c, KC), :])
            if own:
                p = jnp.where(jnp.concatenate([causal(c, KC)] * 2, axis=1), p, 0.0)
            ps = ps + jnp.sum(p.reshape(KC // 8, 8, 2 * L), axis=0)
            pb = p.astype(BF16)
            o = [o[hh] + _dot(vt_ref[j, pl.ds((2 * hp + hh) * HEAD_DIM, HEAD_DIM), pl.ds(c, KC)],
                              pb[:, hh * L:(hh + 1) * L]) for hh in range(2)]
            if c + KC < L:
                continue
            for hh in range(2):
                h = 2 * hp + hh
                rows = slice(h * HEAD_DIM, (h + 1) * HEAD_DIM)
                psh = ps[:, hh * L:(hh + 1) * L]
                if own:
                    acc_ref[rows, :] += o[hh]
                    l8_ref[h] += psh
                else:
                    w = sel_ref[h, pl.ds(j, 1), :]
                    acc_ref[rows, :] += w * o[hh]
                    l8_ref[h] += w * psh

    def run_bounded():
        def tile_of(j):
            return jnp.minimum(i - j, n_near)

        def body(t, carry):
            bounded_blocks([(2 * t, tile_of(2 * t), False), (2 * t + 1, tile_of(2 * t + 1), False)])
            return carry

        lax.fori_loop(0, i // 2, body, 0)

        @pl.when(i % 2 == 1)
        def _():
            bounded_blocks([(i - 1, tile_of(i - 1), False), (i, 0, True)])

        @pl.when(i % 2 == 0)
        def _():
            bounded_blocks([(i, 0, True)])

    def online_tile(j, delta, far, own):
        for h in range(A_HEADS):
            pair = slice((h // 2) * PW, (h // 2 + 1) * PW)
            rows = slice(h * HEAD_DIM, (h + 1) * HEAD_DIM)
            s = _dot_nt(k_ref[pl.ds(pl.multiple_of(j * L, L), L), pair], qh[h])
            if not far:
                s = s + bias_ref[h // 2, delta, :, (h % 2) * L:(h % 2 + 1) * L]
            if own:
                s = jnp.where(causal(0, L), s, NEG_INF)
            mj = jnp.max(s, axis=0, keepdims=True)
            p = jnp.exp2(s - mj)
            lj = jnp.sum(p, axis=0, keepdims=True)
            o = _dot(vt_ref[j, rows, :], p.astype(BF16))
            if far:
                mj = mj + far_ref[h]
            m_old = m_ref[h]
            if own:
                m_new = jnp.maximum(m_old, mj)
                beta = jnp.exp2(mj - m_new)
            else:
                on = sel_ref[h, pl.ds(j, 1), :] > 0.5
                m_new = jnp.where(on, jnp.maximum(m_old, mj), m_old)
                beta = jnp.where(on, jnp.exp2(mj - m_new), 0.0)
            alpha = jnp.exp2(m_old - m_new)
            m_ref[h] = m_new
            l_ref[h] = alpha * l_ref[h] + beta * lj
            acc_ref[rows, :] = alpha * acc_ref[rows, :] + beta * o

    n_far = jnp.maximum(i - (n_near - 1), 0)

    def run(tile):
        def far_body(j, c):
            tile(j, None, True, False)
            return c

        def near_body(j, c):
            tile(j, i - j, False, False)
            return c

        lax.fori_loop(0, n_far, far_body, 0)
        lax.fori_loop(n_far, i, near_body, 0)
        tile(i, 0, False, True)

    @pl.when(flag_ref[0] == 1)
    def _():
        run_bounded()
        for h in range(A_HEADS):
            l_ref[h] = jnp.sum(l8_ref[h], axis=0, keepdims=True)

    @pl.when(flag_ref[0] != 1)
    def _():
        run(online_tile)

    o_t = jnp.concatenate([acc_ref[h * HEAD_DIM:(h + 1) * HEAD_DIM, :] / l_ref[h] for h in range(A_HEADS)],
                          axis=0)
    o_ref[...] = o_t.T.astype(BF16)


def _moba(q, k, v, bias_tiles, far, flag, batch, seq):
    n = q.shape[0]
    L = A_BLOCK
    nb = seq // L
    nbp = max(8, -(-nb // 8) * 8)
    n_near = bias_tiles.shape[1] - 1
    kern = functools.partial(_moba_kernel, nb=nb, n_near=n_near)
    return pl.pallas_call(
        kern,
        grid=(batch, nb),
        in_specs=[_smem_spec(), _smem_spec(),
                  pl.BlockSpec((L, WIDTH), lambda b, i: (b * nb + i, 0)),
                  pl.BlockSpec((seq, WIDTH), lambda b, i: (b, 0)),
                  pl.BlockSpec((seq, WIDTH), lambda b, i: (b, 0)),
                  _const_spec(bias_tiles.shape)],
        out_specs=pl.BlockSpec((L, WIDTH), lambda b, i: (b * nb + i, 0)),
        out_shape=jax.ShapeDtypeStruct((n, WIDTH), BF16),
        scratch_shapes=[pltpu.VMEM((nbp, WIDTH), F32),
                        pltpu.VMEM((nb, WIDTH, L), BF16),
                        pltpu.VMEM((A_HEADS, nbp, L), F32),
                        pltpu.VMEM((A_HEADS, 1, L), F32),
                        pltpu.VMEM((A_HEADS, 1, L), F32),
                        pltpu.VMEM((A_HEADS, 8, L), F32),
                        pltpu.VMEM((WIDTH, L), F32),
                        pltpu.VMEM((QK_AHEAD + 1, KEY_CHUNK, 2 * L), F32)],
        compiler_params=_params(("arbitrary", "arbitrary")),
        name="moba",
    )(flag, far, q, k, v, bias_tiles)


def _swa_kernel(flag_ref, sink_ref, q_ref, kp_ref, kc_ref, vp_ref, vc_ref, bias_ref, o_ref, s_ref):
    g = pl.program_id(0)
    chunk = pl.program_id(2)
    W = C_WINDOW
    G = C_HEADS // C_KV_HEADS
    nsub = q_ref.shape[0] // W
    gw = q_ref.shape[1]
    kall = jnp.concatenate([kp_ref[...], kc_ref[...]], axis=0)
    vall = jnp.concatenate([vp_ref[...], vc_ref[...]], axis=0)
    lane = lax.broadcasted_iota(jnp.int32, (W, gw), 1)
    qi = lax.broadcasted_iota(jnp.int32, (W, 2 * W), 0)
    kj = lax.broadcasted_iota(jnp.int32, (W, 2 * W), 1)
    dist = W + qi - kj
    band = (dist >= 0) & (dist < W)

    def block_mask(r):
        return band & ((chunk > 0) | (kj >= W)) if r == 0 else band

    def head_query(r, hh):
        q = q_ref[r * W:(r + 1) * W, :]
        return jnp.where(lane // HEAD_DIM == hh, q, jnp.zeros_like(q))

    def bounded_heads():
        units = [(r, hh) for r in range(nsub) for hh in range(G)]
        kall_t = kall.astype(F32).T.astype(BF16)

        def qk(u):
            r, hh = units[u]
            s_ref[u % (QK_AHEAD + 1)] = _dot(head_query(r, hh), kall_t[:, r * W:(r + 2) * W])

        for u in range(QK_AHEAD):
            qk(u)
        out = None
        for u, (r, hh) in enumerate(units):
            if u + QK_AHEAD < len(units):
                qk(u + QK_AHEAD)
            if hh == 0:
                out = jnp.zeros((W, gw), F32)
            s = s_ref[u % (QK_AHEAD + 1)] + bias_ref[hh]
            pb = jnp.where(block_mask(r), jnp.exp2(s), 0.0).astype(BF16)
            sink = jnp.exp2(jnp.full((1, 128), sink_ref[g * G + hh], F32))
            inv = 1.0 / (_dot(pb, jnp.ones((2 * W, 128), BF16)) + sink)
            o = _dot(pb, vall[r * W:(r + 2) * W]) * jnp.concatenate([inv] * (gw // 128), axis=-1)
            out = jnp.where(lane // HEAD_DIM == hh, o, out)
            if hh == G - 1:
                o_ref[r * W:(r + 1) * W, :] = out.astype(BF16)

    def online_heads():
        for r in range(nsub):
            kcat = kall[r * W:(r + 2) * W]
            vcat = vall[r * W:(r + 2) * W]
            out = jnp.zeros((W, gw), F32)
            for hh in range(G):
                s = jnp.where(block_mask(r), _dot_nt(head_query(r, hh), kcat) + bias_ref[hh], NEG_INF)
                sink = sink_ref[g * G + hh]
                m = jnp.maximum(jnp.max(s, axis=-1, keepdims=True), sink)
                p = jnp.exp2(s - m)
                den = jnp.sum(p, axis=-1, keepdims=True) + jnp.exp2(sink - m)
                o = _dot(p.astype(BF16), vcat) / den
                out = jnp.where(lane // HEAD_DIM == hh, o, out)
            o_ref[r * W:(r + 1) * W, :] = out.astype(BF16)

    @pl.when(flag_ref[0] == 1)
    def _():
        bounded_heads()

    @pl.when(flag_ref[0] != 1)
    def _():
        online_heads()


def _swa(q, k_t, v_t, bias_tiles, sinks, flag, batch, seq):
    n = q.shape[0]
    W = C_WINDOW
    G = C_HEADS // C_KV_HEADS
    gw = G * HEAD_DIM
    nsub = min(SWA_BLOCKS, seq // W)
    nchunk = seq // (W * nsub)
    cur = lambda g, b, j: (b * nchunk + j, g)
    prev = lambda g, b, j: (jnp.maximum((b * nchunk + j) * nsub - 1, 0), g)
    return pl.pallas_call(
        _swa_kernel,
        grid=(C_KV_HEADS, batch, nchunk),
        in_specs=[_smem_spec(), _smem_spec(),
                  pl.BlockSpec((nsub * W, gw), cur),
                  pl.BlockSpec((W, gw), prev), pl.BlockSpec((nsub * W, gw), cur),
                  pl.BlockSpec((W, gw), prev), pl.BlockSpec((nsub * W, gw), cur),
                  pl.BlockSpec((G, W, 2 * W), lambda g, b, j: (g, 0, 0))],
        out_specs=pl.BlockSpec((nsub * W, gw), cur),
        out_shape=jax.ShapeDtypeStruct((n, WIDTH), BF16),
        scratch_shapes=[pltpu.VMEM((QK_AHEAD + 1, W, 2 * W), F32)],
        compiler_params=_params(("arbitrary", "arbitrary", "arbitrary")),
        name="swa",
    )(flag, sinks, q, k_t, k_t, v_t, v_t, bias_tiles)


def _seg_norm(t, gain, seg):
    outs = []
    for c in range(0, t.shape[1], seg):
        tc = t[:, c:c + seg]
        outs.append(tc * lax.rsqrt(jnp.mean(tc * tc, axis=-1, keepdims=True) + EPS))
    return jnp.concatenate(outs, axis=-1) * gain


def _mem_kv_kernel(mem_ref, g_ref, wk_ref, wv_ref, kg_ref, k_ref, v_ref):
    mb = _rms(mem_ref[...], g_ref[...]).astype(BF16)
    k_ref[...] = _seg_norm(_dot(mb, wk_ref[...]), kg_ref[...], X_HEAD_DIM).astype(BF16)
    v_ref[...] = _dot(mb, wv_ref[...]).astype(BF16)


def _mem_kv(mem, g, w_xk, w_xv, k_gain):
    n, d = mem.shape
    tm = min(ROW_TILE, n)
    row = lambda width: pl.BlockSpec((tm, width), lambda i: (i, 0))
    return pl.pallas_call(
        _mem_kv_kernel,
        grid=(n // tm,),
        in_specs=[row(d), _const_spec(g.shape), _const_spec(w_xk.shape), _const_spec(w_xv.shape),
                  _const_spec(k_gain.shape)],
        out_specs=[row(WIDTH), row(WIDTH)],
        out_shape=[jax.ShapeDtypeStruct((n, WIDTH), BF16)] * 2,
        compiler_params=_params(("parallel",)),
        name="mem_kv",
    )(mem, g, w_xk, w_xv, k_gain)


def _merge_cross_kernel(x_ref, oa_ref, ub_ref, halo_ref, oc_ref, gt_ref, pw_ref, ps_ref, wb_ref, wo_ref,
                        gx_ref, wq_ref, qg_ref, km_ref, vm_ref, wxo_ref, o_ref, *, seq):
    tm = x_ref.shape[0]
    d = x_ref.shape[1]
    t0 = (pl.program_id(0) * tm) % seq
    H = POOL_HALO

    halo = jnp.where(t0 > 0, halo_ref[...], 0.0)
    pos = t0 + lax.broadcasted_iota(jnp.int32, (tm, B_GROUP_DIM), 0)
    mixed = []
    for gi, win in enumerate(B_WINDOWS):
        cols = slice(gi * B_GROUP_DIM, (gi + 1) * B_GROUP_DIM)
        cur = ub_ref[:, cols]
        acc = jnp.concatenate([halo[:, cols], cur], axis=0)
        step = 1
        while step < win:
            acc = acc + pltpu.roll(acc, step, axis=0)
            step *= 2
        cnt = jnp.minimum(pos + 1, win).astype(F32)
        pooled = acc[H:, :] / cnt - cur
        mixed.append(_dot(pooled.astype(BF16), pw_ref[gi]))
    ob = jnp.concatenate(mixed, axis=-1) * ps_ref[...]

    merged = gt_ref[:, 0:d].astype(F32) * _dot(oa_ref[...], wb_ref[0])
    merged = merged + gt_ref[:, d:2 * d].astype(F32) * _dot(ob.astype(BF16), wb_ref[1])
    merged = merged + gt_ref[:, 2 * d:3 * d].astype(F32) * _dot(oc_ref[...], wb_ref[2])
    x1 = x_ref[...] + _dot(merged.astype(BF16), wo_ref[...])

    xb = _rms(x1, gx_ref[...]).astype(BF16)
    qn = _seg_norm(_dot(xb, wq_ref[...]), qg_ref[...], X_HEAD_DIM).astype(BF16)
    heads = []
    for h in range(X_HEADS):
        cols = slice(h * X_HEAD_DIM, (h + 1) * X_HEAD_DIM)
        s = _dot_nt(qn[:, cols], km_ref[:, cols])
        p = jnp.exp(s - jnp.max(s, axis=-1, keepdims=True))
        inv = 1.0 / jnp.sum(p, axis=-1, keepdims=True)
        heads.append(_dot(p.astype(BF16), vm_ref[:, cols]) * inv)
    o = jnp.concatenate(heads, axis=-1).astype(BF16)
    o_ref[...] = x1 + _dot(o, wxo_ref[...])


def _merge_cross(x, oa, ub, oc, gates, pool_w, pool_scale, w_branch, w_out,
                 g_cross, w_xq, q_gain, k_mem, v_mem, w_xo, seq, mem_len):
    n, d = x.shape
    tm = min(ROW_TILE, seq)
    row = lambda width: pl.BlockSpec((tm, width), lambda i: (i, 0))
    halo = pl.BlockSpec((POOL_HALO, WIDTH), lambda i: (jnp.maximum(i * (tm // POOL_HALO) - 1, 0), 0))
    mem = pl.BlockSpec((mem_len, WIDTH), lambda i: ((i * tm) // seq, 0))
    kern = functools.partial(_merge_cross_kernel, seq=seq)
    return pl.pallas_call(
        kern,
        grid=(n // tm,),
        in_specs=[row(d), row(WIDTH), row(WIDTH), halo, row(WIDTH), row(gates.shape[1]),
                  _const_spec(pool_w.shape), _const_spec(pool_scale.shape), _const_spec(w_branch.shape),
                  _const_spec(w_out.shape), _const_spec(g_cross.shape), _const_spec(w_xq.shape),
                  _const_spec(q_gain.shape), mem, mem, _const_spec(w_xo.shape)],
        out_specs=row(d),
        out_shape=jax.ShapeDtypeStruct((n, d), F32),
        compiler_params=_params(("parallel",)),
        name="merge_cross",
    )(x, oa, ub, ub, oc, gates, pool_w, pool_scale, w_branch, w_out,
      g_cross, w_xq, q_gain, k_mem, v_mem, w_xo)


def _swiglu_chunks(xb, w1_ref, w3_ref, w2_ref, acc):
    for c in range(0, w1_ref.shape[1], FF_CHUNK):
        h1 = _dot(xb, w1_ref[:, c:c + FF_CHUNK])
        h3 = _dot(xb, w3_ref[:, c:c + FF_CHUNK])
        acc = acc + _dot((jax.nn.silu(h1) * h3).astype(BF16), w2_ref[c:c + FF_CHUNK, :])
    return acc


def _ffn_kernel(x_ref, g_ref, w1_ref, w3_ref, w2_ref, o_ref):
    x = x_ref[...]
    xb = _rms(x, g_ref[...]).astype(BF16)
    o_ref[...] = _swiglu_chunks(xb, w1_ref, w3_ref, w2_ref, x)


def _ffn(x, g, w1, w3, w2):
    n, d = x.shape
    tm = min(ROW_TILE, n)
    row = pl.BlockSpec((tm, d), lambda i: (i, 0))
    return pl.pallas_call(
        _ffn_kernel,
        grid=(n // tm,),
        in_specs=[row, _const_spec(g.shape), _const_spec(w1.shape), _const_spec(w3.shape),
                  _const_spec(w2.shape)],
        out_specs=row,
        out_shape=jax.ShapeDtypeStruct((n, d), F32),
        compiler_params=_params(("parallel",)),
        name="ffn_dense",
    )(x, g, w1, w3, w2)


def _store_token_tiles(ref, x):
    rows = x.shape[0]
    for k in range(x.shape[1] // 128):
        ref[pl.ds(k, rows, stride=SUBLANES), :] = x[:, k * 128:(k + 1) * 128]


def _load_token_tiles(ref, rows, lead=None):
    idx = (lambda k: (pl.ds(k, rows, stride=SUBLANES), slice(None))) if lead is None else \
        (lambda k: (lead, pl.ds(k, rows, stride=SUBLANES), slice(None)))
    return jnp.concatenate([ref[idx(k)] for k in range(SUBLANES)], axis=-1)


def _router_kernel(x_ref, g_ref, r_ref, lg_ref):
    xn = _rms(x_ref[...], g_ref[...])
    r = r_ref[...]
    x_hi = xn.astype(BF16)
    x_lo = (xn - x_hi.astype(F32)).astype(BF16)
    r_hi = r.astype(BF16)
    r_lo = (r - r_hi.astype(F32)).astype(BF16)
    lg_ref[...] = _dot(x_hi, r_hi) + (_dot(x_lo, r_hi) + _dot(x_hi, r_lo))


def _router(x, g, router_padded):
    n, d = x.shape
    tm = min(ROW_TILE, n)
    row = lambda width: pl.BlockSpec((tm, width), lambda i: (i, 0))
    return pl.pallas_call(
        _router_kernel,
        grid=(n // tm,),
        in_specs=[row(d), _const_spec(g.shape), _const_spec(router_padded.shape)],
        out_specs=row(router_padded.shape[1]),
        out_shape=jax.ShapeDtypeStruct((n, router_padded.shape[1]), F32),
        compiler_params=_params(("parallel",)),
        name="router",
    )(x, g, router_padded)


def _token_tile(ref, first_row):
    return ref.at[pl.ds(pl.multiple_of(first_row, SUBLANES), SUBLANES), :]


DMA_UNROLL = 8
ZERO_TOKENS = 8


def _dispatch_kernel(pad_ref, dst_ref, x_ref, g_ref, xs_hbm, xt_ref, zero_ref, sem, zsem):
    i = pl.program_id(0)
    n_steps = pl.num_programs(0)
    tm = x_ref.shape[0]
    T = SUBLANES
    slot = i % 2

    def wait_step(s):
        for _ in range(TOP_K):
            pltpu.make_async_copy(xt_ref.at[s], xs_hbm.at[pl.ds(0, tm * T), :], sem.at[s]).wait()

    @pl.when(i == 0)
    def _():
        zero_ref[...] = jnp.zeros(zero_ref.shape, F32)
        for e in range(N_EXPERTS + 1):
            first = pad_ref[0, e]
            tokens = 1 if e < N_EXPERTS else ZERO_TOKENS
            src = zero_ref.at[pl.ds(0, tokens * T), :]

            def dst(r):
                return xs_hbm.at[pl.ds(pl.multiple_of((first + r * tokens) * T, T), tokens * T), :]

            def fill(r, carry):
                pltpu.make_async_copy(src, dst(r), zsem).start()
                return carry

            def drain(r, carry):
                pltpu.make_async_copy(src, dst(0), zsem).wait()
                return carry

            lax.fori_loop(0, pad_ref[1, e] // tokens, fill, 0)
            lax.fori_loop(0, pad_ref[1, e] // tokens, drain, 0)

    _store_token_tiles(xt_ref.at[slot], _rms(x_ref[...], g_ref[...]))

    def body(gi, carry):
        for k in range(DMA_UNROLL):
            r = gi * DMA_UNROLL + k
            tok = jnp.where(r >= tm, r - tm, r)
            pltpu.make_async_copy(_token_tile(xt_ref.at[slot], tok * T), _token_tile(xs_hbm, dst_ref[0, 0, r]),
                                  sem.at[slot]).start(priority=k % 2)
        return carry

    lax.fori_loop(0, TOP_K * tm // DMA_UNROLL, body, 0)

    @pl.when(i >= 1)
    def _():
        wait_step(1 - slot)

    @pl.when(i == n_steps - 1)
    def _():
        wait_step(slot)


def _dispatch(x, g, dest_blocks, pad_info, p):
    n, d = x.shape
    tm = dest_blocks.shape[2] // TOP_K
    T = SUBLANES
    grid_spec = pltpu.PrefetchScalarGridSpec(
        num_scalar_prefetch=1,
        grid=(n // tm,),
        in_specs=[pl.BlockSpec((1, 1, TOP_K * tm), lambda i, pad: (i, 0, 0), memory_space=pltpu.SMEM),
                  pl.BlockSpec((tm, d), lambda i, pad: (i, 0)),
                  pl.BlockSpec(g.shape, lambda i, pad: (0, 0))],
        out_specs=pl.BlockSpec(memory_space=pl.ANY),
        scratch_shapes=[pltpu.VMEM((2, tm * T, d // T), F32), pltpu.VMEM((ZERO_TOKENS * T, d // T), F32),
                        pltpu.SemaphoreType.DMA((2,)), pltpu.SemaphoreType.DMA],
    )
    return pl.pallas_call(
        _dispatch_kernel,
        grid_spec=grid_spec,
        out_shape=jax.ShapeDtypeStruct((p * T, d // T), F32),
        compiler_params=_params(("arbitrary",)),
        name="dispatch",
    )(pad_info, dest_blocks, x, g)


def _expert_kernel(be_ref, nu_ref, x_ref, w1_ref, w3_ref, w2_ref, o_ref, xb_ref, y_ref):
    b = pl.program_id(0)
    c = pl.program_id(1)
    tb = xb_ref.shape[0]

    @pl.when(b < nu_ref[0])
    def _():
        @pl.when(c == 0)
        def _():
            xb_ref[...] = _load_token_tiles(x_ref, tb).astype(BF16)
            y_ref[...] = jnp.zeros(y_ref.shape, F32)

        xb = xb_ref[...]
        h1 = _dot(xb, w1_ref[...])
        h3 = _dot(xb, w3_ref[...])
        y_ref[...] += _dot((jax.nn.silu(h1) * h3).astype(BF16), w2_ref[...])

        @pl.when(c == pl.num_programs(1) - 1)
        def _():
            _store_token_tiles(o_ref, y_ref[...])

    @pl.when((b >= nu_ref[0]) & (c == pl.num_programs(1) - 1))
    def _():
        o_ref[...] = jnp.zeros(o_ref.shape, F32)


def _experts(xs, block_e, n_used, w1, w3, w2, tb):
    T = SUBLANES
    d = w1.shape[1]
    nc = w1.shape[2] // FF_CHUNK
    nblk = xs.shape[0] // (tb * T)
    blk = lambda b, nu: jnp.minimum(b, nu[0] - 1)
    chunk = lambda b, c, nu: jnp.where(b < nu[0], c, nc - 1)
    grid_spec = pltpu.PrefetchScalarGridSpec(
        num_scalar_prefetch=2,
        grid=(nblk, nc),
        in_specs=[pl.BlockSpec((tb * T, d // T), lambda b, c, be, nu: (blk(b, nu), 0)),
                  pl.BlockSpec((None, d, FF_CHUNK), lambda b, c, be, nu: (be[b], 0, chunk(b, c, nu))),
                  pl.BlockSpec((None, d, FF_CHUNK), lambda b, c, be, nu: (be[b], 0, chunk(b, c, nu))),
                  pl.BlockSpec((None, FF_CHUNK, d), lambda b, c, be, nu: (be[b], chunk(b, c, nu), 0))],
        out_specs=pl.BlockSpec((tb * T, d // T), lambda b, c, be, nu: (b, 0)),
        scratch_shapes=[pltpu.VMEM((tb, d), BF16), pltpu.VMEM((tb, d), F32)],
    )
    return pl.pallas_call(
        _expert_kernel,
        grid_spec=grid_spec,
        out_shape=jax.ShapeDtypeStruct(xs.shape, F32),
        compiler_params=_params(("arbitrary", "arbitrary")),
        name="experts",
    )(block_e, n_used, xs, w1, w3, w2)


def _combine_kernel(dst_ref, dstn_ref, x_ref, g_ref, ys_hbm, o_ref, yg_ref, sem):
    i = pl.program_id(0)
    n_steps = pl.num_programs(0)
    tm = x_ref.shape[0]
    T = SUBLANES
    slot = i % 2

    def gather(idx_ref, s):
        def body(gi, carry):
            for k in range(DMA_UNROLL):
                r = gi * DMA_UNROLL + k
                pltpu.make_async_copy(_token_tile(ys_hbm, idx_ref[0, 0, r]), _token_tile(yg_ref.at[s], r * T),
                                      sem.at[s]).start(priority=k % 2)
            return carry
        lax.fori_loop(0, TOP_K * tm // DMA_UNROLL, body, 0)

    @pl.when(i == 0)
    def _():
        gather(dst_ref, 0)

    @pl.when(i + 1 < n_steps)
    def _():
        gather(dstn_ref, 1 - slot)

    pltpu.make_async_copy(ys_hbm.at[pl.ds(0, TOP_K * tm * T), :], yg_ref.at[slot], sem.at[slot]).wait()
    g = g_ref[...]
    y0 = _load_token_tiles(yg_ref, tm, lead=slot)
    y1 = jnp.concatenate([yg_ref[slot, pl.ds(tm * T + k, tm, stride=T), :] for k in range(T)], axis=-1)
    o_ref[...] = x_ref[...] + g[:, 0:1] * y0 + g[:, 1:2] * y1


def _combine(x, ys, dest_blocks, gate):
    n, d = x.shape
    tm = dest_blocks.shape[2] // TOP_K
    T = SUBLANES
    n_steps = n // tm
    row = lambda width: pl.BlockSpec((tm, width), lambda i: (i, 0))
    idx = lambda imap: pl.BlockSpec((1, 1, TOP_K * tm), imap, memory_space=pltpu.SMEM)
    return pl.pallas_call(
        _combine_kernel,
        grid=(n_steps,),
        in_specs=[idx(lambda i: (i, 0, 0)), idx(lambda i: (jnp.minimum(i + 1, n_steps - 1), 0, 0)),
                  row(d), row(gate.shape[1]), pl.BlockSpec(memory_space=pl.ANY)],
        out_specs=row(d),
        out_shape=jax.ShapeDtypeStruct((n, d), F32),
        scratch_shapes=[pltpu.VMEM((2, TOP_K * tm * T, d // T), F32), pltpu.SemaphoreType.DMA((2,))],
        compiler_params=_params(("arbitrary",)),
        name="combine",
    )(dest_blocks, dest_blocks, x, gate, ys)


def _moe(x, g, router, w1, w3, w2):
    n, d = x.shape
    a = n * TOP_K
    tb = min(MOE_TILE, a)
    router_padded = jnp.pad(router, ((0, 0), (0, 128 - N_EXPERTS)))
    logits = _router(x, g, router_padded)
    top_logit, top_e = lax.top_k(logits[:, :N_EXPERTS], TOP_K)
    gate = jax.nn.softmax(top_logit, axis=-1)
    e_flat = top_e.reshape(a)
    onehot = (e_flat[:, None] == jnp.arange(N_EXPERTS)[None, :]).astype(jnp.int32)
    csum = jnp.cumsum(onehot, axis=0)
    rank = jnp.sum(onehot * csum, axis=1) - 1
    counts = csum[-1]
    padded = (counts + tb - 1) // tb * tb
    pend = jnp.cumsum(padded)
    pstart = pend - padded
    dest = (jnp.sum(onehot * pstart[None, :], axis=1) + rank).astype(jnp.int32)
    n_blocks = a // tb + N_EXPERTS
    p = n_blocks * tb
    block_e = jnp.minimum(jnp.searchsorted(pend, jnp.arange(n_blocks) * tb, side='right'),
                          N_EXPERTS - 1).astype(jnp.int32)
    n_used = (pend[-1] // tb).astype(jnp.int32).reshape(1)
    tm = min(ROW_TILE, n)
    dest_blocks = (dest * SUBLANES).reshape(n // tm, tm, TOP_K).transpose(0, 2, 1).reshape(n // tm, 1, TOP_K * tm)
    pad_info = jnp.stack([jnp.append(pstart + counts, pend[-1]),
                          jnp.append(padded - counts, p - pend[-1])]).astype(jnp.int32)
    xs = _dispatch(x, g, dest_blocks, pad_info, p)
    ys = _experts(xs, block_e, n_used, w1, w3, w2, tb)
    gate_padded = jnp.pad(gate, ((0, 0), (0, 128 - TOP_K)))
    return _combine(x, ys, dest_blocks, gate_padded)


def _tile_gain(gain, width, scale=1.0):
    return jnp.tile(gain, width // gain.shape[0]) * scale


def kernel(x, mem, rel_bias, norm_mix, w_in, b_gate, a_q_gain, a_k_gain, pool_w, pool_scale,
           c_q_gain, c_k_gain, c_sinks, w_branch, w_out, norm_cross, norm_mem, w_xq, w_xk, w_xv,
           x_q_gain, x_k_gain, w_xo, norm_ffn, ffn_w1, ffn_w3, ffn_w2, router, moe_w1, moe_w3, moe_w2):
    batch, seq, d = x.shape
    mem_len = mem.shape[1]
    depth = norm_mix.shape[0]
    xs = x.reshape(batch * seq, d)
    mems = mem.reshape(batch * mem_len, d)

    tab_a = rel_bias[:, :A_HEADS].T
    tab_c = rel_bias[:, A_HEADS:].T
    nb = seq // A_BLOCK
    n_near = min(nb, (_saturation_distance() + 2 * A_BLOCK - 2) // A_BLOCK)
    far_a = tab_a[:, REL_BUCKETS - 1] * LOG2E
    bias_a = _moba_bias_tiles(_bias_by_distance(tab_a, n_near * A_BLOCK), n_near) * LOG2E
    bias_a = jnp.concatenate([bias_a, jnp.broadcast_to(far_a[:, None, None, None],
                                                       (A_HEADS, 1, A_BLOCK, A_BLOCK))], axis=1)
    bias_a = jnp.concatenate([bias_a[0::2], bias_a[1::2]], axis=-1)
    bias_c = _swa_bias_tiles(_bias_by_distance(tab_c, C_WINDOW)) * LOG2E
    seg = np.arange(256) // HEAD_DIM
    bd = jnp.asarray(seg[:, None] == seg[None, :], BF16)

    row = lambda v: v.reshape(1, -1)
    for l in range(depth):
        scale = HEAD_DIM ** -0.5 * LOG2E
        gains = jnp.stack([_tile_gain(a_q_gain[l], WIDTH, scale), _tile_gain(a_k_gain[l], WIDTH),
                           _tile_gain(c_q_gain[l], WIDTH, scale), _tile_gain(c_k_gain[l], WIDTH)])
        flag_a = _bounded_flag(a_q_gain[l], a_k_gain[l], HEAD_DIM, jnp.max(jnp.abs(tab_a)))
        flag_c = _bounded_flag(c_q_gain[l], c_k_gain[l], HEAD_DIM,
                               jnp.maximum(jnp.max(jnp.abs(tab_c)), jnp.max(jnp.abs(c_sinks[l]))))
        qa, ka, va, ub, qc, kc, vc, gates = _in_proj(
            xs, row(norm_mix[l]), w_in[l].astype(BF16), row(b_gate[l]), gains, bd)
        oa = _moba(qa, ka, va, bias_a, far_a, flag_a, batch, seq)
        oc = _swa(qc, kc, vc, bias_c, c_sinks[l] * LOG2E, flag_c, batch, seq)
        k_mem, v_mem = _mem_kv(mems, row(norm_mem[l]), w_xk[l].astype(BF16), w_xv[l].astype(BF16),
                               row(_tile_gain(x_k_gain[l], WIDTH)))
        xs = _merge_cross(xs, oa, ub, oc, gates, pool_w[l].astype(BF16), row(pool_scale[l]),
                          w_branch[l].astype(BF16), w_out[l].astype(BF16), row(norm_cross[l]),
                          w_xq[l].astype(BF16), row(_tile_gain(x_q_gain[l], WIDTH, X_HEAD_DIM ** -0.5)),
                          k_mem, v_mem, w_xo[l].astype(BF16), seq, mem_len)
        i = l // 2
        if l % 2 == 0:
            xs = _ffn(xs, row(norm_ffn[l]), ffn_w1[i].astype(BF16), ffn_w3[i].astype(BF16),
                      ffn_w2[i].astype(BF16))
        else:
            xs = _moe(xs, row(norm_ffn[l]), router[i], moe_w1[i].astype(BF16), moe_w3[i].astype(BF16),
                      moe_w2[i].astype(BF16))
    return xs.reshape(batch, seq, d)
```

```python
import functools
import math

import jax
import jax.numpy as jnp
import numpy as np
from jax import lax
from jax.experimental import pallas as pl
from jax.experimental.pallas import tpu as pltpu

F32 = jnp.float32
BF16 = jnp.bfloat16

HEAD_DIM = 64
A_HEADS = 8
A_BLOCK = 256
A_TOPK = 3
B_GROUPS = 4
B_GROUP_DIM = 128
B_WINDOWS = (2, 4, 8, 16)
C_HEADS = 8
C_KV_HEADS = 2
C_WINDOW = 128
REL_BUCKETS = 32
REL_MAX_DIST = 1024
X_HEADS = 4
X_HEAD_DIM = 128
N_EXPERTS = 8
TOP_K = 2
EPS = 1e-6
NEG_INF = -1e30
LOG2E = math.log2(math.e)

WIDTH = 512
POOL_HALO = 16
ROW_TILE = 512
MOE_TILE = 1024
FF_CHUNK = 512
SUBLANES = 8
KEY_CHUNK = 128
MOBA_TRIP = 3
QK_AHEAD = 3
SWA_BLOCKS = 4
VMEM_LIMIT = 56 * 1024 * 1024
EXP2_SAFE = 100.0


def _dot(a, b):
    return jnp.dot(a, b, preferred_element_type=F32)


def _dot_nt(a, b):
    return lax.dot_general(a, b, (((1,), (1,)), ((), ())), preferred_element_type=F32)


def _rms(x, g):
    ms = jnp.mean(x * x, axis=-1, keepdims=True)
    return x * lax.rsqrt(ms + EPS) * g


def _const_spec(shape):
    zeros = (0,) * len(shape)
    return pl.BlockSpec(shape, lambda *_: zeros, pipeline_mode=pl.Buffered(1))


def _smem_spec():
    return pl.BlockSpec(memory_space=pltpu.SMEM)


def _params(sem):
    return pltpu.CompilerParams(dimension_semantics=sem, vmem_limit_bytes=VMEM_LIMIT)


def _rel_bucket(dist):
    n = jnp.maximum(dist, 0)
    max_exact = REL_BUCKETS // 2
    nf = jnp.maximum(n, 1).astype(jnp.float32)
    large = max_exact + (jnp.log(nf / max_exact) / math.log(REL_MAX_DIST / max_exact)
                         * (REL_BUCKETS - max_exact)).astype(jnp.int32)
    large = jnp.minimum(large, REL_BUCKETS - 1)
    return jnp.where(n < max_exact, n, large)


def _saturation_distance():
    ratio = REL_MAX_DIST / (REL_BUCKETS // 2)
    return int(math.ceil((REL_BUCKETS // 2) * ratio ** ((REL_BUCKETS - 1 - REL_BUCKETS // 2 + 0.5)
                                                       / (REL_BUCKETS - REL_BUCKETS // 2))))


def _bias_by_distance(tab, n_dist):
    onehot = (_rel_bucket(jnp.arange(n_dist))[:, None] == jnp.arange(REL_BUCKETS)[None, :]).astype(F32)
    return jnp.einsum('hb,db->hd', tab, onehot, precision=lax.Precision.HIGHEST)


def _skew(g, rows):
    n = g.shape[-1]
    lead = g.shape[:-1]
    tiled = jnp.broadcast_to(g[..., None, :], lead + (rows, n)).reshape(lead + (rows * n,))
    return tiled[..., :rows * (n - 1)].reshape(lead + (rows, n - 1))


def _moba_bias_tiles(bvec, n_near):
    L = A_BLOCK
    rows = []
    for delta in range(n_near):
        lo = delta * L - (L - 1)
        seg = bvec[:, max(lo, 0):delta * L + L]
        if lo < 0:
            seg = jnp.concatenate([jnp.zeros((bvec.shape[0], -lo), F32), seg], axis=1)
        rows.append(jnp.pad(seg, ((0, 0), (0, 1))))
    g = jnp.stack(rows, axis=1)
    return _skew(g, L)[..., L - 1:]


def _swa_bias_tiles(bvec):
    W = C_WINDOW
    u = np.arange(3 * W)
    g = bvec[:, np.clip(2 * W - 1 - u, 0, W - 1)]
    return _skew(g, W)[..., W - 1:3 * W - 1]


def _bounded_flag(q_gain, k_gain, head_dim, extra):
    bound = head_dim ** 0.5 * jnp.max(jnp.abs(q_gain)) * jnp.max(jnp.abs(k_gain)) * 1.02 + extra
    return (bound * LOG2E < EXP2_SAFE).astype(jnp.int32).reshape(1)


def _in_proj_kernel(x_ref, g_ref, w_ref, bg_ref, gn_ref, bd_ref,
                    qa_ref, ka_ref, va_ref, ub_ref, qc_ref, kc_ref, vc_ref, gt_ref):
    xb = _rms(x_ref[...], g_ref[...]).astype(BF16)
    bd = bd_ref[...]

    def proj(c0, width):
        return _dot(xb, w_ref[:, c0:c0 + width])

    def head_norm(t, gain):
        outs = []
        for c in range(0, t.shape[1], 256):
            wd = min(256, t.shape[1] - c)
            tc = t[:, c:c + wd]
            ss = _dot((tc * tc).astype(BF16), bd[:wd, :wd])
            outs.append(tc * lax.rsqrt(ss * (1.0 / HEAD_DIM) + EPS))
        y = outs[0] if len(outs) == 1 else jnp.concatenate(outs, axis=-1)
        return y * gain

    def tile_kv_heads(t):
        lane = lax.broadcasted_iota(jnp.int32, t.shape, 1)
        r = pltpu.roll(t, HEAD_DIM, axis=1)
        h0 = jnp.where(lane < HEAD_DIM, t, r)
        h1 = jnp.where(lane < HEAD_DIM, r, t)
        return jnp.concatenate([h0, h0, h1, h1], axis=-1)

    qa_ref[...] = head_norm(proj(0, WIDTH), gn_ref[0:1, :]).astype(BF16)
    ka_ref[...] = head_norm(proj(WIDTH, WIDTH), gn_ref[1:2, :]).astype(BF16)
    va_ref[...] = proj(2 * WIDTH, WIDTH).astype(BF16)
    ub_ref[...] = proj(3 * WIDTH, WIDTH)
    qc_ref[...] = head_norm(proj(4 * WIDTH, WIDTH), gn_ref[2:3, :]).astype(BF16)
    kv = C_KV_HEADS * HEAD_DIM
    c0 = 5 * WIDTH
    kc = head_norm(proj(c0, kv), gn_ref[3:4, :kv])
    kc_ref[...] = tile_kv_heads(kc).astype(BF16)
    vc_ref[...] = tile_kv_heads(proj(c0 + kv, kv)).astype(BF16)
    c0 += 2 * kv
    for c in range(0, gt_ref.shape[1], WIDTH):
        gl = proj(c0 + c, WIDTH) + bg_ref[:, c:c + WIDTH]
        gt_ref[:, c:c + WIDTH] = jax.nn.sigmoid(gl).astype(BF16)


def _in_proj(x, g, w_in, b_gate, gains, bd):
    n, d = x.shape
    tm = min(ROW_TILE, n)
    n_gate = b_gate.shape[1]
    row = lambda width: pl.BlockSpec((tm, width), lambda i: (i, 0))
    out_shape = [jax.ShapeDtypeStruct((n, WIDTH), BF16)] * 3 + [jax.ShapeDtypeStruct((n, WIDTH), F32)] \
        + [jax.ShapeDtypeStruct((n, WIDTH), BF16)] * 3 + [jax.ShapeDtypeStruct((n, n_gate), BF16)]
    return pl.pallas_call(
        _in_proj_kernel,
        grid=(n // tm,),
        in_specs=[row(d), _const_spec(g.shape), _const_spec(w_in.shape), _const_spec(b_gate.shape),
                  _const_spec(gains.shape), _const_spec(bd.shape)],
        out_specs=[row(WIDTH)] * 7 + [row(n_gate)],
        out_shape=out_shape,
        compiler_params=_params(("parallel",)),
        name="in_proj",
    )(x, g, w_in, b_gate, gains, bd)


def _moba_kernel(flag_ref, far_ref, q_ref, k_ref, v_ref, bias_ref, o_ref,
                 kmean_ref, vt_ref, sel_ref, m_ref, l_ref, l8_ref, acc_ref, s_ref, *, nb, n_near):
    i = pl.program_id(1)
    L = A_BLOCK
    KC = KEY_CHUNK
    PW = 2 * HEAD_DIM
    nbp = kmean_ref.shape[0]

    @pl.when(i == 0)
    def _():
        kmean_ref[...] = jnp.zeros(kmean_ref.shape, F32)
        for j in range(nb):
            kj = k_ref[j * L:(j + 1) * L, :].astype(F32)
            kmean_ref[j:j + 1, :] = jnp.mean(kj, axis=0, keepdims=True)
            vt_ref[j] = v_ref[j * L:(j + 1) * L, :].astype(F32).T.astype(BF16)

    lane = lax.broadcasted_iota(jnp.int32, (L, PW), 1)
    blk = lax.broadcasted_iota(jnp.int32, (nbp, L), 0)
    past = blk < i
    qh = []
    for h in range(A_HEADS):
        pair = slice((h // 2) * PW, (h // 2 + 1) * PW)
        q = q_ref[:, pair]
        qm = jnp.where(lane // HEAD_DIM == h % 2, q, jnp.zeros_like(q))
        qh.append(qm)
        km = kmean_ref[:, pair]
        km_hi = km.astype(BF16)
        km_lo = (km - km_hi.astype(F32)).astype(BF16)
        s = jnp.where(past, _dot_nt(km_hi, qm) + _dot_nt(km_lo, qm), -jnp.inf)
        rank = jnp.zeros((nbp, L), jnp.int32)
        for jp in range(nb):
            sj = s[jp:jp + 1, :]
            ahead = (sj > s) | ((sj == s) & (jp < blk))
            rank = rank + ahead.astype(jnp.int32)
        sel_ref[h] = (past & (rank < A_TOPK)).astype(F32)
        m_ref[h] = jnp.full((1, L), NEG_INF, F32)
        l_ref[h] = jnp.zeros((1, L), F32)
        l8_ref[h] = jnp.zeros((8, L), F32)
    acc_ref[...] = jnp.zeros(acc_ref.shape, F32)
    qpair_t = [jnp.concatenate(qh[2 * hp:2 * hp + 2], axis=0).astype(F32).T.astype(BF16)
               for hp in range(A_HEADS // 2)]

    def causal(c, n):
        kk = c + lax.broadcasted_iota(jnp.int32, (n, L), 0)
        qq = lax.broadcasted_iota(jnp.int32, (n, L), 1)
        return kk <= qq

    def bounded_blocks(blocks):
        units = [(bi, hp, c) for bi in range(len(blocks)) for hp in range(A_HEADS // 2) for c in range(0, L, KC)]

        def qk(u):
            bi, hp, c = units[u]
            kc = k_ref[pl.ds(pl.multiple_of(blocks[bi][0] * L + c, KC), KC), hp * PW:(hp + 1) * PW]
            s_ref[u % (QK_AHEAD + 1)] = _dot(kc, qpair_t[hp])

        for u in range(QK_AHEAD):
            qk(u)
        o = ps = None
        for u, (bi, hp, c) in enumerate(units):
            j, tile_idx, own = blocks[bi]
            if u + QK_AHEAD < len(units):
                qk(u + QK_AHEAD)
            if c == 0:
                o = [jnp.zeros((HEAD_DIM, L), F32)] * 2
                ps = jnp.zeros((8, 2 * L), F32)
            p = jnp.exp2(s_ref[u % (QK_AHEAD + 1)] + bias_ref[hp, tile_idx, pl.ds(c, KC), :])
            if own:
                p = jnp.where(jnp.concatenate([causal(c, KC)] * 2, axis=1), p, 0.0)
            ps = ps + jnp.sum(p.reshape(KC // 8, 8, 2 * L), axis=0)
            pb = p.astype(BF16)
            o = [o[hh] + _dot(vt_ref[j, pl.ds((2 * hp + hh) * HEAD_DIM, HEAD_DIM), pl.ds(c, KC)],
                              pb[:, hh * L:(hh + 1) * L]) for hh in range(2)]
            if c + KC < L:
                continue
            for hh in range(2):
                h = 2 * hp + hh
                rows = slice(h * HEAD_DIM, (h + 1) * HEAD_DIM)
                psh = ps[:, hh * L:(hh + 1) * L]
                if own:
                    acc_ref[rows, :] += o[hh]
                    l8_ref[h] += psh
                else:
                    w = sel_ref[h, pl.ds(j, 1), :]
                    acc_ref[rows, :] += w * o[hh]
                    l8_ref[h] += w * psh

    def run_bounded():
        def past(j):
            return (j, jnp.minimum(i - j, n_near), False)

        def body(t, carry):
            bounded_blocks([past(MOBA_TRIP * t + k) for k in range(MOBA_TRIP)])
            return carry

        lax.fori_loop(0, i // MOBA_TRIP, body, 0)
        for left in range(MOBA_TRIP):
            @pl.when(i % MOBA_TRIP == left)
            def _():
                bounded_blocks([past(i - left + k) for k in range(left)] + [(i, 0, True)])

    def online_tile(j, delta, far, own):
        for h in range(A_HEADS):
            pair = slice((h // 2) * PW, (h // 2 + 1) * PW)
            rows = slice(h * HEAD_DIM, (h + 1) * HEAD_DIM)
            s = _dot_nt(k_ref[pl.ds(pl.multiple_of(j * L, L), L), pair], qh[h])
            if not far:
                s = s + bias_ref[h // 2, delta, :, (h % 2) * L:(h % 2 + 1) * L]
            if own:
                s = jnp.where(causal(0, L), s, NEG_INF)
            mj = jnp.max(s, axis=0, keepdims=True)
            p = jnp.exp2(s - mj)
            lj = jnp.sum(p, axis=0, keepdims=True)
            o = _dot(vt_ref[j, rows, :], p.astype(BF16))
            if far:
                mj = mj + far_ref[h]
            m_old = m_ref[h]
            if own:
                m_new = jnp.maximum(m_old, mj)
                beta = jnp.exp2(mj - m_new)
            else:
                on = sel_ref[h, pl.ds(j, 1), :] > 0.5
                m_new = jnp.where(on, jnp.maximum(m_old, mj), m_old)
                beta = jnp.where(on, jnp.exp2(mj - m_new), 0.0)
            alpha = jnp.exp2(m_old - m_new)
            m_ref[h] = m_new
            l_ref[h] = alpha * l_ref[h] + beta * lj
            acc_ref[rows, :] = alpha * acc_ref[rows, :] + beta * o

    n_far = jnp.maximum(i - (n_near - 1), 0)

    def run(tile):
        def far_body(j, c):
            tile(j, None, True, False)
            return c

        def near_body(j, c):
            tile(j, i - j, False, False)
            return c

        lax.fori_loop(0, n_far, far_body, 0)
        lax.fori_loop(n_far, i, near_body, 0)
        tile(i, 0, False, True)

    @pl.when(flag_ref[0] == 1)
    def _():
        run_bounded()
        for h in range(A_HEADS):
            l_ref[h] = jnp.sum(l8_ref[h], axis=0, keepdims=True)

    @pl.when(flag_ref[0] != 1)
    def _():
        run(online_tile)

    o_t = jnp.concatenate([acc_ref[h * HEAD_DIM:(h + 1) * HEAD_DIM, :] / l_ref[h] for h in range(A_HEADS)],
                          axis=0)
    o_ref[...] = o_t.T.astype(BF16)


def _moba(q, k, v, bias_tiles, far, flag, batch, seq):
    n = q.shape[0]
    L = A_BLOCK
    nb = seq // L
    nbp = max(8, -(-nb // 8) * 8)
    n_near = bias_tiles.shape[1] - 1
    kern = functools.partial(_moba_kernel, nb=nb, n_near=n_near)
    return pl.pallas_call(
        kern,
        grid=(batch, nb),
        in_specs=[_smem_spec(), _smem_spec(),
                  pl.BlockSpec((L, WIDTH), lambda b, i: (b * nb + i, 0)),
                  pl.BlockSpec((seq, WIDTH), lambda b, i: (b, 0)),
                  pl.BlockSpec((seq, WIDTH), lambda b, i: (b, 0)),
                  _const_spec(bias_tiles.shape)],
        out_specs=pl.BlockSpec((L, WIDTH), lambda b, i: (b * nb + i, 0)),
        out_shape=jax.ShapeDtypeStruct((n, WIDTH), BF16),
        scratch_shapes=[pltpu.VMEM((nbp, WIDTH), F32),
                        pltpu.VMEM((nb, WIDTH, L), BF16),
                        pltpu.VMEM((A_HEADS, nbp, L), F32),
                        pltpu.VMEM((A_HEADS, 1, L), F32),
                        pltpu.VMEM((A_HEADS, 1, L), F32),
                        pltpu.VMEM((A_HEADS, 8, L), F32),
                        pltpu.VMEM((WIDTH, L), F32),
                        pltpu.VMEM((QK_AHEAD + 1, KEY_CHUNK, 2 * L), F32)],
        compiler_params=_params(("arbitrary", "arbitrary")),
        name="moba",
    )(flag, far, q, k, v, bias_tiles)


def _swa_kernel(flag_ref, sink_ref, q_ref, kp_ref, kc_ref, vp_ref, vc_ref, bias_ref, o_ref, s_ref):
    g = pl.program_id(0)
    chunk = pl.program_id(2)
    W = C_WINDOW
    G = C_HEADS // C_KV_HEADS
    nsub = q_ref.shape[0] // W
    gw = q_ref.shape[1]
    kall = jnp.concatenate([kp_ref[...], kc_ref[...]], axis=0)
    vall = jnp.concatenate([vp_ref[...], vc_ref[...]], axis=0)
    lane = lax.broadcasted_iota(jnp.int32, (W, gw), 1)
    qi = lax.broadcasted_iota(jnp.int32, (W, 2 * W), 0)
    kj = lax.broadcasted_iota(jnp.int32, (W, 2 * W), 1)
    dist = W + qi - kj
    band = (dist >= 0) & (dist < W)

    def block_mask(r):
        return band & ((chunk > 0) | (kj >= W)) if r == 0 else band

    def head_query(r, hh):
        q = q_ref[r * W:(r + 1) * W, :]
        return jnp.where(lane // HEAD_DIM == hh, q, jnp.zeros_like(q))

    def bounded_heads():
        units = [(r, hh) for r in range(nsub) for hh in range(G)]
        kall_t = kall.astype(F32).T.astype(BF16)

        def qk(u):
            r, hh = units[u]
            s_ref[u % (QK_AHEAD + 1)] = _dot(head_query(r, hh), kall_t[:, r * W:(r + 2) * W])

        for u in range(QK_AHEAD):
            qk(u)
        out = None
        for u, (r, hh) in enumerate(units):
            if u + QK_AHEAD < len(units):
                qk(u + QK_AHEAD)
            if hh == 0:
                out = jnp.zeros((W, gw), F32)
            s = s_ref[u % (QK_AHEAD + 1)] + bias_ref[hh]
            pb = jnp.where(block_mask(r), jnp.exp2(s), 0.0).astype(BF16)
            sink = jnp.exp2(jnp.full((1, 128), sink_ref[g * G + hh], F32))
            inv = 1.0 / (_dot(pb, jnp.ones((2 * W, 128), BF16)) + sink)
            o = _dot(pb, vall[r * W:(r + 2) * W]) * jnp.concatenate([inv] * (gw // 128), axis=-1)
            out = jnp.where(lane // HEAD_DIM == hh, o, out)
            if hh == G - 1:
                o_ref[r * W:(r + 1) * W, :] = out.astype(BF16)

    def online_heads():
        for r in range(nsub):
            kcat = kall[r * W:(r + 2) * W]
            vcat = vall[r * W:(r + 2) * W]
            out = jnp.zeros((W, gw), F32)
            for hh in range(G):
                s = jnp.where(block_mask(r), _dot_nt(head_query(r, hh), kcat) + bias_ref[hh], NEG_INF)
                sink = sink_ref[g * G + hh]
                m = jnp.maximum(jnp.max(s, axis=-1, keepdims=True), sink)
                p = jnp.exp2(s - m)
                den = jnp.sum(p, axis=-1, keepdims=True) + jnp.exp2(sink - m)
                o = _dot(p.astype(BF16), vcat) / den
                out = jnp.where(lane // HEAD_DIM == hh, o, out)
            o_ref[r * W:(r + 1) * W, :] = out.astype(BF16)

    @pl.when(flag_ref[0] == 1)
    def _():
        bounded_heads()

    @pl.when(flag_ref[0] != 1)
    def _():
        online_heads()


def _swa(q, k_t, v_t, bias_tiles, sinks, flag, batch, seq):
    n = q.shape[0]
    W = C_WINDOW
    G = C_HEADS // C_KV_HEADS
    gw = G * HEAD_DIM
    nsub = min(SWA_BLOCKS, seq // W)
    nchunk = seq // (W * nsub)
    cur = lambda g, b, j: (b * nchunk + j, g)
    prev = lambda g, b, j: (jnp.maximum((b * nchunk + j) * nsub - 1, 0), g)
    return pl.pallas_call(
        _swa_kernel,
        grid=(C_KV_HEADS, batch, nchunk),
        in_specs=[_smem_spec(), _smem_spec(),
                  pl.BlockSpec((nsub * W, gw), cur),
                  pl.BlockSpec((W, gw), prev), pl.BlockSpec((nsub * W, gw), cur),
                  pl.BlockSpec((W, gw), prev), pl.BlockSpec((nsub * W, gw), cur),
                  pl.BlockSpec((G, W, 2 * W), lambda g, b, j: (g, 0, 0))],
        out_specs=pl.BlockSpec((nsub * W, gw), cur),
        out_shape=jax.ShapeDtypeStruct((n, WIDTH), BF16),
        scratch_shapes=[pltpu.VMEM((QK_AHEAD + 1, W, 2 * W), F32)],
        compiler_params=_params(("arbitrary", "arbitrary", "arbitrary")),
        name="swa",
    )(flag, sinks, q, k_t, k_t, v_t, v_t, bias_tiles)


def _seg_norm(t, gain, seg):
    outs = []
    for c in range(0, t.shape[1], seg):
        tc = t[:, c:c + seg]
        outs.append(tc * lax.rsqrt(jnp.mean(tc * tc, axis=-1, keepdims=True) + EPS))
    return jnp.concatenate(outs, axis=-1) * gain


def _mem_kv_kernel(mem_ref, g_ref, wk_ref, wv_ref, kg_ref, k_ref, v_ref):
    mb = _rms(mem_ref[...], g_ref[...]).astype(BF16)
    k_ref[...] = _seg_norm(_dot(mb, wk_ref[...]), kg_ref[...], X_HEAD_DIM).astype(BF16)
    v_ref[...] = _dot(mb, wv_ref[...]).astype(BF16)


def _mem_kv(mem, g, w_xk, w_xv, k_gain):
    n, d = mem.shape
    tm = min(ROW_TILE, n)
    row = lambda width: pl.BlockSpec((tm, width), lambda i: (i, 0))
    return pl.pallas_call(
        _mem_kv_kernel,
        grid=(n // tm,),
        in_specs=[row(d), _const_spec(g.shape), _const_spec(w_xk.shape), _const_spec(w_xv.shape),
                  _const_spec(k_gain.shape)],
        out_specs=[row(WIDTH), row(WIDTH)],
        out_shape=[jax.ShapeDtypeStruct((n, WIDTH), BF16)] * 2,
        compiler_params=_params(("parallel",)),
        name="mem_kv",
    )(mem, g, w_xk, w_xv, k_gain)


def _merge_cross_kernel(x_ref, oa_ref, ub_ref, halo_ref, oc_ref, gt_ref, pw_ref, ps_ref, wb_ref, wo_ref,
                        gx_ref, wq_ref, qg_ref, km_ref, vm_ref, wxo_ref, o_ref, *, seq):
    tm = x_ref.shape[0]
    d = x_ref.shape[1]
    t0 = (pl.program_id(0) * tm) % seq
    H = POOL_HALO

    halo = jnp.where(t0 > 0, halo_ref[...], 0.0)
    pos = t0 + lax.broadcasted_iota(jnp.int32, (tm, B_GROUP_DIM), 0)
    mixed = []
    for gi, win in enumerate(B_WINDOWS):
        cols = slice(gi * B_GROUP_DIM, (gi + 1) * B_GROUP_DIM)
        cur = ub_ref[:, cols]
        acc = jnp.concatenate([halo[:, cols], cur], axis=0)
        step = 1
        while step < win:
            acc = acc + pltpu.roll(acc, step, axis=0)
            step *= 2
        cnt = jnp.minimum(pos + 1, win).astype(F32)
        pooled = acc[H:, :] / cnt - cur
        mixed.append(_dot(pooled.astype(BF16), pw_ref[gi]))
    ob = jnp.concatenate(mixed, axis=-1) * ps_ref[...]

    merged = gt_ref[:, 0:d].astype(F32) * _dot(oa_ref[...], wb_ref[0])
    merged = merged + gt_ref[:, d:2 * d].astype(F32) * _dot(ob.astype(BF16), wb_ref[1])
    merged = merged + gt_ref[:, 2 * d:3 * d].astype(F32) * _dot(oc_ref[...], wb_ref[2])
    x1 = x_ref[...] + _dot(merged.astype(BF16), wo_ref[...])

    xb = _rms(x1, gx_ref[...]).astype(BF16)
    qn = _seg_norm(_dot(xb, wq_ref[...]), qg_ref[...], X_HEAD_DIM).astype(BF16)
    heads = []
    for h in range(X_HEADS):
        cols = slice(h * X_HEAD_DIM, (h + 1) * X_HEAD_DIM)
        s = _dot_nt(qn[:, cols], km_ref[:, cols])
        p = jnp.exp(s - jnp.max(s, axis=-1, keepdims=True))
        inv = 1.0 / jnp.sum(p, axis=-1, keepdims=True)
        heads.append(_dot(p.astype(BF16), vm_ref[:, cols]) * inv)
    o = jnp.concatenate(heads, axis=-1).astype(BF16)
    o_ref[...] = x1 + _dot(o, wxo_ref[...])


def _merge_cross(x, oa, ub, oc, gates, pool_w, pool_scale, w_branch, w_out,
                 g_cross, w_xq, q_gain, k_mem, v_mem, w_xo, seq, mem_len):
    n, d = x.shape
    tm = min(ROW_TILE, seq)
    row = lambda width: pl.BlockSpec((tm, width), lambda i: (i, 0))
    halo = pl.BlockSpec((POOL_HALO, WIDTH), lambda i: (jnp.maximum(i * (tm // POOL_HALO) - 1, 0), 0))
    mem = pl.BlockSpec((mem_len, WIDTH), lambda i: ((i * tm) // seq, 0))
    kern = functools.partial(_merge_cross_kernel, seq=seq)
    return pl.pallas_call(
        kern,
        grid=(n // tm,),
        in_specs=[row(d), row(WIDTH), row(WIDTH), halo, row(WIDTH), row(gates.shape[1]),
                  _const_spec(pool_w.shape), _const_spec(pool_scale.shape), _const_spec(w_branch.shape),
                  _const_spec(w_out.shape), _const_spec(g_cross.shape), _const_spec(w_xq.shape),
                  _const_spec(q_gain.shape), mem, mem, _const_spec(w_xo.shape)],
        out_specs=row(d),
        out_shape=jax.ShapeDtypeStruct((n, d), F32),
        compiler_params=_params(("parallel",)),
        name="merge_cross",
    )(x, oa, ub, ub, oc, gates, pool_w, pool_scale, w_branch, w_out,
      g_cross, w_xq, q_gain, k_mem, v_mem, w_xo)


def _swiglu_chunks(xb, w1_ref, w3_ref, w2_ref, acc):
    for c in range(0, w1_ref.shape[1], FF_CHUNK):
        h1 = _dot(xb, w1_ref[:, c:c + FF_CHUNK])
        h3 = _dot(xb, w3_ref[:, c:c + FF_CHUNK])
        acc = acc + _dot((jax.nn.silu(h1) * h3).astype(BF16), w2_ref[c:c + FF_CHUNK, :])
    return acc


def _ffn_kernel(x_ref, g_ref, w1_ref, w3_ref, w2_ref, o_ref):
    x = x_ref[...]
    xb = _rms(x, g_ref[...]).astype(BF16)
    o_ref[...] = _swiglu_chunks(xb, w1_ref, w3_ref, w2_ref, x)


def _ffn(x, g, w1, w3, w2):
    n, d = x.shape
    tm = min(ROW_TILE, n)
    row = pl.BlockSpec((tm, d), lambda i: (i, 0))
    return pl.pallas_call(
        _ffn_kernel,
        grid=(n // tm,),
        in_specs=[row, _const_spec(g.shape), _const_spec(w1.shape), _const_spec(w3.shape),
                  _const_spec(w2.shape)],
        out_specs=row,
        out_shape=jax.ShapeDtypeStruct((n, d), F32),
        compiler_params=_params(("parallel",)),
        name="ffn_dense",
    )(x, g, w1, w3, w2)


def _store_token_tiles(ref, x):
    rows = x.shape[0]
    for k in range(x.shape[1] // 128):
        ref[pl.ds(k, rows, stride=SUBLANES), :] = x[:, k * 128:(k + 1) * 128]


def _load_token_tiles(ref, rows, lead=None):
    idx = (lambda k: (pl.ds(k, rows, stride=SUBLANES), slice(None))) if lead is None else \
        (lambda k: (lead, pl.ds(k, rows, stride=SUBLANES), slice(None)))
    return jnp.concatenate([ref[idx(k)] for k in range(SUBLANES)], axis=-1)


def _router_kernel(x_ref, g_ref, r_ref, lg_ref):
    xn = _rms(x_ref[...], g_ref[...])
    r = r_ref[...]
    x_hi = xn.astype(BF16)
    x_lo = (xn - x_hi.astype(F32)).astype(BF16)
    r_hi = r.astype(BF16)
    r_lo = (r - r_hi.astype(F32)).astype(BF16)
    lg_ref[...] = _dot(x_hi, r_hi) + (_dot(x_lo, r_hi) + _dot(x_hi, r_lo))


def _router(x, g, router_padded):
    n, d = x.shape
    tm = min(ROW_TILE, n)
    row = lambda width: pl.BlockSpec((tm, width), lambda i: (i, 0))
    return pl.pallas_call(
        _router_kernel,
        grid=(n // tm,),
        in_specs=[row(d), _const_spec(g.shape), _const_spec(router_padded.shape)],
        out_specs=row(router_padded.shape[1]),
        out_shape=jax.ShapeDtypeStruct((n, router_padded.shape[1]), F32),
        compiler_params=_params(("parallel",)),
        name="router",
    )(x, g, router_padded)


def _token_tile(ref, first_row):
    return ref.at[pl.ds(pl.multiple_of(first_row, SUBLANES), SUBLANES), :]


DMA_UNROLL = 8
ZERO_TOKENS = 8


def _dispatch_kernel(pad_ref, dst_ref, x_ref, g_ref, xs_hbm, xt_ref, zero_ref, sem, zsem):
    i = pl.program_id(0)
    n_steps = pl.num_programs(0)
    tm = x_ref.shape[0]
    T = SUBLANES
    slot = i % 2

    def wait_step(s):
        for _ in range(TOP_K):
            pltpu.make_async_copy(xt_ref.at[s], xs_hbm.at[pl.ds(0, tm * T), :], sem.at[s]).wait()

    @pl.when(i == 0)
    def _():
        zero_ref[...] = jnp.zeros(zero_ref.shape, F32)
        for e in range(N_EXPERTS + 1):
            first = pad_ref[0, e]
            tokens = 1 if e < N_EXPERTS else ZERO_TOKENS
            src = zero_ref.at[pl.ds(0, tokens * T), :]

            def dst(r):
                return xs_hbm.at[pl.ds(pl.multiple_of((first + r * tokens) * T, T), tokens * T), :]

            def fill(r, carry):
                pltpu.make_async_copy(src, dst(r), zsem).start()
                return carry

            def drain(r, carry):
                pltpu.make_async_copy(src, dst(0), zsem).wait()
                return carry

            lax.fori_loop(0, pad_ref[1, e] // tokens, fill, 0)
            lax.fori_loop(0, pad_ref[1, e] // tokens, drain, 0)

    _store_token_tiles(xt_ref.at[slot], _rms(x_ref[...], g_ref[...]))

    def body(gi, carry):
        for k in range(DMA_UNROLL):
            r = gi * DMA_UNROLL + k
            tok = jnp.where(r >= tm, r - tm, r)
            pltpu.make_async_copy(_token_tile(xt_ref.at[slot], tok * T), _token_tile(xs_hbm, dst_ref[0, 0, r]),
                                  sem.at[slot]).start(priority=k % 2)
        return carry

    lax.fori_loop(0, TOP_K * tm // DMA_UNROLL, body, 0)

    @pl.when(i >= 1)
    def _():
        wait_step(1 - slot)

    @pl.when(i == n_steps - 1)
    def _():
        wait_step(slot)


def _dispatch(x, g, dest_blocks, pad_info, p):
    n, d = x.shape
    tm = dest_blocks.shape[2] // TOP_K
    T = SUBLANES
    grid_spec = pltpu.PrefetchScalarGridSpec(
        num_scalar_prefetch=1,
        grid=(n // tm,),
        in_specs=[pl.BlockSpec((1, 1, TOP_K * tm), lambda i, pad: (i, 0, 0), memory_space=pltpu.SMEM),
                  pl.BlockSpec((tm, d), lambda i, pad: (i, 0)),
                  pl.BlockSpec(g.shape, lambda i, pad: (0, 0))],
        out_specs=pl.BlockSpec(memory_space=pl.ANY),
        scratch_shapes=[pltpu.VMEM((2, tm * T, d // T), F32), pltpu.VMEM((ZERO_TOKENS * T, d // T), F32),
                        pltpu.SemaphoreType.DMA((2,)), pltpu.SemaphoreType.DMA],
    )
    return pl.pallas_call(
        _dispatch_kernel,
        grid_spec=grid_spec,
        out_shape=jax.ShapeDtypeStruct((p * T, d // T), F32),
        compiler_params=_params(("arbitrary",)),
        name="dispatch",
    )(pad_info, dest_blocks, x, g)


def _expert_kernel(be_ref, nu_ref, x_ref, w1_ref, w3_ref, w2_ref, o_ref, xb_ref, y_ref):
    b = pl.program_id(0)
    c = pl.program_id(1)
    tb = xb_ref.shape[0]

    @pl.when(b < nu_ref[0])
    def _():
        @pl.when(c == 0)
        def _():
            xb_ref[...] = _load_token_tiles(x_ref, tb).astype(BF16)
            y_ref[...] = jnp.zeros(y_ref.shape, F32)

        xb = xb_ref[...]
        h1 = _dot(xb, w1_ref[...].astype(BF16))
        h3 = _dot(xb, w3_ref[...].astype(BF16))
        y_ref[...] += _dot((jax.nn.silu(h1) * h3).astype(BF16), w2_ref[...].astype(BF16))

        @pl.when(c == pl.num_programs(1) - 1)
        def _():
            _store_token_tiles(o_ref, y_ref[...])

    @pl.when((b >= nu_ref[0]) & (c == pl.num_programs(1) - 1))
    def _():
        o_ref[...] = jnp.zeros(o_ref.shape, F32)


def _experts(xs, block_e, n_used, w1, w3, w2, tb):
    T = SUBLANES
    d = w1.shape[1]
    nc = w1.shape[2] // FF_CHUNK
    nblk = xs.shape[0] // (tb * T)
    blk = lambda b, nu: jnp.minimum(b, nu[0] - 1)
    chunk = lambda b, c, nu: jnp.where(b < nu[0], c, nc - 1)
    grid_spec = pltpu.PrefetchScalarGridSpec(
        num_scalar_prefetch=2,
        grid=(nblk, nc),
        in_specs=[pl.BlockSpec((tb * T, d // T), lambda b, c, be, nu: (blk(b, nu), 0)),
                  pl.BlockSpec((None, d, FF_CHUNK), lambda b, c, be, nu: (be[b], 0, chunk(b, c, nu))),
                  pl.BlockSpec((None, d, FF_CHUNK), lambda b, c, be, nu: (be[b], 0, chunk(b, c, nu))),
                  pl.BlockSpec((None, FF_CHUNK, d), lambda b, c, be, nu: (be[b], chunk(b, c, nu), 0))],
        out_specs=pl.BlockSpec((tb * T, d // T), lambda b, c, be, nu: (b, 0)),
        scratch_shapes=[pltpu.VMEM((tb, d), BF16), pltpu.VMEM((tb, d), F32)],
    )
    return pl.pallas_call(
        _expert_kernel,
        grid_spec=grid_spec,
        out_shape=jax.ShapeDtypeStruct(xs.shape, F32),
        compiler_params=_params(("arbitrary", "arbitrary")),
        name="experts",
    )(block_e, n_used, xs, w1, w3, w2)


def _combine_kernel(dst_ref, dstn_ref, x_ref, g_ref, ys_hbm, o_ref, yg_ref, sem):
    i = pl.program_id(0)
    n_steps = pl.num_programs(0)
    tm = x_ref.shape[0]
    T = SUBLANES
    slot = i % 2

    def gather(idx_ref, s):
        def body(gi, carry):
            for k in range(DMA_UNROLL):
                r = gi * DMA_UNROLL + k
                pltpu.make_async_copy(_token_tile(ys_hbm, idx_ref[0, 0, r]), _token_tile(yg_ref.at[s], r * T),
                                      sem.at[s]).start(priority=k % 2)
            return carry
        lax.fori_loop(0, TOP_K * tm // DMA_UNROLL, body, 0)

    @pl.when(i == 0)
    def _():
        gather(dst_ref, 0)

    @pl.when(i + 1 < n_steps)
    def _():
        gather(dstn_ref, 1 - slot)

    pltpu.make_async_copy(ys_hbm.at[pl.ds(0, TOP_K * tm * T), :], yg_ref.at[slot], sem.at[slot]).wait()
    g = g_ref[...]
    y0 = _load_token_tiles(yg_ref, tm, lead=slot)
    y1 = jnp.concatenate([yg_ref[slot, pl.ds(tm * T + k, tm, stride=T), :] for k in range(T)], axis=-1)
    o_ref[...] = x_ref[...] + g[:, 0:1] * y0 + g[:, 1:2] * y1


def _combine(x, ys, dest_blocks, gate):
    n, d = x.shape
    tm = dest_blocks.shape[2] // TOP_K
    T = SUBLANES
    n_steps = n // tm
    row = lambda width: pl.BlockSpec((tm, width), lambda i: (i, 0))
    idx = lambda imap: pl.BlockSpec((1, 1, TOP_K * tm), imap, memory_space=pltpu.SMEM)
    return pl.pallas_call(
        _combine_kernel,
        grid=(n_steps,),
        in_specs=[idx(lambda i: (i, 0, 0)), idx(lambda i: (jnp.minimum(i + 1, n_steps - 1), 0, 0)),
                  row(d), row(gate.shape[1]), pl.BlockSpec(memory_space=pl.ANY)],
        out_specs=row(d),
        out_shape=jax.ShapeDtypeStruct((n, d), F32),
        scratch_shapes=[pltpu.VMEM((2, TOP_K * tm * T, d // T), F32), pltpu.SemaphoreType.DMA((2,))],
        compiler_params=_params(("arbitrary",)),
        name="combine",
    )(dest_blocks, dest_blocks, x, gate, ys)


def _moe(x, g, router, w1, w3, w2):
    n, d = x.shape
    a = n * TOP_K
    tb = min(MOE_TILE, a)
    router_padded = jnp.pad(router, ((0, 0), (0, 128 - N_EXPERTS)))
    logits = _router(x, g, router_padded)
    top_logit, top_e = lax.top_k(logits[:, :N_EXPERTS], TOP_K)
    gate = jax.nn.softmax(top_logit, axis=-1)
    e_flat = top_e.reshape(a)
    onehot = (e_flat[:, None] == jnp.arange(N_EXPERTS)[None, :]).astype(jnp.int32)
    csum = jnp.cumsum(onehot, axis=0)
    rank = jnp.sum(onehot * csum, axis=1) - 1
    counts = csum[-1]
    padded = (counts + tb - 1) // tb * tb
    pend = jnp.cumsum(padded)
    pstart = pend - padded
    dest = (jnp.sum(onehot * pstart[None, :], axis=1) + rank).astype(jnp.int32)
    n_blocks = a // tb + N_EXPERTS
    p = n_blocks * tb
    block_e = jnp.minimum(jnp.searchsorted(pend, jnp.arange(n_blocks) * tb, side='right'),
                          N_EXPERTS - 1).astype(jnp.int32)
    n_used = (pend[-1] // tb).astype(jnp.int32).reshape(1)
    tm = min(ROW_TILE, n)
    dest_blocks = (dest * SUBLANES).reshape(n // tm, tm, TOP_K).transpose(0, 2, 1).reshape(n // tm, 1, TOP_K * tm)
    pad_info = jnp.stack([jnp.append(pstart + counts, pend[-1]),
                          jnp.append(padded - counts, p - pend[-1])]).astype(jnp.int32)
    xs = _dispatch(x, g, dest_blocks, pad_info, p)
    ys = _experts(xs, block_e, n_used, w1, w3, w2, tb)
    gate_padded = jnp.pad(gate, ((0, 0), (0, 128 - TOP_K)))
    return _combine(x, ys, dest_blocks, gate_padded)


def _tile_gain(gain, width, scale=1.0):
    return jnp.tile(gain, width // gain.shape[0]) * scale


def kernel(x, mem, rel_bias, norm_mix, w_in, b_gate, a_q_gain, a_k_gain, pool_w, pool_scale,
           c_q_gain, c_k_gain, c_sinks, w_branch, w_out, norm_cross, norm_mem, w_xq, w_xk, w_xv,
           x_q_gain, x_k_gain, w_xo, norm_ffn, ffn_w1, ffn_w3, ffn_w2, router, moe_w1, moe_w3, moe_w2):
    batch, seq, d = x.shape
    mem_len = mem.shape[1]
    depth = norm_mix.shape[0]
    xs = x.reshape(batch * seq, d)
    mems = mem.reshape(batch * mem_len, d)

    tab_a = rel_bias[:, :A_HEADS].T
    tab_c = rel_bias[:, A_HEADS:].T
    nb = seq // A_BLOCK
    n_near = min(nb, (_saturation_distance() + 2 * A_BLOCK - 2) // A_BLOCK)
    far_a = tab_a[:, REL_BUCKETS - 1] * LOG2E
    bias_a = _moba_bias_tiles(_bias_by_distance(tab_a, n_near * A_BLOCK), n_near) * LOG2E
    bias_a = jnp.concatenate([bias_a, jnp.broadcast_to(far_a[:, None, None, None],
                                                       (A_HEADS, 1, A_BLOCK, A_BLOCK))], axis=1)
    bias_a = jnp.concatenate([bias_a[0::2], bias_a[1::2]], axis=-1)
    bias_c = _swa_bias_tiles(_bias_by_distance(tab_c, C_WINDOW)) * LOG2E
    seg = np.arange(256) // HEAD_DIM
    bd = jnp.asarray(seg[:, None] == seg[None, :], BF16)

    row = lambda v: v.reshape(1, -1)
    for l in range(depth):
        scale = HEAD_DIM ** -0.5 * LOG2E
        gains = jnp.stack([_tile_gain(a_q_gain[l], WIDTH, scale), _tile_gain(a_k_gain[l], WIDTH),
                           _tile_gain(c_q_gain[l], WIDTH, scale), _tile_gain(c_k_gain[l], WIDTH)])
        flag_a = _bounded_flag(a_q_gain[l], a_k_gain[l], HEAD_DIM, jnp.max(jnp.abs(tab_a)))
        flag_c = _bounded_flag(c_q_gain[l], c_k_gain[l], HEAD_DIM,
                               jnp.maximum(jnp.max(jnp.abs(tab_c)), jnp.max(jnp.abs(c_sinks[l]))))
        qa, ka, va, ub, qc, kc, vc, gates = _in_proj(
            xs, row(norm_mix[l]), w_in[l].astype(BF16), row(b_gate[l]), gains, bd)
        oa = _moba(qa, ka, va, bias_a, far_a, flag_a, batch, seq)
        oc = _swa(qc, kc, vc, bias_c, c_sinks[l] * LOG2E, flag_c, batch, seq)
        k_mem, v_mem = _mem_kv(mems, row(norm_mem[l]), w_xk[l].astype(BF16), w_xv[l].astype(BF16),
                               row(_tile_gain(x_k_gain[l], WIDTH)))
        xs = _merge_cross(xs, oa, ub, oc, gates, pool_w[l].astype(BF16), row(pool_scale[l]),
                          w_branch[l].astype(BF16), w_out[l].astype(BF16), row(norm_cross[l]),
                          w_xq[l].astype(BF16), row(_tile_gain(x_q_gain[l], WIDTH, X_HEAD_DIM ** -0.5)),
                          k_mem, v_mem, w_xo[l].astype(BF16), seq, mem_len)
        i = l // 2
        if l % 2 == 0:
            xs = _ffn(xs, row(norm_ffn[l]), ffn_w1[i].astype(BF16), ffn_w3[i].astype(BF16),
                      ffn_w2[i].astype(BF16))
        else:
            xs = _moe(xs, row(norm_ffn[l]), router[i], moe_w1[i], moe_w3[i], moe_w2[i])
    return xs.reshape(batch, seq, d)
```

```python
import functools
import math

import jax
import jax.numpy as jnp
import numpy as np
from jax import lax
from jax.experimental import pallas as pl
from jax.experimental.pallas import tpu as pltpu

F32 = jnp.float32
BF16 = jnp.bfloat16

HEAD_DIM = 64
A_HEADS = 8
A_BLOCK = 256
A_TOPK = 3
B_GROUPS = 4
B_GROUP_DIM = 128
B_WINDOWS = (2, 4, 8, 16)
C_HEADS = 8
C_KV_HEADS = 2
C_WINDOW = 128
REL_BUCKETS = 32
REL_MAX_DIST = 1024
X_HEADS = 4
X_HEAD_DIM = 128
N_EXPERTS = 8
TOP_K = 2
EPS = 1e-6
NEG_INF = -1e30
LOG2E = math.log2(math.e)

WIDTH = 512
POOL_HALO = 16
ROW_TILE = 512
MOE_TILE = 1024
FF_CHUNK = 512
SUBLANES = 8
KEY_CHUNK = 128
MOBA_TRIP = 3
QK_AHEAD = 3
SWA_BLOCKS = 4
VMEM_LIMIT = 56 * 1024 * 1024
EXP2_SAFE = 100.0


def _dot(a, b):
    return jnp.dot(a, b, preferred_element_type=F32)


def _dot_nt(a, b):
    return lax.dot_general(a, b, (((1,), (1,)), ((), ())), preferred_element_type=F32)


def _rms(x, g):
    ms = jnp.mean(x * x, axis=-1, keepdims=True)
    return x * lax.rsqrt(ms + EPS) * g


def _const_spec(shape):
    zeros = (0,) * len(shape)
    return pl.BlockSpec(shape, lambda *_: zeros, pipeline_mode=pl.Buffered(1))


def _smem_spec():
    return pl.BlockSpec(memory_space=pltpu.SMEM)


def _params(sem):
    return pltpu.CompilerParams(dimension_semantics=sem, vmem_limit_bytes=VMEM_LIMIT)


def _rel_bucket(dist):
    n = jnp.maximum(dist, 0)
    max_exact = REL_BUCKETS // 2
    nf = jnp.maximum(n, 1).astype(jnp.float32)
    large = max_exact + (jnp.log(nf / max_exact) / math.log(REL_MAX_DIST / max_exact)
                         * (REL_BUCKETS - max_exact)).astype(jnp.int32)
    large = jnp.minimum(large, REL_BUCKETS - 1)
    return jnp.where(n < max_exact, n, large)


def _saturation_distance():
    ratio = REL_MAX_DIST / (REL_BUCKETS // 2)
    return int(math.ceil((REL_BUCKETS // 2) * ratio ** ((REL_BUCKETS - 1 - REL_BUCKETS // 2 + 0.5)
                                                       / (REL_BUCKETS - REL_BUCKETS // 2))))


def _bias_by_distance(tab, n_dist):
    onehot = (_rel_bucket(jnp.arange(n_dist))[:, None] == jnp.arange(REL_BUCKETS)[None, :]).astype(F32)
    return jnp.einsum('hb,db->hd', tab, onehot, precision=lax.Precision.HIGHEST)


def _skew(g, rows):
    n = g.shape[-1]
    lead = g.shape[:-1]
    tiled = jnp.broadcast_to(g[..., None, :], lead + (rows, n)).reshape(lead + (rows * n,))
    return tiled[..., :rows * (n - 1)].reshape(lead + (rows, n - 1))


def _moba_bias_tiles(bvec, n_near):
    L = A_BLOCK
    rows = []
    for delta in range(n_near):
        lo = delta * L - (L - 1)
        seg = bvec[:, max(lo, 0):delta * L + L]
        if lo < 0:
            seg = jnp.concatenate([jnp.zeros((bvec.shape[0], -lo), F32), seg], axis=1)
        rows.append(jnp.pad(seg, ((0, 0), (0, 1))))
    g = jnp.stack(rows, axis=1)
    return _skew(g, L)[..., L - 1:]


def _swa_bias_tiles(bvec):
    W = C_WINDOW
    u = np.arange(3 * W)
    g = bvec[:, np.clip(2 * W - 1 - u, 0, W - 1)]
    return _skew(g, W)[..., W - 1:3 * W - 1]


def _bounded_flag(q_gain, k_gain, head_dim, extra):
    bound = head_dim ** 0.5 * jnp.max(jnp.abs(q_gain)) * jnp.max(jnp.abs(k_gain)) * 1.02 + extra
    return (bound * LOG2E < EXP2_SAFE).astype(jnp.int32).reshape(1)


def _in_proj_kernel(x_ref, g_ref, w_ref, bg_ref, gn_ref, bd_ref,
                    qa_ref, ka_ref, va_ref, ub_ref, qc_ref, kc_ref, vc_ref, gt_ref):
    xb = _rms(x_ref[...], g_ref[...]).astype(BF16)
    bd = bd_ref[...]

    def proj(c0, width):
        return _dot(xb, w_ref[:, c0:c0 + width])

    def head_norm(t, gain):
        outs = []
        for c in range(0, t.shape[1], 256):
            wd = min(256, t.shape[1] - c)
            tc = t[:, c:c + wd]
            ss = _dot((tc * tc).astype(BF16), bd[:wd, :wd])
            outs.append(tc * lax.rsqrt(ss * (1.0 / HEAD_DIM) + EPS))
        y = outs[0] if len(outs) == 1 else jnp.concatenate(outs, axis=-1)
        return y * gain

    def tile_kv_heads(t):
        lane = lax.broadcasted_iota(jnp.int32, t.shape, 1)
        r = pltpu.roll(t, HEAD_DIM, axis=1)
        h0 = jnp.where(lane < HEAD_DIM, t, r)
        h1 = jnp.where(lane < HEAD_DIM, r, t)
        return jnp.concatenate([h0, h0, h1, h1], axis=-1)

    qa_ref[...] = head_norm(proj(0, WIDTH), gn_ref[0:1, :]).astype(BF16)
    ka_ref[...] = head_norm(proj(WIDTH, WIDTH), gn_ref[1:2, :]).astype(BF16)
    va_ref[...] = proj(2 * WIDTH, WIDTH).astype(BF16)
    ub_ref[...] = proj(3 * WIDTH, WIDTH)
    qc_ref[...] = head_norm(proj(4 * WIDTH, WIDTH), gn_ref[2:3, :]).astype(BF16)
    kv = C_KV_HEADS * HEAD_DIM
    c0 = 5 * WIDTH
    kc = head_norm(proj(c0, kv), gn_ref[3:4, :kv])
    kc_ref[...] = tile_kv_heads(kc).astype(BF16)
    vc_ref[...] = tile_kv_heads(proj(c0 + kv, kv)).astype(BF16)
    c0 += 2 * kv
    for c in range(0, gt_ref.shape[1], WIDTH):
        gl = proj(c0 + c, WIDTH) + bg_ref[:, c:c + WIDTH]
        gt_ref[:, c:c + WIDTH] = jax.nn.sigmoid(gl).astype(BF16)


def _in_proj(x, g, w_in, b_gate, gains, bd):
    n, d = x.shape
    tm = min(ROW_TILE, n)
    n_gate = b_gate.shape[1]
    row = lambda width: pl.BlockSpec((tm, width), lambda i: (i, 0))
    out_shape = [jax.ShapeDtypeStruct((n, WIDTH), BF16)] * 3 + [jax.ShapeDtypeStruct((n, WIDTH), F32)] \
        + [jax.ShapeDtypeStruct((n, WIDTH), BF16)] * 3 + [jax.ShapeDtypeStruct((n, n_gate), BF16)]
    return pl.pallas_call(
        _in_proj_kernel,
        grid=(n // tm,),
        in_specs=[row(d), _const_spec(g.shape), _const_spec(w_in.shape), _const_spec(b_gate.shape),
                  _const_spec(gains.shape), _const_spec(bd.shape)],
        out_specs=[row(WIDTH)] * 7 + [row(n_gate)],
        out_shape=out_shape,
        compiler_params=_params(("parallel",)),
        name="in_proj",
    )(x, g, w_in, b_gate, gains, bd)


def _moba_kernel(flag_ref, far_ref, q_ref, k_ref, v_ref, bias_ref, o_ref,
                 kmean_ref, vt_ref, sel_ref, m_ref, l_ref, l8_ref, acc_ref, s_ref, *, nb, n_near):
    i = pl.program_id(1)
    L = A_BLOCK
    KC = KEY_CHUNK
    PW = 2 * HEAD_DIM
    nbp = kmean_ref.shape[0]

    @pl.when(i == 0)
    def _():
        kmean_ref[...] = jnp.zeros(kmean_ref.shape, F32)
        for j in range(nb):
            kj = k_ref[j * L:(j + 1) * L, :].astype(F32)
            kmean_ref[j:j + 1, :] = jnp.mean(kj, axis=0, keepdims=True)
            vt_ref[j] = v_ref[j * L:(j + 1) * L, :].astype(F32).T.astype(BF16)

    lane = lax.broadcasted_iota(jnp.int32, (L, PW), 1)
    blk = lax.broadcasted_iota(jnp.int32, (nbp, L), 0)
    past = blk < i
    qh = []
    for h in range(A_HEADS):
        pair = slice((h // 2) * PW, (h // 2 + 1) * PW)
        q = q_ref[:, pair]
        qm = jnp.where(lane // HEAD_DIM == h % 2, q, jnp.zeros_like(q))
        qh.append(qm)
        km = kmean_ref[:, pair]
        km_hi = km.astype(BF16)
        km_lo = (km - km_hi.astype(F32)).astype(BF16)
        s = jnp.where(past, _dot_nt(km_hi, qm) + _dot_nt(km_lo, qm), -jnp.inf)
        rank = jnp.zeros((nbp, L), jnp.int32)
        for jp in range(nb):
            sj = s[jp:jp + 1, :]
            ahead = (sj > s) | ((sj == s) & (jp < blk))
            rank = rank + ahead.astype(jnp.int32)
        sel_ref[h] = (past & (rank < A_TOPK)).astype(F32)
        m_ref[h] = jnp.full((1, L), NEG_INF, F32)
        l_ref[h] = jnp.zeros((1, L), F32)
        l8_ref[h] = jnp.zeros((8, L), F32)
    acc_ref[...] = jnp.zeros(acc_ref.shape, F32)
    qpair_t = [jnp.concatenate(qh[2 * hp:2 * hp + 2], axis=0).astype(F32).T.astype(BF16)
               for hp in range(A_HEADS // 2)]

    def causal(c, n):
        kk = c + lax.broadcasted_iota(jnp.int32, (n, L), 0)
        qq = lax.broadcasted_iota(jnp.int32, (n, L), 1)
        return kk <= qq

    def bounded_blocks(blocks):
        units = [(bi, hp, c) for bi in range(len(blocks)) for hp in range(A_HEADS // 2) for c in range(0, L, KC)]

        def qk(u):
            bi, hp, c = units[u]
            kc = k_ref[pl.ds(pl.multiple_of(blocks[bi][0] * L + c, KC), KC), hp * PW:(hp + 1) * PW]
            s_ref[u % (QK_AHEAD + 1)] = _dot(kc, qpair_t[hp])

        for u in range(QK_AHEAD):
            qk(u)
        o = ps = None
        for u, (bi, hp, c) in enumerate(units):
            j, tile_idx, own = blocks[bi]
            if u + QK_AHEAD < len(units):
                qk(u + QK_AHEAD)
            if c == 0:
                o = [jnp.zeros((HEAD_DIM, L), F32)] * 2
                ps = jnp.zeros((8, 2 * L), F32)
            p = jnp.exp2(s_ref[u % (QK_AHEAD + 1)] + bias_ref[hp, tile_idx, pl.ds(c, KC), :])
            if own:
                p = jnp.where(jnp.concatenate([causal(c, KC)] * 2, axis=1), p, 0.0)
            ps = ps + jnp.sum(p.reshape(KC // 8, 8, 2 * L), axis=0)
            pb = p.astype(BF16)
            o = [o[hh] + _dot(vt_ref[j, pl.ds((2 * hp + hh) * HEAD_DIM, HEAD_DIM), pl.ds(c, KC)],
                              pb[:, hh * L:(hh + 1) * L]) for hh in range(2)]
            if c + KC < L:
                continue
            for hh in range(2):
                h = 2 * hp + hh
                rows = slice(h * HEAD_DIM, (h + 1) * HEAD_DIM)
                psh = ps[:, hh * L:(hh + 1) * L]
                if own:
                    acc_ref[rows, :] += o[hh]
                    l8_ref[h] += psh
                else:
                    w = sel_ref[h, pl.ds(j, 1), :]
                    acc_ref[rows, :] += w * o[hh]
                    l8_ref[h] += w * psh

    def run_bounded():
        def past(j):
            return (j, jnp.minimum(i - j, n_near), False)

        def body(t, carry):
            bounded_blocks([past(MOBA_TRIP * t + k) for k in range(MOBA_TRIP)])
            return carry

        lax.fori_loop(0, i // MOBA_TRIP, body, 0)
        for left in range(MOBA_TRIP):
            @pl.when(i % MOBA_TRIP == left)
            def _():
                bounded_blocks([past(i - left + k) for k in range(left)] + [(i, 0, True)])

    def online_tile(j, delta, far, own):
        for h in range(A_HEADS):
            pair = slice((h // 2) * PW, (h // 2 + 1) * PW)
            rows = slice(h * HEAD_DIM, (h + 1) * HEAD_DIM)
            s = _dot_nt(k_ref[pl.ds(pl.multiple_of(j * L, L), L), pair], qh[h])
            if not far:
                s = s + bias_ref[h // 2, delta, :, (h % 2) * L:(h % 2 + 1) * L]
            if own:
                s = jnp.where(causal(0, L), s, NEG_INF)
            mj = jnp.max(s, axis=0, keepdims=True)
            p = jnp.exp2(s - mj)
            lj = jnp.sum(p, axis=0, keepdims=True)
            o = _dot(vt_ref[j, rows, :], p.astype(BF16))
            if far:
                mj = mj + far_ref[h]
            m_old = m_ref[h]
            if own:
                m_new = jnp.maximum(m_old, mj)
                beta = jnp.exp2(mj - m_new)
            else:
                on = sel_ref[h, pl.ds(j, 1), :] > 0.5
                m_new = jnp.where(on, jnp.maximum(m_old, mj), m_old)
                beta = jnp.where(on, jnp.exp2(mj - m_new), 0.0)
            alpha = jnp.exp2(m_old - m_new)
            m_ref[h] = m_new
            l_ref[h] = alpha * l_ref[h] + beta * lj
            acc_ref[rows, :] = alpha * acc_ref[rows, :] + beta * o

    n_far = jnp.maximum(i - (n_near - 1), 0)

    def run(tile):
        def far_body(j, c):
            tile(j, None, True, False)
            return c

        def near_body(j, c):
            tile(j, i - j, False, False)
            return c

        lax.fori_loop(0, n_far, far_body, 0)
        lax.fori_loop(n_far, i, near_body, 0)
        tile(i, 0, False, True)

    @pl.when(flag_ref[0] == 1)
    def _():
        run_bounded()
        for h in range(A_HEADS):
            l_ref[h] = jnp.sum(l8_ref[h], axis=0, keepdims=True)

    @pl.when(flag_ref[0] != 1)
    def _():
        run(online_tile)

    o_t = jnp.concatenate([acc_ref[h * HEAD_DIM:(h + 1) * HEAD_DIM, :] / l_ref[h] for h in range(A_HEADS)],
                          axis=0)
    o_ref[...] = o_t.T.astype(BF16)


def _moba(q, k, v, bias_tiles, far, flag, batch, seq):
    n = q.shape[0]
    L = A_BLOCK
    nb = seq // L
    nbp = max(8, -(-nb // 8) * 8)
    n_near = bias_tiles.shape[1] - 1
    kern = functools.partial(_moba_kernel, nb=nb, n_near=n_near)
    return pl.pallas_call(
        kern,
        grid=(batch, nb),
        in_specs=[_smem_spec(), _smem_spec(),
                  pl.BlockSpec((L, WIDTH), lambda b, i: (b * nb + i, 0)),
                  pl.BlockSpec((seq, WIDTH), lambda b, i: (b, 0)),
                  pl.BlockSpec((seq, WIDTH), lambda b, i: (b, 0)),
                  _const_spec(bias_tiles.shape)],
        out_specs=pl.BlockSpec((L, WIDTH), lambda b, i: (b * nb + i, 0)),
        out_shape=jax.ShapeDtypeStruct((n, WIDTH), BF16),
        scratch_shapes=[pltpu.VMEM((nbp, WIDTH), F32),
                        pltpu.VMEM((nb, WIDTH, L), BF16),
                        pltpu.VMEM((A_HEADS, nbp, L), F32),
                        pltpu.VMEM((A_HEADS, 1, L), F32),
                        pltpu.VMEM((A_HEADS, 1, L), F32),
                        pltpu.VMEM((A_HEADS, 8, L), F32),
                        pltpu.VMEM((WIDTH, L), F32),
                        pltpu.VMEM((QK_AHEAD + 1, KEY_CHUNK, 2 * L), F32)],
        compiler_params=_params(("arbitrary", "arbitrary")),
        name="moba",
    )(flag, far, q, k, v, bias_tiles)


def _swa_kernel(flag_ref, sink_ref, q_ref, kp_ref, kc_ref, vp_ref, vc_ref, bias_ref, o_ref, s_ref):
    g = pl.program_id(0)
    chunk = pl.program_id(2)
    W = C_WINDOW
    G = C_HEADS // C_KV_HEADS
    nsub = q_ref.shape[0] // W
    gw = q_ref.shape[1]
    kall = jnp.concatenate([kp_ref[...], kc_ref[...]], axis=0)
    vall = jnp.concatenate([vp_ref[...], vc_ref[...]], axis=0)
    lane = lax.broadcasted_iota(jnp.int32, (W, gw), 1)
    qi = lax.broadcasted_iota(jnp.int32, (W, 2 * W), 0)
    kj = lax.broadcasted_iota(jnp.int32, (W, 2 * W), 1)
    dist = W + qi - kj
    band = (dist >= 0) & (dist < W)

    def block_mask(r):
        return band & ((chunk > 0) | (kj >= W)) if r == 0 else band

    def head_query(r, hh):
        q = q_ref[r * W:(r + 1) * W, :]
        return jnp.where(lane // HEAD_DIM == hh, q, jnp.zeros_like(q))

    def bounded_heads():
        units = [(r, hh) for r in range(nsub) for hh in range(G)]
        kall_t = kall.astype(F32).T.astype(BF16)

        def qk(u):
            r, hh = units[u]
            s_ref[u % (QK_AHEAD + 1)] = _dot(head_query(r, hh), kall_t[:, r * W:(r + 2) * W])

        for u in range(QK_AHEAD):
            qk(u)
        out = None
        for u, (r, hh) in enumerate(units):
            if u + QK_AHEAD < len(units):
                qk(u + QK_AHEAD)
            if hh == 0:
                out = jnp.zeros((W, gw), F32)
            s = s_ref[u % (QK_AHEAD + 1)] + bias_ref[hh]
            pb = jnp.where(block_mask(r), jnp.exp2(s), 0.0).astype(BF16)
            sink = jnp.exp2(jnp.full((1, 128), sink_ref[g * G + hh], F32))
            inv = 1.0 / (_dot(pb, jnp.ones((2 * W, 128), BF16)) + sink)
            o = _dot(pb, vall[r * W:(r + 2) * W]) * jnp.concatenate([inv] * (gw // 128), axis=-1)
            out = jnp.where(lane // HEAD_DIM == hh, o, out)
            if hh == G - 1:
                o_ref[r * W:(r + 1) * W, :] = out.astype(BF16)

    def online_heads():
        for r in range(nsub):
            kcat = kall[r * W:(r + 2) * W]
            vcat = vall[r * W:(r + 2) * W]
            out = jnp.zeros((W, gw), F32)
            for hh in range(G):
                s = jnp.where(block_mask(r), _dot_nt(head_query(r, hh), kcat) + bias_ref[hh], NEG_INF)
                sink = sink_ref[g * G + hh]
                m = jnp.maximum(jnp.max(s, axis=-1, keepdims=True), sink)
                p = jnp.exp2(s - m)
                den = jnp.sum(p, axis=-1, keepdims=True) + jnp.exp2(sink - m)
                o = _dot(p.astype(BF16), vcat) / den
                out = jnp.where(lane // HEAD_DIM == hh, o, out)
            o_ref[r * W:(r + 1) * W, :] = out.astype(BF16)

    @pl.when(flag_ref[0] == 1)
    def _():
        bounded_heads()

    @pl.when(flag_ref[0] != 1)
    def _():
        online_heads()


def _swa(q, k_t, v_t, bias_tiles, sinks, flag, batch, seq):
    n = q.shape[0]
    W = C_WINDOW
    G = C_HEADS // C_KV_HEADS
    gw = G * HEAD_DIM
    nsub = min(SWA_BLOCKS, seq // W)
    nchunk = seq // (W * nsub)
    cur = lambda g, b, j: (b * nchunk + j, g)
    prev = lambda g, b, j: (jnp.maximum((b * nchunk + j) * nsub - 1, 0), g)
    return pl.pallas_call(
        _swa_kernel,
        grid=(C_KV_HEADS, batch, nchunk),
        in_specs=[_smem_spec(), _smem_spec(),
                  pl.BlockSpec((nsub * W, gw), cur),
                  pl.BlockSpec((W, gw), prev), pl.BlockSpec((nsub * W, gw), cur),
                  pl.BlockSpec((W, gw), prev), pl.BlockSpec((nsub * W, gw), cur),
                  pl.BlockSpec((G, W, 2 * W), lambda g, b, j: (g, 0, 0))],
        out_specs=pl.BlockSpec((nsub * W, gw), cur),
        out_shape=jax.ShapeDtypeStruct((n, WIDTH), BF16),
        scratch_shapes=[pltpu.VMEM((QK_AHEAD + 1, W, 2 * W), F32)],
        compiler_params=_params(("arbitrary", "arbitrary", "arbitrary")),
        name="swa",
    )(flag, sinks, q, k_t, k_t, v_t, v_t, bias_tiles)


def _seg_norm(t, gain, seg):
    outs = []
    for c in range(0, t.shape[1], seg):
        tc = t[:, c:c + seg]
        outs.append(tc * lax.rsqrt(jnp.mean(tc * tc, axis=-1, keepdims=True) + EPS))
    return jnp.concatenate(outs, axis=-1) * gain


def _mem_kv_kernel(mem_ref, g_ref, wk_ref, wv_ref, kg_ref, k_ref, v_ref):
    mb = _rms(mem_ref[...], g_ref[...]).astype(BF16)
    k_ref[...] = _seg_norm(_dot(mb, wk_ref[...]), kg_ref[...], X_HEAD_DIM).astype(BF16)
    v_ref[...] = _dot(mb, wv_ref[...]).astype(BF16)


def _mem_kv(mem, g, w_xk, w_xv, k_gain):
    n, d = mem.shape
    tm = min(ROW_TILE, n)
    row = lambda width: pl.BlockSpec((tm, width), lambda i: (i, 0))
    return pl.pallas_call(
        _mem_kv_kernel,
        grid=(n // tm,),
        in_specs=[row(d), _const_spec(g.shape), _const_spec(w_xk.shape), _const_spec(w_xv.shape),
                  _const_spec(k_gain.shape)],
        out_specs=[row(WIDTH), row(WIDTH)],
        out_shape=[jax.ShapeDtypeStruct((n, WIDTH), BF16)] * 2,
        compiler_params=_params(("parallel",)),
        name="mem_kv",
    )(mem, g, w_xk, w_xv, k_gain)


def _merge_cross_kernel(x_ref, oa_ref, ub_ref, halo_ref, oc_ref, gt_ref, pw_ref, ps_ref, wb_ref, wo_ref,
                        gx_ref, wq_ref, qg_ref, km_ref, vm_ref, wxo_ref, o_ref, *, seq):
    tm = x_ref.shape[0]
    d = x_ref.shape[1]
    t0 = (pl.program_id(0) * tm) % seq
    H = POOL_HALO

    halo = jnp.where(t0 > 0, halo_ref[...], 0.0)
    pos = t0 + lax.broadcasted_iota(jnp.int32, (tm, B_GROUP_DIM), 0)
    mixed = []
    for gi, win in enumerate(B_WINDOWS):
        cols = slice(gi * B_GROUP_DIM, (gi + 1) * B_GROUP_DIM)
        cur = ub_ref[:, cols]
        acc = jnp.concatenate([halo[:, cols], cur], axis=0)
        step = 1
        while step < win:
            acc = acc + pltpu.roll(acc, step, axis=0)
            step *= 2
        cnt = jnp.minimum(pos + 1, win).astype(F32)
        pooled = acc[H:, :] / cnt - cur
        mixed.append(_dot(pooled.astype(BF16), pw_ref[gi]))
    ob = jnp.concatenate(mixed, axis=-1) * ps_ref[...]

    merged = gt_ref[:, 0:d].astype(F32) * _dot(oa_ref[...], wb_ref[0])
    merged = merged + gt_ref[:, d:2 * d].astype(F32) * _dot(ob.astype(BF16), wb_ref[1])
    merged = merged + gt_ref[:, 2 * d:3 * d].astype(F32) * _dot(oc_ref[...], wb_ref[2])
    x1 = x_ref[...] + _dot(merged.astype(BF16), wo_ref[...])

    xb = _rms(x1, gx_ref[...]).astype(BF16)
    qn = _seg_norm(_dot(xb, wq_ref[...]), qg_ref[...], X_HEAD_DIM).astype(BF16)
    heads = []
    for h in range(X_HEADS):
        cols = slice(h * X_HEAD_DIM, (h + 1) * X_HEAD_DIM)
        s = _dot_nt(qn[:, cols], km_ref[:, cols])
        p = jnp.exp(s - jnp.max(s, axis=-1, keepdims=True))
        inv = 1.0 / jnp.sum(p, axis=-1, keepdims=True)
        heads.append(_dot(p.astype(BF16), vm_ref[:, cols]) * inv)
    o = jnp.concatenate(heads, axis=-1).astype(BF16)
    o_ref[...] = x1 + _dot(o, wxo_ref[...])


def _merge_cross(x, oa, ub, oc, gates, pool_w, pool_scale, w_branch, w_out,
                 g_cross, w_xq, q_gain, k_mem, v_mem, w_xo, seq, mem_len):
    n, d = x.shape
    tm = min(ROW_TILE, seq)
    row = lambda width: pl.BlockSpec((tm, width), lambda i: (i, 0))
    halo = pl.BlockSpec((POOL_HALO, WIDTH), lambda i: (jnp.maximum(i * (tm // POOL_HALO) - 1, 0), 0))
    mem = pl.BlockSpec((mem_len, WIDTH), lambda i: ((i * tm) // seq, 0))
    kern = functools.partial(_merge_cross_kernel, seq=seq)
    return pl.pallas_call(
        kern,
        grid=(n // tm,),
        in_specs=[row(d), row(WIDTH), row(WIDTH), halo, row(WIDTH), row(gates.shape[1]),
                  _const_spec(pool_w.shape), _const_spec(pool_scale.shape), _const_spec(w_branch.shape),
                  _const_spec(w_out.shape), _const_spec(g_cross.shape), _const_spec(w_xq.shape),
                  _const_spec(q_gain.shape), mem, mem, _const_spec(w_xo.shape)],
        out_specs=row(d),
        out_shape=jax.ShapeDtypeStruct((n, d), F32),
        compiler_params=_params(("parallel",)),
        name="merge_cross",
    )(x, oa, ub, ub, oc, gates, pool_w, pool_scale, w_branch, w_out,
      g_cross, w_xq, q_gain, k_mem, v_mem, w_xo)


def _swiglu_chunks(xb, w1_ref, w3_ref, w2_ref, acc):
    for c in range(0, w1_ref.shape[1], FF_CHUNK):
        h1 = _dot(xb, w1_ref[:, c:c + FF_CHUNK])
        h3 = _dot(xb, w3_ref[:, c:c + FF_CHUNK])
        acc = acc + _dot((jax.nn.silu(h1) * h3).astype(BF16), w2_ref[c:c + FF_CHUNK, :])
    return acc


def _ffn_kernel(x_ref, g_ref, w1_ref, w3_ref, w2_ref, o_ref):
    x = x_ref[...]
    xb = _rms(x, g_ref[...]).astype(BF16)
    o_ref[...] = _swiglu_chunks(xb, w1_ref, w3_ref, w2_ref, x)


def _ffn(x, g, w1, w3, w2):
    n, d = x.shape
    tm = min(ROW_TILE, n)
    row = pl.BlockSpec((tm, d), lambda i: (i, 0))
    return pl.pallas_call(
        _ffn_kernel,
        grid=(n // tm,),
        in_specs=[row, _const_spec(g.shape), _const_spec(w1.shape), _const_spec(w3.shape),
                  _const_spec(w2.shape)],
        out_specs=row,
        out_shape=jax.ShapeDtypeStruct((n, d), F32),
        compiler_params=_params(("parallel",)),
        name="ffn_dense",
    )(x, g, w1, w3, w2)


def _store_token_tiles(ref, x):
    rows = x.shape[0]
    for k in range(x.shape[1] // 128):
        ref[pl.ds(k, rows, stride=SUBLANES), :] = x[:, k * 128:(k + 1) * 128]


def _load_token_tiles(ref, rows, lead=None):
    idx = (lambda k: (pl.ds(k, rows, stride=SUBLANES), slice(None))) if lead is None else \
        (lambda k: (lead, pl.ds(k, rows, stride=SUBLANES), slice(None)))
    return jnp.concatenate([ref[idx(k)] for k in range(SUBLANES)], axis=-1)


ROUTE_EXPERT, ROUTE_GATE, ROUTE_RANK = 0, TOP_K, 2 * TOP_K


def _router_kernel(x_ref, g_ref, r_ref, rt_ref, cnt_ref, carry_ref):
    i = pl.program_id(0)
    tm = x_ref.shape[0]
    xn = _rms(x_ref[...], g_ref[...])
    r = r_ref[...]
    x_hi = xn.astype(BF16)
    x_lo = (xn - x_hi.astype(F32)).astype(BF16)
    r_hi = r.astype(BF16)
    r_lo = (r - r_hi.astype(F32)).astype(BF16)
    logits = _dot(x_hi, r_hi) + (_dot(x_lo, r_hi) + _dot(x_hi, r_lo))
    lane = lax.broadcasted_iota(jnp.int32, logits.shape, 1)
    lg = jnp.where(lane < N_EXPERTS, logits, -jnp.inf)
    m1 = jnp.max(lg, axis=-1, keepdims=True)
    e1 = jnp.min(jnp.where(lg == m1, lane, lg.shape[1]), axis=-1, keepdims=True)
    lg = jnp.where(lane == e1, -jnp.inf, lg)
    m2 = jnp.max(lg, axis=-1, keepdims=True)
    e2 = jnp.min(jnp.where(lg == m2, lane, lg.shape[1]), axis=-1, keepdims=True)
    ex = jnp.exp(m2 - m1)
    gate1 = 1.0 / (1.0 + ex)
    gate2 = ex * gate1

    @pl.when(i == 0)
    def _():
        carry_ref[...] = jnp.zeros(carry_ref.shape, F32)

    hot1 = (lane == e1).astype(F32)
    hot2 = (lane == e2).astype(F32)
    hot = hot1 + hot2
    before = lax.broadcasted_iota(jnp.int32, (tm, tm), 1) < lax.broadcasted_iota(jnp.int32, (tm, tm), 0)
    seen = _dot(before.astype(BF16), hot.astype(BF16)) + carry_ref[...]
    rank1 = jnp.sum(seen * hot1, axis=-1, keepdims=True)
    rank2 = jnp.sum(seen * hot2, axis=-1, keepdims=True)
    carry_ref[...] += jnp.sum(hot, axis=0, keepdims=True)
    cnt_ref[...] = carry_ref[...]
    cols = [e1.astype(F32), e2.astype(F32), gate1, gate2, rank1, rank2]
    table = jnp.zeros(logits.shape, F32)
    for c, v in enumerate(cols):
        table = jnp.where(lane == c, v, table)
    rt_ref[...] = table


def _router(x, g, router_padded):
    n, d = x.shape
    tm = min(ROW_TILE, n)
    width = router_padded.shape[1]
    row = lambda w: pl.BlockSpec((tm, w), lambda i: (i, 0))
    return pl.pallas_call(
        _router_kernel,
        grid=(n // tm,),
        in_specs=[row(d), _const_spec(g.shape), _const_spec(router_padded.shape)],
        out_specs=[row(width), pl.BlockSpec((1, width), lambda i: (0, 0))],
        out_shape=[jax.ShapeDtypeStruct((n, width), F32), jax.ShapeDtypeStruct((1, width), F32)],
        scratch_shapes=[pltpu.VMEM((1, width), F32)],
        compiler_params=_params(("arbitrary",)),
        name="router",
    )(x, g, router_padded)


def _token_tile(ref, first_row):
    return ref.at[pl.ds(pl.multiple_of(first_row, SUBLANES), SUBLANES), :]


DMA_UNROLL = 8
ZERO_TOKENS = 8


def _dispatch_kernel(pad_ref, dst_ref, x_ref, g_ref, xs_hbm, xt_ref, zero_ref, sem, zsem):
    i = pl.program_id(0)
    n_steps = pl.num_programs(0)
    tm = x_ref.shape[0]
    T = SUBLANES
    slot = i % 2

    def wait_step(s):
        for _ in range(TOP_K):
            pltpu.make_async_copy(xt_ref.at[s], xs_hbm.at[pl.ds(0, tm * T), :], sem.at[s]).wait()

    @pl.when(i == 0)
    def _():
        zero_ref[...] = jnp.zeros(zero_ref.shape, F32)
        for e in range(N_EXPERTS + 1):
            first = pad_ref[0, e]
            tokens = 1 if e < N_EXPERTS else ZERO_TOKENS
            src = zero_ref.at[pl.ds(0, tokens * T), :]

            def dst(r):
                return xs_hbm.at[pl.ds(pl.multiple_of((first + r * tokens) * T, T), tokens * T), :]

            def fill(r, carry):
                pltpu.make_async_copy(src, dst(r), zsem).start()
                return carry

            def drain(r, carry):
                pltpu.make_async_copy(src, dst(0), zsem).wait()
                return carry

            lax.fori_loop(0, pad_ref[1, e] // tokens, fill, 0)
            lax.fori_loop(0, pad_ref[1, e] // tokens, drain, 0)

    _store_token_tiles(xt_ref.at[slot], _rms(x_ref[...], g_ref[...]))

    for choice in range(TOP_K):
        def body(gi, carry):
            for k in range(DMA_UNROLL):
                tok = gi * DMA_UNROLL + k
                pltpu.make_async_copy(_token_tile(xt_ref.at[slot], tok * T),
                                      _token_tile(xs_hbm, dst_ref[0, 0, choice * tm + tok]),
                                      sem.at[slot]).start(priority=k % 2)
            return carry

        lax.fori_loop(0, tm // DMA_UNROLL, body, 0)

    @pl.when(i >= 1)
    def _():
        wait_step(1 - slot)

    @pl.when(i == n_steps - 1)
    def _():
        wait_step(slot)


def _dispatch(x, g, dest_blocks, pad_info, p):
    n, d = x.shape
    tm = dest_blocks.shape[2] // TOP_K
    T = SUBLANES
    grid_spec = pltpu.PrefetchScalarGridSpec(
        num_scalar_prefetch=1,
        grid=(n // tm,),
        in_specs=[pl.BlockSpec((1, 1, TOP_K * tm), lambda i, pad: (i, 0, 0), memory_space=pltpu.SMEM),
                  pl.BlockSpec((tm, d), lambda i, pad: (i, 0)),
                  pl.BlockSpec(g.shape, lambda i, pad: (0, 0))],
        out_specs=pl.BlockSpec(memory_space=pl.ANY),
        scratch_shapes=[pltpu.VMEM((2, tm * T, d // T), F32), pltpu.VMEM((ZERO_TOKENS * T, d // T), F32),
                        pltpu.SemaphoreType.DMA((2,)), pltpu.SemaphoreType.DMA],
    )
    return pl.pallas_call(
        _dispatch_kernel,
        grid_spec=grid_spec,
        out_shape=jax.ShapeDtypeStruct((p * T, d // T), F32),
        compiler_params=_params(("arbitrary",)),
        name="dispatch",
    )(pad_info, dest_blocks, x, g)


def _expert_kernel(be_ref, nu_ref, x_ref, w1_ref, w3_ref, w2_ref, o_ref, xb_ref, y_ref):
    b = pl.program_id(0)
    c = pl.program_id(1)
    tb = xb_ref.shape[0]

    @pl.when(b < nu_ref[0])
    def _():
        @pl.when(c == 0)
        def _():
            xb_ref[...] = _load_token_tiles(x_ref, tb).astype(BF16)
            y_ref[...] = jnp.zeros(y_ref.shape, F32)

        xb = xb_ref[...]
        h1 = _dot(xb, w1_ref[...].astype(BF16))
        h3 = _dot(xb, w3_ref[...].astype(BF16))
        y_ref[...] += _dot((jax.nn.silu(h1) * h3).astype(BF16), w2_ref[...].astype(BF16))

        @pl.when(c == pl.num_programs(1) - 1)
        def _():
            _store_token_tiles(o_ref, y_ref[...])

    @pl.when((b >= nu_ref[0]) & (c == pl.num_programs(1) - 1))
    def _():
        o_ref[...] = jnp.zeros(o_ref.shape, F32)


def _experts(xs, block_e, n_used, w1, w3, w2, tb):
    T = SUBLANES
    d = w1.shape[1]
    nc = w1.shape[2] // FF_CHUNK
    nblk = xs.shape[0] // (tb * T)
    blk = lambda b, nu: jnp.minimum(b, nu[0] - 1)
    chunk = lambda b, c, nu: jnp.where(b < nu[0], c, nc - 1)
    grid_spec = pltpu.PrefetchScalarGridSpec(
        num_scalar_prefetch=2,
        grid=(nblk, nc),
        in_specs=[pl.BlockSpec((tb * T, d // T), lambda b, c, be, nu: (blk(b, nu), 0)),
                  pl.BlockSpec((None, d, FF_CHUNK), lambda b, c, be, nu: (be[b], 0, chunk(b, c, nu))),
                  pl.BlockSpec((None, d, FF_CHUNK), lambda b, c, be, nu: (be[b], 0, chunk(b, c, nu))),
                  pl.BlockSpec((None, FF_CHUNK, d), lambda b, c, be, nu: (be[b], chunk(b, c, nu), 0))],
        out_specs=pl.BlockSpec((tb * T, d // T), lambda b, c, be, nu: (b, 0)),
        scratch_shapes=[pltpu.VMEM((tb, d), BF16), pltpu.VMEM((tb, d), F32)],
    )
    return pl.pallas_call(
        _expert_kernel,
        grid_spec=grid_spec,
        out_shape=jax.ShapeDtypeStruct(xs.shape, F32),
        compiler_params=_params(("arbitrary", "arbitrary")),
        name="experts",
    )(block_e, n_used, xs, w1, w3, w2)


def _combine_kernel(dst_ref, dstn_ref, x_ref, g_ref, ys_hbm, o_ref, yg_ref, sem):
    i = pl.program_id(0)
    n_steps = pl.num_programs(0)
    tm = x_ref.shape[0]
    T = SUBLANES
    slot = i % 2

    def gather(idx_ref, s):
        def body(gi, carry):
            for k in range(DMA_UNROLL):
                r = gi * DMA_UNROLL + k
                pltpu.make_async_copy(_token_tile(ys_hbm, idx_ref[0, 0, r]), _token_tile(yg_ref.at[s], r * T),
                                      sem.at[s]).start(priority=k % 2)
            return carry
        lax.fori_loop(0, TOP_K * tm // DMA_UNROLL, body, 0)

    @pl.when(i == 0)
    def _():
        gather(dst_ref, 0)

    @pl.when(i + 1 < n_steps)
    def _():
        gather(dstn_ref, 1 - slot)

    pltpu.make_async_copy(ys_hbm.at[pl.ds(0, TOP_K * tm * T), :], yg_ref.at[slot], sem.at[slot]).wait()
    g = g_ref[...]
    y0 = _load_token_tiles(yg_ref, tm, lead=slot)
    y1 = jnp.concatenate([yg_ref[slot, pl.ds(tm * T + k, tm, stride=T), :] for k in range(T)], axis=-1)
    o_ref[...] = x_ref[...] + g[:, ROUTE_GATE:ROUTE_GATE + 1] * y0 + g[:, ROUTE_GATE + 1:ROUTE_GATE + 2] * y1


def _combine(x, ys, dest_blocks, gate):
    n, d = x.shape
    tm = dest_blocks.shape[2] // TOP_K
    T = SUBLANES
    n_steps = n // tm
    row = lambda width: pl.BlockSpec((tm, width), lambda i: (i, 0))
    idx = lambda imap: pl.BlockSpec((1, 1, TOP_K * tm), imap, memory_space=pltpu.SMEM)
    return pl.pallas_call(
        _combine_kernel,
        grid=(n_steps,),
        in_specs=[idx(lambda i: (i, 0, 0)), idx(lambda i: (jnp.minimum(i + 1, n_steps - 1), 0, 0)),
                  row(d), row(gate.shape[1]), pl.BlockSpec(memory_space=pl.ANY)],
        out_specs=row(d),
        out_shape=jax.ShapeDtypeStruct((n, d), F32),
        scratch_shapes=[pltpu.VMEM((2, TOP_K * tm * T, d // T), F32), pltpu.SemaphoreType.DMA((2,))],
        compiler_params=_params(("arbitrary",)),
        name="combine",
    )(dest_blocks, dest_blocks, x, gate, ys)


def _moe(x, g, router, w1, w3, w2):
    n, d = x.shape
    a = n * TOP_K
    tb = min(MOE_TILE, a)
    router_padded = jnp.pad(router, ((0, 0), (0, 128 - N_EXPERTS)))
    table, cnt = _router(x, g, router_padded)
    top_e = table[:, ROUTE_EXPERT:ROUTE_EXPERT + TOP_K].astype(jnp.int32)
    rank = table[:, ROUTE_RANK:ROUTE_RANK + TOP_K].astype(jnp.int32)
    counts = cnt[0, :N_EXPERTS].astype(jnp.int32)
    padded = (counts + tb - 1) // tb * tb
    pend = jnp.cumsum(padded)
    pstart = pend - padded
    onehot = (top_e[:, :, None] == jnp.arange(N_EXPERTS)[None, None, :]).astype(jnp.int32)
    dest = jnp.sum(onehot * pstart[None, None, :], axis=-1) + rank
    n_blocks = a // tb + N_EXPERTS
    p = n_blocks * tb
    block_e = jnp.minimum(jnp.searchsorted(pend, jnp.arange(n_blocks) * tb, side='right'),
                          N_EXPERTS - 1).astype(jnp.int32)
    n_used = (pend[-1] // tb).astype(jnp.int32).reshape(1)
    tm = min(ROW_TILE, n)
    dest_blocks = (dest * SUBLANES).reshape(n // tm, tm, TOP_K).transpose(0, 2, 1).reshape(n // tm, 1, TOP_K * tm)
    pad_info = jnp.stack([jnp.append(pstart + counts, pend[-1]),
                          jnp.append(padded - counts, p - pend[-1])]).astype(jnp.int32)
    xs = _dispatch(x, g, dest_blocks, pad_info, p)
    ys = _experts(xs, block_e, n_used, w1, w3, w2, tb)
    return _combine(x, ys, dest_blocks, table)


def _tile_gain(gain, width, scale=1.0):
    return jnp.tile(gain, width // gain.shape[0]) * scale


def kernel(x, mem, rel_bias, norm_mix, w_in, b_gate, a_q_gain, a_k_gain, pool_w, pool_scale,
           c_q_gain, c_k_gain, c_sinks, w_branch, w_out, norm_cross, norm_mem, w_xq, w_xk, w_xv,
           x_q_gain, x_k_gain, w_xo, norm_ffn, ffn_w1, ffn_w3, ffn_w2, router, moe_w1, moe_w3, moe_w2):
    batch, seq, d = x.shape
    mem_len = mem.shape[1]
    depth = norm_mix.shape[0]
    xs = x.reshape(batch * seq, d)
    mems = mem.reshape(batch * mem_len, d)

    tab_a = rel_bias[:, :A_HEADS].T
    tab_c = rel_bias[:, A_HEADS:].T
    nb = seq // A_BLOCK
    n_near = min(nb, (_saturation_distance() + 2 * A_BLOCK - 2) // A_BLOCK)
    far_a = tab_a[:, REL_BUCKETS - 1] * LOG2E
    bias_a = _moba_bias_tiles(_bias_by_distance(tab_a, n_near * A_BLOCK), n_near) * LOG2E
    bias_a = jnp.concatenate([bias_a, jnp.broadcast_to(far_a[:, None, None, None],
                                                       (A_HEADS, 1, A_BLOCK, A_BLOCK))], axis=1)
    bias_a = jnp.concatenate([bias_a[0::2], bias_a[1::2]], axis=-1)
    bias_c = _swa_bias_tiles(_bias_by_distance(tab_c, C_WINDOW)) * LOG2E
    seg = np.arange(256) // HEAD_DIM
    bd = jnp.asarray(seg[:, None] == seg[None, :], BF16)

    row = lambda v: v.reshape(1, -1)
    for l in range(depth):
        scale = HEAD_DIM ** -0.5 * LOG2E
        gains = jnp.stack([_tile_gain(a_q_gain[l], WIDTH, scale), _tile_gain(a_k_gain[l], WIDTH),
                           _tile_gain(c_q_gain[l], WIDTH, scale), _tile_gain(c_k_gain[l], WIDTH)])
        flag_a = _bounded_flag(a_q_gain[l], a_k_gain[l], HEAD_DIM, jnp.max(jnp.abs(tab_a)))
        flag_c = _bounded_flag(c_q_gain[l], c_k_gain[l], HEAD_DIM,
                               jnp.maximum(jnp.max(jnp.abs(tab_c)), jnp.max(jnp.abs(c_sinks[l]))))
        qa, ka, va, ub, qc, kc, vc, gates = _in_proj(
            xs, row(norm_mix[l]), w_in[l].astype(BF16), row(b_gate[l]), gains, bd)
        oa = _moba(qa, ka, va, bias_a, far_a, flag_a, batch, seq)
        oc = _swa(qc, kc, vc, bias_c, c_sinks[l] * LOG2E, flag_c, batch, seq)
        k_mem, v_mem = _mem_kv(mems, row(norm_mem[l]), w_xk[l].astype(BF16), w_xv[l].astype(BF16),
                               row(_tile_gain(x_k_gain[l], WIDTH)))
        xs = _merge_cross(xs, oa, ub, oc, gates, pool_w[l].astype(BF16), row(pool_scale[l]),
                          w_branch[l].astype(BF16), w_out[l].astype(BF16), row(norm_cross[l]),
                          w_xq[l].astype(BF16), row(_tile_gain(x_q_gain[l], WIDTH, X_HEAD_DIM ** -0.5)),
                          k_mem, v_mem, w_xo[l].astype(BF16), seq, mem_len)
        i = l // 2
        if l % 2 == 0:
            xs = _ffn(xs, row(norm_ffn[l]), ffn_w1[i].astype(BF16), ffn_w3[i].astype(BF16),
                      ffn_w2[i].astype(BF16))
        else:
            xs = _moe(xs, row(norm_ffn[l]), router[i], moe_w1[i], moe_w3[i], moe_w2[i])
    return xs.reshape(batch, seq, d)
```

```python
import functools
import math

import jax
import jax.numpy as jnp
import numpy as np
from jax import lax
from jax.experimental import pallas as pl
from jax.experimental.pallas import tpu as pltpu

F32 = jnp.float32
BF16 = jnp.bfloat16

HEAD_DIM = 64
A_HEADS = 8
A_BLOCK = 256
A_TOPK = 3
B_GROUPS = 4
B_GROUP_DIM = 128
B_WINDOWS = (2, 4, 8, 16)
C_HEADS = 8
C_KV_HEADS = 2
C_WINDOW = 128
REL_BUCKETS = 32
REL_MAX_DIST = 1024
X_HEADS = 4
X_HEAD_DIM = 128
N_EXPERTS = 8
TOP_K = 2
EPS = 1e-6
NEG_INF = -1e30
LOG2E = math.log2(math.e)

WIDTH = 512
POOL_HALO = 16
ROW_TILE = 512
MOE_TILE = 1024
FF_CHUNK = 512
SUBLANES = 8
DMA_UNROLL = 8
ZERO_TOKENS = 8
KEY_CHUNK = 128
MOBA_TRIP = 3
QK_AHEAD = 3
SWA_BLOCKS = 4
VMEM_LIMIT = 56 * 1024 * 1024
EXP2_SAFE = 100.0


def _dot(a, b):
    return jnp.dot(a, b, preferred_element_type=F32)


def _dot_nt(a, b):
    return lax.dot_general(a, b, (((1,), (1,)), ((), ())), preferred_element_type=F32)


def _rms(x, g):
    ms = jnp.mean(x * x, axis=-1, keepdims=True)
    return x * lax.rsqrt(ms + EPS) * g


def _const_spec(shape):
    zeros = (0,) * len(shape)
    return pl.BlockSpec(shape, lambda *_: zeros, pipeline_mode=pl.Buffered(1))


def _smem_spec():
    return pl.BlockSpec(memory_space=pltpu.SMEM)


def _params(sem):
    return pltpu.CompilerParams(dimension_semantics=sem, vmem_limit_bytes=VMEM_LIMIT)


def _rel_bucket(dist):
    n = jnp.maximum(dist, 0)
    max_exact = REL_BUCKETS // 2
    nf = jnp.maximum(n, 1).astype(jnp.float32)
    large = max_exact + (jnp.log(nf / max_exact) / math.log(REL_MAX_DIST / max_exact)
                         * (REL_BUCKETS - max_exact)).astype(jnp.int32)
    large = jnp.minimum(large, REL_BUCKETS - 1)
    return jnp.where(n < max_exact, n, large)


def _saturation_distance():
    ratio = REL_MAX_DIST / (REL_BUCKETS // 2)
    return int(math.ceil((REL_BUCKETS // 2) * ratio ** ((REL_BUCKETS - 1 - REL_BUCKETS // 2 + 0.5)
                                                       / (REL_BUCKETS - REL_BUCKETS // 2))))


def _bias_by_distance(tab, n_dist):
    onehot = (_rel_bucket(jnp.arange(n_dist))[:, None] == jnp.arange(REL_BUCKETS)[None, :]).astype(F32)
    return jnp.einsum('hb,db->hd', tab, onehot, precision=lax.Precision.HIGHEST)


def _skew(g, rows):
    n = g.shape[-1]
    lead = g.shape[:-1]
    tiled = jnp.broadcast_to(g[..., None, :], lead + (rows, n)).reshape(lead + (rows * n,))
    return tiled[..., :rows * (n - 1)].reshape(lead + (rows, n - 1))


def _moba_bias_tiles(bvec, n_near):
    L = A_BLOCK
    rows = []
    for delta in range(n_near):
        lo = delta * L - (L - 1)
        seg = bvec[:, max(lo, 0):delta * L + L]
        if lo < 0:
            seg = jnp.concatenate([jnp.zeros((bvec.shape[0], -lo), F32), seg], axis=1)
        rows.append(jnp.pad(seg, ((0, 0), (0, 1))))
    g = jnp.stack(rows, axis=1)
    return _skew(g, L)[..., L - 1:]


def _swa_bias_tiles(bvec):
    W = C_WINDOW
    u = np.arange(3 * W)
    g = bvec[:, np.clip(2 * W - 1 - u, 0, W - 1)]
    return _skew(g, W)[..., W - 1:3 * W - 1]


def _bounded_flag(q_gain, k_gain, head_dim, extra):
    bound = head_dim ** 0.5 * jnp.max(jnp.abs(q_gain)) * jnp.max(jnp.abs(k_gain)) * 1.02 + extra
    return (bound * LOG2E < EXP2_SAFE).astype(jnp.int32).reshape(1)


def _in_proj_kernel(x_ref, g_ref, w_ref, bg_ref, gn_ref, bd_ref,
                    qa_ref, ka_ref, va_ref, ub_ref, qc_ref, kc_ref, vc_ref, gt_ref):
    xb = _rms(x_ref[...], g_ref[...]).astype(BF16)
    bd = bd_ref[...]

    def proj(c0, width):
        return _dot(xb, w_ref[:, c0:c0 + width])

    def head_norm(t, gain):
        outs = []
        for c in range(0, t.shape[1], 256):
            wd = min(256, t.shape[1] - c)
            tc = t[:, c:c + wd]
            ss = _dot((tc * tc).astype(BF16), bd[:wd, :wd])
            outs.append(tc * lax.rsqrt(ss * (1.0 / HEAD_DIM) + EPS))
        y = outs[0] if len(outs) == 1 else jnp.concatenate(outs, axis=-1)
        return y * gain

    def tile_kv_heads(t):
        lane = lax.broadcasted_iota(jnp.int32, t.shape, 1)
        r = pltpu.roll(t, HEAD_DIM, axis=1)
        h0 = jnp.where(lane < HEAD_DIM, t, r)
        h1 = jnp.where(lane < HEAD_DIM, r, t)
        return jnp.concatenate([h0, h0, h1, h1], axis=-1)

    qa_ref[...] = head_norm(proj(0, WIDTH), gn_ref[0:1, :]).astype(BF16)
    ka_ref[...] = head_norm(proj(WIDTH, WIDTH), gn_ref[1:2, :]).astype(BF16)
    va_ref[...] = proj(2 * WIDTH, WIDTH).astype(BF16)
    ub_ref[...] = proj(3 * WIDTH, WIDTH)
    qc_ref[...] = head_norm(proj(4 * WIDTH, WIDTH), gn_ref[2:3, :]).astype(BF16)
    kv = C_KV_HEADS * HEAD_DIM
    c0 = 5 * WIDTH
    kc = head_norm(proj(c0, kv), gn_ref[3:4, :kv])
    kc_ref[...] = tile_kv_heads(kc).astype(BF16)
    vc_ref[...] = tile_kv_heads(proj(c0 + kv, kv)).astype(BF16)
    c0 += 2 * kv
    for c in range(0, gt_ref.shape[1], WIDTH):
        gl = proj(c0 + c, WIDTH) + bg_ref[:, c:c + WIDTH]
        gt_ref[:, c:c + WIDTH] = jax.nn.sigmoid(gl).astype(BF16)


def _in_proj(x, g, w_in, b_gate, gains, bd):
    n, d = x.shape
    tm = min(ROW_TILE, n)
    n_gate = b_gate.shape[1]
    row = lambda width: pl.BlockSpec((tm, width), lambda i: (i, 0))
    out_shape = [jax.ShapeDtypeStruct((n, WIDTH), BF16)] * 3 + [jax.ShapeDtypeStruct((n, WIDTH), F32)] \
        + [jax.ShapeDtypeStruct((n, WIDTH), BF16)] * 3 + [jax.ShapeDtypeStruct((n, n_gate), BF16)]
    return pl.pallas_call(
        _in_proj_kernel,
        grid=(n // tm,),
        in_specs=[row(d), _const_spec(g.shape), _const_spec(w_in.shape), _const_spec(b_gate.shape),
                  _const_spec(gains.shape), _const_spec(bd.shape)],
        out_specs=[row(WIDTH)] * 7 + [row(n_gate)],
        out_shape=out_shape,
        compiler_params=_params(("parallel",)),
        name="in_proj",
    )(x, g, w_in, b_gate, gains, bd)


def _moba_kernel(flag_ref, far_ref, q_ref, k_ref, v_ref, bias_ref, o_ref,
                 kmean_ref, vt_ref, sel_ref, m_ref, l_ref, l8_ref, acc_ref, s_ref, *, nb, n_near):
    i = pl.program_id(1)
    L = A_BLOCK
    KC = KEY_CHUNK
    PW = 2 * HEAD_DIM
    nbp = kmean_ref.shape[0]

    @pl.when(i == 0)
    def _():
        kmean_ref[...] = jnp.zeros(kmean_ref.shape, F32)
        for j in range(nb):
            kj = k_ref[j * L:(j + 1) * L, :].astype(F32)
            kmean_ref[j:j + 1, :] = jnp.mean(kj, axis=0, keepdims=True)
            vt_ref[j] = v_ref[j * L:(j + 1) * L, :].astype(F32).T.astype(BF16)

    lane = lax.broadcasted_iota(jnp.int32, (L, PW), 1)
    blk = lax.broadcasted_iota(jnp.int32, (nbp, L), 0)
    past = blk < i
    qh = []
    for h in range(A_HEADS):
        pair = slice((h // 2) * PW, (h // 2 + 1) * PW)
        q = q_ref[:, pair]
        qm = jnp.where(lane // HEAD_DIM == h % 2, q, jnp.zeros_like(q))
        qh.append(qm)
        km = kmean_ref[:, pair]
        km_hi = km.astype(BF16)
        km_lo = (km - km_hi.astype(F32)).astype(BF16)
        s = jnp.where(past, _dot_nt(km_hi, qm) + _dot_nt(km_lo, qm), -jnp.inf)
        rank = jnp.zeros((nbp, L), jnp.int32)
        for jp in range(nb):
            sj = s[jp:jp + 1, :]
            ahead = (sj > s) | ((sj == s) & (jp < blk))
            rank = rank + ahead.astype(jnp.int32)
        sel_ref[h] = (past & (rank < A_TOPK)).astype(F32)
        m_ref[h] = jnp.full((1, L), NEG_INF, F32)
        l_ref[h] = jnp.zeros((1, L), F32)
        l8_ref[h] = jnp.zeros((8, L), F32)
    acc_ref[...] = jnp.zeros(acc_ref.shape, F32)
    qpair_t = [jnp.concatenate(qh[2 * hp:2 * hp + 2], axis=0).astype(F32).T.astype(BF16)
               for hp in range(A_HEADS // 2)]

    def causal(c, n):
        kk = c + lax.broadcasted_iota(jnp.int32, (n, L), 0)
        qq = lax.broadcasted_iota(jnp.int32, (n, L), 1)
        return kk <= qq

    def bounded_blocks(blocks):
        units = [(bi, hp, c) for bi in range(len(blocks)) for hp in range(A_HEADS // 2) for c in range(0, L, KC)]

        def qk(u):
            bi, hp, c = units[u]
            kc = k_ref[pl.ds(pl.multiple_of(blocks[bi][0] * L + c, KC), KC), hp * PW:(hp + 1) * PW]
            s_ref[u % (QK_AHEAD + 1)] = _dot(kc, qpair_t[hp])

        for u in range(QK_AHEAD):
            qk(u)
        o = ps = None
        for u, (bi, hp, c) in enumerate(units):
            j, tile_idx, own = blocks[bi]
            if u + QK_AHEAD < len(units):
                qk(u + QK_AHEAD)
            if c == 0:
                o = [jnp.zeros((HEAD_DIM, L), F32)] * 2
                ps = jnp.zeros((8, 2 * L), F32)
            p = jnp.exp2(s_ref[u % (QK_AHEAD + 1)] + bias_ref[hp, tile_idx, pl.ds(c, KC), :])
            if own:
                p = jnp.where(jnp.concatenate([causal(c, KC)] * 2, axis=1), p, 0.0)
            ps = ps + jnp.sum(p.reshape(KC // 8, 8, 2 * L), axis=0)
            pb = p.astype(BF16)
            o = [o[hh] + _dot(vt_ref[j, pl.ds((2 * hp + hh) * HEAD_DIM, HEAD_DIM), pl.ds(c, KC)],
                              pb[:, hh * L:(hh + 1) * L]) for hh in range(2)]
            if c + KC < L:
                continue
            for hh in range(2):
                h = 2 * hp + hh
                rows = slice(h * HEAD_DIM, (h + 1) * HEAD_DIM)
                psh = ps[:, hh * L:(hh + 1) * L]
                if own:
                    acc_ref[rows, :] += o[hh]
                    l8_ref[h] += psh
                else:
                    w = sel_ref[h, pl.ds(j, 1), :]
                    acc_ref[rows, :] += w * o[hh]
                    l8_ref[h] += w * psh

    def run_bounded():
        def past(j):
            return (j, jnp.minimum(i - j, n_near), False)

        def body(t, carry):
            bounded_blocks([past(MOBA_TRIP * t + k) for k in range(MOBA_TRIP)])
            return carry

        lax.fori_loop(0, i // MOBA_TRIP, body, 0)
        for left in range(MOBA_TRIP):
            @pl.when(i % MOBA_TRIP == left)
            def _():
                bounded_blocks([past(i - left + k) for k in range(left)] + [(i, 0, True)])

    def online_tile(j, delta, far, own):
        for h in range(A_HEADS):
            pair = slice((h // 2) * PW, (h // 2 + 1) * PW)
            rows = slice(h * HEAD_DIM, (h + 1) * HEAD_DIM)
            s = _dot_nt(k_ref[pl.ds(pl.multiple_of(j * L, L), L), pair], qh[h])
            if not far:
                s = s + bias_ref[h // 2, delta, :, (h % 2) * L:(h % 2 + 1) * L]
            if own:
                s = jnp.where(causal(0, L), s, NEG_INF)
            mj = jnp.max(s, axis=0, keepdims=True)
            p = jnp.exp2(s - mj)
            lj = jnp.sum(p, axis=0, keepdims=True)
            o = _dot(vt_ref[j, rows, :], p.astype(BF16))
            if far:
                mj = mj + far_ref[h]
            m_old = m_ref[h]
            if own:
                m_new = jnp.maximum(m_old, mj)
                beta = jnp.exp2(mj - m_new)
            else:
                on = sel_ref[h, pl.ds(j, 1), :] > 0.5
                m_new = jnp.where(on, jnp.maximum(m_old, mj), m_old)
                beta = jnp.where(on, jnp.exp2(mj - m_new), 0.0)
            alpha = jnp.exp2(m_old - m_new)
            m_ref[h] = m_new
            l_ref[h] = alpha * l_ref[h] + beta * lj
            acc_ref[rows, :] = alpha * acc_ref[rows, :] + beta * o

    n_far = jnp.maximum(i - (n_near - 1), 0)

    def run(tile):
        def far_body(j, c):
            tile(j, None, True, False)
            return c

        def near_body(j, c):
            tile(j, i - j, False, False)
            return c

        lax.fori_loop(0, n_far, far_body, 0)
        lax.fori_loop(n_far, i, near_body, 0)
        tile(i, 0, False, True)

    @pl.when(flag_ref[0] == 1)
    def _():
        run_bounded()
        for h in range(A_HEADS):
            l_ref[h] = jnp.sum(l8_ref[h], axis=0, keepdims=True)

    @pl.when(flag_ref[0] != 1)
    def _():
        run(online_tile)

    o_t = jnp.concatenate([acc_ref[h * HEAD_DIM:(h + 1) * HEAD_DIM, :] / l_ref[h] for h in range(A_HEADS)],
                          axis=0)
    o_ref[...] = o_t.T.astype(BF16)


def _moba(q, k, v, bias_tiles, far, flag, batch, seq):
    n = q.shape[0]
    L = A_BLOCK
    nb = seq // L
    nbp = max(8, -(-nb // 8) * 8)
    n_near = bias_tiles.shape[1] - 1
    kern = functools.partial(_moba_kernel, nb=nb, n_near=n_near)
    return pl.pallas_call(
        kern,
        grid=(batch, nb),
        in_specs=[_smem_spec(), _smem_spec(),
                  pl.BlockSpec((L, WIDTH), lambda b, i: (b * nb + i, 0)),
                  pl.BlockSpec((seq, WIDTH), lambda b, i: (b, 0)),
                  pl.BlockSpec((seq, WIDTH), lambda b, i: (b, 0)),
                  _const_spec(bias_tiles.shape)],
        out_specs=pl.BlockSpec((L, WIDTH), lambda b, i: (b * nb + i, 0)),
        out_shape=jax.ShapeDtypeStruct((n, WIDTH), BF16),
        scratch_shapes=[pltpu.VMEM((nbp, WIDTH), F32),
                        pltpu.VMEM((nb, WIDTH, L), BF16),
                        pltpu.VMEM((A_HEADS, nbp, L), F32),
                        pltpu.VMEM((A_HEADS, 1, L), F32),
                        pltpu.VMEM((A_HEADS, 1, L), F32),
                        pltpu.VMEM((A_HEADS, 8, L), F32),
                        pltpu.VMEM((WIDTH, L), F32),
                        pltpu.VMEM((QK_AHEAD + 1, KEY_CHUNK, 2 * L), F32)],
        compiler_params=_params(("arbitrary", "arbitrary")),
        name="moba",
    )(flag, far, q, k, v, bias_tiles)


def _swa_kernel(flag_ref, sink_ref, q_ref, kp_ref, kc_ref, vp_ref, vc_ref, bias_ref, o_ref, s_ref):
    g = pl.program_id(0)
    chunk = pl.program_id(2)
    W = C_WINDOW
    G = C_HEADS // C_KV_HEADS
    nsub = q_ref.shape[0] // W
    gw = q_ref.shape[1]
    kall = jnp.concatenate([kp_ref[...], kc_ref[...]], axis=0)
    vall = jnp.concatenate([vp_ref[...], vc_ref[...]], axis=0)
    lane = lax.broadcasted_iota(jnp.int32, (W, gw), 1)
    qi = lax.broadcasted_iota(jnp.int32, (W, 2 * W), 0)
    kj = lax.broadcasted_iota(jnp.int32, (W, 2 * W), 1)
    dist = W + qi - kj
    band = (dist >= 0) & (dist < W)

    def block_mask(r):
        return band & ((chunk > 0) | (kj >= W)) if r == 0 else band

    def head_query(r, hh):
        q = q_ref[r * W:(r + 1) * W, :]
        return jnp.where(lane // HEAD_DIM == hh, q, jnp.zeros_like(q))

    def bounded_heads():
        units = [(r, hh) for r in range(nsub) for hh in range(G)]
        kall_t = kall.astype(F32).T.astype(BF16)

        def qk(u):
            r, hh = units[u]
            s_ref[u % (QK_AHEAD + 1)] = _dot(head_query(r, hh), kall_t[:, r * W:(r + 2) * W])

        for u in range(QK_AHEAD):
            qk(u)
        out = None
        for u, (r, hh) in enumerate(units):
            if u + QK_AHEAD < len(units):
                qk(u + QK_AHEAD)
            if hh == 0:
                out = jnp.zeros((W, gw), F32)
            s = s_ref[u % (QK_AHEAD + 1)] + bias_ref[hh]
            pb = jnp.where(block_mask(r), jnp.exp2(s), 0.0).astype(BF16)
            sink = jnp.exp2(jnp.full((1, 128), sink_ref[g * G + hh], F32))
            inv = 1.0 / (_dot(pb, jnp.ones((2 * W, 128), BF16)) + sink)
            o = _dot(pb, vall[r * W:(r + 2) * W]) * jnp.concatenate([inv] * (gw // 128), axis=-1)
            out = jnp.where(lane // HEAD_DIM == hh, o, out)
            if hh == G - 1:
                o_ref[r * W:(r + 1) * W, :] = out.astype(BF16)

    def online_heads():
        for r in range(nsub):
            kcat = kall[r * W:(r + 2) * W]
            vcat = vall[r * W:(r + 2) * W]
            out = jnp.zeros((W, gw), F32)
            for hh in range(G):
                s = jnp.where(block_mask(r), _dot_nt(head_query(r, hh), kcat) + bias_ref[hh], NEG_INF)
                sink = sink_ref[g * G + hh]
                m = jnp.maximum(jnp.max(s, axis=-1, keepdims=True), sink)
                p = jnp.exp2(s - m)
                den = jnp.sum(p, axis=-1, keepdims=True) + jnp.exp2(sink - m)
                o = _dot(p.astype(BF16), vcat) / den
                out = jnp.where(lane // HEAD_DIM == hh, o, out)
            o_ref[r * W:(r + 1) * W, :] = out.astype(BF16)

    @pl.when(flag_ref[0] == 1)
    def _():
        bounded_heads()

    @pl.when(flag_ref[0] != 1)
    def _():
        online_heads()


def _swa(q, k_t, v_t, bias_tiles, sinks, flag, batch, seq):
    n = q.shape[0]
    W = C_WINDOW
    G = C_HEADS // C_KV_HEADS
    gw = G * HEAD_DIM
    nsub = min(SWA_BLOCKS, seq // W)
    nchunk = seq // (W * nsub)
    cur = lambda g, b, j: (b * nchunk + j, g)
    prev = lambda g, b, j: (jnp.maximum((b * nchunk + j) * nsub - 1, 0), g)
    return pl.pallas_call(
        _swa_kernel,
        grid=(C_KV_HEADS, batch, nchunk),
        in_specs=[_smem_spec(), _smem_spec(),
                  pl.BlockSpec((nsub * W, gw), cur),
                  pl.BlockSpec((W, gw), prev), pl.BlockSpec((nsub * W, gw), cur),
                  pl.BlockSpec((W, gw), prev), pl.BlockSpec((nsub * W, gw), cur),
                  pl.BlockSpec((G, W, 2 * W), lambda g, b, j: (g, 0, 0))],
        out_specs=pl.BlockSpec((nsub * W, gw), cur),
        out_shape=jax.ShapeDtypeStruct((n, WIDTH), BF16),
        scratch_shapes=[pltpu.VMEM((QK_AHEAD + 1, W, 2 * W), F32)],
        compiler_params=_params(("arbitrary", "arbitrary", "arbitrary")),
        name="swa",
    )(flag, sinks, q, k_t, k_t, v_t, v_t, bias_tiles)


def _seg_norm(t, gain, seg):
    outs = []
    for c in range(0, t.shape[1], seg):
        tc = t[:, c:c + seg]
        outs.append(tc * lax.rsqrt(jnp.mean(tc * tc, axis=-1, keepdims=True) + EPS))
    return jnp.concatenate(outs, axis=-1) * gain


def _mem_kv_kernel(mem_ref, g_ref, wk_ref, wv_ref, kg_ref, k_ref, v_ref):
    mb = _rms(mem_ref[...], g_ref[...]).astype(BF16)
    k_ref[...] = _seg_norm(_dot(mb, wk_ref[...]), kg_ref[...], X_HEAD_DIM).astype(BF16)
    v_ref[...] = _dot(mb, wv_ref[...]).astype(BF16)


def _mem_kv(mem, g, w_xk, w_xv, k_gain):
    n, d = mem.shape
    tm = min(ROW_TILE, n)
    row = lambda width: pl.BlockSpec((tm, width), lambda i: (i, 0))
    return pl.pallas_call(
        _mem_kv_kernel,
        grid=(n // tm,),
        in_specs=[row(d), _const_spec(g.shape), _const_spec(w_xk.shape), _const_spec(w_xv.shape),
                  _const_spec(k_gain.shape)],
        out_specs=[row(WIDTH), row(WIDTH)],
        out_shape=[jax.ShapeDtypeStruct((n, WIDTH), BF16)] * 2,
        compiler_params=_params(("parallel",)),
        name="mem_kv",
    )(mem, g, w_xk, w_xv, k_gain)


def _merge_cross_kernel(x_ref, oa_ref, ub_ref, halo_ref, oc_ref, gt_ref, pw_ref, ps_ref, wb_ref, wo_ref,
                        gx_ref, wq_ref, qg_ref, km_ref, vm_ref, wxo_ref, o_ref, *, seq):
    tm = x_ref.shape[0]
    d = x_ref.shape[1]
    t0 = (pl.program_id(0) * tm) % seq
    H = POOL_HALO

    halo = jnp.where(t0 > 0, halo_ref[...], 0.0)
    pos = t0 + lax.broadcasted_iota(jnp.int32, (tm, B_GROUP_DIM), 0)
    mixed = []
    for gi, win in enumerate(B_WINDOWS):
        cols = slice(gi * B_GROUP_DIM, (gi + 1) * B_GROUP_DIM)
        cur = ub_ref[:, cols]
        acc = jnp.concatenate([halo[:, cols], cur], axis=0)
        step = 1
        while step < win:
            acc = acc + pltpu.roll(acc, step, axis=0)
            step *= 2
        cnt = jnp.minimum(pos + 1, win).astype(F32)
        pooled = acc[H:, :] / cnt - cur
        mixed.append(_dot(pooled.astype(BF16), pw_ref[gi]))
    ob = jnp.concatenate(mixed, axis=-1) * ps_ref[...]

    merged = gt_ref[:, 0:d].astype(F32) * _dot(oa_ref[...], wb_ref[0])
    merged = merged + gt_ref[:, d:2 * d].astype(F32) * _dot(ob.astype(BF16), wb_ref[1])
    merged = merged + gt_ref[:, 2 * d:3 * d].astype(F32) * _dot(oc_ref[...], wb_ref[2])
    x1 = x_ref[...] + _dot(merged.astype(BF16), wo_ref[...])

    xb = _rms(x1, gx_ref[...]).astype(BF16)
    qn = _seg_norm(_dot(xb, wq_ref[...]), qg_ref[...], X_HEAD_DIM).astype(BF16)
    heads = []
    for h in range(X_HEADS):
        cols = slice(h * X_HEAD_DIM, (h + 1) * X_HEAD_DIM)
        s = _dot_nt(qn[:, cols], km_ref[:, cols])
        p = jnp.exp(s - jnp.max(s, axis=-1, keepdims=True))
        inv = 1.0 / jnp.sum(p, axis=-1, keepdims=True)
        heads.append(_dot(p.astype(BF16), vm_ref[:, cols]) * inv)
    o = jnp.concatenate(heads, axis=-1).astype(BF16)
    o_ref[...] = x1 + _dot(o, wxo_ref[...])


def _merge_cross(x, oa, ub, oc, gates, pool_w, pool_scale, w_branch, w_out,
                 g_cross, w_xq, q_gain, k_mem, v_mem, w_xo, seq, mem_len):
    n, d = x.shape
    tm = min(ROW_TILE, seq)
    row = lambda width: pl.BlockSpec((tm, width), lambda i: (i, 0))
    halo = pl.BlockSpec((POOL_HALO, WIDTH), lambda i: (jnp.maximum(i * (tm // POOL_HALO) - 1, 0), 0))
    mem = pl.BlockSpec((mem_len, WIDTH), lambda i: ((i * tm) // seq, 0))
    kern = functools.partial(_merge_cross_kernel, seq=seq)
    return pl.pallas_call(
        kern,
        grid=(n // tm,),
        in_specs=[row(d), row(WIDTH), row(WIDTH), halo, row(WIDTH), row(gates.shape[1]),
                  _const_spec(pool_w.shape), _const_spec(pool_scale.shape), _const_spec(w_branch.shape),
                  _const_spec(w_out.shape), _const_spec(g_cross.shape), _const_spec(w_xq.shape),
                  _const_spec(q_gain.shape), mem, mem, _const_spec(w_xo.shape)],
        out_specs=row(d),
        out_shape=jax.ShapeDtypeStruct((n, d), F32),
        compiler_params=_params(("parallel",)),
        name="merge_cross",
    )(x, oa, ub, ub, oc, gates, pool_w, pool_scale, w_branch, w_out,
      g_cross, w_xq, q_gain, k_mem, v_mem, w_xo)


def _swiglu_chunks(xb, w1_ref, w3_ref, w2_ref, acc):
    for c in range(0, w1_ref.shape[1], FF_CHUNK):
        h1 = _dot(xb, w1_ref[:, c:c + FF_CHUNK])
        h3 = _dot(xb, w3_ref[:, c:c + FF_CHUNK])
        acc = acc + _dot((jax.nn.silu(h1) * h3).astype(BF16), w2_ref[c:c + FF_CHUNK, :])
    return acc


def _ffn_kernel(x_ref, g_ref, w1_ref, w3_ref, w2_ref, o_ref):
    x = x_ref[...]
    xb = _rms(x, g_ref[...]).astype(BF16)
    o_ref[...] = _swiglu_chunks(xb, w1_ref, w3_ref, w2_ref, x)


def _ffn(x, g, w1, w3, w2):
    n, d = x.shape
    tm = min(ROW_TILE, n)
    row = pl.BlockSpec((tm, d), lambda i: (i, 0))
    return pl.pallas_call(
        _ffn_kernel,
        grid=(n // tm,),
        in_specs=[row, _const_spec(g.shape), _const_spec(w1.shape), _const_spec(w3.shape),
                  _const_spec(w2.shape)],
        out_specs=row,
        out_shape=jax.ShapeDtypeStruct((n, d), F32),
        compiler_params=_params(("parallel",)),
        name="ffn_dense",
    )(x, g, w1, w3, w2)


def _store_token_tiles(ref, x):
    rows = x.shape[0]
    for k in range(x.shape[1] // 128):
        ref[pl.ds(k, rows, stride=SUBLANES), :] = x[:, k * 128:(k + 1) * 128]


def _load_token_tiles(ref, rows, lead=None):
    idx = (lambda k: (pl.ds(k, rows, stride=SUBLANES), slice(None))) if lead is None else \
        (lambda k: (lead, pl.ds(k, rows, stride=SUBLANES), slice(None)))
    return jnp.concatenate([ref[idx(k)] for k in range(SUBLANES)], axis=-1)


ROUTE_EXPERT, ROUTE_GATE, ROUTE_RANK = 0, TOP_K, 2 * TOP_K


def _router_kernel(x_ref, g_ref, r_ref, rt_ref, cnt_ref, carry_ref):
    i = pl.program_id(0)
    tm = x_ref.shape[0]
    xn = _rms(x_ref[...], g_ref[...])
    r = r_ref[...]
    x_hi = xn.astype(BF16)
    x_lo = (xn - x_hi.astype(F32)).astype(BF16)
    r_hi = r.astype(BF16)
    r_lo = (r - r_hi.astype(F32)).astype(BF16)
    logits = _dot(x_hi, r_hi) + (_dot(x_lo, r_hi) + _dot(x_hi, r_lo))
    lane = lax.broadcasted_iota(jnp.int32, logits.shape, 1)
    lg = jnp.where(lane < N_EXPERTS, logits, -jnp.inf)
    m1 = jnp.max(lg, axis=-1, keepdims=True)
    e1 = jnp.min(jnp.where(lg == m1, lane, lg.shape[1]), axis=-1, keepdims=True)
    lg = jnp.where(lane == e1, -jnp.inf, lg)
    m2 = jnp.max(lg, axis=-1, keepdims=True)
    e2 = jnp.min(jnp.where(lg == m2, lane, lg.shape[1]), axis=-1, keepdims=True)
    ex = jnp.exp(m2 - m1)
    gate1 = 1.0 / (1.0 + ex)
    gate2 = ex * gate1

    @pl.when(i == 0)
    def _():
        carry_ref[...] = jnp.zeros(carry_ref.shape, F32)

    hot1 = (lane == e1).astype(F32)
    hot2 = (lane == e2).astype(F32)
    hot = hot1 + hot2
    before = lax.broadcasted_iota(jnp.int32, (tm, tm), 1) < lax.broadcasted_iota(jnp.int32, (tm, tm), 0)
    seen = _dot(before.astype(BF16), hot.astype(BF16)) + carry_ref[...]
    rank1 = jnp.sum(seen * hot1, axis=-1, keepdims=True)
    rank2 = jnp.sum(seen * hot2, axis=-1, keepdims=True)
    carry_ref[...] += jnp.sum(hot, axis=0, keepdims=True)
    cnt_ref[...] = carry_ref[...]
    cols = [e1.astype(F32), e2.astype(F32), gate1, gate2, rank1, rank2]
    table = jnp.zeros(logits.shape, F32)
    for c, v in enumerate(cols):
        table = jnp.where(lane == c, v, table)
    rt_ref[...] = table


def _router(x, g, router_padded):
    n, d = x.shape
    tm = min(ROW_TILE, n)
    width = router_padded.shape[1]
    row = lambda w: pl.BlockSpec((tm, w), lambda i: (i, 0))
    return pl.pallas_call(
        _router_kernel,
        grid=(n // tm,),
        in_specs=[row(d), _const_spec(g.shape), _const_spec(router_padded.shape)],
        out_specs=[row(width), pl.BlockSpec((1, width), lambda i: (0, 0))],
        out_shape=[jax.ShapeDtypeStruct((n, width), F32), jax.ShapeDtypeStruct((1, width), F32)],
        scratch_shapes=[pltpu.VMEM((1, width), F32)],
        compiler_params=_params(("arbitrary",)),
        name="router",
    )(x, g, router_padded)


def _token_tile(ref, first_row):
    return ref.at[pl.ds(pl.multiple_of(first_row, SUBLANES), SUBLANES), :]


def _dispatch_kernel(pad_ref, dst_ref, x_ref, g_ref, xs_hbm, xt_ref, zero_ref, sem, zsem):
    i = pl.program_id(0)
    n_steps = pl.num_programs(0)
    tm = x_ref.shape[0]
    T = SUBLANES
    slot = i % 2

    def wait_step(s):
        for _ in range(TOP_K):
            pltpu.make_async_copy(xt_ref.at[s], xs_hbm.at[pl.ds(0, tm * T), :], sem.at[s]).wait()

    @pl.when(i == 0)
    def _():
        zero_ref[...] = jnp.zeros(zero_ref.shape, F32)
        for e in range(N_EXPERTS + 1):
            first = pad_ref[0, e]
            tokens = 1 if e < N_EXPERTS else ZERO_TOKENS
            src = zero_ref.at[pl.ds(0, tokens * T), :]

            def dst(r):
                return xs_hbm.at[pl.ds(pl.multiple_of((first + r * tokens) * T, T), tokens * T), :]

            def fill(r, carry):
                pltpu.make_async_copy(src, dst(r), zsem).start()
                return carry

            def drain(r, carry):
                pltpu.make_async_copy(src, dst(0), zsem).wait()
                return carry

            lax.fori_loop(0, pad_ref[1, e] // tokens, fill, 0)
            lax.fori_loop(0, pad_ref[1, e] // tokens, drain, 0)

    _store_token_tiles(xt_ref.at[slot], _rms(x_ref[...], g_ref[...]))

    for choice in range(TOP_K):
        def body(gi, carry):
            for k in range(DMA_UNROLL):
                tok = gi * DMA_UNROLL + k
                pltpu.make_async_copy(_token_tile(xt_ref.at[slot], tok * T),
                                      _token_tile(xs_hbm, dst_ref[0, 0, choice * tm + tok]),
                                      sem.at[slot]).start(priority=k % 2)
            return carry

        lax.fori_loop(0, tm // DMA_UNROLL, body, 0)

    @pl.when(i >= 1)
    def _():
        wait_step(1 - slot)

    @pl.when(i == n_steps - 1)
    def _():
        wait_step(slot)


def _dispatch(x, g, dest_blocks, pad_info, p):
    n, d = x.shape
    tm = dest_blocks.shape[2] // TOP_K
    T = SUBLANES
    grid_spec = pltpu.PrefetchScalarGridSpec(
        num_scalar_prefetch=1,
        grid=(n // tm,),
        in_specs=[pl.BlockSpec((1, 1, TOP_K * tm), lambda i, pad: (i, 0, 0), memory_space=pltpu.SMEM),
                  pl.BlockSpec((tm, d), lambda i, pad: (i, 0)),
                  pl.BlockSpec(g.shape, lambda i, pad: (0, 0))],
        out_specs=pl.BlockSpec(memory_space=pl.ANY),
        scratch_shapes=[pltpu.VMEM((2, tm * T, d // T), F32), pltpu.VMEM((ZERO_TOKENS * T, d // T), F32),
                        pltpu.SemaphoreType.DMA((2,)), pltpu.SemaphoreType.DMA],
    )
    return pl.pallas_call(
        _dispatch_kernel,
        grid_spec=grid_spec,
        out_shape=jax.ShapeDtypeStruct((p * T, d // T), F32),
        compiler_params=_params(("arbitrary",)),
        name="dispatch",
    )(pad_info, dest_blocks, x, g)


def _expert_kernel(be_ref, nu_ref, x_ref, w1_ref, w3_ref, w2_ref, o_ref, xb_ref, y_ref):
    b = pl.program_id(0)
    c = pl.program_id(1)
    tb = xb_ref.shape[0]

    @pl.when(b < nu_ref[0])
    def _():
        @pl.when(c == 0)
        def _():
            xb_ref[...] = _load_token_tiles(x_ref, tb).astype(BF16)
            y_ref[...] = jnp.zeros(y_ref.shape, F32)

        xb = xb_ref[...]
        h1 = _dot(xb, w1_ref[...].astype(BF16))
        h3 = _dot(xb, w3_ref[...].astype(BF16))
        y_ref[...] += _dot((jax.nn.silu(h1) * h3).astype(BF16), w2_ref[...].astype(BF16))

        @pl.when(c == pl.num_programs(1) - 1)
        def _():
            _store_token_tiles(o_ref, y_ref[...])

    @pl.when((b >= nu_ref[0]) & (c == pl.num_programs(1) - 1))
    def _():
        o_ref[...] = jnp.zeros(o_ref.shape, F32)


def _experts(xs, block_e, n_used, w1, w3, w2, tb):
    T = SUBLANES
    d = w1.shape[1]
    nc = w1.shape[2] // FF_CHUNK
    nblk = xs.shape[0] // (tb * T)
    blk = lambda b, nu: jnp.minimum(b, nu[0] - 1)
    chunk = lambda b, c, nu: jnp.where(b < nu[0], c, nc - 1)
    grid_spec = pltpu.PrefetchScalarGridSpec(
        num_scalar_prefetch=2,
        grid=(nblk, nc),
        in_specs=[pl.BlockSpec((tb * T, d // T), lambda b, c, be, nu: (blk(b, nu), 0)),
                  pl.BlockSpec((None, d, FF_CHUNK), lambda b, c, be, nu: (be[b], 0, chunk(b, c, nu))),
                  pl.BlockSpec((None, d, FF_CHUNK), lambda b, c, be, nu: (be[b], 0, chunk(b, c, nu))),
                  pl.BlockSpec((None, FF_CHUNK, d), lambda b, c, be, nu: (be[b], chunk(b, c, nu), 0))],
        out_specs=pl.BlockSpec((tb * T, d // T), lambda b, c, be, nu: (b, 0)),
        scratch_shapes=[pltpu.VMEM((tb, d), BF16), pltpu.VMEM((tb, d), F32)],
    )
    return pl.pallas_call(
        _expert_kernel,
        grid_spec=grid_spec,
        out_shape=jax.ShapeDtypeStruct(xs.shape, F32),
        compiler_params=_params(("arbitrary", "arbitrary")),
        name="experts",
    )(block_e, n_used, xs, w1, w3, w2)


def _combine_kernel(dst_ref, dstn_ref, x_ref, g_ref, ys_hbm, o_ref, yg_ref, sem):
    i = pl.program_id(0)
    n_steps = pl.num_programs(0)
    tm = x_ref.shape[0]
    T = SUBLANES
    slot = i % 2

    def gather(idx_ref, s):
        def body(gi, carry):
            for k in range(DMA_UNROLL):
                r = gi * DMA_UNROLL + k
                pltpu.make_async_copy(_token_tile(ys_hbm, idx_ref[0, 0, r]), _token_tile(yg_ref.at[s], r * T),
                                      sem.at[s]).start(priority=k % 2)
            return carry
        lax.fori_loop(0, TOP_K * tm // DMA_UNROLL, body, 0)

    @pl.when(i == 0)
    def _():
        gather(dst_ref, 0)

    @pl.when(i + 1 < n_steps)
    def _():
        gather(dstn_ref, 1 - slot)

    pltpu.make_async_copy(ys_hbm.at[pl.ds(0, TOP_K * tm * T), :], yg_ref.at[slot], sem.at[slot]).wait()
    g = g_ref[...]
    y0 = _load_token_tiles(yg_ref, tm, lead=slot)
    y1 = jnp.concatenate([yg_ref[slot, pl.ds(tm * T + k, tm, stride=T), :] for k in range(T)], axis=-1)
    o_ref[...] = x_ref[...] + g[:, ROUTE_GATE:ROUTE_GATE + 1] * y0 + g[:, ROUTE_GATE + 1:ROUTE_GATE + 2] * y1


def _combine(x, ys, dest_blocks, gate):
    n, d = x.shape
    tm = dest_blocks.shape[2] // TOP_K
    T = SUBLANES
    n_steps = n // tm
    row = lambda width: pl.BlockSpec((tm, width), lambda i: (i, 0))
    idx = lambda imap: pl.BlockSpec((1, 1, TOP_K * tm), imap, memory_space=pltpu.SMEM)
    return pl.pallas_call(
        _combine_kernel,
        grid=(n_steps,),
        in_specs=[idx(lambda i: (i, 0, 0)), idx(lambda i: (jnp.minimum(i + 1, n_steps - 1), 0, 0)),
                  row(d), row(gate.shape[1]), pl.BlockSpec(memory_space=pl.ANY)],
        out_specs=row(d),
        out_shape=jax.ShapeDtypeStruct((n, d), F32),
        scratch_shapes=[pltpu.VMEM((2, TOP_K * tm * T, d // T), F32), pltpu.SemaphoreType.DMA((2,))],
        compiler_params=_params(("arbitrary",)),
        name="combine",
    )(dest_blocks, dest_blocks, x, gate, ys)


def _moe(x, g, router, w1, w3, w2):
    n, d = x.shape
    a = n * TOP_K
    tb = min(MOE_TILE, a)
    router_padded = jnp.pad(router, ((0, 0), (0, 128 - N_EXPERTS)))
    table, cnt = _router(x, g, router_padded)
    top_e = table[:, ROUTE_EXPERT:ROUTE_EXPERT + TOP_K].astype(jnp.int32)
    rank = table[:, ROUTE_RANK:ROUTE_RANK + TOP_K].astype(jnp.int32)
    counts = cnt[0, :N_EXPERTS].astype(jnp.int32)
    padded = (counts + tb - 1) // tb * tb
    pend = jnp.cumsum(padded)
    pstart = pend - padded
    onehot = (top_e[:, :, None] == jnp.arange(N_EXPERTS)[None, None, :]).astype(jnp.int32)
    dest = jnp.sum(onehot * pstart[None, None, :], axis=-1) + rank
    n_blocks = a // tb + N_EXPERTS
    p = n_blocks * tb
    block_e = jnp.minimum(jnp.sum(pend[None, :] <= (jnp.arange(n_blocks) * tb)[:, None], axis=1),
                          N_EXPERTS - 1).astype(jnp.int32)
    n_used = (pend[-1] // tb).astype(jnp.int32).reshape(1)
    tm = min(ROW_TILE, n)
    dest_blocks = (dest * SUBLANES).reshape(n // tm, tm, TOP_K).transpose(0, 2, 1).reshape(n // tm, 1, TOP_K * tm)
    pad_info = jnp.stack([jnp.append(pstart + counts, pend[-1]),
                          jnp.append(padded - counts, p - pend[-1])]).astype(jnp.int32)
    xs = _dispatch(x, g, dest_blocks, pad_info, p)
    ys = _experts(xs, block_e, n_used, w1, w3, w2, tb)
    return _combine(x, ys, dest_blocks, table)


def _tile_gain(gain, width, scale=1.0):
    return jnp.tile(gain, width // gain.shape[0]) * scale


def kernel(x, mem, rel_bias, norm_mix, w_in, b_gate, a_q_gain, a_k_gain, pool_w, pool_scale,
           c_q_gain, c_k_gain, c_sinks, w_branch, w_out, norm_cross, norm_mem, w_xq, w_xk, w_xv,
           x_q_gain, x_k_gain, w_xo, norm_ffn, ffn_w1, ffn_w3, ffn_w2, router, moe_w1, moe_w3, moe_w2):
    batch, seq, d = x.shape
    mem_len = mem.shape[1]
    depth = norm_mix.shape[0]
    xs = x.reshape(batch * seq, d)
    mems = mem.reshape(batch * mem_len, d)

    tab_a = rel_bias[:, :A_HEADS].T
    tab_c = rel_bias[:, A_HEADS:].T
    nb = seq // A_BLOCK
    n_near = min(nb, (_saturation_distance() + 2 * A_BLOCK - 2) // A_BLOCK)
    far_a = tab_a[:, REL_BUCKETS - 1] * LOG2E
    bias_a = _moba_bias_tiles(_bias_by_distance(tab_a, n_near * A_BLOCK), n_near) * LOG2E
    bias_a = jnp.concatenate([bias_a, jnp.broadcast_to(far_a[:, None, None, None],
                                                       (A_HEADS, 1, A_BLOCK, A_BLOCK))], axis=1)
    bias_a = jnp.concatenate([bias_a[0::2], bias_a[1::2]], axis=-1)
    bias_c = _swa_bias_tiles(_bias_by_distance(tab_c, C_WINDOW)) * LOG2E
    seg = np.arange(256) // HEAD_DIM
    bd = jnp.asarray(seg[:, None] == seg[None, :], BF16)

    row = lambda v: v.reshape(1, -1)
    for l in range(depth):
        scale = HEAD_DIM ** -0.5 * LOG2E
        gains = jnp.stack([_tile_gain(a_q_gain[l], WIDTH, scale), _tile_gain(a_k_gain[l], WIDTH),
                           _tile_gain(c_q_gain[l], WIDTH, scale), _tile_gain(c_k_gain[l], WIDTH)])
        flag_a = _bounded_flag(a_q_gain[l], a_k_gain[l], HEAD_DIM, jnp.max(jnp.abs(tab_a)))
        flag_c = _bounded_flag(c_q_gain[l], c_k_gain[l], HEAD_DIM,
                               jnp.maximum(jnp.max(jnp.abs(tab_c)), jnp.max(jnp.abs(c_sinks[l]))))
        qa, ka, va, ub, qc, kc, vc, gates = _in_proj(
            xs, row(norm_mix[l]), w_in[l].astype(BF16), row(b_gate[l]), gains, bd)
        oa = _moba(qa, ka, va, bias_a, far_a, flag_a, batch, seq)
        oc = _swa(qc, kc, vc, bias_c, c_sinks[l] * LOG2E, flag_c, batch, seq)
        k_mem, v_mem = _mem_kv(mems, row(norm_mem[l]), w_xk[l].astype(BF16), w_xv[l].astype(BF16),
                               row(_tile_gain(x_k_gain[l], WIDTH)))
        xs = _merge_cross(xs, oa, ub, oc, gates, pool_w[l].astype(BF16), row(pool_scale[l]),
                          w_branch[l].astype(BF16), w_out[l].astype(BF16), row(norm_cross[l]),
                          w_xq[l].astype(BF16), row(_tile_gain(x_q_gain[l], WIDTH, X_HEAD_DIM ** -0.5)),
                          k_mem, v_mem, w_xo[l].astype(BF16), seq, mem_len)
        i = l // 2
        if l % 2 == 0:
            xs = _ffn(xs, row(norm_ffn[l]), ffn_w1[i].astype(BF16), ffn_w3[i].astype(BF16),
                      ffn_w2[i].astype(BF16))
        else:
            xs = _moe(xs, row(norm_ffn[l]), router[i], moe_w1[i], moe_w3[i], moe_w2[i])
    return xs.reshape(batch, seq, d)
```

```python
import functools
import math

import jax
import jax.numpy as jnp
import numpy as np
from jax import lax
from jax.experimental import pallas as pl
from jax.experimental.pallas import tpu as pltpu

F32 = jnp.float32
BF16 = jnp.bfloat16

HEAD_DIM = 64
A_HEADS = 8
A_BLOCK = 256
A_TOPK = 3
B_GROUPS = 4
B_GROUP_DIM = 128
B_WINDOWS = (2, 4, 8, 16)
C_HEADS = 8
C_KV_HEADS = 2
C_WINDOW = 128
REL_BUCKETS = 32
REL_MAX_DIST = 1024
X_HEADS = 4
X_HEAD_DIM = 128
N_EXPERTS = 8
TOP_K = 2
EPS = 1e-6
NEG_INF = -1e30
LOG2E = math.log2(math.e)

WIDTH = 512
POOL_HALO = 16
ROW_TILE = 512
MOE_TILE = 1024
FF_CHUNK = 512
SUBLANES = 8
DMA_UNROLL = 8
ZERO_TOKENS = 8
KEY_CHUNK = 128
MOBA_TRIP = 3
QK_AHEAD = 3
SWA_BLOCKS = 4
VMEM_LIMIT = 56 * 1024 * 1024
EXP2_SAFE = 100.0


def _dot(a, b):
    return jnp.dot(a, b, preferred_element_type=F32)


def _dot_nt(a, b):
    return lax.dot_general(a, b, (((1,), (1,)), ((), ())), preferred_element_type=F32)


def _rms(x, g):
    ms = jnp.mean(x * x, axis=-1, keepdims=True)
    return x * lax.rsqrt(ms + EPS) * g


def _const_spec(shape):
    zeros = (0,) * len(shape)
    return pl.BlockSpec(shape, lambda *_: zeros, pipeline_mode=pl.Buffered(1))


def _smem_spec():
    return pl.BlockSpec(memory_space=pltpu.SMEM)


def _params(sem):
    return pltpu.CompilerParams(dimension_semantics=sem, vmem_limit_bytes=VMEM_LIMIT)


def _rel_bucket(dist):
    n = jnp.maximum(dist, 0)
    max_exact = REL_BUCKETS // 2
    nf = jnp.maximum(n, 1).astype(jnp.float32)
    large = max_exact + (jnp.log(nf / max_exact) / math.log(REL_MAX_DIST / max_exact)
                         * (REL_BUCKETS - max_exact)).astype(jnp.int32)
    large = jnp.minimum(large, REL_BUCKETS - 1)
    return jnp.where(n < max_exact, n, large)


def _saturation_distance():
    ratio = REL_MAX_DIST / (REL_BUCKETS // 2)
    return int(math.ceil((REL_BUCKETS // 2) * ratio ** ((REL_BUCKETS - 1 - REL_BUCKETS // 2 + 0.5)
                                                       / (REL_BUCKETS - REL_BUCKETS // 2))))


def _bias_by_distance(tab, n_dist):
    onehot = (_rel_bucket(jnp.arange(n_dist))[:, None] == jnp.arange(REL_BUCKETS)[None, :]).astype(F32)
    return jnp.einsum('hb,db->hd', tab, onehot, precision=lax.Precision.HIGHEST)


def _skew(g, rows):
    n = g.shape[-1]
    lead = g.shape[:-1]
    tiled = jnp.broadcast_to(g[..., None, :], lead + (rows, n)).reshape(lead + (rows * n,))
    return tiled[..., :rows * (n - 1)].reshape(lead + (rows, n - 1))


def _moba_bias_tiles(bvec, n_near):
    L = A_BLOCK
    rows = []
    for delta in range(n_near):
        lo = delta * L - (L - 1)
        seg = bvec[:, max(lo, 0):delta * L + L]
        if lo < 0:
            seg = jnp.concatenate([jnp.zeros((bvec.shape[0], -lo), F32), seg], axis=1)
        rows.append(jnp.pad(seg, ((0, 0), (0, 1))))
    g = jnp.stack(rows, axis=1)
    return _skew(g, L)[..., L - 1:]


def _swa_bias_tiles(bvec):
    W = C_WINDOW
    u = np.arange(3 * W)
    g = bvec[:, np.clip(2 * W - 1 - u, 0, W - 1)]
    return _skew(g, W)[..., W - 1:3 * W - 1]


def _bounded_flag(q_gain, k_gain, head_dim, extra):
    bound = head_dim ** 0.5 * jnp.max(jnp.abs(q_gain)) * jnp.max(jnp.abs(k_gain)) * 1.02 + extra
    return (bound * LOG2E < EXP2_SAFE).astype(jnp.int32).reshape(1)


def _in_proj_kernel(x_ref, g_ref, w_ref, bg_ref, gn_ref, bd_ref,
                    qa_ref, ka_ref, va_ref, ub_ref, qc_ref, kc_ref, vc_ref, gt_ref):
    xb = _rms(x_ref[...], g_ref[...]).astype(BF16)
    bd = bd_ref[...]

    def proj(c0, width):
        return _dot(xb, w_ref[:, c0:c0 + width])

    def head_norm(t, gain):
        outs = []
        for c in range(0, t.shape[1], 256):
            wd = min(256, t.shape[1] - c)
            tc = t[:, c:c + wd]
            ss = _dot((tc * tc).astype(BF16), bd[:wd, :wd])
            outs.append(tc * lax.rsqrt(ss * (1.0 / HEAD_DIM) + EPS))
        y = outs[0] if len(outs) == 1 else jnp.concatenate(outs, axis=-1)
        return y * gain

    def tile_kv_heads(t):
        lane = lax.broadcasted_iota(jnp.int32, t.shape, 1)
        r = pltpu.roll(t, HEAD_DIM, axis=1)
        h0 = jnp.where(lane < HEAD_DIM, t, r)
        h1 = jnp.where(lane < HEAD_DIM, r, t)
        return jnp.concatenate([h0, h0, h1, h1], axis=-1)

    qa_ref[...] = head_norm(proj(0, WIDTH), gn_ref[0:1, :]).astype(BF16)
    ka_ref[...] = head_norm(proj(WIDTH, WIDTH), gn_ref[1:2, :]).astype(BF16)
    va_ref[...] = proj(2 * WIDTH, WIDTH).astype(BF16)
    ub_ref[...] = proj(3 * WIDTH, WIDTH)
    qc_ref[...] = head_norm(proj(4 * WIDTH, WIDTH), gn_ref[2:3, :]).astype(BF16)
    kv = C_KV_HEADS * HEAD_DIM
    c0 = 5 * WIDTH
    kc = head_norm(proj(c0, kv), gn_ref[3:4, :kv])
    kc_ref[...] = tile_kv_heads(kc).astype(BF16)
    vc_ref[...] = tile_kv_heads(proj(c0 + kv, kv)).astype(BF16)
    c0 += 2 * kv
    for c in range(0, gt_ref.shape[1], WIDTH):
        gl = proj(c0 + c, WIDTH) + bg_ref[:, c:c + WIDTH]
        gt_ref[:, c:c + WIDTH] = jax.nn.sigmoid(gl).astype(BF16)


def _in_proj(x, g, w_in, b_gate, gains, bd):
    n, d = x.shape
    tm = min(ROW_TILE, n)
    n_gate = b_gate.shape[1]
    row = lambda width: pl.BlockSpec((tm, width), lambda i: (i, 0))
    out_shape = [jax.ShapeDtypeStruct((n, WIDTH), BF16)] * 3 + [jax.ShapeDtypeStruct((n, WIDTH), F32)] \
        + [jax.ShapeDtypeStruct((n, WIDTH), BF16)] * 3 + [jax.ShapeDtypeStruct((n, n_gate), BF16)]
    return pl.pallas_call(
        _in_proj_kernel,
        grid=(n // tm,),
        in_specs=[row(d), _const_spec(g.shape), _const_spec(w_in.shape), _const_spec(b_gate.shape),
                  _const_spec(gains.shape), _const_spec(bd.shape)],
        out_specs=[row(WIDTH)] * 7 + [row(n_gate)],
        out_shape=out_shape,
        compiler_params=_params(("parallel",)),
        name="in_proj",
    )(x, g, w_in, b_gate, gains, bd)


def _moba_kernel(flag_ref, far_ref, q_ref, k_ref, v_ref, bias_ref, o_ref,
                 kmean_ref, vt_ref, sel_ref, m_ref, l_ref, l8_ref, acc_ref, s_ref, *, nb, n_near):
    i = pl.program_id(1)
    L = A_BLOCK
    KC = KEY_CHUNK
    PW = 2 * HEAD_DIM
    nbp = kmean_ref.shape[0]

    @pl.when(i == 0)
    def _():
        kmean_ref[...] = jnp.zeros(kmean_ref.shape, F32)
        for j in range(nb):
            kj = k_ref[j * L:(j + 1) * L, :].astype(F32)
            kmean_ref[j:j + 1, :] = jnp.mean(kj, axis=0, keepdims=True)
            vt_ref[j] = v_ref[j * L:(j + 1) * L, :].astype(F32).T.astype(BF16)

    lane = lax.broadcasted_iota(jnp.int32, (L, PW), 1)
    blk = lax.broadcasted_iota(jnp.int32, (nbp, L), 0)
    past = blk < i
    qh = []
    for h in range(A_HEADS):
        pair = slice((h // 2) * PW, (h // 2 + 1) * PW)
        q = q_ref[:, pair]
        qm = jnp.where(lane // HEAD_DIM == h % 2, q, jnp.zeros_like(q))
        qh.append(qm)
        km = kmean_ref[:, pair]
        km_hi = km.astype(BF16)
        km_lo = (km - km_hi.astype(F32)).astype(BF16)
        s = jnp.where(past, _dot_nt(km_hi, qm) + _dot_nt(km_lo, qm), -jnp.inf)
        rank = jnp.zeros((nbp, L), jnp.int32)
        for jp in range(nb):
            sj = s[jp:jp + 1, :]
            ahead = (sj > s) | ((sj == s) & (jp < blk))
            rank = rank + ahead.astype(jnp.int32)
        sel_ref[h] = (past & (rank < A_TOPK)).astype(F32)
        m_ref[h] = jnp.full((1, L), NEG_INF, F32)
        l_ref[h] = jnp.zeros((1, L), F32)
        l8_ref[h] = jnp.zeros((8, L), F32)
    acc_ref[...] = jnp.zeros(acc_ref.shape, F32)
    qpair_t = [jnp.concatenate(qh[2 * hp:2 * hp + 2], axis=0).astype(F32).T.astype(BF16)
               for hp in range(A_HEADS // 2)]

    def causal(c, n):
        kk = c + lax.broadcasted_iota(jnp.int32, (n, L), 0)
        qq = lax.broadcasted_iota(jnp.int32, (n, L), 1)
        return kk <= qq

    def bounded_blocks(blocks):
        units = [(bi, hp, c) for bi in range(len(blocks)) for hp in range(A_HEADS // 2) for c in range(0, L, KC)]

        def qk(u):
            bi, hp, c = units[u]
            kc = k_ref[pl.ds(pl.multiple_of(blocks[bi][0] * L + c, KC), KC), hp * PW:(hp + 1) * PW]
            s_ref[u % (QK_AHEAD + 1)] = _dot(kc, qpair_t[hp])

        for u in range(QK_AHEAD):
            qk(u)
        o = ps = None
        for u, (bi, hp, c) in enumerate(units):
            j, tile_idx, own = blocks[bi]
            if u + QK_AHEAD < len(units):
                qk(u + QK_AHEAD)
            if c == 0:
                o = [jnp.zeros((HEAD_DIM, L), F32)] * 2
                ps = jnp.zeros((8, 2 * L), F32)
            p = jnp.exp2(s_ref[u % (QK_AHEAD + 1)] + bias_ref[hp, tile_idx, pl.ds(c, KC), :])
            if own:
                p = jnp.where(jnp.concatenate([causal(c, KC)] * 2, axis=1), p, 0.0)
            ps = ps + jnp.sum(p.reshape(KC // 8, 8, 2 * L), axis=0)
            pb = p.astype(BF16)
            o = [o[hh] + _dot(vt_ref[j, pl.ds((2 * hp + hh) * HEAD_DIM, HEAD_DIM), pl.ds(c, KC)],
                              pb[:, hh * L:(hh + 1) * L]) for hh in range(2)]
            if c + KC < L:
                continue
            for hh in range(2):
                h = 2 * hp + hh
                rows = slice(h * HEAD_DIM, (h + 1) * HEAD_DIM)
                psh = ps[:, hh * L:(hh + 1) * L]
                if own:
                    acc_ref[rows, :] += o[hh]
                    l8_ref[h] += psh
                else:
                    w = sel_ref[h, pl.ds(j, 1), :]
                    acc_ref[rows, :] += w * o[hh]
                    l8_ref[h] += w * psh

    def run_bounded():
        def past(j):
            return (j, jnp.minimum(i - j, n_near), False)

        def body(t, carry):
            bounded_blocks([past(MOBA_TRIP * t + k) for k in range(MOBA_TRIP)])
            return carry

        lax.fori_loop(0, i // MOBA_TRIP, body, 0)
        for left in range(MOBA_TRIP):
            @pl.when(i % MOBA_TRIP == left)
            def _():
                bounded_blocks([past(i - left + k) for k in range(left)] + [(i, 0, True)])

    def online_tile(j, delta, far, own):
        for h in range(A_HEADS):
            pair = slice((h // 2) * PW, (h // 2 + 1) * PW)
            rows = slice(h * HEAD_DIM, (h + 1) * HEAD_DIM)
            s = _dot_nt(k_ref[pl.ds(pl.multiple_of(j * L, L), L), pair], qh[h])
            if not far:
                s = s + bias_ref[h // 2, delta, :, (h % 2) * L:(h % 2 + 1) * L]
            if own:
                s = jnp.where(causal(0, L), s, NEG_INF)
            mj = jnp.max(s, axis=0, keepdims=True)
            p = jnp.exp2(s - mj)
            lj = jnp.sum(p, axis=0, keepdims=True)
            o = _dot(vt_ref[j, rows, :], p.astype(BF16))
            if far:
                mj = mj + far_ref[h]
            m_old = m_ref[h]
            if own:
                m_new = jnp.maximum(m_old, mj)
                beta = jnp.exp2(mj - m_new)
            else:
                on = sel_ref[h, pl.ds(j, 1), :] > 0.5
                m_new = jnp.where(on, jnp.maximum(m_old, mj), m_old)
                beta = jnp.where(on, jnp.exp2(mj - m_new), 0.0)
            alpha = jnp.exp2(m_old - m_new)
            m_ref[h] = m_new
            l_ref[h] = alpha * l_ref[h] + beta * lj
            acc_ref[rows, :] = alpha * acc_ref[rows, :] + beta * o

    n_far = jnp.maximum(i - (n_near - 1), 0)

    def run(tile):
        def far_body(j, c):
            tile(j, None, True, False)
            return c

        def near_body(j, c):
            tile(j, i - j, False, False)
            return c

        lax.fori_loop(0, n_far, far_body, 0)
        lax.fori_loop(n_far, i, near_body, 0)
        tile(i, 0, False, True)

    @pl.when(flag_ref[0] == 1)
    def _():
        run_bounded()
        for h in range(A_HEADS):
            l_ref[h] = jnp.sum(l8_ref[h], axis=0, keepdims=True)

    @pl.when(flag_ref[0] != 1)
    def _():
        run(online_tile)

    o_t = jnp.concatenate([acc_ref[h * HEAD_DIM:(h + 1) * HEAD_DIM, :] / l_ref[h] for h in range(A_HEADS)],
                          axis=0)
    o_ref[...] = o_t.T.astype(BF16)


def _moba(q, k, v, bias_tiles, far, flag, batch, seq):
    n = q.shape[0]
    L = A_BLOCK
    nb = seq // L
    nbp = max(8, -(-nb // 8) * 8)
    n_near = bias_tiles.shape[1] - 1
    kern = functools.partial(_moba_kernel, nb=nb, n_near=n_near)
    return pl.pallas_call(
        kern,
        grid=(batch, nb),
        in_specs=[_smem_spec(), _smem_spec(),
                  pl.BlockSpec((L, WIDTH), lambda b, i: (b * nb + i, 0)),
                  pl.BlockSpec((seq, WIDTH), lambda b, i: (b, 0)),
                  pl.BlockSpec((seq, WIDTH), lambda b, i: (b, 0)),
                  _const_spec(bias_tiles.shape)],
        out_specs=pl.BlockSpec((L, WIDTH), lambda b, i: (b * nb + i, 0)),
        out_shape=jax.ShapeDtypeStruct((n, WIDTH), BF16),
        scratch_shapes=[pltpu.VMEM((nbp, WIDTH), F32),
                        pltpu.VMEM((nb, WIDTH, L), BF16),
                        pltpu.VMEM((A_HEADS, nbp, L), F32),
                        pltpu.VMEM((A_HEADS, 1, L), F32),
                        pltpu.VMEM((A_HEADS, 1, L), F32),
                        pltpu.VMEM((A_HEADS, 8, L), F32),
                        pltpu.VMEM((WIDTH, L), F32),
                        pltpu.VMEM((QK_AHEAD + 1, KEY_CHUNK, 2 * L), F32)],
        compiler_params=_params(("arbitrary", "arbitrary")),
        name="moba",
    )(flag, far, q, k, v, bias_tiles)


def _swa_kernel(flag_ref, sink_ref, q_ref, kp_ref, kc_ref, vp_ref, vc_ref, bias_ref, o_ref, s_ref):
    g = pl.program_id(0)
    chunk = pl.program_id(2)
    W = C_WINDOW
    G = C_HEADS // C_KV_HEADS
    nsub = q_ref.shape[0] // W
    gw = q_ref.shape[1]
    kall = jnp.concatenate([kp_ref[...], kc_ref[...]], axis=0)
    vall = jnp.concatenate([vp_ref[...], vc_ref[...]], axis=0)
    lane = lax.broadcasted_iota(jnp.int32, (W, gw), 1)
    qi = lax.broadcasted_iota(jnp.int32, (W, 2 * W), 0)
    kj = lax.broadcasted_iota(jnp.int32, (W, 2 * W), 1)
    dist = W + qi - kj
    band = (dist >= 0) & (dist < W)

    def block_mask(r):
        return band & ((chunk > 0) | (kj >= W)) if r == 0 else band

    def head_query(r, hh):
        q = q_ref[r * W:(r + 1) * W, :]
        return jnp.where(lane // HEAD_DIM == hh, q, jnp.zeros_like(q))

    def bounded_heads():
        units = [(r, hh) for r in range(nsub) for hh in range(G)]
        kall_t = kall.astype(F32).T.astype(BF16)

        def qk(u):
            r, hh = units[u]
            s_ref[u % (QK_AHEAD + 1)] = _dot(head_query(r, hh), kall_t[:, r * W:(r + 2) * W])

        for u in range(QK_AHEAD):
            qk(u)
        out = None
        for u, (r, hh) in enumerate(units):
            if u + QK_AHEAD < len(units):
                qk(u + QK_AHEAD)
            if hh == 0:
                out = jnp.zeros((W, gw), F32)
            s = s_ref[u % (QK_AHEAD + 1)] + bias_ref[hh]
            pb = jnp.where(block_mask(r), jnp.exp2(s), 0.0).astype(BF16)
            sink = jnp.exp2(jnp.full((1, 128), sink_ref[g * G + hh], F32))
            inv = 1.0 / (_dot(pb, jnp.ones((2 * W, 128), BF16)) + sink)
            o = _dot(pb, vall[r * W:(r + 2) * W]) * jnp.concatenate([inv] * (gw // 128), axis=-1)
            out = jnp.where(lane // HEAD_DIM == hh, o, out)
            if hh == G - 1:
                o_ref[r * W:(r + 1) * W, :] = out.astype(BF16)

    def online_heads():
        for r in range(nsub):
            kcat = kall[r * W:(r + 2) * W]
            vcat = vall[r * W:(r + 2) * W]
            out = jnp.zeros((W, gw), F32)
            for hh in range(G):
                s = jnp.where(block_mask(r), _dot_nt(head_query(r, hh), kcat) + bias_ref[hh], NEG_INF)
                sink = sink_ref[g * G + hh]
                m = jnp.maximum(jnp.max(s, axis=-1, keepdims=True), sink)
                p = jnp.exp2(s - m)
                den = jnp.sum(p, axis=-1, keepdims=True) + jnp.exp2(sink - m)
                o = _dot(p.astype(BF16), vcat) / den
                out = jnp.where(lane // HEAD_DIM == hh, o, out)
            o_ref[r * W:(r + 1) * W, :] = out.astype(BF16)

    @pl.when(flag_ref[0] == 1)
    def _():
        bounded_heads()

    @pl.when(flag_ref[0] != 1)
    def _():
        online_heads()


def _swa(q, k_t, v_t, bias_tiles, sinks, flag, batch, seq):
    n = q.shape[0]
    W = C_WINDOW
    G = C_HEADS // C_KV_HEADS
    gw = G * HEAD_DIM
    nsub = min(SWA_BLOCKS, seq // W)
    nchunk = seq // (W * nsub)
    cur = lambda g, b, j: (b * nchunk + j, g)
    prev = lambda g, b, j: (jnp.maximum((b * nchunk + j) * nsub - 1, 0), g)
    return pl.pallas_call(
        _swa_kernel,
        grid=(C_KV_HEADS, batch, nchunk),
        in_specs=[_smem_spec(), _smem_spec(),
                  pl.BlockSpec((nsub * W, gw), cur),
                  pl.BlockSpec((W, gw), prev), pl.BlockSpec((nsub * W, gw), cur),
                  pl.BlockSpec((W, gw), prev), pl.BlockSpec((nsub * W, gw), cur),
                  pl.BlockSpec((G, W, 2 * W), lambda g, b, j: (g, 0, 0))],
        out_specs=pl.BlockSpec((nsub * W, gw), cur),
        out_shape=jax.ShapeDtypeStruct((n, WIDTH), BF16),
        scratch_shapes=[pltpu.VMEM((QK_AHEAD + 1, W, 2 * W), F32)],
        compiler_params=_params(("arbitrary", "arbitrary", "arbitrary")),
        name="swa",
    )(flag, sinks, q, k_t, k_t, v_t, v_t, bias_tiles)


def _seg_norm(t, gain, seg):
    outs = []
    for c in range(0, t.shape[1], seg):
        tc = t[:, c:c + seg]
        outs.append(tc * lax.rsqrt(jnp.mean(tc * tc, axis=-1, keepdims=True) + EPS))
    return jnp.concatenate(outs, axis=-1) * gain


def _mem_kv_kernel(mem_ref, g_ref, wk_ref, wv_ref, kg_ref, k_ref, v_ref):
    mb = _rms(mem_ref[...], g_ref[...]).astype(BF16)
    k_ref[...] = _seg_norm(_dot(mb, wk_ref[...]), kg_ref[...], X_HEAD_DIM).astype(BF16)
    v_ref[...] = _dot(mb, wv_ref[...]).astype(BF16)


def _mem_kv(mem, g, w_xk, w_xv, k_gain):
    n, d = mem.shape
    tm = min(ROW_TILE, n)
    row = lambda width: pl.BlockSpec((tm, width), lambda i: (i, 0))
    return pl.pallas_call(
        _mem_kv_kernel,
        grid=(n // tm,),
        in_specs=[row(d), _const_spec(g.shape), _const_spec(w_xk.shape), _const_spec(w_xv.shape),
                  _const_spec(k_gain.shape)],
        out_specs=[row(WIDTH), row(WIDTH)],
        out_shape=[jax.ShapeDtypeStruct((n, WIDTH), BF16)] * 2,
        compiler_params=_params(("parallel",)),
        name="mem_kv",
    )(mem, g, w_xk, w_xv, k_gain)


def _merge_cross_kernel(flag_ref, x_ref, oa_ref, ub_ref, halo_ref, oc_ref, gt_ref, pw_ref, ps_ref, wb_ref, wo_ref,
                        gx_ref, wq_ref, qg_ref, km_ref, vm_ref, wxo_ref, o_ref, *, seq):
    tm = x_ref.shape[0]
    d = x_ref.shape[1]
    t0 = (pl.program_id(0) * tm) % seq
    H = POOL_HALO

    halo = jnp.where(t0 > 0, halo_ref[...], 0.0)
    pos = t0 + lax.broadcasted_iota(jnp.int32, (tm, B_GROUP_DIM), 0)
    mixed = []
    for gi, win in enumerate(B_WINDOWS):
        cols = slice(gi * B_GROUP_DIM, (gi + 1) * B_GROUP_DIM)
        cur = ub_ref[:, cols]
        acc = jnp.concatenate([halo[:, cols], cur], axis=0)
        step = 1
        while step < win:
            acc = acc + pltpu.roll(acc, step, axis=0)
            step *= 2
        cnt = jnp.minimum(pos + 1, win).astype(F32)
        pooled = acc[H:, :] / cnt - cur
        mixed.append(_dot(pooled.astype(BF16), pw_ref[gi]))
    ob = jnp.concatenate(mixed, axis=-1) * ps_ref[...]

    merged = gt_ref[:, 0:d].astype(F32) * _dot(oa_ref[...], wb_ref[0])
    merged = merged + gt_ref[:, d:2 * d].astype(F32) * _dot(ob.astype(BF16), wb_ref[1])
    merged = merged + gt_ref[:, 2 * d:3 * d].astype(F32) * _dot(oc_ref[...], wb_ref[2])
    x1 = x_ref[...] + _dot(merged.astype(BF16), wo_ref[...])

    xb = _rms(x1, gx_ref[...]).astype(BF16)
    qn = _seg_norm(_dot(xb, wq_ref[...]), qg_ref[...], X_HEAD_DIM).astype(BF16)
    def attend(bounded):
        heads = []
        for h in range(X_HEADS):
            cols = slice(h * X_HEAD_DIM, (h + 1) * X_HEAD_DIM)
            s = _dot_nt(qn[:, cols], km_ref[:, cols])
            if bounded:
                pb = jnp.exp(s).astype(BF16)
                inv = 1.0 / _dot(pb, jnp.ones((s.shape[1], X_HEAD_DIM), BF16))
                heads.append(_dot(pb, vm_ref[:, cols]) * inv)
            else:
                p = jnp.exp(s - jnp.max(s, axis=-1, keepdims=True))
                inv = 1.0 / jnp.sum(p, axis=-1, keepdims=True)
                heads.append(_dot(p.astype(BF16), vm_ref[:, cols]) * inv)
        o = jnp.concatenate(heads, axis=-1).astype(BF16)
        o_ref[...] = x1 + _dot(o, wxo_ref[...])

    @pl.when(flag_ref[0] == 1)
    def _():
        attend(True)

    @pl.when(flag_ref[0] != 1)
    def _():
        attend(False)


def _merge_cross(x, oa, ub, oc, gates, pool_w, pool_scale, w_branch, w_out,
                 g_cross, w_xq, q_gain, k_mem, v_mem, w_xo, flag, seq, mem_len):
    n, d = x.shape
    tm = min(ROW_TILE, seq)
    row = lambda width: pl.BlockSpec((tm, width), lambda i: (i, 0))
    halo = pl.BlockSpec((POOL_HALO, WIDTH), lambda i: (jnp.maximum(i * (tm // POOL_HALO) - 1, 0), 0))
    mem = pl.BlockSpec((mem_len, WIDTH), lambda i: ((i * tm) // seq, 0))
    kern = functools.partial(_merge_cross_kernel, seq=seq)
    return pl.pallas_call(
        kern,
        grid=(n // tm,),
        in_specs=[_smem_spec(), row(d), row(WIDTH), row(WIDTH), halo, row(WIDTH), row(gates.shape[1]),
                  _const_spec(pool_w.shape), _const_spec(pool_scale.shape), _const_spec(w_branch.shape),
                  _const_spec(w_out.shape), _const_spec(g_cross.shape), _const_spec(w_xq.shape),
                  _const_spec(q_gain.shape), mem, mem, _const_spec(w_xo.shape)],
        out_specs=row(d),
        out_shape=jax.ShapeDtypeStruct((n, d), F32),
        compiler_params=_params(("parallel",)),
        name="merge_cross",
    )(flag, x, oa, ub, ub, oc, gates, pool_w, pool_scale, w_branch, w_out,
      g_cross, w_xq, q_gain, k_mem, v_mem, w_xo)


def _swiglu_chunks(xb, w1_ref, w3_ref, w2_ref, acc):
    for c in range(0, w1_ref.shape[1], FF_CHUNK):
        h1 = _dot(xb, w1_ref[:, c:c + FF_CHUNK])
        h3 = _dot(xb, w3_ref[:, c:c + FF_CHUNK])
        acc = acc + _dot((jax.nn.silu(h1) * h3).astype(BF16), w2_ref[c:c + FF_CHUNK, :])
    return acc


def _ffn_kernel(x_ref, g_ref, w1_ref, w3_ref, w2_ref, o_ref):
    x = x_ref[...]
    xb = _rms(x, g_ref[...]).astype(BF16)
    o_ref[...] = _swiglu_chunks(xb, w1_ref, w3_ref, w2_ref, x)


def _ffn(x, g, w1, w3, w2):
    n, d = x.shape
    tm = min(ROW_TILE, n)
    row = pl.BlockSpec((tm, d), lambda i: (i, 0))
    return pl.pallas_call(
        _ffn_kernel,
        grid=(n // tm,),
        in_specs=[row, _const_spec(g.shape), _const_spec(w1.shape), _const_spec(w3.shape),
                  _const_spec(w2.shape)],
        out_specs=row,
        out_shape=jax.ShapeDtypeStruct((n, d), F32),
        compiler_params=_params(("parallel",)),
        name="ffn_dense",
    )(x, g, w1, w3, w2)


def _store_token_tiles(ref, x):
    rows = x.shape[0]
    for k in range(x.shape[1] // 128):
        ref[pl.ds(k, rows, stride=SUBLANES), :] = x[:, k * 128:(k + 1) * 128]


def _load_token_tiles(ref, rows, lead=None):
    idx = (lambda k: (pl.ds(k, rows, stride=SUBLANES), slice(None))) if lead is None else \
        (lambda k: (lead, pl.ds(k, rows, stride=SUBLANES), slice(None)))
    return jnp.concatenate([ref[idx(k)] for k in range(SUBLANES)], axis=-1)


ROUTE_EXPERT, ROUTE_GATE, ROUTE_RANK = 0, TOP_K, 2 * TOP_K


def _router_kernel(x_ref, g_ref, r_ref, rt_ref, cnt_ref, carry_ref):
    i = pl.program_id(0)
    tm = x_ref.shape[0]
    xn = _rms(x_ref[...], g_ref[...])
    r = r_ref[...]
    x_hi = xn.astype(BF16)
    x_lo = (xn - x_hi.astype(F32)).astype(BF16)
    r_hi = r.astype(BF16)
    r_lo = (r - r_hi.astype(F32)).astype(BF16)
    logits = _dot(x_hi, r_hi) + (_dot(x_lo, r_hi) + _dot(x_hi, r_lo))
    lane = lax.broadcasted_iota(jnp.int32, logits.shape, 1)
    lg = jnp.where(lane < N_EXPERTS, logits, -jnp.inf)
    m1 = jnp.max(lg, axis=-1, keepdims=True)
    e1 = jnp.min(jnp.where(lg == m1, lane, lg.shape[1]), axis=-1, keepdims=True)
    lg = jnp.where(lane == e1, -jnp.inf, lg)
    m2 = jnp.max(lg, axis=-1, keepdims=True)
    e2 = jnp.min(jnp.where(lg == m2, lane, lg.shape[1]), axis=-1, keepdims=True)
    ex = jnp.exp(m2 - m1)
    gate1 = 1.0 / (1.0 + ex)
    gate2 = ex * gate1

    @pl.when(i == 0)
    def _():
        carry_ref[...] = jnp.zeros(carry_ref.shape, F32)

    hot1 = (lane == e1).astype(F32)
    hot2 = (lane == e2).astype(F32)
    hot = hot1 + hot2
    before = lax.broadcasted_iota(jnp.int32, (tm, tm), 1) < lax.broadcasted_iota(jnp.int32, (tm, tm), 0)
    seen = _dot(before.astype(BF16), hot.astype(BF16)) + carry_ref[...]
    rank1 = jnp.sum(seen * hot1, axis=-1, keepdims=True)
    rank2 = jnp.sum(seen * hot2, axis=-1, keepdims=True)
    carry_ref[...] += jnp.sum(hot, axis=0, keepdims=True)
    cnt_ref[...] = carry_ref[...]
    cols = [e1.astype(F32), e2.astype(F32), gate1, gate2, rank1, rank2]
    table = jnp.zeros(logits.shape, F32)
    for c, v in enumerate(cols):
        table = jnp.where(lane == c, v, table)
    rt_ref[...] = table


def _router(x, g, router_padded):
    n, d = x.shape
    tm = min(ROW_TILE, n)
    width = router_padded.shape[1]
    row = lambda w: pl.BlockSpec((tm, w), lambda i: (i, 0))
    return pl.pallas_call(
        _router_kernel,
        grid=(n // tm,),
        in_specs=[row(d), _const_spec(g.shape), _const_spec(router_padded.shape)],
        out_specs=[row(width), pl.BlockSpec((1, width), lambda i: (0, 0))],
        out_shape=[jax.ShapeDtypeStruct((n, width), F32), jax.ShapeDtypeStruct((1, width), F32)],
        scratch_shapes=[pltpu.VMEM((1, width), F32)],
        compiler_params=_params(("arbitrary",)),
        name="router",
    )(x, g, router_padded)


def _token_tile(ref, first_row):
    return ref.at[pl.ds(pl.multiple_of(first_row, SUBLANES), SUBLANES), :]


def _dispatch_kernel(pad_ref, dst_ref, x_ref, g_ref, xs_hbm, xt_ref, zero_ref, sem, zsem):
    i = pl.program_id(0)
    n_steps = pl.num_programs(0)
    tm = x_ref.shape[0]
    T = SUBLANES
    slot = i % 2

    def wait_step(s):
        for _ in range(TOP_K):
            pltpu.make_async_copy(xt_ref.at[s], xs_hbm.at[pl.ds(0, tm * T), :], sem.at[s]).wait()

    @pl.when(i == 0)
    def _():
        zero_ref[...] = jnp.zeros(zero_ref.shape, F32)
        for e in range(N_EXPERTS + 1):
            first = pad_ref[0, e]
            tokens = 1 if e < N_EXPERTS else ZERO_TOKENS
            src = zero_ref.at[pl.ds(0, tokens * T), :]

            def dst(r):
                return xs_hbm.at[pl.ds(pl.multiple_of((first + r * tokens) * T, T), tokens * T), :]

            def fill(r, carry):
                pltpu.make_async_copy(src, dst(r), zsem).start()
                return carry

            def drain(r, carry):
                pltpu.make_async_copy(src, dst(0), zsem).wait()
                return carry

            lax.fori_loop(0, pad_ref[1, e] // tokens, fill, 0)
            lax.fori_loop(0, pad_ref[1, e] // tokens, drain, 0)

    _store_token_tiles(xt_ref.at[slot], _rms(x_ref[...], g_ref[...]))

    for choice in range(TOP_K):
        def body(gi, carry):
            for k in range(DMA_UNROLL):
                tok = gi * DMA_UNROLL + k
                pltpu.make_async_copy(_token_tile(xt_ref.at[slot], tok * T),
                                      _token_tile(xs_hbm, dst_ref[0, 0, choice * tm + tok]),
                                      sem.at[slot]).start(priority=k % 2)
            return carry

        lax.fori_loop(0, tm // DMA_UNROLL, body, 0)

    @pl.when(i >= 1)
    def _():
        wait_step(1 - slot)

    @pl.when(i == n_steps - 1)
    def _():
        wait_step(slot)


def _dispatch(x, g, dest_blocks, pad_info, p):
    n, d = x.shape
    tm = dest_blocks.shape[2] // TOP_K
    T = SUBLANES
    grid_spec = pltpu.PrefetchScalarGridSpec(
        num_scalar_prefetch=1,
        grid=(n // tm,),
        in_specs=[pl.BlockSpec((1, 1, TOP_K * tm), lambda i, pad: (i, 0, 0), memory_space=pltpu.SMEM),
                  pl.BlockSpec((tm, d), lambda i, pad: (i, 0)),
                  pl.BlockSpec(g.shape, lambda i, pad: (0, 0))],
        out_specs=pl.BlockSpec(memory_space=pl.ANY),
        scratch_shapes=[pltpu.VMEM((2, tm * T, d // T), F32), pltpu.VMEM((ZERO_TOKENS * T, d // T), F32),
                        pltpu.SemaphoreType.DMA((2,)), pltpu.SemaphoreType.DMA],
    )
    return pl.pallas_call(
        _dispatch_kernel,
        grid_spec=grid_spec,
        out_shape=jax.ShapeDtypeStruct((p * T, d // T), F32),
        compiler_params=_params(("arbitrary",)),
        name="dispatch",
    )(pad_info, dest_blocks, x, g)


def _expert_kernel(be_ref, nu_ref, x_ref, w1_ref, w3_ref, w2_ref, o_ref, xb_ref, y_ref):
    b = pl.program_id(0)
    c = pl.program_id(1)
    tb = xb_ref.shape[0]

    @pl.when(b < nu_ref[0])
    def _():
        @pl.when(c == 0)
        def _():
            xb_ref[...] = _load_token_tiles(x_ref, tb).astype(BF16)
            y_ref[...] = jnp.zeros(y_ref.shape, F32)

        xb = xb_ref[...]
        h1 = _dot(xb, w1_ref[...].astype(BF16))
        h3 = _dot(xb, w3_ref[...].astype(BF16))
        y_ref[...] += _dot((jax.nn.silu(h1) * h3).astype(BF16), w2_ref[...].astype(BF16))

        @pl.when(c == pl.num_programs(1) - 1)
        def _():
            _store_token_tiles(o_ref, y_ref[...])

    @pl.when((b >= nu_ref[0]) & (c == pl.num_programs(1) - 1))
    def _():
        o_ref[...] = jnp.zeros(o_ref.shape, F32)


def _experts(xs, block_e, n_used, w1, w3, w2, tb):
    T = SUBLANES
    d = w1.shape[1]
    nc = w1.shape[2] // FF_CHUNK
    nblk = xs.shape[0] // (tb * T)
    blk = lambda b, nu: jnp.minimum(b, nu[0] - 1)
    chunk = lambda b, c, nu: jnp.where(b < nu[0], c, nc - 1)
    grid_spec = pltpu.PrefetchScalarGridSpec(
        num_scalar_prefetch=2,
        grid=(nblk, nc),
        in_specs=[pl.BlockSpec((tb * T, d // T), lambda b, c, be, nu: (blk(b, nu), 0)),
                  pl.BlockSpec((None, d, FF_CHUNK), lambda b, c, be, nu: (be[b], 0, chunk(b, c, nu))),
                  pl.BlockSpec((None, d, FF_CHUNK), lambda b, c, be, nu: (be[b], 0, chunk(b, c, nu))),
                  pl.BlockSpec((None, FF_CHUNK, d), lambda b, c, be, nu: (be[b], chunk(b, c, nu), 0))],
        out_specs=pl.BlockSpec((tb * T, d // T), lambda b, c, be, nu: (b, 0)),
        scratch_shapes=[pltpu.VMEM((tb, d), BF16), pltpu.VMEM((tb, d), F32)],
    )
    return pl.pallas_call(
        _expert_kernel,
        grid_spec=grid_spec,
        out_shape=jax.ShapeDtypeStruct(xs.shape, F32),
        compiler_params=_params(("arbitrary", "arbitrary")),
        name="experts",
    )(block_e, n_used, xs, w1, w3, w2)


def _combine_kernel(dst_ref, dstn_ref, x_ref, g_ref, ys_hbm, o_ref, yg_ref, sem):
    i = pl.program_id(0)
    n_steps = pl.num_programs(0)
    tm = x_ref.shape[0]
    T = SUBLANES
    slot = i % 2

    def gather(idx_ref, s):
        def body(gi, carry):
            for k in range(DMA_UNROLL):
                r = gi * DMA_UNROLL + k
                pltpu.make_async_copy(_token_tile(ys_hbm, idx_ref[0, 0, r]), _token_tile(yg_ref.at[s], r * T),
                                      sem.at[s]).start(priority=k % 2)
            return carry
        lax.fori_loop(0, TOP_K * tm // DMA_UNROLL, body, 0)

    @pl.when(i == 0)
    def _():
        gather(dst_ref, 0)

    @pl.when(i + 1 < n_steps)
    def _():
        gather(dstn_ref, 1 - slot)

    pltpu.make_async_copy(ys_hbm.at[pl.ds(0, TOP_K * tm * T), :], yg_ref.at[slot], sem.at[slot]).wait()
    g = g_ref[...]
    y0 = _load_token_tiles(yg_ref, tm, lead=slot)
    y1 = jnp.concatenate([yg_ref[slot, pl.ds(tm * T + k, tm, stride=T), :] for k in range(T)], axis=-1)
    o_ref[...] = x_ref[...] + g[:, ROUTE_GATE:ROUTE_GATE + 1] * y0 + g[:, ROUTE_GATE + 1:ROUTE_GATE + 2] * y1


def _combine(x, ys, dest_blocks, gate):
    n, d = x.shape
    tm = dest_blocks.shape[2] // TOP_K
    T = SUBLANES
    n_steps = n // tm
    row = lambda width: pl.BlockSpec((tm, width), lambda i: (i, 0))
    idx = lambda imap: pl.BlockSpec((1, 1, TOP_K * tm), imap, memory_space=pltpu.SMEM)
    return pl.pallas_call(
        _combine_kernel,
        grid=(n_steps,),
        in_specs=[idx(lambda i: (i, 0, 0)), idx(lambda i: (jnp.minimum(i + 1, n_steps - 1), 0, 0)),
                  row(d), row(gate.shape[1]), pl.BlockSpec(memory_space=pl.ANY)],
        out_specs=row(d),
        out_shape=jax.ShapeDtypeStruct((n, d), F32),
        scratch_shapes=[pltpu.VMEM((2, TOP_K * tm * T, d // T), F32), pltpu.SemaphoreType.DMA((2,))],
        compiler_params=_params(("arbitrary",)),
        name="combine",
    )(dest_blocks, dest_blocks, x, gate, ys)


def _moe(x, g, router, w1, w3, w2):
    n, d = x.shape
    a = n * TOP_K
    tb = min(MOE_TILE, a)
    router_padded = jnp.pad(router, ((0, 0), (0, 128 - N_EXPERTS)))
    table, cnt = _router(x, g, router_padded)
    top_e = table[:, ROUTE_EXPERT:ROUTE_EXPERT + TOP_K].astype(jnp.int32)
    rank = table[:, ROUTE_RANK:ROUTE_RANK + TOP_K].astype(jnp.int32)
    counts = cnt[0, :N_EXPERTS].astype(jnp.int32)
    padded = (counts + tb - 1) // tb * tb
    pend = jnp.cumsum(padded)
    pstart = pend - padded
    onehot = (top_e[:, :, None] == jnp.arange(N_EXPERTS)[None, None, :]).astype(jnp.int32)
    dest = jnp.sum(onehot * pstart[None, None, :], axis=-1) + rank
    n_blocks = a // tb + N_EXPERTS
    p = n_blocks * tb
    block_e = jnp.minimum(jnp.sum(pend[None, :] <= (jnp.arange(n_blocks) * tb)[:, None], axis=1),
                          N_EXPERTS - 1).astype(jnp.int32)
    n_used = (pend[-1] // tb).astype(jnp.int32).reshape(1)
    tm = min(ROW_TILE, n)
    dest_blocks = (dest * SUBLANES).reshape(n // tm, tm, TOP_K).transpose(0, 2, 1).reshape(n // tm, 1, TOP_K * tm)
    pad_info = jnp.stack([jnp.append(pstart + counts, pend[-1]),
                          jnp.append(padded - counts, p - pend[-1])]).astype(jnp.int32)
    xs = _dispatch(x, g, dest_blocks, pad_info, p)
    ys = _experts(xs, block_e, n_used, w1, w3, w2, tb)
    return _combine(x, ys, dest_blocks, table)


def _tile_gain(gain, width, scale=1.0):
    return jnp.tile(gain, width // gain.shape[0]) * scale


def kernel(x, mem, rel_bias, norm_mix, w_in, b_gate, a_q_gain, a_k_gain, pool_w, pool_scale,
           c_q_gain, c_k_gain, c_sinks, w_branch, w_out, norm_cross, norm_mem, w_xq, w_xk, w_xv,
           x_q_gain, x_k_gain, w_xo, norm_ffn, ffn_w1, ffn_w3, ffn_w2, router, moe_w1, moe_w3, moe_w2):
    batch, seq, d = x.shape
    mem_len = mem.shape[1]
    depth = norm_mix.shape[0]
    xs = x.reshape(batch * seq, d)
    mems = mem.reshape(batch * mem_len, d)

    tab_a = rel_bias[:, :A_HEADS].T
    tab_c = rel_bias[:, A_HEADS:].T
    nb = seq // A_BLOCK
    n_near = min(nb, (_saturation_distance() + 2 * A_BLOCK - 2) // A_BLOCK)
    far_a = tab_a[:, REL_BUCKETS - 1] * LOG2E
    bias_a = _moba_bias_tiles(_bias_by_distance(tab_a, n_near * A_BLOCK), n_near) * LOG2E
    bias_a = jnp.concatenate([bias_a, jnp.broadcast_to(far_a[:, None, None, None],
                                                       (A_HEADS, 1, A_BLOCK, A_BLOCK))], axis=1)
    bias_a = jnp.concatenate([bias_a[0::2], bias_a[1::2]], axis=-1)
    bias_c = _swa_bias_tiles(_bias_by_distance(tab_c, C_WINDOW)) * LOG2E
    seg = np.arange(256) // HEAD_DIM
    bd = jnp.asarray(seg[:, None] == seg[None, :], BF16)

    row = lambda v: v.reshape(1, -1)
    for l in range(depth):
        scale = HEAD_DIM ** -0.5 * LOG2E
        gains = jnp.stack([_tile_gain(a_q_gain[l], WIDTH, scale), _tile_gain(a_k_gain[l], WIDTH),
                           _tile_gain(c_q_gain[l], WIDTH, scale), _tile_gain(c_k_gain[l], WIDTH)])
        flag_a = _bounded_flag(a_q_gain[l], a_k_gain[l], HEAD_DIM, jnp.max(jnp.abs(tab_a)))
        flag_c = _bounded_flag(c_q_gain[l], c_k_gain[l], HEAD_DIM,
                               jnp.maximum(jnp.max(jnp.abs(tab_c)), jnp.max(jnp.abs(c_sinks[l]))))
        qa, ka, va, ub, qc, kc, vc, gates = _in_proj(
            xs, row(norm_mix[l]), w_in[l].astype(BF16), row(b_gate[l]), gains, bd)
        oa = _moba(qa, ka, va, bias_a, far_a, flag_a, batch, seq)
        oc = _swa(qc, kc, vc, bias_c, c_sinks[l] * LOG2E, flag_c, batch, seq)
        k_mem, v_mem = _mem_kv(mems, row(norm_mem[l]), w_xk[l].astype(BF16), w_xv[l].astype(BF16),
                               row(_tile_gain(x_k_gain[l], WIDTH)))
        xs = _merge_cross(xs, oa, ub, oc, gates, pool_w[l].astype(BF16), row(pool_scale[l]),
                          w_branch[l].astype(BF16), w_out[l].astype(BF16), row(norm_cross[l]),
                          w_xq[l].astype(BF16), row(_tile_gain(x_q_gain[l], WIDTH, X_HEAD_DIM ** -0.5)),
                          k_mem, v_mem, w_xo[l].astype(BF16),
                          _bounded_flag(x_q_gain[l], x_k_gain[l], X_HEAD_DIM, 0.0), seq, mem_len)
        i = l // 2
        if l % 2 == 0:
            xs = _ffn(xs, row(norm_ffn[l]), ffn_w1[i].astype(BF16), ffn_w3[i].astype(BF16),
                      ffn_w2[i].astype(BF16))
        else:
            xs = _moe(xs, row(norm_ffn[l]), router[i], moe_w1[i], moe_w3[i], moe_w2[i])
    return xs.reshape(batch, seq, d)
```

```python
import functools
import math

import jax
import jax.numpy as jnp
import numpy as np
from jax import lax
from jax.experimental import pallas as pl
from jax.experimental.pallas import tpu as pltpu

F32 = jnp.float32
BF16 = jnp.bfloat16

HEAD_DIM = 64
A_HEADS = 8
A_BLOCK = 256
A_TOPK = 3
B_GROUPS = 4
B_GROUP_DIM = 128
B_WINDOWS = (2, 4, 8, 16)
C_HEADS = 8
C_KV_HEADS = 2
C_WINDOW = 128
REL_BUCKETS = 32
REL_MAX_DIST = 1024
X_HEADS = 4
X_HEAD_DIM = 128
N_EXPERTS = 8
TOP_K = 2
EPS = 1e-6
NEG_INF = -1e30
LOG2E = math.log2(math.e)

WIDTH = 512
POOL_HALO = 16
ROW_TILE = 512
MOE_TILE = 1024
FF_CHUNK = 512
SUBLANES = 8
DMA_UNROLL = 8
ZERO_TOKENS = 8
KEY_CHUNK = 128
MOBA_TRIP = 4
QK_AHEAD = 3
SWA_BLOCKS = 4
VMEM_LIMIT = 56 * 1024 * 1024
EXP2_SAFE = 100.0


def _dot(a, b):
    return jnp.dot(a, b, preferred_element_type=F32)


def _dot_nt(a, b):
    return lax.dot_general(a, b, (((1,), (1,)), ((), ())), preferred_element_type=F32)


def _rms(x, g):
    ms = jnp.mean(x * x, axis=-1, keepdims=True)
    return x * lax.rsqrt(ms + EPS) * g


def _const_spec(shape):
    zeros = (0,) * len(shape)
    return pl.BlockSpec(shape, lambda *_: zeros, pipeline_mode=pl.Buffered(1))


def _smem_spec():
    return pl.BlockSpec(memory_space=pltpu.SMEM)


def _params(sem):
    return pltpu.CompilerParams(dimension_semantics=sem, vmem_limit_bytes=VMEM_LIMIT)


def _rel_bucket(dist):
    n = jnp.maximum(dist, 0)
    max_exact = REL_BUCKETS // 2
    nf = jnp.maximum(n, 1).astype(jnp.float32)
    large = max_exact + (jnp.log(nf / max_exact) / math.log(REL_MAX_DIST / max_exact)
                         * (REL_BUCKETS - max_exact)).astype(jnp.int32)
    large = jnp.minimum(large, REL_BUCKETS - 1)
    return jnp.where(n < max_exact, n, large)


def _saturation_distance():
    ratio = REL_MAX_DIST / (REL_BUCKETS // 2)
    return int(math.ceil((REL_BUCKETS // 2) * ratio ** ((REL_BUCKETS - 1 - REL_BUCKETS // 2 + 0.5)
                                                       / (REL_BUCKETS - REL_BUCKETS // 2))))


def _bias_by_distance(tab, n_dist):
    onehot = (_rel_bucket(jnp.arange(n_dist))[:, None] == jnp.arange(REL_BUCKETS)[None, :]).astype(F32)
    return jnp.einsum('hb,db->hd', tab, onehot, precision=lax.Precision.HIGHEST)


def _skew(g, rows):
    n = g.shape[-1]
    lead = g.shape[:-1]
    tiled = jnp.broadcast_to(g[..., None, :], lead + (rows, n)).reshape(lead + (rows * n,))
    return tiled[..., :rows * (n - 1)].reshape(lead + (rows, n - 1))


def _moba_bias_tiles(bvec, n_near):
    L = A_BLOCK
    rows = []
    for delta in range(n_near):
        lo = delta * L - (L - 1)
        seg = bvec[:, max(lo, 0):delta * L + L]
        if lo < 0:
            seg = jnp.concatenate([jnp.zeros((bvec.shape[0], -lo), F32), seg], axis=1)
        rows.append(jnp.pad(seg, ((0, 0), (0, 1))))
    g = jnp.stack(rows, axis=1)
    return _skew(g, L)[..., L - 1:]


def _swa_bias_tiles(bvec):
    W = C_WINDOW
    u = np.arange(3 * W)
    g = bvec[:, np.clip(2 * W - 1 - u, 0, W - 1)]
    return _skew(g, W)[..., W - 1:3 * W - 1]


def _bounded_flag(q_gain, k_gain, head_dim, extra):
    bound = head_dim ** 0.5 * jnp.max(jnp.abs(q_gain)) * jnp.max(jnp.abs(k_gain)) * 1.02 + extra
    return (bound * LOG2E < EXP2_SAFE).astype(jnp.int32).reshape(1)


def _in_proj_kernel(x_ref, g_ref, w_ref, bg_ref, gn_ref, bd_ref,
                    qa_ref, ka_ref, va_ref, ub_ref, qc_ref, kc_ref, vc_ref, gt_ref):
    xb = _rms(x_ref[...], g_ref[...]).astype(BF16)
    bd = bd_ref[...]

    def proj(c0, width):
        return _dot(xb, w_ref[:, c0:c0 + width])

    def head_norm(t, gain):
        outs = []
        for c in range(0, t.shape[1], 256):
            wd = min(256, t.shape[1] - c)
            tc = t[:, c:c + wd]
            ss = _dot((tc * tc).astype(BF16), bd[:wd, :wd])
            outs.append(tc * lax.rsqrt(ss * (1.0 / HEAD_DIM) + EPS))
        y = outs[0] if len(outs) == 1 else jnp.concatenate(outs, axis=-1)
        return y * gain

    def tile_kv_heads(t):
        lane = lax.broadcasted_iota(jnp.int32, t.shape, 1)
        r = pltpu.roll(t, HEAD_DIM, axis=1)
        h0 = jnp.where(lane < HEAD_DIM, t, r)
        h1 = jnp.where(lane < HEAD_DIM, r, t)
        return jnp.concatenate([h0, h0, h1, h1], axis=-1)

    qa_ref[...] = head_norm(proj(0, WIDTH), gn_ref[0:1, :]).astype(BF16)
    ka_ref[...] = head_norm(proj(WIDTH, WIDTH), gn_ref[1:2, :]).astype(BF16)
    va_ref[...] = proj(2 * WIDTH, WIDTH).astype(BF16)
    ub_ref[...] = proj(3 * WIDTH, WIDTH)
    qc_ref[...] = head_norm(proj(4 * WIDTH, WIDTH), gn_ref[2:3, :]).astype(BF16)
    kv = C_KV_HEADS * HEAD_DIM
    c0 = 5 * WIDTH
    kc = head_norm(proj(c0, kv), gn_ref[3:4, :kv])
    kc_ref[...] = tile_kv_heads(kc).astype(BF16)
    vc_ref[...] = tile_kv_heads(proj(c0 + kv, kv)).astype(BF16)
    c0 += 2 * kv
    for c in range(0, gt_ref.shape[1], WIDTH):
        gl = proj(c0 + c, WIDTH) + bg_ref[:, c:c + WIDTH]
        gt_ref[:, c:c + WIDTH] = jax.nn.sigmoid(gl).astype(BF16)


def _in_proj(x, g, w_in, b_gate, gains, bd):
    n, d = x.shape
    tm = min(ROW_TILE, n)
    n_gate = b_gate.shape[1]
    row = lambda width: pl.BlockSpec((tm, width), lambda i: (i, 0))
    out_shape = [jax.ShapeDtypeStruct((n, WIDTH), BF16)] * 3 + [jax.ShapeDtypeStruct((n, WIDTH), F32)] \
        + [jax.ShapeDtypeStruct((n, WIDTH), BF16)] * 3 + [jax.ShapeDtypeStruct((n, n_gate), BF16)]
    return pl.pallas_call(
        _in_proj_kernel,
        grid=(n // tm,),
        in_specs=[row(d), _const_spec(g.shape), _const_spec(w_in.shape), _const_spec(b_gate.shape),
                  _const_spec(gains.shape), _const_spec(bd.shape)],
        out_specs=[row(WIDTH)] * 7 + [row(n_gate)],
        out_shape=out_shape,
        compiler_params=_params(("parallel",)),
        name="in_proj",
    )(x, g, w_in, b_gate, gains, bd)


def _moba_kernel(flag_ref, far_ref, q_ref, k_ref, v_ref, bias_ref, o_ref,
                 kmean_ref, vt_ref, sel_ref, m_ref, l_ref, l8_ref, acc_ref, s_ref, *, nb, n_near):
    i = pl.program_id(1)
    L = A_BLOCK
    KC = KEY_CHUNK
    PW = 2 * HEAD_DIM
    nbp = kmean_ref.shape[0]

    @pl.when(i == 0)
    def _():
        kmean_ref[...] = jnp.zeros(kmean_ref.shape, F32)
        for j in range(nb):
            kj = k_ref[j * L:(j + 1) * L, :].astype(F32)
            kmean_ref[j:j + 1, :] = jnp.mean(kj, axis=0, keepdims=True)
            vt_ref[j] = v_ref[j * L:(j + 1) * L, :].astype(F32).T.astype(BF16)

    lane = lax.broadcasted_iota(jnp.int32, (L, PW), 1)
    blk = lax.broadcasted_iota(jnp.int32, (nbp, L), 0)
    past = blk < i
    qh = []
    for h in range(A_HEADS):
        pair = slice((h // 2) * PW, (h // 2 + 1) * PW)
        q = q_ref[:, pair]
        qm = jnp.where(lane // HEAD_DIM == h % 2, q, jnp.zeros_like(q))
        qh.append(qm)
        km = kmean_ref[:, pair]
        km_hi = km.astype(BF16)
        km_lo = (km - km_hi.astype(F32)).astype(BF16)
        s = jnp.where(past, _dot_nt(km_hi, qm) + _dot_nt(km_lo, qm), -jnp.inf)
        rank = jnp.zeros((nbp, L), jnp.int32)
        for jp in range(nb):
            sj = s[jp:jp + 1, :]
            ahead = (sj > s) | ((sj == s) & (jp < blk))
            rank = rank + ahead.astype(jnp.int32)
        sel_ref[h] = (past & (rank < A_TOPK)).astype(F32)
        m_ref[h] = jnp.full((1, L), NEG_INF, F32)
        l_ref[h] = jnp.zeros((1, L), F32)
        l8_ref[h] = jnp.zeros((8, L), F32)
    acc_ref[...] = jnp.zeros(acc_ref.shape, F32)
    qpair_t = [jnp.concatenate(qh[2 * hp:2 * hp + 2], axis=0).astype(F32).T.astype(BF16)
               for hp in range(A_HEADS // 2)]

    def causal(c, n):
        kk = c + lax.broadcasted_iota(jnp.int32, (n, L), 0)
        qq = lax.broadcasted_iota(jnp.int32, (n, L), 1)
        return kk <= qq

    def bounded_blocks(blocks):
        units = [(bi, hp, c) for bi in range(len(blocks)) for hp in range(A_HEADS // 2) for c in range(0, L, KC)]

        def qk(u):
            bi, hp, c = units[u]
            kc = k_ref[pl.ds(pl.multiple_of(blocks[bi][0] * L + c, KC), KC), hp * PW:(hp + 1) * PW]
            s_ref[u % (QK_AHEAD + 1)] = _dot(kc, qpair_t[hp])

        for u in range(QK_AHEAD):
            qk(u)
        o = ps = None
        for u, (bi, hp, c) in enumerate(units):
            j, tile_idx, own = blocks[bi]
            if u + QK_AHEAD < len(units):
                qk(u + QK_AHEAD)
            if c == 0:
                o = [jnp.zeros((HEAD_DIM, L), F32)] * 2
                ps = jnp.zeros((8, 2 * L), F32)
            p = jnp.exp2(s_ref[u % (QK_AHEAD + 1)] + bias_ref[hp, tile_idx, pl.ds(c, KC), :])
            if own:
                p = jnp.where(jnp.concatenate([causal(c, KC)] * 2, axis=1), p, 0.0)
            ps = ps + jnp.sum(p.reshape(KC // 8, 8, 2 * L), axis=0)
            pb = p.astype(BF16)
            o = [o[hh] + _dot(vt_ref[j, pl.ds((2 * hp + hh) * HEAD_DIM, HEAD_DIM), pl.ds(c, KC)],
                              pb[:, hh * L:(hh + 1) * L]) for hh in range(2)]
            if c + KC < L:
                continue
            for hh in range(2):
                h = 2 * hp + hh
                rows = slice(h * HEAD_DIM, (h + 1) * HEAD_DIM)
                psh = ps[:, hh * L:(hh + 1) * L]
                if own:
                    acc_ref[rows, :] += o[hh]
                    l8_ref[h] += psh
                else:
                    w = sel_ref[h, pl.ds(j, 1), :]
                    acc_ref[rows, :] += w * o[hh]
                    l8_ref[h] += w * psh

    def run_bounded():
        def past(j):
            return (j, jnp.minimum(i - j, n_near), False)

        def body(t, carry):
            bounded_blocks([past(MOBA_TRIP * t + k) for k in range(MOBA_TRIP)])
            return carry

        lax.fori_loop(0, i // MOBA_TRIP, body, 0)
        for left in range(MOBA_TRIP):
            @pl.when(i % MOBA_TRIP == left)
            def _():
                bounded_blocks([past(i - left + k) for k in range(left)] + [(i, 0, True)])

    def online_tile(j, delta, far, own):
        for h in range(A_HEADS):
            pair = slice((h // 2) * PW, (h // 2 + 1) * PW)
            rows = slice(h * HEAD_DIM, (h + 1) * HEAD_DIM)
            s = _dot_nt(k_ref[pl.ds(pl.multiple_of(j * L, L), L), pair], qh[h])
            if not far:
                s = s + bias_ref[h // 2, delta, :, (h % 2) * L:(h % 2 + 1) * L]
            if own:
                s = jnp.where(causal(0, L), s, NEG_INF)
            mj = jnp.max(s, axis=0, keepdims=True)
            p = jnp.exp2(s - mj)
            lj = jnp.sum(p, axis=0, keepdims=True)
            o = _dot(vt_ref[j, rows, :], p.astype(BF16))
            if far:
                mj = mj + far_ref[h]
            m_old = m_ref[h]
            if own:
                m_new = jnp.maximum(m_old, mj)
                beta = jnp.exp2(mj - m_new)
            else:
                on = sel_ref[h, pl.ds(j, 1), :] > 0.5
                m_new = jnp.where(on, jnp.maximum(m_old, mj), m_old)
                beta = jnp.where(on, jnp.exp2(mj - m_new), 0.0)
            alpha = jnp.exp2(m_old - m_new)
            m_ref[h] = m_new
            l_ref[h] = alpha * l_ref[h] + beta * lj
            acc_ref[rows, :] = alpha * acc_ref[rows, :] + beta * o

    n_far = jnp.maximum(i - (n_near - 1), 0)

    def run(tile):
        def far_body(j, c):
            tile(j, None, True, False)
            return c

        def near_body(j, c):
            tile(j, i - j, False, False)
            return c

        lax.fori_loop(0, n_far, far_body, 0)
        lax.fori_loop(n_far, i, near_body, 0)
        tile(i, 0, False, True)

    @pl.when(flag_ref[0] == 1)
    def _():
        run_bounded()
        for h in range(A_HEADS):
            l_ref[h] = jnp.sum(l8_ref[h], axis=0, keepdims=True)

    @pl.when(flag_ref[0] != 1)
    def _():
        run(online_tile)

    o_t = jnp.concatenate([acc_ref[h * HEAD_DIM:(h + 1) * HEAD_DIM, :] / l_ref[h] for h in range(A_HEADS)],
                          axis=0)
    o_ref[...] = o_t.T.astype(BF16)


def _moba(q, k, v, bias_tiles, far, flag, batch, seq):
    n = q.shape[0]
    L = A_BLOCK
    nb = seq // L
    nbp = max(8, -(-nb // 8) * 8)
    n_near = bias_tiles.shape[1] - 1
    kern = functools.partial(_moba_kernel, nb=nb, n_near=n_near)
    return pl.pallas_call(
        kern,
        grid=(batch, nb),
        in_specs=[_smem_spec(), _smem_spec(),
                  pl.BlockSpec((L, WIDTH), lambda b, i: (b * nb + i, 0)),
                  pl.BlockSpec((seq, WIDTH), lambda b, i: (b, 0)),
                  pl.BlockSpec((seq, WIDTH), lambda b, i: (b, 0)),
                  _const_spec(bias_tiles.shape)],
        out_specs=pl.BlockSpec((L, WIDTH), lambda b, i: (b * nb + i, 0)),
        out_shape=jax.ShapeDtypeStruct((n, WIDTH), BF16),
        scratch_shapes=[pltpu.VMEM((nbp, WIDTH), F32),
                        pltpu.VMEM((nb, WIDTH, L), BF16),
                        pltpu.VMEM((A_HEADS, nbp, L), F32),
                        pltpu.VMEM((A_HEADS, 1, L), F32),
                        pltpu.VMEM((A_HEADS, 1, L), F32),
                        pltpu.VMEM((A_HEADS, 8, L), F32),
                        pltpu.VMEM((WIDTH, L), F32),
                        pltpu.VMEM((QK_AHEAD + 1, KEY_CHUNK, 2 * L), F32)],
        compiler_params=_params(("arbitrary", "arbitrary")),
        name="moba",
    )(flag, far, q, k, v, bias_tiles)


def _swa_kernel(flag_ref, sink_ref, q_ref, kp_ref, kc_ref, vp_ref, vc_ref, bias_ref, o_ref, s_ref):
    g = pl.program_id(0)
    chunk = pl.program_id(2)
    W = C_WINDOW
    G = C_HEADS // C_KV_HEADS
    nsub = q_ref.shape[0] // W
    gw = q_ref.shape[1]
    kall = jnp.concatenate([kp_ref[...], kc_ref[...]], axis=0)
    vall = jnp.concatenate([vp_ref[...], vc_ref[...]], axis=0)
    lane = lax.broadcasted_iota(jnp.int32, (W, gw), 1)
    qi = lax.broadcasted_iota(jnp.int32, (W, 2 * W), 0)
    kj = lax.broadcasted_iota(jnp.int32, (W, 2 * W), 1)
    dist = W + qi - kj
    band = (dist >= 0) & (dist < W)

    def block_mask(r):
        return band & ((chunk > 0) | (kj >= W)) if r == 0 else band

    def head_query(r, hh):
        q = q_ref[r * W:(r + 1) * W, :]
        return jnp.where(lane // HEAD_DIM == hh, q, jnp.zeros_like(q))

    def bounded_heads():
        units = [(r, hh) for r in range(nsub) for hh in range(G)]
        kall_t = kall.astype(F32).T.astype(BF16)

        def qk(u):
            r, hh = units[u]
            s_ref[u % (QK_AHEAD + 1)] = _dot(head_query(r, hh), kall_t[:, r * W:(r + 2) * W])

        for u in range(QK_AHEAD):
            qk(u)
        out = None
        for u, (r, hh) in enumerate(units):
            if u + QK_AHEAD < len(units):
                qk(u + QK_AHEAD)
            if hh == 0:
                out = jnp.zeros((W, gw), F32)
            s = s_ref[u % (QK_AHEAD + 1)] + bias_ref[hh]
            pb = jnp.where(block_mask(r), jnp.exp2(s), 0.0).astype(BF16)
            sink = jnp.exp2(jnp.full((1, 128), sink_ref[g * G + hh], F32))
            inv = 1.0 / (_dot(pb, jnp.ones((2 * W, 128), BF16)) + sink)
            o = _dot(pb, vall[r * W:(r + 2) * W]) * jnp.concatenate([inv] * (gw // 128), axis=-1)
            out = jnp.where(lane // HEAD_DIM == hh, o, out)
            if hh == G - 1:
                o_ref[r * W:(r + 1) * W, :] = out.astype(BF16)

    def online_heads():
        for r in range(nsub):
            kcat = kall[r * W:(r + 2) * W]
            vcat = vall[r * W:(r + 2) * W]
            out = jnp.zeros((W, gw), F32)
            for hh in range(G):
                s = jnp.where(block_mask(r), _dot_nt(head_query(r, hh), kcat) + bias_ref[hh], NEG_INF)
                sink = sink_ref[g * G + hh]
                m = jnp.maximum(jnp.max(s, axis=-1, keepdims=True), sink)
                p = jnp.exp2(s - m)
                den = jnp.sum(p, axis=-1, keepdims=True) + jnp.exp2(sink - m)
                o = _dot(p.astype(BF16), vcat) / den
                out = jnp.where(lane // HEAD_DIM == hh, o, out)
            o_ref[r * W:(r + 1) * W, :] = out.astype(BF16)

    @pl.when(flag_ref[0] == 1)
    def _():
        bounded_heads()

    @pl.when(flag_ref[0] != 1)
    def _():
        online_heads()


def _swa(q, k_t, v_t, bias_tiles, sinks, flag, batch, seq):
    n = q.shape[0]
    W = C_WINDOW
    G = C_HEADS // C_KV_HEADS
    gw = G * HEAD_DIM
    nsub = min(SWA_BLOCKS, seq // W)
    nchunk = seq // (W * nsub)
    cur = lambda g, b, j: (b * nchunk + j, g)
    prev = lambda g, b, j: (jnp.maximum((b * nchunk + j) * nsub - 1, 0), g)
    return pl.pallas_call(
        _swa_kernel,
        grid=(C_KV_HEADS, batch, nchunk),
        in_specs=[_smem_spec(), _smem_spec(),
                  pl.BlockSpec((nsub * W, gw), cur),
                  pl.BlockSpec((W, gw), prev), pl.BlockSpec((nsub * W, gw), cur),
                  pl.BlockSpec((W, gw), prev), pl.BlockSpec((nsub * W, gw), cur),
                  pl.BlockSpec((G, W, 2 * W), lambda g, b, j: (g, 0, 0))],
        out_specs=pl.BlockSpec((nsub * W, gw), cur),
        out_shape=jax.ShapeDtypeStruct((n, WIDTH), BF16),
        scratch_shapes=[pltpu.VMEM((QK_AHEAD + 1, W, 2 * W), F32)],
        compiler_params=_params(("arbitrary", "arbitrary", "arbitrary")),
        name="swa",
    )(flag, sinks, q, k_t, k_t, v_t, v_t, bias_tiles)


def _seg_norm(t, gain, seg):
    outs = []
    for c in range(0, t.shape[1], seg):
        tc = t[:, c:c + seg]
        outs.append(tc * lax.rsqrt(jnp.mean(tc * tc, axis=-1, keepdims=True) + EPS))
    return jnp.concatenate(outs, axis=-1) * gain


def _mem_kv_kernel(mem_ref, g_ref, wk_ref, wv_ref, kg_ref, k_ref, v_ref):
    mb = _rms(mem_ref[...], g_ref[...]).astype(BF16)
    k_ref[...] = _seg_norm(_dot(mb, wk_ref[...]), kg_ref[...], X_HEAD_DIM).astype(BF16)
    v_ref[...] = _dot(mb, wv_ref[...]).astype(BF16)


def _mem_kv(mem, g, w_xk, w_xv, k_gain):
    n, d = mem.shape
    tm = min(ROW_TILE, n)
    row = lambda width: pl.BlockSpec((tm, width), lambda i: (i, 0))
    return pl.pallas_call(
        _mem_kv_kernel,
        grid=(n // tm,),
        in_specs=[row(d), _const_spec(g.shape), _const_spec(w_xk.shape), _const_spec(w_xv.shape),
                  _const_spec(k_gain.shape)],
        out_specs=[row(WIDTH), row(WIDTH)],
        out_shape=[jax.ShapeDtypeStruct((n, WIDTH), BF16)] * 2,
        compiler_params=_params(("parallel",)),
        name="mem_kv",
    )(mem, g, w_xk, w_xv, k_gain)


def _merge_cross_kernel(x_ref, oa_ref, ub_ref, halo_ref, oc_ref, gt_ref, pw_ref, ps_ref, wb_ref, wo_ref,
                        gx_ref, wq_ref, qg_ref, km_ref, vm_ref, wxo_ref, o_ref, *, seq):
    tm = x_ref.shape[0]
    d = x_ref.shape[1]
    t0 = (pl.program_id(0) * tm) % seq
    H = POOL_HALO

    halo = jnp.where(t0 > 0, halo_ref[...], 0.0)
    pos = t0 + lax.broadcasted_iota(jnp.int32, (tm, B_GROUP_DIM), 0)
    mixed = []
    for gi, win in enumerate(B_WINDOWS):
        cols = slice(gi * B_GROUP_DIM, (gi + 1) * B_GROUP_DIM)
        cur = ub_ref[:, cols]
        acc = jnp.concatenate([halo[:, cols], cur], axis=0)
        step = 1
        while step < win:
            acc = acc + pltpu.roll(acc, step, axis=0)
            step *= 2
        cnt = jnp.minimum(pos + 1, win).astype(F32)
        pooled = acc[H:, :] / cnt - cur
        mixed.append(_dot(pooled.astype(BF16), pw_ref[gi]))
    ob = jnp.concatenate(mixed, axis=-1) * ps_ref[...]

    merged = gt_ref[:, 0:d].astype(F32) * _dot(oa_ref[...], wb_ref[0])
    merged = merged + gt_ref[:, d:2 * d].astype(F32) * _dot(ob.astype(BF16), wb_ref[1])
    merged = merged + gt_ref[:, 2 * d:3 * d].astype(F32) * _dot(oc_ref[...], wb_ref[2])
    x1 = x_ref[...] + _dot(merged.astype(BF16), wo_ref[...])

    xb = _rms(x1, gx_ref[...]).astype(BF16)
    qn = _seg_norm(_dot(xb, wq_ref[...]), qg_ref[...], X_HEAD_DIM).astype(BF16)
    heads = []
    for h in range(X_HEADS):
        cols = slice(h * X_HEAD_DIM, (h + 1) * X_HEAD_DIM)
        s = _dot_nt(qn[:, cols], km_ref[:, cols])
        p = jnp.exp(s - jnp.max(s, axis=-1, keepdims=True))
        inv = 1.0 / jnp.sum(p, axis=-1, keepdims=True)
        heads.append(_dot(p.astype(BF16), vm_ref[:, cols]) * inv)
    o = jnp.concatenate(heads, axis=-1).astype(BF16)
    o_ref[...] = x1 + _dot(o, wxo_ref[...])


def _merge_cross(x, oa, ub, oc, gates, pool_w, pool_scale, w_branch, w_out,
                 g_cross, w_xq, q_gain, k_mem, v_mem, w_xo, seq, mem_len):
    n, d = x.shape
    tm = min(ROW_TILE, seq)
    row = lambda width: pl.BlockSpec((tm, width), lambda i: (i, 0))
    halo = pl.BlockSpec((POOL_HALO, WIDTH), lambda i: (jnp.maximum(i * (tm // POOL_HALO) - 1, 0), 0))
    mem = pl.BlockSpec((mem_len, WIDTH), lambda i: ((i * tm) // seq, 0))
    kern = functools.partial(_merge_cross_kernel, seq=seq)
    return pl.pallas_call(
        kern,
        grid=(n // tm,),
        in_specs=[row(d), row(WIDTH), row(WIDTH), halo, row(WIDTH), row(gates.shape[1]),
                  _const_spec(pool_w.shape), _const_spec(pool_scale.shape), _const_spec(w_branch.shape),
                  _const_spec(w_out.shape), _const_spec(g_cross.shape), _const_spec(w_xq.shape),
                  _const_spec(q_gain.shape), mem, mem, _const_spec(w_xo.shape)],
        out_specs=row(d),
        out_shape=jax.ShapeDtypeStruct((n, d), F32),
        compiler_params=_params(("parallel",)),
        name="merge_cross",
    )(x, oa, ub, ub, oc, gates, pool_w, pool_scale, w_branch, w_out,
      g_cross, w_xq, q_gain, k_mem, v_mem, w_xo)


def _swiglu_chunks(xb, w1_ref, w3_ref, w2_ref, acc):
    for c in range(0, w1_ref.shape[1], FF_CHUNK):
        h1 = _dot(xb, w1_ref[:, c:c + FF_CHUNK])
        h3 = _dot(xb, w3_ref[:, c:c + FF_CHUNK])
        acc = acc + _dot((jax.nn.silu(h1) * h3).astype(BF16), w2_ref[c:c + FF_CHUNK, :])
    return acc


def _ffn_kernel(x_ref, g_ref, w1_ref, w3_ref, w2_ref, o_ref):
    x = x_ref[...]
    xb = _rms(x, g_ref[...]).astype(BF16)
    o_ref[...] = _swiglu_chunks(xb, w1_ref, w3_ref, w2_ref, x)


def _ffn(x, g, w1, w3, w2):
    n, d = x.shape
    tm = min(ROW_TILE, n)
    row = pl.BlockSpec((tm, d), lambda i: (i, 0))
    return pl.pallas_call(
        _ffn_kernel,
        grid=(n // tm,),
        in_specs=[row, _const_spec(g.shape), _const_spec(w1.shape), _const_spec(w3.shape),
                  _const_spec(w2.shape)],
        out_specs=row,
        out_shape=jax.ShapeDtypeStruct((n, d), F32),
        compiler_params=_params(("parallel",)),
        name="ffn_dense",
    )(x, g, w1, w3, w2)


def _store_token_tiles(ref, x):
    rows = x.shape[0]
    for k in range(x.shape[1] // 128):
        ref[pl.ds(k, rows, stride=SUBLANES), :] = x[:, k * 128:(k + 1) * 128]


def _load_token_tiles(ref, rows, lead=None):
    idx = (lambda k: (pl.ds(k, rows, stride=SUBLANES), slice(None))) if lead is None else \
        (lambda k: (lead, pl.ds(k, rows, stride=SUBLANES), slice(None)))
    return jnp.concatenate([ref[idx(k)] for k in range(SUBLANES)], axis=-1)


ROUTE_EXPERT, ROUTE_GATE, ROUTE_RANK = 0, TOP_K, 2 * TOP_K


def _router_kernel(x_ref, g_ref, r_ref, rt_ref, cnt_ref, carry_ref):
    i = pl.program_id(0)
    tm = x_ref.shape[0]
    xn = _rms(x_ref[...], g_ref[...])
    r = r_ref[...]
    x_hi = xn.astype(BF16)
    x_lo = (xn - x_hi.astype(F32)).astype(BF16)
    r_hi = r.astype(BF16)
    r_lo = (r - r_hi.astype(F32)).astype(BF16)
    logits = _dot(x_hi, r_hi) + (_dot(x_lo, r_hi) + _dot(x_hi, r_lo))
    lane = lax.broadcasted_iota(jnp.int32, logits.shape, 1)
    lg = jnp.where(lane < N_EXPERTS, logits, -jnp.inf)
    m1 = jnp.max(lg, axis=-1, keepdims=True)
    e1 = jnp.min(jnp.where(lg == m1, lane, lg.shape[1]), axis=-1, keepdims=True)
    lg = jnp.where(lane == e1, -jnp.inf, lg)
    m2 = jnp.max(lg, axis=-1, keepdims=True)
    e2 = jnp.min(jnp.where(lg == m2, lane, lg.shape[1]), axis=-1, keepdims=True)
    ex = jnp.exp(m2 - m1)
    gate1 = 1.0 / (1.0 + ex)
    gate2 = ex * gate1

    @pl.when(i == 0)
    def _():
        carry_ref[...] = jnp.zeros(carry_ref.shape, F32)

    hot1 = (lane == e1).astype(F32)
    hot2 = (lane == e2).astype(F32)
    hot = hot1 + hot2
    before = lax.broadcasted_iota(jnp.int32, (tm, tm), 1) < lax.broadcasted_iota(jnp.int32, (tm, tm), 0)
    seen = _dot(before.astype(BF16), hot.astype(BF16)) + carry_ref[...]
    rank1 = jnp.sum(seen * hot1, axis=-1, keepdims=True)
    rank2 = jnp.sum(seen * hot2, axis=-1, keepdims=True)
    carry_ref[...] += jnp.sum(hot, axis=0, keepdims=True)
    cnt_ref[...] = carry_ref[...]
    cols = [e1.astype(F32), e2.astype(F32), gate1, gate2, rank1, rank2]
    table = jnp.zeros(logits.shape, F32)
    for c, v in enumerate(cols):
        table = jnp.where(lane == c, v, table)
    rt_ref[...] = table


def _router(x, g, router_padded):
    n, d = x.shape
    tm = min(ROW_TILE, n)
    width = router_padded.shape[1]
    row = lambda w: pl.BlockSpec((tm, w), lambda i: (i, 0))
    return pl.pallas_call(
        _router_kernel,
        grid=(n // tm,),
        in_specs=[row(d), _const_spec(g.shape), _const_spec(router_padded.shape)],
        out_specs=[row(width), pl.BlockSpec((1, width), lambda i: (0, 0))],
        out_shape=[jax.ShapeDtypeStruct((n, width), F32), jax.ShapeDtypeStruct((1, width), F32)],
        scratch_shapes=[pltpu.VMEM((1, width), F32)],
        compiler_params=_params(("arbitrary",)),
        name="router",
    )(x, g, router_padded)


def _token_tile(ref, first_row):
    return ref.at[pl.ds(pl.multiple_of(first_row, SUBLANES), SUBLANES), :]


def _dispatch_kernel(pad_ref, dst_ref, x_ref, g_ref, xs_hbm, xt_ref, zero_ref, sem, zsem):
    i = pl.program_id(0)
    n_steps = pl.num_programs(0)
    tm = x_ref.shape[0]
    T = SUBLANES
    slot = i % 2

    def wait_step(s):
        for _ in range(TOP_K):
            pltpu.make_async_copy(xt_ref.at[s], xs_hbm.at[pl.ds(0, tm * T), :], sem.at[s]).wait()

    @pl.when(i == 0)
    def _():
        zero_ref[...] = jnp.zeros(zero_ref.shape, F32)
        for e in range(N_EXPERTS + 1):
            first = pad_ref[0, e]
            tokens = 1 if e < N_EXPERTS else ZERO_TOKENS
            src = zero_ref.at[pl.ds(0, tokens * T), :]

            def dst(r):
                return xs_hbm.at[pl.ds(pl.multiple_of((first + r * tokens) * T, T), tokens * T), :]

            def fill(r, carry):
                pltpu.make_async_copy(src, dst(r), zsem).start()
                return carry

            def drain(r, carry):
                pltpu.make_async_copy(src, dst(0), zsem).wait()
                return carry

            lax.fori_loop(0, pad_ref[1, e] // tokens, fill, 0)
            lax.fori_loop(0, pad_ref[1, e] // tokens, drain, 0)

    _store_token_tiles(xt_ref.at[slot], _rms(x_ref[...], g_ref[...]))

    for choice in range(TOP_K):
        def body(gi, carry):
            for k in range(DMA_UNROLL):
                tok = gi * DMA_UNROLL + k
                pltpu.make_async_copy(_token_tile(xt_ref.at[slot], tok * T),
                                      _token_tile(xs_hbm, dst_ref[0, 0, choice * tm + tok]),
                                      sem.at[slot]).start(priority=k % 2)
            return carry

        lax.fori_loop(0, tm // DMA_UNROLL, body, 0)

    @pl.when(i >= 1)
    def _():
        wait_step(1 - slot)

    @pl.when(i == n_steps - 1)
    def _():
        wait_step(slot)


def _dispatch(x, g, dest_blocks, pad_info, p):
    n, d = x.shape
    tm = dest_blocks.shape[2] // TOP_K
    T = SUBLANES
    grid_spec = pltpu.PrefetchScalarGridSpec(
        num_scalar_prefetch=1,
        grid=(n // tm,),
        in_specs=[pl.BlockSpec((1, 1, TOP_K * tm), lambda i, pad: (i, 0, 0), memory_space=pltpu.SMEM),
                  pl.BlockSpec((tm, d), lambda i, pad: (i, 0)),
                  pl.BlockSpec(g.shape, lambda i, pad: (0, 0))],
        out_specs=pl.BlockSpec(memory_space=pl.ANY),
        scratch_shapes=[pltpu.VMEM((2, tm * T, d // T), F32), pltpu.VMEM((ZERO_TOKENS * T, d // T), F32),
                        pltpu.SemaphoreType.DMA((2,)), pltpu.SemaphoreType.DMA],
    )
    return pl.pallas_call(
        _dispatch_kernel,
        grid_spec=grid_spec,
        out_shape=jax.ShapeDtypeStruct((p * T, d // T), F32),
        compiler_params=_params(("arbitrary",)),
        name="dispatch",
    )(pad_info, dest_blocks, x, g)


def _expert_kernel(be_ref, nu_ref, x_ref, w1_ref, w3_ref, w2_ref, o_ref, xb_ref, y_ref):
    b = pl.program_id(0)
    c = pl.program_id(1)
    tb = xb_ref.shape[0]

    @pl.when(b < nu_ref[0])
    def _():
        @pl.when(c == 0)
        def _():
            xb_ref[...] = _load_token_tiles(x_ref, tb).astype(BF16)
            y_ref[...] = jnp.zeros(y_ref.shape, F32)

        xb = xb_ref[...]
        h1 = _dot(xb, w1_ref[...].astype(BF16))
        h3 = _dot(xb, w3_ref[...].astype(BF16))
        y_ref[...] += _dot((jax.nn.silu(h1) * h3).astype(BF16), w2_ref[...].astype(BF16))

        @pl.when(c == pl.num_programs(1) - 1)
        def _():
            _store_token_tiles(o_ref, y_ref[...])

    @pl.when((b >= nu_ref[0]) & (c == pl.num_programs(1) - 1))
    def _():
        o_ref[...] = jnp.zeros(o_ref.shape, F32)


def _experts(xs, block_e, n_used, w1, w3, w2, tb):
    T = SUBLANES
    d = w1.shape[1]
    nc = w1.shape[2] // FF_CHUNK
    nblk = xs.shape[0] // (tb * T)
    blk = lambda b, nu: jnp.minimum(b, nu[0] - 1)
    chunk = lambda b, c, nu: jnp.where(b < nu[0], c, nc - 1)
    grid_spec = pltpu.PrefetchScalarGridSpec(
        num_scalar_prefetch=2,
        grid=(nblk, nc),
        in_specs=[pl.BlockSpec((tb * T, d // T), lambda b, c, be, nu: (blk(b, nu), 0)),
                  pl.BlockSpec((None, d, FF_CHUNK), lambda b, c, be, nu: (be[b], 0, chunk(b, c, nu))),
                  pl.BlockSpec((None, d, FF_CHUNK), lambda b, c, be, nu: (be[b], 0, chunk(b, c, nu))),
                  pl.BlockSpec((None, FF_CHUNK, d), lambda b, c, be, nu: (be[b], chunk(b, c, nu), 0))],
        out_specs=pl.BlockSpec((tb * T, d // T), lambda b, c, be, nu: (b, 0)),
        scratch_shapes=[pltpu.VMEM((tb, d), BF16), pltpu.VMEM((tb, d), F32)],
    )
    return pl.pallas_call(
        _expert_kernel,
        grid_spec=grid_spec,
        out_shape=jax.ShapeDtypeStruct(xs.shape, F32),
        compiler_params=_params(("arbitrary", "arbitrary")),
        name="experts",
    )(block_e, n_used, xs, w1, w3, w2)


def _combine_kernel(dst_ref, dstn_ref, x_ref, g_ref, ys_hbm, o_ref, yg_ref, sem):
    i = pl.program_id(0)
    n_steps = pl.num_programs(0)
    tm = x_ref.shape[0]
    T = SUBLANES
    slot = i % 2

    def gather(idx_ref, s):
        def body(gi, carry):
            for k in range(DMA_UNROLL):
                r = gi * DMA_UNROLL + k
                pltpu.make_async_copy(_token_tile(ys_hbm, idx_ref[0, 0, r]), _token_tile(yg_ref.at[s], r * T),
                                      sem.at[s]).start(priority=k % 2)
            return carry
        lax.fori_loop(0, TOP_K * tm // DMA_UNROLL, body, 0)

    @pl.when(i == 0)
    def _():
        gather(dst_ref, 0)

    @pl.when(i + 1 < n_steps)
    def _():
        gather(dstn_ref, 1 - slot)

    pltpu.make_async_copy(ys_hbm.at[pl.ds(0, TOP_K * tm * T), :], yg_ref.at[slot], sem.at[slot]).wait()
    g = g_ref[...]
    y0 = _load_token_tiles(yg_ref, tm, lead=slot)
    y1 = jnp.concatenate([yg_ref[slot, pl.ds(tm * T + k, tm, stride=T), :] for k in range(T)], axis=-1)
    o_ref[...] = x_ref[...] + g[:, ROUTE_GATE:ROUTE_GATE + 1] * y0 + g[:, ROUTE_GATE + 1:ROUTE_GATE + 2] * y1


def _combine(x, ys, dest_blocks, gate):
    n, d = x.shape
    tm = dest_blocks.shape[2] // TOP_K
    T = SUBLANES
    n_steps = n // tm
    row = lambda width: pl.BlockSpec((tm, width), lambda i: (i, 0))
    idx = lambda imap: pl.BlockSpec((1, 1, TOP_K * tm), imap, memory_space=pltpu.SMEM)
    return pl.pallas_call(
        _combine_kernel,
        grid=(n_steps,),
        in_specs=[idx(lambda i: (i, 0, 0)), idx(lambda i: (jnp.minimum(i + 1, n_steps - 1), 0, 0)),
                  row(d), row(gate.shape[1]), pl.BlockSpec(memory_space=pl.ANY)],
        out_specs=row(d),
        out_shape=jax.ShapeDtypeStruct((n, d), F32),
        scratch_shapes=[pltpu.VMEM((2, TOP_K * tm * T, d // T), F32), pltpu.SemaphoreType.DMA((2,))],
        compiler_params=_params(("arbitrary",)),
        name="combine",
    )(dest_blocks, dest_blocks, x, gate, ys)


def _moe(x, g, router, w1, w3, w2):
    n, d = x.shape
    a = n * TOP_K
    tb = min(MOE_TILE, a)
    router_padded = jnp.pad(router, ((0, 0), (0, 128 - N_EXPERTS)))
    table, cnt = _router(x, g, router_padded)
    top_e = table[:, ROUTE_EXPERT:ROUTE_EXPERT + TOP_K].astype(jnp.int32)
    rank = table[:, ROUTE_RANK:ROUTE_RANK + TOP_K].astype(jnp.int32)
    counts = cnt[0, :N_EXPERTS].astype(jnp.int32)
    padded = (counts + tb - 1) // tb * tb
    pend = jnp.cumsum(padded)
    pstart = pend - padded
    onehot = (top_e[:, :, None] == jnp.arange(N_EXPERTS)[None, None, :]).astype(jnp.int32)
    dest = jnp.sum(onehot * pstart[None, None, :], axis=-1) + rank
    n_blocks = a // tb + N_EXPERTS
    p = n_blocks * tb
    block_e = jnp.minimum(jnp.sum(pend[None, :] <= (jnp.arange(n_blocks) * tb)[:, None], axis=1),
                          N_EXPERTS - 1).astype(jnp.int32)
    n_used = (pend[-1] // tb).astype(jnp.int32).reshape(1)
    tm = min(ROW_TILE, n)
    dest_blocks = (dest * SUBLANES).reshape(n // tm, tm, TOP_K).transpose(0, 2, 1).reshape(n // tm, 1, TOP_K * tm)
    pad_info = jnp.stack([jnp.append(pstart + counts, pend[-1]),
                          jnp.append(padded - counts, p - pend[-1])]).astype(jnp.int32)
    xs = _dispatch(x, g, dest_blocks, pad_info, p)
    ys = _experts(xs, block_e, n_used, w1, w3, w2, tb)
    return _combine(x, ys, dest_blocks, table)


def _tile_gain(gain, width, scale=1.0):
    return jnp.tile(gain, width // gain.shape[0]) * scale


def kernel(x, mem, rel_bias, norm_mix, w_in, b_gate, a_q_gain, a_k_gain, pool_w, pool_scale,
           c_q_gain, c_k_gain, c_sinks, w_branch, w_out, norm_cross, norm_mem, w_xq, w_xk, w_xv,
           x_q_gain, x_k_gain, w_xo, norm_ffn, ffn_w1, ffn_w3, ffn_w2, router, moe_w1, moe_w3, moe_w2):
    batch, seq, d = x.shape
    mem_len = mem.shape[1]
    depth = norm_mix.shape[0]
    xs = x.reshape(batch * seq, d)
    mems = mem.reshape(batch * mem_len, d)

    tab_a = rel_bias[:, :A_HEADS].T
    tab_c = rel_bias[:, A_HEADS:].T
    nb = seq // A_BLOCK
    n_near = min(nb, (_saturation_distance() + 2 * A_BLOCK - 2) // A_BLOCK)
    far_a = tab_a[:, REL_BUCKETS - 1] * LOG2E
    bias_a = _moba_bias_tiles(_bias_by_distance(tab_a, n_near * A_BLOCK), n_near) * LOG2E
    bias_a = jnp.concatenate([bias_a, jnp.broadcast_to(far_a[:, None, None, None],
                                                       (A_HEADS, 1, A_BLOCK, A_BLOCK))], axis=1)
    bias_a = jnp.concatenate([bias_a[0::2], bias_a[1::2]], axis=-1)
    bias_c = _swa_bias_tiles(_bias_by_distance(tab_c, C_WINDOW)) * LOG2E
    seg = np.arange(256) // HEAD_DIM
    bd = jnp.asarray(seg[:, None] == seg[None, :], BF16)

    row = lambda v: v.reshape(1, -1)
    for l in range(depth):
        scale = HEAD_DIM ** -0.5 * LOG2E
        gains = jnp.stack([_tile_gain(a_q_gain[l], WIDTH, scale), _tile_gain(a_k_gain[l], WIDTH),
                           _tile_gain(c_q_gain[l], WIDTH, scale), _tile_gain(c_k_gain[l], WIDTH)])
        flag_a = _bounded_flag(a_q_gain[l], a_k_gain[l], HEAD_DIM, jnp.max(jnp.abs(tab_a)))
        flag_c = _bounded_flag(c_q_gain[l], c_k_gain[l], HEAD_DIM,
                               jnp.maximum(jnp.max(jnp.abs(tab_c)), jnp.max(jnp.abs(c_sinks[l]))))
        qa, ka, va, ub, qc, kc, vc, gates = _in_proj(
            xs, row(norm_mix[l]), w_in[l].astype(BF16), row(b_gate[l]), gains, bd)
        oa = _moba(qa, ka, va, bias_a, far_a, flag_a, batch, seq)
        oc = _swa(qc, kc, vc, bias_c, c_sinks[l] * LOG2E, flag_c, batch, seq)
        k_mem, v_mem = _mem_kv(mems, row(norm_mem[l]), w_xk[l].astype(BF16), w_xv[l].astype(BF16),
                               row(_tile_gain(x_k_gain[l], WIDTH)))
        xs = _merge_cross(xs, oa, ub, oc, gates, pool_w[l].astype(BF16), row(pool_scale[l]),
                          w_branch[l].astype(BF16), w_out[l].astype(BF16), row(norm_cross[l]),
                          w_xq[l].astype(BF16), row(_tile_gain(x_q_gain[l], WIDTH, X_HEAD_DIM ** -0.5)),
                          k_mem, v_mem, w_xo[l].astype(BF16), seq, mem_len)
        i = l // 2
        if l % 2 == 0:
            xs = _ffn(xs, row(norm_ffn[l]), ffn_w1[i].astype(BF16), ffn_w3[i].astype(BF16),
                      ffn_w2[i].astype(BF16))
        else:
            xs = _moe(xs, row(norm_ffn[l]), router[i], moe_w1[i], moe_w3[i], moe_w2[i])
    return xs.reshape(batch, seq, d)
```
